```python
import math
import jax, jax.numpy as jnp
from jax import lax
import numpy as np

D_MODEL = 1024
BATCH = 4
SEQ = 4096
DEPTH = 2

N_MIXERS = 2
N_A_LAYERS = (DEPTH + 1) // 2
N_B_LAYERS = DEPTH // 2
A_INNER = 3 * D_MODEL
A_CHUNK = 128
A_GROUPS = 8
A_GROUP_W = A_INNER // A_GROUPS
B_HEADS = 16
B_HEAD_DIM = 64
B_V_DIM = 64
B_KV_LATENT = 256
B_IDX_HEADS = 8
B_IDX_DIM = 64
B_TOPK_MAX = 256
B_QBLK = 128
B_PROJ = B_HEADS * B_HEAD_DIM + B_KV_LATENT + B_IDX_HEADS * B_IDX_DIM + B_IDX_DIM + B_IDX_HEADS
REL_BUCKETS = 32
REL_MAX_DIST = 128
N_EXPERTS = 16
N_GROUPS = 4
EXPERTS_PER_GROUP = N_EXPERTS // N_GROUPS
TOP_K = 2
D_EXPERT = 3584
MOE_BLOCK = 128
EPS = 1e-6

kernel_name = "hybrid_gmlp_dsa_grouped_moe_adaln"


def rms_norm(x, g):
    xf = x.astype(jnp.float32)
    y = xf * lax.rsqrt(jnp.mean(xf * xf, axis=-1, keepdims=True) + EPS)
    return (y * g.astype(jnp.float32)).astype(x.dtype)


def layer_norm(x, g, b):
    xf = x.astype(jnp.float32)
    mu = jnp.mean(xf, axis=-1, keepdims=True)
    var = jnp.mean(jnp.square(xf - mu), axis=-1, keepdims=True)
    y = (xf - mu) * lax.rsqrt(var + EPS)
    return (y * g.astype(jnp.float32) + b.astype(jnp.float32)).astype(x.dtype)


def t5_bucket(dist):
    n = jnp.maximum(dist, 0)
    exact = REL_BUCKETS // 2
    nf = jnp.maximum(n, 1).astype(jnp.float32)
    large = exact + (jnp.log(nf / exact) / math.log(REL_MAX_DIST / exact)
                     * (REL_BUCKETS - exact)).astype(jnp.int32)
    large = jnp.minimum(large, REL_BUCKETS - 1)
    return jnp.where(n < exact, n, large)


def gmlp_mixer(h, w_in, ln_g, ln_b, w_sp, b_sp, w_out):
    bsz, L, _ = h.shape
    z = jax.nn.gelu(h @ w_in)
    u, v = jnp.split(z, 2, axis=-1)
    v = layer_norm(v, ln_g, ln_b)
    nc = L // A_CHUNK
    v = v.reshape(bsz, nc, A_CHUNK, A_GROUPS, A_GROUP_W)
    causal = jnp.tril(jnp.ones((A_CHUNK, A_CHUNK), dtype=w_sp.dtype))
    ws = w_sp * causal
    fv = jnp.einsum('gts,bnsgc->bntgc', ws, v) + b_sp.T[None, None, :, :, None]
    s = u * fv.reshape(bsz, L, A_INNER)
    return s @ w_out


def dsa_mixer(h, w_in, kv_norm_g, w_uk, w_uv, kidx_g, w_out, rel_bias):
    bsz, L, _ = h.shape
    proj = h @ w_in
    o1 = B_HEADS * B_HEAD_DIM
    o2 = o1 + B_KV_LATENT
    o3 = o2 + B_IDX_HEADS * B_IDX_DIM
    o4 = o3 + B_IDX_DIM
    q, c_kv, q_i, k_i, w_i = jnp.split(proj, [o1, o2, o3, o4], axis=-1)
    q = q.reshape(bsz, L, B_HEADS, B_HEAD_DIM)
    c_kv = rms_norm(c_kv, kv_norm_g)
    q_abs = jnp.einsum('blhd,chd->blhc', q, w_uk) * (B_HEAD_DIM ** -0.5)
    q_i = q_i.reshape(bsz, L, B_IDX_HEADS, B_IDX_DIM)
    k_i = rms_norm(k_i, kidx_g)
    w_i = w_i * (B_IDX_HEADS ** -0.5 * B_IDX_DIM ** -0.5)
    k_top = min(B_TOPK_MAX, L // 4)
    nb = L // B_QBLK
    pos_all = jnp.arange(L, dtype=jnp.int32)

    def to_blocks(a):
        return a.reshape(bsz, nb, B_QBLK, *a.shape[2:]).swapaxes(0, 1)

    def block(args):
        qa, qi, wi, start = args
        t = start + jnp.arange(B_QBLK, dtype=jnp.int32)
        sc = jnp.einsum('bthd,bsd->bths', qi, k_i)
        sc = jnp.einsum('bths,bth->bts', jax.nn.relu(sc), wi).astype(jnp.float32)
        causal = pos_all[None, :] <= t[:, None]
        sc = jnp.where(causal[None], sc, -jnp.inf)
        _, sel = lax.top_k(sc, k_top)
        kv_sel = jax.vmap(lambda cb, ib: cb[ib])(c_kv, sel)
        dist = t[None, :, None] - sel
        valid = dist >= 0
        bias = rel_bias[t5_bucket(dist)]
        logits = (jnp.einsum('bthc,btkc->bthk', qa, kv_sel).astype(jnp.float32)
                  + jnp.transpose(bias, (0, 1, 3, 2)).astype(jnp.float32))
        logits = jnp.where(valid[:, :, None, :], logits, -jnp.inf)
        p = jax.nn.softmax(logits, axis=-1).astype(qa.dtype)
        o_lat = jnp.einsum('bthk,btkc->bthc', p, kv_sel)
        o = jnp.einsum('bthc,chd->bthd', o_lat, w_uv)
        return o.reshape(bsz, B_QBLK, B_HEADS * B_V_DIM)

    starts = jnp.arange(nb, dtype=jnp.int32) * B_QBLK
    o = lax.map(block, (to_blocks(q_abs), to_blocks(q_i), to_blocks(w_i), starts))
    o = o.swapaxes(0, 1).reshape(bsz, L, B_HEADS * B_V_DIM)
    return o @ w_out


def grouped_moe(h, router_w, router_b, w_gate, w_up, w_down):
    bsz, L, D = h.shape
    N = bsz * L
    hf = h.reshape(N, D)
    aff = jax.nn.sigmoid((hf @ router_w).astype(jnp.float32))
    sel = aff + router_b.astype(jnp.float32)
    sel_g = sel.reshape(N, N_GROUPS, EXPERTS_PER_GROUP)
    grp_score = lax.top_k(sel_g, 2)[0].sum(-1)
    grp = jnp.argmax(grp_score, axis=-1).astype(jnp.int32)
    in_grp = jnp.take_along_axis(sel_g, grp[:, None, None], axis=1)[:, 0]
    _, local = lax.top_k(in_grp, TOP_K)
    experts = grp[:, None] * EXPERTS_PER_GROUP + local
    gates = jnp.take_along_axis(aff, experts, axis=1)
    gates = gates / jnp.sum(gates, axis=-1, keepdims=True)

    A = N * TOP_K
    e_flat = experts.reshape(A)
    order = jnp.argsort(e_flat)
    e_sorted = e_flat[order]
    counts = jnp.bincount(e_flat, length=N_EXPERTS)
    padded = ((counts + MOE_BLOCK - 1) // MOE_BLOCK) * MOE_BLOCK
    pad_end = jnp.cumsum(padded)
    pad_start = pad_end - padded
    seg_start = jnp.cumsum(counts) - counts
    rank = jnp.arange(A, dtype=jnp.int32) - seg_start[e_sorted]
    slot_sorted = (pad_start[e_sorted] + rank).astype(jnp.int32)
    P = A + N_EXPERTS * MOE_BLOCK
    n_blk = P // MOE_BLOCK
    tok_of_slot = jnp.zeros((P,), jnp.int32).at[slot_sorted].set((order // TOP_K).astype(jnp.int32))
    blk_start = jnp.arange(n_blk, dtype=jnp.int32) * MOE_BLOCK
    blk_expert = jnp.minimum(jnp.searchsorted(pad_end, blk_start, side='right'),
                             N_EXPERTS - 1).astype(jnp.int32)
    x_blk = hf[tok_of_slot].reshape(n_blk, MOE_BLOCK, D)

    def expert_block(args):
        xb, e = args
        return (jax.nn.silu(xb @ w_gate[e]) * (xb @ w_up[e])) @ w_down[e]

    y_slot = lax.map(expert_block, (x_blk, blk_expert)).reshape(P, D)
    slot_of_assign = jnp.zeros((A,), jnp.int32).at[order].set(slot_sorted)
    y = y_slot[slot_of_assign].reshape(N, TOP_K, D)
    out = jnp.einsum('nk,nkd->nd', gates.astype(y.dtype), y)
    return out.reshape(bsz, L, D)


def setup_inputs(seed: int = 0) -> dict:
    key = jax.random.key(seed)
    ks = jax.random.split(key, 32)
    f32 = jnp.float32
    D = D_MODEL

    def nrm(k, shape, scale):
        return jax.random.normal(k, shape, f32) * scale

    return {
        "x": nrm(ks[0], (BATCH, SEQ, D), 1.0),
        "c": nrm(ks[1], (BATCH, D), 1.0),
        "ada_w": nrm(ks[2], (DEPTH, D, 6 * D), 0.5 * D ** -0.5),
        "ada_b": nrm(ks[3], (DEPTH, 6 * D), 0.02),
        "norm1_g": 1.0 + nrm(ks[4], (DEPTH, D), 0.05),
        "norm2_g": 1.0 + nrm(ks[5], (DEPTH, D), 0.05),
        "a_w_in": nrm(ks[6], (N_A_LAYERS, D, 2 * A_INNER), D ** -0.5),
        "a_ln_g": 1.0 + nrm(ks[7], (N_A_LAYERS, A_INNER), 0.05),
        "a_ln_b": nrm(ks[8], (N_A_LAYERS, A_INNER), 0.02),
        "a_w_sp": nrm(ks[9], (N_A_LAYERS, A_GROUPS, A_CHUNK, A_CHUNK), A_CHUNK ** -0.5),
        "a_b_sp": 1.0 + nrm(ks[10], (N_A_LAYERS, A_GROUPS, A_CHUNK), 0.1),
        "a_w_out": nrm(ks[11], (N_A_LAYERS, A_INNER, D), A_INNER ** -0.5),
        "b_w_in": nrm(ks[12], (N_B_LAYERS, D, B_PROJ), D ** -0.5),
        "b_kv_norm_g": 1.0 + nrm(ks[13], (N_B_LAYERS, B_KV_LATENT), 0.05),
        "b_w_uk": nrm(ks[14], (N_B_LAYERS, B_KV_LATENT, B_HEADS, B_HEAD_DIM), B_KV_LATENT ** -0.5),
        "b_w_uv": nrm(ks[15], (N_B_LAYERS, B_KV_LATENT, B_HEADS, B_V_DIM), B_KV_LATENT ** -0.5),
        "b_kidx_g": 1.0 + nrm(ks[16], (N_B_LAYERS, B_IDX_DIM), 0.05),
        "b_w_out": nrm(ks[17], (N_B_LAYERS, B_HEADS * B_V_DIM, D), (B_HEADS * B_V_DIM) ** -0.5),
        "rel_bias": nrm(ks[18], (REL_BUCKETS, B_HEADS), 0.3),
        "router_w": nrm(ks[19], (D, N_EXPERTS), D ** -0.5),
        "router_b": nrm(ks[20], (N_EXPERTS,), 0.01),
        "moe_w_gate": nrm(ks[21], (DEPTH, N_EXPERTS, D, D_EXPERT), D ** -0.5),
        "moe_w_up": nrm(ks[22], (DEPTH, N_EXPERTS, D, D_EXPERT), D ** -0.5),
        "moe_w_down": nrm(ks[23], (DEPTH, N_EXPERTS, D_EXPERT, D), D_EXPERT ** -0.5),
        "final_g": 1.0 + nrm(ks[24], (D,), 0.05),
    }


def reference(x, c, ada_w, ada_b, norm1_g, norm2_g, a_w_in, a_ln_g, a_ln_b, a_w_sp, a_b_sp,
              a_w_out, b_w_in, b_kv_norm_g, b_w_uk, b_w_uv, b_kidx_g, b_w_out, rel_bias,
              router_w, router_b, moe_w_gate, moe_w_up, moe_w_down, final_g):
    mod = jnp.einsum('bd,lde->lbe', jax.nn.silu(c), ada_w) + ada_b[:, None, :]
    h = x
    for i in range(DEPTH):
        sh1, sc1, g1, sh2, sc2, g2 = jnp.split(mod[i], 6, axis=-1)
        hn = rms_norm(h, norm1_g[i]) * (1.0 + sc1[:, None, :]) + sh1[:, None, :]
        j = i // N_MIXERS
        if i % N_MIXERS == 0:
            y = gmlp_mixer(hn, a_w_in[j], a_ln_g[j], a_ln_b[j], a_w_sp[j], a_b_sp[j], a_w_out[j])
        else:
            y = dsa_mixer(hn, b_w_in[j], b_kv_norm_g[j], b_w_uk[j], b_w_uv[j], b_kidx_g[j],
                          b_w_out[j], rel_bias)
        h = h + g1[:, None, :] * y
        hn = rms_norm(h, norm2_g[i]) * (1.0 + sc2[:, None, :]) + sh2[:, None, :]
        h = h + g2[:, None, :] * grouped_moe(hn, router_w, router_b, moe_w_gate[i],
                                             moe_w_up[i], moe_w_down[i])
    return rms_norm(h, final_g)
```

```python
import functools
import math

import jax
import jax.numpy as jnp
from jax import lax
from jax.experimental import pallas as pl
from jax.experimental.pallas import tpu as pltpu

F32 = jnp.float32
BF16 = jnp.bfloat16
I32 = jnp.int32
HIGHEST = lax.Precision.HIGHEST

EPS = 1e-6
A_CHUNK = 128
A_GROUPS = 8
B_HEADS = 16
B_HEAD_DIM = 64
B_V_DIM = 64
B_KV_LATENT = 256
B_IDX_HEADS = 8
B_IDX_DIM = 64
B_TOPK_MAX = 256
QBLK = 128
KTILE = 256
REL_BUCKETS = 32
REL_MAX_DIST = 128
N_EXPERTS = 16
N_GROUPS = 4
EXPERTS_PER_GROUP = 4
TOP_K = 2
MOE_ROWS = 1024
MOE_SUB = 256
MOE_FT = 512

ROW_TILE = 256
VMEM_LIMIT = 56 * 1024 * 1024

INT_MIN = -2 ** 31
NEG_BIG = -1e30


def _cparams(sem):
    return pltpu.CompilerParams(dimension_semantics=sem, vmem_limit_bytes=VMEM_LIMIT)


def _mod_rmsnorm(h, g, scale, shift):
    ms = jnp.mean(h * h, axis=-1, keepdims=True)
    return (h * lax.rsqrt(ms + EPS) * g) * (1.0 + scale) + shift


def _gelu_tanh(x):
    c = math.sqrt(2.0 / math.pi)
    return 0.5 * x * (1.0 + jnp.tanh(c * (x + 0.044715 * (x * x * x))))


def _adaln_kernel(c_ref, w_ref, b_ref, o_ref):
    c = c_ref[...]
    sc = c * jax.nn.sigmoid(c)
    o_ref[0] = jnp.dot(sc, w_ref[0], precision=HIGHEST, preferred_element_type=F32) + b_ref[0]


def _adaln(c, ada_w, ada_b):
    depth, d, e = ada_w.shape
    bsz = c.shape[0]
    bp = 8
    c_pad = jnp.zeros((bp, d), F32).at[:bsz].set(c)
    tn = 1024
    out = pl.pallas_call(
        _adaln_kernel,
        out_shape=jax.ShapeDtypeStruct((depth, bp, e), F32),
        grid=(depth, e // tn),
        in_specs=[
            pl.BlockSpec((bp, d), lambda l, j: (0, 0)),
            pl.BlockSpec((1, d, tn), lambda l, j: (l, 0, j)),
            pl.BlockSpec((1, 1, tn), lambda l, j: (l, 0, j)),
        ],
        out_specs=pl.BlockSpec((1, bp, tn), lambda l, j: (l, 0, j)),
        compiler_params=_cparams(("arbitrary", "arbitrary")),
        name="adaln",
    )(c_pad, ada_w, ada_b.reshape(depth, 1, e))
    return out[:, :bsz]


def _residual_epilogue(h, y, mod, n2g, rw, h_out, hn_out, lg_out):
    h1 = h + mod[2:3] * y
    h_out[...] = h1
    hn2 = _mod_rmsnorm(h1, n2g, mod[4:5], mod[3:4])
    hn_out[...] = hn2.astype(BF16)
    lg_out[...] = jnp.dot(hn2, rw, precision=HIGHEST, preferred_element_type=F32)


def _gmlp_kernel(h_ref, mod_ref, n1g_ref, win_ref, lng_ref, lnb_ref, wsp_ref, bsp_ref, wout_ref,
                 n2g_ref, rw_ref, h_out, hn_out, lg_out, u_scr, v_scr, s_scr):
    tm = h_ref.shape[0]
    inner = u_scr.shape[1]
    gw = inner // A_GROUPS
    tn = 512
    h = h_ref[...]
    mod = mod_ref[0]
    hn = _mod_rmsnorm(h, n1g_ref[...], mod[1:2], mod[0:1]).astype(BF16)
    for j in range(2 * inner // tn):
        z = _gelu_tanh(jnp.dot(hn, win_ref[:, j * tn:(j + 1) * tn], preferred_element_type=F32))
        if j < inner // tn:
            u_scr[:, j * tn:(j + 1) * tn] = z
        else:
            jj = j - inner // tn
            v_scr[:, jj * tn:(jj + 1) * tn] = z
    vsum = jnp.zeros((tm, 1), F32)
    for j in range(inner // tn):
        vsum = vsum + jnp.sum(v_scr[:, j * tn:(j + 1) * tn], axis=-1, keepdims=True)
    mu = vsum * (1.0 / inner)
    vsq = jnp.zeros((tm, 1), F32)
    for j in range(inner // tn):
        d = v_scr[:, j * tn:(j + 1) * tn] - mu
        vsq = vsq + jnp.sum(d * d, axis=-1, keepdims=True)
    rstd = lax.rsqrt(vsq * (1.0 / inner) + EPS)
    row = lax.broadcasted_iota(I32, (A_CHUNK, A_CHUNK), 0)
    col = lax.broadcasted_iota(I32, (A_CHUNK, A_CHUNK), 1)
    tril = row >= col
    for g in range(A_GROUPS):
        ws = jnp.where(tril, wsp_ref[g], 0.0).astype(BF16)
        bcol = bsp_ref[:, g:g + 1]
        lg = lng_ref[:, g * gw:(g + 1) * gw]
        lb = lnb_ref[:, g * gw:(g + 1) * gw]
        for c in range(tm // A_CHUNK):
            r0, r1 = c * A_CHUNK, (c + 1) * A_CHUNK
            vt = v_scr[r0:r1, g * gw:(g + 1) * gw]
            vn = ((vt - mu[r0:r1]) * rstd[r0:r1]) * lg + lb
            fv = jnp.dot(ws, vn.astype(BF16), preferred_element_type=F32) + bcol
            s_scr[r0:r1, g * gw:(g + 1) * gw] = (u_scr[r0:r1, g * gw:(g + 1) * gw] * fv).astype(BF16)
    y = jnp.dot(s_scr[...], wout_ref[...], preferred_element_type=F32)
    _residual_epilogue(h, y, mod, n2g_ref[...], rw_ref[...], h_out, hn_out, lg_out)


def _const_spec(shape):
    nd = len(shape)
    return pl.BlockSpec(shape, lambda i, _nd=nd: (0,) * _nd, pipeline_mode=pl.Buffered(1))


def _gmlp_layer(h, mod_l, n1g, w_in, ln_g, ln_b, w_sp, b_sp, w_out, n2g, router_w, seq):
    n, d = h.shape
    inner = w_out.shape[0]
    tm = ROW_TILE
    per_b = seq // tm
    ne = router_w.shape[1]
    return pl.pallas_call(
        _gmlp_kernel,
        out_shape=(jax.ShapeDtypeStruct((n, d), F32), jax.ShapeDtypeStruct((n, d), BF16),
                   jax.ShapeDtypeStruct((n, ne), F32)),
        grid=(n // tm,),
        in_specs=[
            pl.BlockSpec((tm, d), lambda i: (i, 0)),
            pl.BlockSpec((1, 6, d), lambda i: (i // per_b, 0, 0)),
            _const_spec((1, d)),
            _const_spec((d, 2 * inner)),
            _const_spec((1, inner)),
            _const_spec((1, inner)),
            _const_spec((A_GROUPS, A_CHUNK, A_CHUNK)),
            _const_spec((A_CHUNK, A_GROUPS)),
            _const_spec((inner, d)),
            _const_spec((1, d)),
            _const_spec((d, ne)),
        ],
        out_specs=(pl.BlockSpec((tm, d), lambda i: (i, 0)), pl.BlockSpec((tm, d), lambda i: (i, 0)),
                   pl.BlockSpec((tm, ne), lambda i: (i, 0))),
        scratch_shapes=[pltpu.VMEM((tm, inner), F32), pltpu.VMEM((tm, inner), F32),
                        pltpu.VMEM((tm, inner), BF16)],
        compiler_params=_cparams(("arbitrary",)),
        name="gmlp_layer",
    )(h, mod_l, n1g.reshape(1, d), w_in.astype(BF16), ln_g.reshape(1, inner), ln_b.reshape(1, inner),
      w_sp, b_sp.T, w_out.astype(BF16), n2g.reshape(1, d), router_w)


def _first_max4(rows):
    m = jnp.maximum(jnp.maximum(rows[0], rows[1]), jnp.maximum(rows[2], rows[3]))
    idx = jnp.where(rows[0] == m, 0, jnp.where(rows[1] == m, 1, jnp.where(rows[2] == m, 2, 3)))
    return m, idx.astype(I32)


def _route_kernel(lt_ref, rb_ref, e_ref, g_ref):
    aff = jax.nn.sigmoid(lt_ref[...])
    sel = aff + rb_ref[...]
    neg = jnp.float32(-jnp.inf)
    g_score, g_i1, g_i2 = [], [], []
    for g in range(N_GROUPS):
        rows = [sel[4 * g + k:4 * g + k + 1, :] for k in range(EXPERTS_PER_GROUP)]
        m1, i1 = _first_max4(rows)
        rest = [jnp.where(i1 == k, neg, rows[k]) for k in range(EXPERTS_PER_GROUP)]
        m2, i2 = _first_max4(rest)
        g_score.append(m1 + m2)
        g_i1.append(i1)
        g_i2.append(i2)
    _, grp = _first_max4(g_score)
    l1 = jnp.zeros_like(grp)
    l2 = jnp.zeros_like(grp)
    for g in range(N_GROUPS):
        l1 = jnp.where(grp == g, g_i1[g], l1)
        l2 = jnp.where(grp == g, g_i2[g], l2)
    e1 = grp * EXPERTS_PER_GROUP + l1
    e2 = grp * EXPERTS_PER_GROUP + l2
    a1 = jnp.zeros_like(g_score[0])
    a2 = jnp.zeros_like(g_score[0])
    for e in range(N_EXPERTS):
        ae = aff[e:e + 1, :]
        a1 = jnp.where(e1 == e, ae, a1)
        a2 = jnp.where(e2 == e, ae, a2)
    tot = a1 + a2
    e_ref[0:1, :] = e1
    e_ref[1:2, :] = e2
    g_ref[0:1, :] = a1 / tot
    g_ref[1:2, :] = a2 / tot


def _route(logits, router_b):
    n, ne = logits.shape
    tn = min(2048, n)
    return pl.pallas_call(
        _route_kernel,
        out_shape=(jax.ShapeDtypeStruct((TOP_K, n), I32), jax.ShapeDtypeStruct((TOP_K, n), F32)),
        grid=(n // tn,),
        in_specs=[pl.BlockSpec((ne, tn), lambda i: (0, i)), pl.BlockSpec((ne, 1), lambda i: (0, 0))],
        out_specs=(pl.BlockSpec((TOP_K, tn), lambda i: (0, i)), pl.BlockSpec((TOP_K, tn), lambda i: (0, i))),
        compiler_params=_cparams(("arbitrary",)),
        name="route",
    )(logits.T, router_b.reshape(ne, 1))


def _slot_tables(experts):
    n = experts.shape[1]
    a = n * TOP_K
    e_flat = experts.T.reshape(a)
    order = jnp.argsort(e_flat).astype(I32)
    e_sorted = e_flat[order]
    counts = jnp.bincount(e_flat, length=N_EXPERTS).astype(I32)
    padded = ((counts + MOE_ROWS - 1) // MOE_ROWS) * MOE_ROWS
    pad_end = jnp.cumsum(padded)
    pad_start = pad_end - padded
    seg_start = jnp.cumsum(counts) - counts
    rank = jnp.arange(a, dtype=I32) - seg_start[e_sorted]
    slot_sorted = (pad_start[e_sorted] + rank).astype(I32)
    p = a + N_EXPERTS * MOE_ROWS
    n_items = p // MOE_ROWS
    tok_of_slot = jnp.zeros((p,), I32).at[slot_sorted].set(order // TOP_K)
    slot_of_assign = jnp.zeros((a,), I32).at[order].set(slot_sorted)
    item_start = jnp.arange(n_items, dtype=I32) * MOE_ROWS
    n_used = pad_end[-1] // MOE_ROWS
    item_e_raw = jnp.minimum(jnp.searchsorted(pad_end, item_start, side='right'), N_EXPERTS - 1).astype(I32)
    used = item_start < pad_end[-1]
    last_e = item_e_raw[jnp.maximum(n_used - 1, 0)]
    item_e = jnp.where(used, item_e_raw, last_e).astype(I32)
    valid = jnp.clip(counts[item_e_raw] - (item_start - pad_start[item_e_raw]), 0, MOE_ROWS)
    item_rows = jnp.where(used, valid, 0).astype(I32)
    item_blk = jnp.where(used, jnp.arange(n_items, dtype=I32), jnp.maximum(n_used - 1, 0)).astype(I32)
    return tok_of_slot, slot_of_assign.reshape(n, TOP_K), item_e, item_rows, item_blk


def _moe_kernel(ie_ref, ir_ref, ib_ref, x_ref, wg_ref, wu_ref, wd_ref, o_ref, wg_s, wu_s, wd_s):
    it = pl.program_id(0)
    f = pl.program_id(1)
    nrows = ir_ref[it]

    @pl.when(f == 0)
    def _():
        o_ref[...] = jnp.zeros_like(o_ref)

    @pl.when(nrows > 0)
    def _():
        wg_s[...] = wg_ref[0].astype(BF16)
        wu_s[...] = wu_ref[0].astype(BF16)
        wd_s[...] = wd_ref[0].astype(BF16)
        nsb = (nrows + MOE_SUB - 1) // MOE_SUB

        def body(sb, carry):
            r0 = pl.multiple_of(sb * MOE_SUB, MOE_SUB)
            x = x_ref[pl.ds(r0, MOE_SUB), :]
            g = jnp.dot(x, wg_s[...], preferred_element_type=F32)
            u = jnp.dot(x, wu_s[...], preferred_element_type=F32)
            hmid = ((g * jax.nn.sigmoid(g)) * u).astype(BF16)
            o_ref[pl.ds(r0, MOE_SUB), :] += jnp.dot(hmid, wd_s[...], preferred_element_type=F32)
            return carry

        lax.fori_loop(0, nsb, body, 0)


def _moe_experts(x_sorted, item_e, item_rows, item_blk, w_gate, w_up, w_down):
    p, d = x_sorted.shape
    n_items = p // MOE_ROWS
    dff = w_gate.shape[2]
    nf = dff // MOE_FT
    grid_spec = pltpu.PrefetchScalarGridSpec(
        num_scalar_prefetch=3,
        grid=(n_items, nf),
        in_specs=[
            pl.BlockSpec((MOE_ROWS, d), lambda i, f, ie, ir, ib: (ib[i], 0)),
            pl.BlockSpec((1, d, MOE_FT), lambda i, f, ie, ir, ib: (ie[i], 0, jnp.where(ir[i] > 0, f, nf - 1))),
            pl.BlockSpec((1, d, MOE_FT), lambda i, f, ie, ir, ib: (ie[i], 0, jnp.where(ir[i] > 0, f, nf - 1))),
            pl.BlockSpec((1, MOE_FT, d), lambda i, f, ie, ir, ib: (ie[i], jnp.where(ir[i] > 0, f, nf - 1), 0)),
        ],
        out_specs=pl.BlockSpec((MOE_ROWS, d), lambda i, f, ie, ir, ib: (i, 0)),
        scratch_shapes=[pltpu.VMEM((d, MOE_FT), BF16), pltpu.VMEM((d, MOE_FT), BF16),
                        pltpu.VMEM((MOE_FT, d), BF16)],
    )
    return pl.pallas_call(
        _moe_kernel,
        out_shape=jax.ShapeDtypeStruct((p, d), F32),
        grid_spec=grid_spec,
        compiler_params=_cparams(("arbitrary", "arbitrary")),
        name="moe_experts",
    )(item_e, item_rows, item_blk, x_sorted, w_gate, w_up, w_down)


def _moe(hn2, logits, router_b, w_gate, w_up, w_down):
    experts, gates = _route(logits, router_b)
    tok_of_slot, slot_of_assign, item_e, item_rows, item_blk = _slot_tables(experts)
    x_sorted = jnp.take(hn2, tok_of_slot, axis=0)
    y_slot = _moe_experts(x_sorted, item_e, item_rows, item_blk, w_gate, w_up, w_down)
    y0 = jnp.take(y_slot, slot_of_assign[:, 0], axis=0)
    y1 = jnp.take(y_slot, slot_of_assign[:, 1], axis=0)
    return y0, y1, gates.T


def _moe_combine(h1, y0, y1, gates, mod):
    return h1 + mod[5:6] * (gates[:, 0:1] * y0 + gates[:, 1:2] * y1)


def _bproj_kernel(h1_ref, y0_ref, y1_ref, gt_ref, modp_ref, mod_ref, n1g_ref, w_ref, kvg_ref, kig_ref,
                  wuk_ref, h_out, qa_out, ckv_out, qi_out, ki_out, wi_out):
    tm = h1_ref.shape[0]
    h = _moe_combine(h1_ref[...], y0_ref[...], y1_ref[...], gt_ref[...], modp_ref[0])
    h_out[...] = h
    mod = mod_ref[0]
    hn = _mod_rmsnorm(h, n1g_ref[...], mod[1:2], mod[0:1]).astype(BF16)
    proj = jnp.dot(hn, w_ref[...], preferred_element_type=F32)
    o1 = B_HEADS * B_HEAD_DIM
    o2 = o1 + B_KV_LATENT
    o3 = o2 + B_IDX_HEADS * B_IDX_DIM
    ckv = proj[:, o1:o2]
    ckv = ckv * lax.rsqrt(jnp.mean(ckv * ckv, axis=-1, keepdims=True) + EPS) * kvg_ref[...]
    ckv_out[...] = ckv.astype(BF16)
    tail = proj[:, o3:o3 + 128]
    ki = tail[:, :B_IDX_DIM]
    ki = ki * lax.rsqrt(jnp.mean(ki * ki, axis=-1, keepdims=True) + EPS) * kig_ref[...]
    ki_out[...] = ki.astype(BF16)
    wi = tail[:, B_IDX_DIM:B_IDX_DIM + B_IDX_HEADS] * (B_IDX_HEADS ** -0.5 * B_IDX_DIM ** -0.5)
    scale = B_HEAD_DIM ** -0.5
    for blk in range(tm // QBLK):
        r0, r1 = blk * QBLK, (blk + 1) * QBLK
        for hh in range(B_HEADS):
            qh = proj[r0:r1, hh * B_HEAD_DIM:(hh + 1) * B_HEAD_DIM].astype(BF16)
            qa = jnp.dot(qh, wuk_ref[hh], preferred_element_type=F32) * scale
            qa_out[blk, hh * QBLK:(hh + 1) * QBLK, :] = qa.astype(BF16)
        for hh in range(B_IDX_HEADS):
            qi_out[blk, hh * QBLK:(hh + 1) * QBLK, :] = proj[r0:r1, o2 + hh * B_IDX_DIM:
                                                            o2 + (hh + 1) * B_IDX_DIM].astype(BF16)
            wi_out[blk, hh * QBLK:(hh + 1) * QBLK, :] = wi[r0:r1, hh:hh + 1]


def _bproj(h1, y0, y1, gates, mod_prev, mod_l, n1g, b_w_in, kv_g, w_uk, kidx_g, seq):
    n, d = h1.shape
    tm = ROW_TILE
    per_b = seq // tm
    nq = n // QBLK
    qpb = tm // QBLK
    o1 = B_HEADS * B_HEAD_DIM
    o2 = o1 + B_KV_LATENT
    o3 = o2 + B_IDX_HEADS * B_IDX_DIM
    wcat = jnp.zeros((d, o3 + 128), F32).at[:, :b_w_in.shape[1]].set(b_w_in).astype(BF16)
    wuk_t = jnp.transpose(w_uk, (1, 2, 0)).astype(BF16)
    row = lambda i: (i, 0)
    modm = lambda i: (i // per_b, 0, 0)
    blk3 = lambda i: (i, 0, 0)
    return pl.pallas_call(
        _bproj_kernel,
        out_shape=(
            jax.ShapeDtypeStruct((n, d), F32),
            jax.ShapeDtypeStruct((nq, B_HEADS * QBLK, B_KV_LATENT), BF16),
            jax.ShapeDtypeStruct((n, B_KV_LATENT), BF16),
            jax.ShapeDtypeStruct((nq, B_IDX_HEADS * QBLK, B_IDX_DIM), BF16),
            jax.ShapeDtypeStruct((n, B_IDX_DIM), BF16),
            jax.ShapeDtypeStruct((nq, B_IDX_HEADS * QBLK, 1), F32),
        ),
        grid=(n // tm,),
        in_specs=[
            pl.BlockSpec((tm, d), row), pl.BlockSpec((tm, d), row), pl.BlockSpec((tm, d), row),
            pl.BlockSpec((tm, TOP_K), row),
            pl.BlockSpec((1, 6, d), modm), pl.BlockSpec((1, 6, d), modm),
            _const_spec((1, d)),
            _const_spec((d, o3 + 128)),
            _const_spec((1, B_KV_LATENT)),
            _const_spec((1, B_IDX_DIM)),
            _const_spec((B_HEADS, B_HEAD_DIM, B_KV_LATENT)),
        ],
        out_specs=(
            pl.BlockSpec((tm, d), row),
            pl.BlockSpec((qpb, B_HEADS * QBLK, B_KV_LATENT), blk3),
            pl.BlockSpec((tm, B_KV_LATENT), row),
            pl.BlockSpec((qpb, B_IDX_HEADS * QBLK, B_IDX_DIM), blk3),
            pl.BlockSpec((tm, B_IDX_DIM), row),
            pl.BlockSpec((qpb, B_IDX_HEADS * QBLK, 1), blk3),
        ),
        compiler_params=_cparams(("arbitrary",)),
        name="dsa_proj",
    )(h1, y0, y1, gates, mod_prev, mod_l, n1g.reshape(1, d), wcat, kv_g.reshape(1, -1),
      kidx_g.reshape(1, -1), wuk_t)


def _t5_bucket(dist):
    n = jnp.maximum(dist, 0)
    exact = REL_BUCKETS // 2
    nf = jnp.maximum(n, 1).astype(F32)
    large = exact + (jnp.log(nf / exact) / math.log(REL_MAX_DIST / exact)
                     * (REL_BUCKETS - exact)).astype(I32)
    large = jnp.minimum(large, REL_BUCKETS - 1)
    return jnp.where(n < exact, n, large)


def _bias_tables(rel_bias):
    t = jnp.arange(QBLK, dtype=I32)[:, None]
    s = jnp.arange(QBLK, dtype=I32)[None, :]
    far = rel_bias[REL_BUCKETS - 1]
    diag = rel_bias[_t5_bucket(t - s)] - far
    prev = rel_bias[_t5_bucket(t - s + QBLK)] - far
    diag = jnp.transpose(diag, (2, 0, 1))
    prev = jnp.transpose(prev, (2, 0, 1))
    zero = jnp.zeros_like(diag)
    cat = lambda a, b: jnp.concatenate([a, b], axis=-1)
    return jnp.stack([cat(zero, zero), cat(zero, prev), cat(prev, diag), cat(diag, zero)])


def _attn_kernel(qa_ref, qi_ref, wi_ref, ckv_ref, ki_ref, bt_ref, o_ref, kbuf, m_scr, l_scr, acc_scr):
    i = pl.program_id(1)
    nt = i // 2 + 1
    t_row = i * QBLK + lax.broadcasted_iota(I32, (QBLK, KTILE), 0)
    lane = lax.broadcasted_iota(I32, (QBLK, KTILE), 1)

    qi = qi_ref[0]
    wcol = wi_ref[0]

    def score_body(j, carry):
        k0 = pl.multiple_of(j * KTILE, KTILE)
        kt = ki_ref[0, pl.ds(k0, KTILE), :]
        p = lax.dot_general(qi, kt, (((1,), (1,)), ((), ())), preferred_element_type=F32)
        p = jnp.maximum(p, 0.0) * wcol
        sc = p[0:QBLK]
        for hh in range(1, B_IDX_HEADS):
            sc = sc + p[hh * QBLK:(hh + 1) * QBLK]
        bits = pltpu.bitcast(sc + 0.0, I32)
        key = jnp.where(bits < 0, bits ^ jnp.int32(0x7FFFFFFF), bits)
        key = jnp.where(k0 + lane <= t_row, key, jnp.int32(INT_MIN))
        kbuf[:, pl.ds(k0, KTILE)] = key
        return carry

    lax.fori_loop(0, nt, score_body, 0)

    def count_ge(cand):
        def body(j, acc):
            k0 = pl.multiple_of(j * KTILE, KTILE)
            return acc + jnp.where(kbuf[:, pl.ds(k0, KTILE)] >= cand, 1, 0)
        acc = lax.fori_loop(0, nt, body, jnp.zeros((QBLK, KTILE), I32))
        return jnp.sum(acc, axis=-1, keepdims=True)

    def bit_body(b, lo):
        cand = lo + lax.shift_left(jnp.int32(1), 31 - b)
        return jnp.where(count_ge(cand) >= B_TOPK_MAX, cand, lo)

    tau = lax.fori_loop(0, 32, bit_body, jnp.full((QBLK, 1), INT_MIN, I32))
    tau = jnp.maximum(tau, jnp.int32(INT_MIN + 1))
    n_ge = count_ge(tau)
    n_gt = count_ge(tau + 1)
    excess = jnp.max(n_ge) > B_TOPK_MAX

    @pl.when(excess)
    def _():
        need = B_TOPK_MAX - n_gt

        def count_eq_before(pos):
            def body(j, acc):
                k0 = pl.multiple_of(j * KTILE, KTILE)
                hit = (kbuf[:, pl.ds(k0, KTILE)] == tau) & (k0 + lane < pos)
                return acc + jnp.where(hit, 1, 0)
            acc = lax.fori_loop(0, nt, body, jnp.zeros((QBLK, KTILE), I32))
            return jnp.sum(acc, axis=-1, keepdims=True)

        def pos_body(b, pos):
            cand = pos + lax.shift_left(jnp.int32(1), 12 - b)
            return jnp.where(count_eq_before(cand) < need, cand, pos)

        pos = lax.fori_loop(0, 13, pos_body, jnp.zeros((QBLK, 1), I32))

        def drop_body(j, carry):
            k0 = pl.multiple_of(j * KTILE, KTILE)
            kk = kbuf[:, pl.ds(k0, KTILE)]
            drop = (kk == tau) & (k0 + lane > pos) & (n_ge > B_TOPK_MAX)
            kbuf[:, pl.ds(k0, KTILE)] = jnp.where(drop, jnp.int32(INT_MIN), kk)
            return carry

        lax.fori_loop(0, nt, drop_body, 0)

    qa = qa_ref[0]
    m_scr[...] = jnp.full_like(m_scr, NEG_BIG)
    l_scr[...] = jnp.zeros_like(l_scr)
    acc_scr[...] = jnp.zeros_like(acc_scr)

    def attend(j, bias_idx):
        k0 = pl.multiple_of(j * KTILE, KTILE)
        kv = ckv_ref[0, pl.ds(k0, KTILE), :]
        s = lax.dot_general(qa, kv, (((1,), (1,)), ((), ())), preferred_element_type=F32)
        s = s.reshape(B_HEADS, QBLK, KTILE)
        if bias_idx is not None:
            s = s + bt_ref[bias_idx]
        keep = (kbuf[:, pl.ds(k0, KTILE)] >= tau)[None]
        s = jnp.where(keep, s, NEG_BIG)
        m_old = m_scr[...]
        m_new = jnp.maximum(m_old, jnp.max(s, axis=-1, keepdims=True))
        p = jnp.where(keep, jnp.exp(s - m_new), 0.0)
        alpha = jnp.exp(m_old - m_new)
        l_scr[...] = alpha * l_scr[...] + jnp.sum(p, axis=-1, keepdims=True)
        pv = jnp.dot(p.reshape(B_HEADS * QBLK, KTILE).astype(BF16), kv, preferred_element_type=F32)
        acc_scr[...] = alpha * acc_scr[...] + pv.reshape(B_HEADS, QBLK, B_KV_LATENT)
        m_scr[...] = m_new

    def far_body(j, carry):
        attend(j, None)
        return carry

    lax.fori_loop(0, jnp.maximum(nt - 2, 0), far_body, 0)
    odd = i % 2

    @pl.when(nt >= 2)
    def _():
        attend(nt - 2, 1 - odd)

    attend(nt - 1, 3 - odd)
    o = acc_scr[...] / l_scr[...]
    o_ref[0] = o.reshape(B_HEADS * QBLK, B_KV_LATENT).astype(BF16)


def _attention(qa, qi, wi, ckv, ki, btab, bsz, seq):
    nqb = seq // QBLK
    gq = lambda b, i: (b * nqb + i, 0, 0)
    gb = lambda b, i: (b, 0, 0)
    return pl.pallas_call(
        _attn_kernel,
        out_shape=jax.ShapeDtypeStruct(qa.shape, BF16),
        grid=(bsz, nqb),
        in_specs=[
            pl.BlockSpec((1, B_HEADS * QBLK, B_KV_LATENT), gq),
            pl.BlockSpec((1, B_IDX_HEADS * QBLK, B_IDX_DIM), gq),
            pl.BlockSpec((1, B_IDX_HEADS * QBLK, 1), gq),
            pl.BlockSpec((1, seq, B_KV_LATENT), gb),
            pl.BlockSpec((1, seq, B_IDX_DIM), gb),
            pl.BlockSpec((4, B_HEADS, QBLK, KTILE), lambda b, i: (0, 0, 0, 0), pipeline_mode=pl.Buffered(1)),
        ],
        out_specs=pl.BlockSpec((1, B_HEADS * QBLK, B_KV_LATENT), gq),
        scratch_shapes=[
            pltpu.VMEM((QBLK, seq), I32),
            pltpu.VMEM((B_HEADS, QBLK, 1), F32),
            pltpu.VMEM((B_HEADS, QBLK, 1), F32),
            pltpu.VMEM((B_HEADS, QBLK, B_KV_LATENT), F32),
        ],
        compiler_params=_cparams(("arbitrary", "arbitrary")),
        name="dsa_attention",
    )(qa, qi, wi, ckv.reshape(bsz, seq, -1), ki.reshape(bsz, seq, -1), btab)


def _bout_kernel(h_ref, ol_ref, mod_ref, wuv_ref, wout_ref, n2g_ref, rw_ref, h_out, hn_out, lg_out, o_scr):
    mod = mod_ref[0]
    for blk in range(ol_ref.shape[0]):
        for hh in range(B_HEADS):
            oh = jnp.dot(ol_ref[blk, hh * QBLK:(hh + 1) * QBLK, :], wuv_ref[hh], preferred_element_type=F32)
            o_scr[blk * QBLK:(blk + 1) * QBLK, hh * B_V_DIM:(hh + 1) * B_V_DIM] = oh.astype(BF16)
    y = jnp.dot(o_scr[...], wout_ref[...], preferred_element_type=F32)
    _residual_epilogue(h_ref[...], y, mod, n2g_ref[...], rw_ref[...], h_out, hn_out, lg_out)


def _bout(h, o_lat, mod_l, w_uv, w_out, n2g, router_w, seq):
    n, d = h.shape
    tm = ROW_TILE
    per_b = seq // tm
    qpb = tm // QBLK
    ne = router_w.shape[1]
    wuv_t = jnp.transpose(w_uv, (1, 0, 2)).astype(BF16)
    row = lambda i: (i, 0)
    return pl.pallas_call(
        _bout_kernel,
        out_shape=(jax.ShapeDtypeStruct((n, d), F32), jax.ShapeDtypeStruct((n, d), BF16),
                   jax.ShapeDtypeStruct((n, ne), F32)),
        grid=(n // tm,),
        in_specs=[
            pl.BlockSpec((tm, d), row),
            pl.BlockSpec((qpb, B_HEADS * QBLK, B_KV_LATENT), lambda i: (i, 0, 0)),
            pl.BlockSpec((1, 6, d), lambda i: (i // per_b, 0, 0)),
            _const_spec((B_HEADS, B_KV_LATENT, B_V_DIM)),
            _const_spec((B_HEADS * B_V_DIM, d)),
            _const_spec((1, d)),
            _const_spec((d, ne)),
        ],
        out_specs=(pl.BlockSpec((tm, d), row), pl.BlockSpec((tm, d), row), pl.BlockSpec((tm, ne), row)),
        scratch_shapes=[pltpu.VMEM((tm, B_HEADS * B_V_DIM), BF16)],
        compiler_params=_cparams(("arbitrary",)),
        name="dsa_out",
    )(h, o_lat, mod_l, wuv_t, w_out.astype(BF16), n2g.reshape(1, d), router_w)


def _final_kernel(h1_ref, y0_ref, y1_ref, gt_ref, mod_ref, g_ref, o_ref):
    h = _moe_combine(h1_ref[...], y0_ref[...], y1_ref[...], gt_ref[...], mod_ref[0])
    o_ref[...] = h * lax.rsqrt(jnp.mean(h * h, axis=-1, keepdims=True) + EPS) * g_ref[...]


def _final(h1, y0, y1, gates, mod_l, final_g, seq):
    n, d = h1.shape
    tm = 512
    per_b = seq // tm
    row = lambda i: (i, 0)
    return pl.pallas_call(
        _final_kernel,
        out_shape=jax.ShapeDtypeStruct((n, d), F32),
        grid=(n // tm,),
        in_specs=[pl.BlockSpec((tm, d), row), pl.BlockSpec((tm, d), row), pl.BlockSpec((tm, d), row),
                  pl.BlockSpec((tm, TOP_K), row), pl.BlockSpec((1, 6, d), lambda i: (i // per_b, 0, 0)),
                  _const_spec((1, d))],
        out_specs=pl.BlockSpec((tm, d), row),
        compiler_params=_cparams(("arbitrary",)),
        name="final_norm",
    )(h1, y0, y1, gates, mod_l, final_g.reshape(1, d))


def kernel(x, c, ada_w, ada_b, norm1_g, norm2_g, a_w_in, a_ln_g, a_ln_b, a_w_sp, a_b_sp, a_w_out, b_w_in,
           b_kv_norm_g, b_w_uk, b_w_uv, b_kidx_g, b_w_out, rel_bias, router_w, router_b, moe_w_gate,
           moe_w_up, moe_w_down, final_g):
    bsz, seq, d = x.shape
    n = bsz * seq
    mod = _adaln(c, ada_w, ada_b).reshape(ada_w.shape[0], bsz, 6, d)
    h = x.reshape(n, d)

    h1, hn2, logits = _gmlp_layer(h, mod[0], norm1_g[0], a_w_in[0], a_ln_g[0], a_ln_b[0], a_w_sp[0],
                                  a_b_sp[0], a_w_out[0], norm2_g[0], router_w, seq)
    y0, y1, gates = _moe(hn2, logits, router_b, moe_w_gate[0], moe_w_up[0], moe_w_down[0])

    h, qa, ckv, qi, ki, wi = _bproj(h1, y0, y1, gates, mod[0], mod[1], norm1_g[1], b_w_in[0],
                                    b_kv_norm_g[0], b_w_uk[0], b_kidx_g[0], seq)
    o_lat = _attention(qa, qi, wi, ckv, ki, _bias_tables(rel_bias), bsz, seq)
    h1, hn2, logits = _bout(h, o_lat, mod[1], b_w_uv[0], b_w_out[0], norm2_g[1], router_w, seq)
    y0, y1, gates = _moe(hn2, logits, router_b, moe_w_gate[1], moe_w_up[1], moe_w_down[1])

    out = _final(h1, y0, y1, gates, mod[1], final_g, seq)
    return out.reshape(bsz, seq, d)
```

```python
import functools
import math

import jax
import jax.numpy as jnp
from jax import lax
from jax.experimental import pallas as pl
from jax.experimental.pallas import tpu as pltpu

F32 = jnp.float32
BF16 = jnp.bfloat16
I32 = jnp.int32
HIGHEST = lax.Precision.HIGHEST

EPS = 1e-6
A_CHUNK = 128
A_GROUPS = 8
B_HEADS = 16
B_HEAD_DIM = 64
B_V_DIM = 64
B_KV_LATENT = 256
B_IDX_HEADS = 8
B_IDX_DIM = 64
B_TOPK_MAX = 256
QBLK = 128
KTILE = 256
REL_BUCKETS = 32
REL_MAX_DIST = 128
N_EXPERTS = 16
N_GROUPS = 4
EXPERTS_PER_GROUP = 4
TOP_K = 2
MOE_ROWS = 1024
MOE_SUB = 256
MOE_FT = 512

ROW_TILE = 256
VMEM_LIMIT = 56 * 1024 * 1024

INT_MIN = -2 ** 31
NEG_BIG = -1e30


def _cparams(sem):
    return pltpu.CompilerParams(dimension_semantics=sem, vmem_limit_bytes=VMEM_LIMIT)


def _mod_rmsnorm(h, g, scale, shift):
    ms = jnp.mean(h * h, axis=-1, keepdims=True)
    return (h * lax.rsqrt(ms + EPS) * g) * (1.0 + scale) + shift


def _gelu_tanh(x):
    c = math.sqrt(2.0 / math.pi)
    return 0.5 * x * (1.0 + jnp.tanh(c * (x + 0.044715 * (x * x * x))))


def _adaln_kernel(c_ref, w_ref, b_ref, o_ref):
    c = c_ref[...]
    sc = c * jax.nn.sigmoid(c)
    o_ref[0] = jnp.dot(sc, w_ref[0], precision=HIGHEST, preferred_element_type=F32) + b_ref[0]


def _adaln(c, ada_w, ada_b):
    depth, d, e = ada_w.shape
    bsz = c.shape[0]
    bp = 8
    c_pad = jnp.zeros((bp, d), F32).at[:bsz].set(c)
    tn = 1024
    out = pl.pallas_call(
        _adaln_kernel,
        out_shape=jax.ShapeDtypeStruct((depth, bp, e), F32),
        grid=(depth, e // tn),
        in_specs=[
            pl.BlockSpec((bp, d), lambda l, j: (0, 0)),
            pl.BlockSpec((1, d, tn), lambda l, j: (l, 0, j)),
            pl.BlockSpec((1, 1, tn), lambda l, j: (l, 0, j)),
        ],
        out_specs=pl.BlockSpec((1, bp, tn), lambda l, j: (l, 0, j)),
        compiler_params=_cparams(("arbitrary", "arbitrary")),
        name="adaln",
    )(c_pad, ada_w, ada_b.reshape(depth, 1, e))
    return out[:, :bsz]


def _residual_epilogue(h, y, mod, n2g, rw, h_out, hn_out, lg_out):
    h1 = h + mod[2:3] * y
    h_out[...] = h1
    hn2 = _mod_rmsnorm(h1, n2g, mod[4:5], mod[3:4])
    hn_out[...] = hn2
    lg_out[...] = jnp.dot(hn2, rw, precision=HIGHEST, preferred_element_type=F32)


def _gmlp_kernel(h_ref, mod_ref, n1g_ref, win_ref, lng_ref, lnb_ref, wsp_ref, bsp_ref, wout_ref,
                 n2g_ref, rw_ref, h_out, hn_out, lg_out, u_scr, v_scr, s_scr):
    tm = h_ref.shape[0]
    inner = u_scr.shape[1]
    gw = inner // A_GROUPS
    tn = 512
    h = h_ref[...]
    mod = mod_ref[0]
    hn = _mod_rmsnorm(h, n1g_ref[...], mod[1:2], mod[0:1]).astype(BF16)
    for j in range(2 * inner // tn):
        z = _gelu_tanh(jnp.dot(hn, win_ref[:, j * tn:(j + 1) * tn], preferred_element_type=F32))
        if j < inner // tn:
            u_scr[:, j * tn:(j + 1) * tn] = z
        else:
            jj = j - inner // tn
            v_scr[:, jj * tn:(jj + 1) * tn] = z
    vsum = jnp.zeros((tm, 1), F32)
    for j in range(inner // tn):
        vsum = vsum + jnp.sum(v_scr[:, j * tn:(j + 1) * tn], axis=-1, keepdims=True)
    mu = vsum * (1.0 / inner)
    vsq = jnp.zeros((tm, 1), F32)
    for j in range(inner // tn):
        d = v_scr[:, j * tn:(j + 1) * tn] - mu
        vsq = vsq + jnp.sum(d * d, axis=-1, keepdims=True)
    rstd = lax.rsqrt(vsq * (1.0 / inner) + EPS)
    row = lax.broadcasted_iota(I32, (A_CHUNK, A_CHUNK), 0)
    col = lax.broadcasted_iota(I32, (A_CHUNK, A_CHUNK), 1)
    tril = row >= col
    for g in range(A_GROUPS):
        ws = jnp.where(tril, wsp_ref[g], 0.0).astype(BF16)
        bcol = bsp_ref[:, g:g + 1]
        lg = lng_ref[:, g * gw:(g + 1) * gw]
        lb = lnb_ref[:, g * gw:(g + 1) * gw]
        for c in range(tm // A_CHUNK):
            r0, r1 = c * A_CHUNK, (c + 1) * A_CHUNK
            vt = v_scr[r0:r1, g * gw:(g + 1) * gw]
            vn = ((vt - mu[r0:r1]) * rstd[r0:r1]) * lg + lb
            fv = jnp.dot(ws, vn.astype(BF16), preferred_element_type=F32) + bcol
            s_scr[r0:r1, g * gw:(g + 1) * gw] = (u_scr[r0:r1, g * gw:(g + 1) * gw] * fv).astype(BF16)
    y = jnp.dot(s_scr[...], wout_ref[...], preferred_element_type=F32)
    _residual_epilogue(h, y, mod, n2g_ref[...], rw_ref[...], h_out, hn_out, lg_out)


def _const_spec(shape):
    nd = len(shape)
    return pl.BlockSpec(shape, lambda i, _nd=nd: (0,) * _nd, pipeline_mode=pl.Buffered(1))


def _gmlp_layer(h, mod_l, n1g, w_in, ln_g, ln_b, w_sp, b_sp, w_out, n2g, router_w, seq):
    n, d = h.shape
    inner = w_out.shape[0]
    tm = ROW_TILE
    per_b = seq // tm
    ne = router_w.shape[1]
    return pl.pallas_call(
        _gmlp_kernel,
        out_shape=(jax.ShapeDtypeStruct((n, d), F32), jax.ShapeDtypeStruct((n, d), F32),
                   jax.ShapeDtypeStruct((n, ne), F32)),
        grid=(n // tm,),
        in_specs=[
            pl.BlockSpec((tm, d), lambda i: (i, 0)),
            pl.BlockSpec((1, 6, d), lambda i: (i // per_b, 0, 0)),
            _const_spec((1, d)),
            _const_spec((d, 2 * inner)),
            _const_spec((1, inner)),
            _const_spec((1, inner)),
            _const_spec((A_GROUPS, A_CHUNK, A_CHUNK)),
            _const_spec((A_CHUNK, A_GROUPS)),
            _const_spec((inner, d)),
            _const_spec((1, d)),
            _const_spec((d, ne)),
        ],
        out_specs=(pl.BlockSpec((tm, d), lambda i: (i, 0)), pl.BlockSpec((tm, d), lambda i: (i, 0)),
                   pl.BlockSpec((tm, ne), lambda i: (i, 0))),
        scratch_shapes=[pltpu.VMEM((tm, inner), F32), pltpu.VMEM((tm, inner), F32),
                        pltpu.VMEM((tm, inner), BF16)],
        compiler_params=_cparams(("arbitrary",)),
        name="gmlp_layer",
    )(h, mod_l, n1g.reshape(1, d), w_in.astype(BF16), ln_g.reshape(1, inner), ln_b.reshape(1, inner),
      w_sp, b_sp.T, w_out.astype(BF16), n2g.reshape(1, d), router_w)


def _first_max4(rows):
    m = jnp.maximum(jnp.maximum(rows[0], rows[1]), jnp.maximum(rows[2], rows[3]))
    idx = jnp.where(rows[0] == m, 0, jnp.where(rows[1] == m, 1, jnp.where(rows[2] == m, 2, 3)))
    return m, idx.astype(I32)


def _route_kernel(lt_ref, rb_ref, e_ref, g_ref):
    aff = jax.nn.sigmoid(lt_ref[...])
    sel = aff + rb_ref[...]
    neg = jnp.float32(-jnp.inf)
    g_score, g_i1, g_i2 = [], [], []
    for g in range(N_GROUPS):
        rows = [sel[4 * g + k:4 * g + k + 1, :] for k in range(EXPERTS_PER_GROUP)]
        m1, i1 = _first_max4(rows)
        rest = [jnp.where(i1 == k, neg, rows[k]) for k in range(EXPERTS_PER_GROUP)]
        m2, i2 = _first_max4(rest)
        g_score.append(m1 + m2)
        g_i1.append(i1)
        g_i2.append(i2)
    _, grp = _first_max4(g_score)
    l1 = jnp.zeros_like(grp)
    l2 = jnp.zeros_like(grp)
    for g in range(N_GROUPS):
        l1 = jnp.where(grp == g, g_i1[g], l1)
        l2 = jnp.where(grp == g, g_i2[g], l2)
    e1 = grp * EXPERTS_PER_GROUP + l1
    e2 = grp * EXPERTS_PER_GROUP + l2
    a1 = jnp.zeros_like(g_score[0])
    a2 = jnp.zeros_like(g_score[0])
    for e in range(N_EXPERTS):
        ae = aff[e:e + 1, :]
        a1 = jnp.where(e1 == e, ae, a1)
        a2 = jnp.where(e2 == e, ae, a2)
    tot = a1 + a2
    e_ref[0:1, :] = e1
    e_ref[1:2, :] = e2
    g_ref[0:1, :] = a1 / tot
    g_ref[1:2, :] = a2 / tot


def _route(logits, router_b):
    n, ne = logits.shape
    tn = min(2048, n)
    return pl.pallas_call(
        _route_kernel,
        out_shape=(jax.ShapeDtypeStruct((TOP_K, n), I32), jax.ShapeDtypeStruct((TOP_K, n), F32)),
        grid=(n // tn,),
        in_specs=[pl.BlockSpec((ne, tn), lambda i: (0, i)), pl.BlockSpec((ne, 1), lambda i: (0, 0))],
        out_specs=(pl.BlockSpec((TOP_K, tn), lambda i: (0, i)), pl.BlockSpec((TOP_K, tn), lambda i: (0, i))),
        compiler_params=_cparams(("arbitrary",)),
        name="route",
    )(logits.T, router_b.reshape(ne, 1))


def _slot_tables(experts):
    n = experts.shape[1]
    a = n * TOP_K
    e_flat = experts.T.reshape(a)
    order = jnp.argsort(e_flat).astype(I32)
    e_sorted = e_flat[order]
    counts = jnp.bincount(e_flat, length=N_EXPERTS).astype(I32)
    padded = ((counts + MOE_ROWS - 1) // MOE_ROWS) * MOE_ROWS
    pad_end = jnp.cumsum(padded)
    pad_start = pad_end - padded
    seg_start = jnp.cumsum(counts) - counts
    rank = jnp.arange(a, dtype=I32) - seg_start[e_sorted]
    slot_sorted = (pad_start[e_sorted] + rank).astype(I32)
    p = a + N_EXPERTS * MOE_ROWS
    n_items = p // MOE_ROWS
    tok_of_slot = jnp.zeros((p,), I32).at[slot_sorted].set(order // TOP_K)
    slot_of_assign = jnp.zeros((a,), I32).at[order].set(slot_sorted)
    item_start = jnp.arange(n_items, dtype=I32) * MOE_ROWS
    n_used = pad_end[-1] // MOE_ROWS
    item_e_raw = jnp.minimum(jnp.searchsorted(pad_end, item_start, side='right'), N_EXPERTS - 1).astype(I32)
    used = item_start < pad_end[-1]
    last_e = item_e_raw[jnp.maximum(n_used - 1, 0)]
    item_e = jnp.where(used, item_e_raw, last_e).astype(I32)
    valid = jnp.clip(counts[item_e_raw] - (item_start - pad_start[item_e_raw]), 0, MOE_ROWS)
    item_rows = jnp.where(used, valid, 0).astype(I32)
    item_blk = jnp.where(used, jnp.arange(n_items, dtype=I32), jnp.maximum(n_used - 1, 0)).astype(I32)
    return tok_of_slot, slot_of_assign.reshape(n, TOP_K), item_e, item_rows, item_blk


def _moe_kernel(ie_ref, ir_ref, ib_ref, x_ref, wg_ref, wu_ref, wd_ref, o_ref, wg_s, wu_s, wd_s):
    it = pl.program_id(0)
    f = pl.program_id(1)
    nrows = ir_ref[it]

    @pl.when(f == 0)
    def _():
        o_ref[...] = jnp.zeros_like(o_ref)

    @pl.when(nrows > 0)
    def _():
        wg_s[...] = wg_ref[0, 0].astype(BF16)
        wu_s[...] = wu_ref[0, 0].astype(BF16)
        wd_s[...] = wd_ref[0, 0].astype(BF16)
        nsb = (nrows + MOE_SUB - 1) // MOE_SUB

        def body(sb, carry):
            r0 = pl.multiple_of(sb * MOE_SUB, MOE_SUB)
            x = x_ref[pl.ds(r0, MOE_SUB), :].astype(BF16)
            g = jnp.dot(x, wg_s[...], preferred_element_type=F32)
            u = jnp.dot(x, wu_s[...], preferred_element_type=F32)
            hmid = ((g * jax.nn.sigmoid(g)) * u).astype(BF16)
            o_ref[pl.ds(r0, MOE_SUB), :] += jnp.dot(hmid, wd_s[...], preferred_element_type=F32)
            return carry

        lax.fori_loop(0, nsb, body, 0)


def _moe_experts(x_sorted, item_e, item_rows, item_blk, w_gate, w_up, w_down, layer):
    p, d = x_sorted.shape
    n_items = p // MOE_ROWS
    dff = w_gate.shape[3]
    nf = dff // MOE_FT
    grid_spec = pltpu.PrefetchScalarGridSpec(
        num_scalar_prefetch=3,
        grid=(n_items, nf),
        in_specs=[
            pl.BlockSpec((MOE_ROWS, d), lambda i, f, ie, ir, ib: (ib[i], 0)),
            pl.BlockSpec((1, 1, d, MOE_FT),
                         lambda i, f, ie, ir, ib: (layer, ie[i], 0, jnp.where(ir[i] > 0, f, nf - 1))),
            pl.BlockSpec((1, 1, d, MOE_FT),
                         lambda i, f, ie, ir, ib: (layer, ie[i], 0, jnp.where(ir[i] > 0, f, nf - 1))),
            pl.BlockSpec((1, 1, MOE_FT, d),
                         lambda i, f, ie, ir, ib: (layer, ie[i], jnp.where(ir[i] > 0, f, nf - 1), 0)),
        ],
        out_specs=pl.BlockSpec((MOE_ROWS, d), lambda i, f, ie, ir, ib: (i, 0)),
        scratch_shapes=[pltpu.VMEM((d, MOE_FT), BF16), pltpu.VMEM((d, MOE_FT), BF16),
                        pltpu.VMEM((MOE_FT, d), BF16)],
    )
    return pl.pallas_call(
        _moe_kernel,
        out_shape=jax.ShapeDtypeStruct((p, d), F32),
        grid_spec=grid_spec,
        compiler_params=_cparams(("arbitrary", "arbitrary")),
        name="moe_experts",
    )(item_e, item_rows, item_blk, x_sorted, w_gate, w_up, w_down)


def _moe(hn2, logits, router_b, w_gate, w_up, w_down, layer):
    experts, gates = _route(logits, router_b)
    tok_of_slot, slot_of_assign, item_e, item_rows, item_blk = _slot_tables(experts)
    x_sorted = jnp.take(hn2, tok_of_slot, axis=0)
    y_slot = _moe_experts(x_sorted, item_e, item_rows, item_blk, w_gate, w_up, w_down, layer)
    y0 = jnp.take(y_slot, slot_of_assign[:, 0], axis=0)
    y1 = jnp.take(y_slot, slot_of_assign[:, 1], axis=0)
    return y0, y1, gates.T


def _moe_combine(h1, y0, y1, gates, mod):
    return h1 + mod[5:6] * (gates[:, 0:1] * y0 + gates[:, 1:2] * y1)


def _bproj_kernel(h1_ref, y0_ref, y1_ref, gt_ref, modp_ref, mod_ref, n1g_ref, w_ref, kvg_ref, kig_ref,
                  wuk_ref, h_out, qa_out, ckv_out, qi_out, ki_out, wi_out):
    tm = h1_ref.shape[0]
    h = _moe_combine(h1_ref[...], y0_ref[...], y1_ref[...], gt_ref[...], modp_ref[0])
    h_out[...] = h
    mod = mod_ref[0]
    hn = _mod_rmsnorm(h, n1g_ref[...], mod[1:2], mod[0:1]).astype(BF16)
    proj = jnp.dot(hn, w_ref[...], preferred_element_type=F32)
    o1 = B_HEADS * B_HEAD_DIM
    o2 = o1 + B_KV_LATENT
    o3 = o2 + B_IDX_HEADS * B_IDX_DIM
    ckv = proj[:, o1:o2]
    ckv = ckv * lax.rsqrt(jnp.mean(ckv * ckv, axis=-1, keepdims=True) + EPS) * kvg_ref[...]
    ckv_out[...] = ckv.astype(BF16)
    tail = proj[:, o3:o3 + 128]
    ki = tail[:, :B_IDX_DIM]
    ki = ki * lax.rsqrt(jnp.mean(ki * ki, axis=-1, keepdims=True) + EPS) * kig_ref[...]
    ki_out[...] = ki.astype(BF16)
    wi = tail[:, B_IDX_DIM:B_IDX_DIM + B_IDX_HEADS] * (B_IDX_HEADS ** -0.5 * B_IDX_DIM ** -0.5)
    scale = B_HEAD_DIM ** -0.5
    for blk in range(tm // QBLK):
        r0, r1 = blk * QBLK, (blk + 1) * QBLK
        for hh in range(B_HEADS):
            qh = proj[r0:r1, hh * B_HEAD_DIM:(hh + 1) * B_HEAD_DIM].astype(BF16)
            qa = jnp.dot(qh, wuk_ref[hh], preferred_element_type=F32) * scale
            qa_out[blk, hh * QBLK:(hh + 1) * QBLK, :] = qa.astype(BF16)
        for hh in range(B_IDX_HEADS):
            qi_out[blk, hh * QBLK:(hh + 1) * QBLK, :] = proj[r0:r1, o2 + hh * B_IDX_DIM:
                                                            o2 + (hh + 1) * B_IDX_DIM].astype(BF16)
            wi_out[blk, hh * QBLK:(hh + 1) * QBLK, :] = wi[r0:r1, hh:hh + 1]


def _bproj(h1, y0, y1, gates, mod_prev, mod_l, n1g, b_w_in, kv_g, w_uk, kidx_g, seq):
    n, d = h1.shape
    tm = ROW_TILE
    per_b = seq // tm
    nq = n // QBLK
    qpb = tm // QBLK
    o1 = B_HEADS * B_HEAD_DIM
    o2 = o1 + B_KV_LATENT
    o3 = o2 + B_IDX_HEADS * B_IDX_DIM
    wcat = jnp.zeros((d, o3 + 128), F32).at[:, :b_w_in.shape[1]].set(b_w_in).astype(BF16)
    wuk_t = jnp.transpose(w_uk, (1, 2, 0)).astype(BF16)
    row = lambda i: (i, 0)
    modm = lambda i: (i // per_b, 0, 0)
    blk3 = lambda i: (i, 0, 0)
    return pl.pallas_call(
        _bproj_kernel,
        out_shape=(
            jax.ShapeDtypeStruct((n, d), F32),
            jax.ShapeDtypeStruct((nq, B_HEADS * QBLK, B_KV_LATENT), BF16),
            jax.ShapeDtypeStruct((n, B_KV_LATENT), BF16),
            jax.ShapeDtypeStruct((nq, B_IDX_HEADS * QBLK, B_IDX_DIM), BF16),
            jax.ShapeDtypeStruct((n, B_IDX_DIM), BF16),
            jax.ShapeDtypeStruct((nq, B_IDX_HEADS * QBLK, 1), F32),
        ),
        grid=(n // tm,),
        in_specs=[
            pl.BlockSpec((tm, d), row), pl.BlockSpec((tm, d), row), pl.BlockSpec((tm, d), row),
            pl.BlockSpec((tm, TOP_K), row),
            pl.BlockSpec((1, 6, d), modm), pl.BlockSpec((1, 6, d), modm),
            _const_spec((1, d)),
            _const_spec((d, o3 + 128)),
            _const_spec((1, B_KV_LATENT)),
            _const_spec((1, B_IDX_DIM)),
            _const_spec((B_HEADS, B_HEAD_DIM, B_KV_LATENT)),
        ],
        out_specs=(
            pl.BlockSpec((tm, d), row),
            pl.BlockSpec((qpb, B_HEADS * QBLK, B_KV_LATENT), blk3),
            pl.BlockSpec((tm, B_KV_LATENT), row),
            pl.BlockSpec((qpb, B_IDX_HEADS * QBLK, B_IDX_DIM), blk3),
            pl.BlockSpec((tm, B_IDX_DIM), row),
            pl.BlockSpec((qpb, B_IDX_HEADS * QBLK, 1), blk3),
        ),
        compiler_params=_cparams(("arbitrary",)),
        name="dsa_proj",
    )(h1, y0, y1, gates, mod_prev, mod_l, n1g.reshape(1, d), wcat, kv_g.reshape(1, -1),
      kidx_g.reshape(1, -1), wuk_t)


def _t5_bucket(dist):
    n = jnp.maximum(dist, 0)
    exact = REL_BUCKETS // 2
    nf = jnp.maximum(n, 1).astype(F32)
    large = exact + (jnp.log(nf / exact) / math.log(REL_MAX_DIST / exact)
                     * (REL_BUCKETS - exact)).astype(I32)
    large = jnp.minimum(large, REL_BUCKETS - 1)
    return jnp.where(n < exact, n, large)


def _bias_tables(rel_bias):
    t = jnp.arange(QBLK, dtype=I32)[:, None]
    s = jnp.arange(QBLK, dtype=I32)[None, :]
    far = rel_bias[REL_BUCKETS - 1]
    diag = rel_bias[_t5_bucket(t - s)] - far
    prev = rel_bias[_t5_bucket(t - s + QBLK)] - far
    diag = jnp.transpose(diag, (2, 0, 1))
    prev = jnp.transpose(prev, (2, 0, 1))
    zero = jnp.zeros_like(diag)
    cat = lambda a, b: jnp.concatenate([a, b], axis=-1)
    return jnp.stack([cat(zero, zero), cat(zero, prev), cat(prev, diag), cat(diag, zero)])


def _attn_kernel(qa_ref, qi_ref, wi_ref, ckv_ref, ki_ref, bt_ref, o_ref, kbuf, m_scr, l_scr, a_scr, acc_scr,
                 s_scr, p_scr):
    i = pl.program_id(1)
    nt = i // 2 + 1
    t_row = i * QBLK + lax.broadcasted_iota(I32, (QBLK, KTILE), 0)
    lane = lax.broadcasted_iota(I32, (QBLK, KTILE), 1)

    qi = qi_ref[0]
    wcol = wi_ref[0]

    def score_body(j, carry):
        k0 = pl.multiple_of(j * KTILE, KTILE)
        kt = ki_ref[0, pl.ds(k0, KTILE), :]
        p = lax.dot_general(qi, kt, (((1,), (1,)), ((), ())), preferred_element_type=F32)
        p = jnp.maximum(p, 0.0) * wcol
        sc = p[0:QBLK]
        for hh in range(1, B_IDX_HEADS):
            sc = sc + p[hh * QBLK:(hh + 1) * QBLK]
        bits = pltpu.bitcast(sc + 0.0, I32)
        key = jnp.where(bits < 0, bits ^ jnp.int32(0x7FFFFFFF), bits)
        key = jnp.where(k0 + lane <= t_row, key, jnp.int32(INT_MIN))
        kbuf[:, pl.ds(k0, KTILE)] = key
        return carry

    lax.fori_loop(0, nt, score_body, 0)

    def count_ge(cand):
        def body(j, acc):
            k0 = pl.multiple_of(j * KTILE, KTILE)
            return acc + jnp.where(kbuf[:, pl.ds(k0, KTILE)] >= cand, 1, 0)
        acc = lax.fori_loop(0, nt, body, jnp.zeros((QBLK, KTILE), I32))
        return jnp.sum(acc, axis=-1, keepdims=True)

    def bit_body(b, lo):
        cand = lo + lax.shift_left(jnp.int32(1), 31 - b)
        return jnp.where(count_ge(cand) >= B_TOPK_MAX, cand, lo)

    tau = lax.fori_loop(0, 32, bit_body, jnp.full((QBLK, 1), INT_MIN, I32))
    tau = jnp.maximum(tau, jnp.int32(INT_MIN + 1))
    n_ge = count_ge(tau)
    n_gt = count_ge(tau + 1)
    excess = jnp.max(n_ge) > B_TOPK_MAX

    @pl.when(excess)
    def _():
        need = B_TOPK_MAX - n_gt

        def count_eq_before(pos):
            def body(j, acc):
                k0 = pl.multiple_of(j * KTILE, KTILE)
                hit = (kbuf[:, pl.ds(k0, KTILE)] == tau) & (k0 + lane < pos)
                return acc + jnp.where(hit, 1, 0)
            acc = lax.fori_loop(0, nt, body, jnp.zeros((QBLK, KTILE), I32))
            return jnp.sum(acc, axis=-1, keepdims=True)

        def pos_body(b, pos):
            cand = pos + lax.shift_left(jnp.int32(1), 12 - b)
            return jnp.where(count_eq_before(cand) < need, cand, pos)

        pos = lax.fori_loop(0, 13, pos_body, jnp.zeros((QBLK, 1), I32))

        def drop_body(j, carry):
            k0 = pl.multiple_of(j * KTILE, KTILE)
            kk = kbuf[:, pl.ds(k0, KTILE)]
            drop = (kk == tau) & (k0 + lane > pos) & (n_ge > B_TOPK_MAX)
            kbuf[:, pl.ds(k0, KTILE)] = jnp.where(drop, jnp.int32(INT_MIN), kk)
            return carry

        lax.fori_loop(0, nt, drop_body, 0)

    m_scr[...] = jnp.full_like(m_scr, NEG_BIG)
    l_scr[...] = jnp.zeros_like(l_scr)
    acc_scr[...] = jnp.zeros_like(acc_scr)

    def attend(j, bias_idx):
        k0 = pl.multiple_of(j * KTILE, KTILE)
        kv = ckv_ref[0, pl.ds(k0, KTILE), :]
        s_scr[...] = lax.dot_general(qa_ref[0], kv, (((1,), (1,)), ((), ())), preferred_element_type=F32)
        madd = jnp.where(kbuf[:, pl.ds(k0, KTILE)] >= tau, 0.0, NEG_BIG)
        for hh in range(B_HEADS):
            r0, r1 = hh * QBLK, (hh + 1) * QBLK
            s = s_scr[r0:r1, :] + madd
            if bias_idx is not None:
                s = s + bt_ref[bias_idx, hh]
            m_old = m_scr[r0:r1, :]
            m_new = jnp.maximum(m_old, jnp.max(s, axis=-1, keepdims=True))
            p = jnp.exp(s - pltpu.repeat(m_new, KTILE // 128, axis=1))
            alpha = jnp.exp(m_old - m_new)
            l_scr[r0:r1, :] = alpha * l_scr[r0:r1, :] + jnp.sum(p, axis=-1, keepdims=True)
            m_scr[r0:r1, :] = m_new
            a_scr[r0:r1, :] = alpha
            p_scr[r0:r1, :] = p.astype(BF16)
        pv = jnp.dot(p_scr[...], kv, preferred_element_type=F32)
        acc_scr[...] = pltpu.repeat(a_scr[...], B_KV_LATENT // 128, axis=1) * acc_scr[...] + pv

    def far_body(j, carry):
        attend(j, None)
        return carry

    lax.fori_loop(0, jnp.maximum(nt - 2, 0), far_body, 0)
    odd = i % 2

    @pl.when(nt >= 2)
    def _():
        attend(nt - 2, 1 - odd)

    attend(nt - 1, 3 - odd)
    inv_l = 1.0 / l_scr[...]
    o_ref[0] = (acc_scr[...] * pltpu.repeat(inv_l, B_KV_LATENT // 128, axis=1)).astype(BF16)


def _attention(qa, qi, wi, ckv, ki, btab, bsz, seq):
    nqb = seq // QBLK
    gq = lambda b, i: (b * nqb + i, 0, 0)
    gb = lambda b, i: (b, 0, 0)
    return pl.pallas_call(
        _attn_kernel,
        out_shape=jax.ShapeDtypeStruct(qa.shape, BF16),
        grid=(bsz, nqb),
        in_specs=[
            pl.BlockSpec((1, B_HEADS * QBLK, B_KV_LATENT), gq),
            pl.BlockSpec((1, B_IDX_HEADS * QBLK, B_IDX_DIM), gq),
            pl.BlockSpec((1, B_IDX_HEADS * QBLK, 1), gq),
            pl.BlockSpec((1, seq, B_KV_LATENT), gb),
            pl.BlockSpec((1, seq, B_IDX_DIM), gb),
            pl.BlockSpec((4, B_HEADS, QBLK, KTILE), lambda b, i: (0, 0, 0, 0), pipeline_mode=pl.Buffered(1)),
        ],
        out_specs=pl.BlockSpec((1, B_HEADS * QBLK, B_KV_LATENT), gq),
        scratch_shapes=[
            pltpu.VMEM((QBLK, seq), I32),
            pltpu.VMEM((B_HEADS * QBLK, 128), F32),
            pltpu.VMEM((B_HEADS * QBLK, 128), F32),
            pltpu.VMEM((B_HEADS * QBLK, 128), F32),
            pltpu.VMEM((B_HEADS * QBLK, B_KV_LATENT), F32),
            pltpu.VMEM((B_HEADS * QBLK, KTILE), F32),
            pltpu.VMEM((B_HEADS * QBLK, KTILE), BF16),
        ],
        compiler_params=_cparams(("arbitrary", "arbitrary")),
        name="dsa_attention",
    )(qa, qi, wi, ckv.reshape(bsz, seq, -1), ki.reshape(bsz, seq, -1), btab)


def _bout_kernel(h_ref, ol_ref, mod_ref, wuv_ref, wout_ref, n2g_ref, rw_ref, h_out, hn_out, lg_out, o_scr):
    mod = mod_ref[0]
    for blk in range(ol_ref.shape[0]):
        for hh in range(B_HEADS):
            oh = jnp.dot(ol_ref[blk, hh * QBLK:(hh + 1) * QBLK, :], wuv_ref[hh], preferred_element_type=F32)
            o_scr[blk * QBLK:(blk + 1) * QBLK, hh * B_V_DIM:(hh + 1) * B_V_DIM] = oh.astype(BF16)
    y = jnp.dot(o_scr[...], wout_ref[...], preferred_element_type=F32)
    _residual_epilogue(h_ref[...], y, mod, n2g_ref[...], rw_ref[...], h_out, hn_out, lg_out)


def _bout(h, o_lat, mod_l, w_uv, w_out, n2g, router_w, seq):
    n, d = h.shape
    tm = ROW_TILE
    per_b = seq // tm
    qpb = tm // QBLK
    ne = router_w.shape[1]
    wuv_t = jnp.transpose(w_uv, (1, 0, 2)).astype(BF16)
    row = lambda i: (i, 0)
    return pl.pallas_call(
        _bout_kernel,
        out_shape=(jax.ShapeDtypeStruct((n, d), F32), jax.ShapeDtypeStruct((n, d), F32),
                   jax.ShapeDtypeStruct((n, ne), F32)),
        grid=(n // tm,),
        in_specs=[
            pl.BlockSpec((tm, d), row),
            pl.BlockSpec((qpb, B_HEADS * QBLK, B_KV_LATENT), lambda i: (i, 0, 0)),
            pl.BlockSpec((1, 6, d), lambda i: (i // per_b, 0, 0)),
            _const_spec((B_HEADS, B_KV_LATENT, B_V_DIM)),
            _const_spec((B_HEADS * B_V_DIM, d)),
            _const_spec((1, d)),
            _const_spec((d, ne)),
        ],
        out_specs=(pl.BlockSpec((tm, d), row), pl.BlockSpec((tm, d), row), pl.BlockSpec((tm, ne), row)),
        scratch_shapes=[pltpu.VMEM((tm, B_HEADS * B_V_DIM), BF16)],
        compiler_params=_cparams(("arbitrary",)),
        name="dsa_out",
    )(h, o_lat, mod_l, wuv_t, w_out.astype(BF16), n2g.reshape(1, d), router_w)


def _final_kernel(h1_ref, y0_ref, y1_ref, gt_ref, mod_ref, g_ref, o_ref):
    h = _moe_combine(h1_ref[...], y0_ref[...], y1_ref[...], gt_ref[...], mod_ref[0])
    o_ref[...] = h * lax.rsqrt(jnp.mean(h * h, axis=-1, keepdims=True) + EPS) * g_ref[...]


def _final(h1, y0, y1, gates, mod_l, final_g, seq):
    n, d = h1.shape
    tm = 512
    per_b = seq // tm
    row = lambda i: (i, 0)
    return pl.pallas_call(
        _final_kernel,
        out_shape=jax.ShapeDtypeStruct((n, d), F32),
        grid=(n // tm,),
        in_specs=[pl.BlockSpec((tm, d), row), pl.BlockSpec((tm, d), row), pl.BlockSpec((tm, d), row),
                  pl.BlockSpec((tm, TOP_K), row), pl.BlockSpec((1, 6, d), lambda i: (i // per_b, 0, 0)),
                  _const_spec((1, d))],
        out_specs=pl.BlockSpec((tm, d), row),
        compiler_params=_cparams(("arbitrary",)),
        name="final_norm",
    )(h1, y0, y1, gates, mod_l, final_g.reshape(1, d))


def kernel(x, c, ada_w, ada_b, norm1_g, norm2_g, a_w_in, a_ln_g, a_ln_b, a_w_sp, a_b_sp, a_w_out, b_w_in,
           b_kv_norm_g, b_w_uk, b_w_uv, b_kidx_g, b_w_out, rel_bias, router_w, router_b, moe_w_gate,
           moe_w_up, moe_w_down, final_g):
    bsz, seq, d = x.shape
    n = bsz * seq
    mod = _adaln(c, ada_w, ada_b).reshape(ada_w.shape[0], bsz, 6, d)
    h = x.reshape(n, d)

    h1, hn2, logits = _gmlp_layer(h, mod[0], norm1_g[0], a_w_in[0], a_ln_g[0], a_ln_b[0], a_w_sp[0],
                                  a_b_sp[0], a_w_out[0], norm2_g[0], router_w, seq)
    y0, y1, gates = _moe(hn2, logits, router_b, moe_w_gate, moe_w_up, moe_w_down, 0)

    h, qa, ckv, qi, ki, wi = _bproj(h1, y0, y1, gates, mod[0], mod[1], norm1_g[1], b_w_in[0],
                                    b_kv_norm_g[0], b_w_uk[0], b_kidx_g[0], seq)
    o_lat = _attention(qa, qi, wi, ckv, ki, _bias_tables(rel_bias), bsz, seq)
    h1, hn2, logits = _bout(h, o_lat, mod[1], b_w_uv[0], b_w_out[0], norm2_g[1], router_w, seq)
    y0, y1, gates = _moe(hn2, logits, router_b, moe_w_gate, moe_w_up, moe_w_down, 1)

    out = _final(h1, y0, y1, gates, mod[1], final_g, seq)
    return out.reshape(bsz, seq, d)
```

```python
import functools
import math

import jax
import jax.numpy as jnp
from jax import lax
from jax.experimental import pallas as pl
from jax.experimental.pallas import tpu as pltpu

F32 = jnp.float32
BF16 = jnp.bfloat16
I32 = jnp.int32
HIGHEST = lax.Precision.HIGHEST

EPS = 1e-6
A_CHUNK = 128
A_GROUPS = 8
B_HEADS = 16
B_HEAD_DIM = 64
B_V_DIM = 64
B_KV_LATENT = 256
B_IDX_HEADS = 8
B_IDX_DIM = 64
B_TOPK_MAX = 256
QBLK = 128
KTILE = 256
REL_BUCKETS = 32
REL_MAX_DIST = 128
N_EXPERTS = 16
N_GROUPS = 4
EXPERTS_PER_GROUP = 4
TOP_K = 2
MOE_ROWS = 1024
MOE_SUB = 256
MOE_FT = 512

ROW_TILE = 256
VMEM_LIMIT = 56 * 1024 * 1024

INT_MIN = -2 ** 31
NEG_BIG = -1e30


def _cparams(sem):
    return pltpu.CompilerParams(dimension_semantics=sem, vmem_limit_bytes=VMEM_LIMIT)


def _mod_rmsnorm(h, g, scale, shift):
    ms = jnp.mean(h * h, axis=-1, keepdims=True)
    return (h * lax.rsqrt(ms + EPS) * g) * (1.0 + scale) + shift


def _gelu_tanh(x):
    c = math.sqrt(2.0 / math.pi)
    return 0.5 * x * (1.0 + jnp.tanh(c * (x + 0.044715 * (x * x * x))))


def _adaln_kernel(c_ref, w_ref, b_ref, o_ref):
    c = c_ref[...]
    sc = c * jax.nn.sigmoid(c)
    o_ref[0] = jnp.dot(sc, w_ref[0], precision=HIGHEST, preferred_element_type=F32) + b_ref[0]


def _adaln(c, ada_w, ada_b):
    depth, d, e = ada_w.shape
    bsz = c.shape[0]
    bp = 8
    c_pad = jnp.zeros((bp, d), F32).at[:bsz].set(c)
    tn = 1024
    out = pl.pallas_call(
        _adaln_kernel,
        out_shape=jax.ShapeDtypeStruct((depth, bp, e), F32),
        grid=(depth, e // tn),
        in_specs=[
            pl.BlockSpec((bp, d), lambda l, j: (0, 0)),
            pl.BlockSpec((1, d, tn), lambda l, j: (l, 0, j)),
            pl.BlockSpec((1, 1, tn), lambda l, j: (l, 0, j)),
        ],
        out_specs=pl.BlockSpec((1, bp, tn), lambda l, j: (l, 0, j)),
        compiler_params=_cparams(("arbitrary", "arbitrary")),
        name="adaln",
    )(c_pad, ada_w, ada_b.reshape(depth, 1, e))
    return out[:, :bsz]


def _residual_epilogue(h, y, mod, n2g, rw, h_out, hn_out, lg_out):
    h1 = h + mod[2:3] * y
    h_out[...] = h1
    hn2 = _mod_rmsnorm(h1, n2g, mod[4:5], mod[3:4])
    bits = pltpu.bitcast(hn2.astype(BF16).astype(F32), I32)
    half = bits.shape[1] // 2
    hn_out[...] = bits[:, :half] | lax.shift_right_logical(bits[:, half:], 16)
    lg_out[...] = jnp.dot(hn2, rw, precision=HIGHEST, preferred_element_type=F32)


def _gmlp_kernel(h_ref, mod_ref, n1g_ref, win_ref, lng_ref, lnb_ref, wsp_ref, bsp_ref, wout_ref,
                 n2g_ref, rw_ref, h_out, hn_out, lg_out, u_scr, v_scr, s_scr):
    tm = h_ref.shape[0]
    inner = u_scr.shape[1]
    gw = inner // A_GROUPS
    tn = 512
    h = h_ref[...]
    mod = mod_ref[0]
    hn = _mod_rmsnorm(h, n1g_ref[...], mod[1:2], mod[0:1]).astype(BF16)
    for j in range(2 * inner // tn):
        z = _gelu_tanh(jnp.dot(hn, win_ref[:, j * tn:(j + 1) * tn], preferred_element_type=F32))
        if j < inner // tn:
            u_scr[:, j * tn:(j + 1) * tn] = z
        else:
            jj = j - inner // tn
            v_scr[:, jj * tn:(jj + 1) * tn] = z
    vsum = jnp.zeros((tm, 1), F32)
    for j in range(inner // tn):
        vsum = vsum + jnp.sum(v_scr[:, j * tn:(j + 1) * tn], axis=-1, keepdims=True)
    mu = vsum * (1.0 / inner)
    vsq = jnp.zeros((tm, 1), F32)
    for j in range(inner // tn):
        d = v_scr[:, j * tn:(j + 1) * tn] - mu
        vsq = vsq + jnp.sum(d * d, axis=-1, keepdims=True)
    rstd = lax.rsqrt(vsq * (1.0 / inner) + EPS)
    row = lax.broadcasted_iota(I32, (A_CHUNK, A_CHUNK), 0)
    col = lax.broadcasted_iota(I32, (A_CHUNK, A_CHUNK), 1)
    tril = row >= col
    for g in range(A_GROUPS):
        ws = jnp.where(tril, wsp_ref[g], 0.0).astype(BF16)
        bcol = bsp_ref[:, g:g + 1]
        lg = lng_ref[:, g * gw:(g + 1) * gw]
        lb = lnb_ref[:, g * gw:(g + 1) * gw]
        for c in range(tm // A_CHUNK):
            r0, r1 = c * A_CHUNK, (c + 1) * A_CHUNK
            vt = v_scr[r0:r1, g * gw:(g + 1) * gw]
            vn = ((vt - mu[r0:r1]) * rstd[r0:r1]) * lg + lb
            fv = jnp.dot(ws, vn.astype(BF16), preferred_element_type=F32) + bcol
            s_scr[r0:r1, g * gw:(g + 1) * gw] = (u_scr[r0:r1, g * gw:(g + 1) * gw] * fv).astype(BF16)
    y = jnp.dot(s_scr[...], wout_ref[...], preferred_element_type=F32)
    _residual_epilogue(h, y, mod, n2g_ref[...], rw_ref[...], h_out, hn_out, lg_out)


def _const_spec(shape):
    nd = len(shape)
    return pl.BlockSpec(shape, lambda i, _nd=nd: (0,) * _nd, pipeline_mode=pl.Buffered(1))


def _gmlp_layer(h, mod_l, n1g, w_in, ln_g, ln_b, w_sp, b_sp, w_out, n2g, router_w, seq):
    n, d = h.shape
    inner = w_out.shape[0]
    tm = ROW_TILE
    per_b = seq // tm
    ne = router_w.shape[1]
    return pl.pallas_call(
        _gmlp_kernel,
        out_shape=(jax.ShapeDtypeStruct((n, d), F32), jax.ShapeDtypeStruct((n, d // 2), I32),
                   jax.ShapeDtypeStruct((n, ne), F32)),
        grid=(n // tm,),
        in_specs=[
            pl.BlockSpec((tm, d), lambda i: (i, 0)),
            pl.BlockSpec((1, 6, d), lambda i: (i // per_b, 0, 0)),
            _const_spec((1, d)),
            _const_spec((d, 2 * inner)),
            _const_spec((1, inner)),
            _const_spec((1, inner)),
            _const_spec((A_GROUPS, A_CHUNK, A_CHUNK)),
            _const_spec((A_CHUNK, A_GROUPS)),
            _const_spec((inner, d)),
            _const_spec((1, d)),
            _const_spec((d, ne)),
        ],
        out_specs=(pl.BlockSpec((tm, d), lambda i: (i, 0)), pl.BlockSpec((tm, d // 2), lambda i: (i, 0)),
                   pl.BlockSpec((tm, ne), lambda i: (i, 0))),
        scratch_shapes=[pltpu.VMEM((tm, inner), F32), pltpu.VMEM((tm, inner), F32),
                        pltpu.VMEM((tm, inner), BF16)],
        compiler_params=_cparams(("arbitrary",)),
        name="gmlp_layer",
    )(h, mod_l, n1g.reshape(1, d), w_in.astype(BF16), ln_g.reshape(1, inner), ln_b.reshape(1, inner),
      w_sp, b_sp.T, w_out.astype(BF16), n2g.reshape(1, d), router_w)


def _first_max4(rows):
    m = jnp.maximum(jnp.maximum(rows[0], rows[1]), jnp.maximum(rows[2], rows[3]))
    idx = jnp.where(rows[0] == m, 0, jnp.where(rows[1] == m, 1, jnp.where(rows[2] == m, 2, 3)))
    return m, idx.astype(I32)


def _route_kernel(lt_ref, rb_ref, e_ref, g_ref):
    aff = jax.nn.sigmoid(lt_ref[...])
    sel = aff + rb_ref[...]
    neg = jnp.float32(-jnp.inf)
    g_score, g_i1, g_i2 = [], [], []
    for g in range(N_GROUPS):
        rows = [sel[4 * g + k:4 * g + k + 1, :] for k in range(EXPERTS_PER_GROUP)]
        m1, i1 = _first_max4(rows)
        rest = [jnp.where(i1 == k, neg, rows[k]) for k in range(EXPERTS_PER_GROUP)]
        m2, i2 = _first_max4(rest)
        g_score.append(m1 + m2)
        g_i1.append(i1)
        g_i2.append(i2)
    _, grp = _first_max4(g_score)
    l1 = jnp.zeros_like(grp)
    l2 = jnp.zeros_like(grp)
    for g in range(N_GROUPS):
        l1 = jnp.where(grp == g, g_i1[g], l1)
        l2 = jnp.where(grp == g, g_i2[g], l2)
    e1 = grp * EXPERTS_PER_GROUP + l1
    e2 = grp * EXPERTS_PER_GROUP + l2
    a1 = jnp.zeros_like(g_score[0])
    a2 = jnp.zeros_like(g_score[0])
    for e in range(N_EXPERTS):
        ae = aff[e:e + 1, :]
        a1 = jnp.where(e1 == e, ae, a1)
        a2 = jnp.where(e2 == e, ae, a2)
    tot = a1 + a2
    e_ref[0:1, :] = e1
    e_ref[1:2, :] = e2
    g_ref[0:1, :] = a1 / tot
    g_ref[1:2, :] = a2 / tot


def _route(logits, router_b):
    n, ne = logits.shape
    tn = min(2048, n)
    return pl.pallas_call(
        _route_kernel,
        out_shape=(jax.ShapeDtypeStruct((TOP_K, n), I32), jax.ShapeDtypeStruct((TOP_K, n), F32)),
        grid=(n // tn,),
        in_specs=[pl.BlockSpec((ne, tn), lambda i: (0, i)), pl.BlockSpec((ne, 1), lambda i: (0, 0))],
        out_specs=(pl.BlockSpec((TOP_K, tn), lambda i: (0, i)), pl.BlockSpec((TOP_K, tn), lambda i: (0, i))),
        compiler_params=_cparams(("arbitrary",)),
        name="route",
    )(logits.T, router_b.reshape(ne, 1))


def _slot_tables(experts):
    n = experts.shape[1]
    a = n * TOP_K
    e_flat = experts.T.reshape(a)
    order = jnp.argsort(e_flat).astype(I32)
    e_sorted = e_flat[order]
    counts = jnp.bincount(e_flat, length=N_EXPERTS).astype(I32)
    padded = ((counts + MOE_ROWS - 1) // MOE_ROWS) * MOE_ROWS
    pad_end = jnp.cumsum(padded)
    pad_start = pad_end - padded
    seg_start = jnp.cumsum(counts) - counts
    rank = jnp.arange(a, dtype=I32) - seg_start[e_sorted]
    slot_sorted = (pad_start[e_sorted] + rank).astype(I32)
    p = a + N_EXPERTS * MOE_ROWS
    n_items = p // MOE_ROWS
    tok_of_slot = jnp.zeros((p,), I32).at[slot_sorted].set(order // TOP_K)
    slot_of_assign = jnp.zeros((a,), I32).at[order].set(slot_sorted)
    item_start = jnp.arange(n_items, dtype=I32) * MOE_ROWS
    n_used = pad_end[-1] // MOE_ROWS
    item_e_raw = jnp.minimum(jnp.searchsorted(pad_end, item_start, side='right'), N_EXPERTS - 1).astype(I32)
    used = item_start < pad_end[-1]
    last_e = item_e_raw[jnp.maximum(n_used - 1, 0)]
    item_e = jnp.where(used, item_e_raw, last_e).astype(I32)
    valid = jnp.clip(counts[item_e_raw] - (item_start - pad_start[item_e_raw]), 0, MOE_ROWS)
    item_rows = jnp.where(used, valid, 0).astype(I32)
    item_blk = jnp.where(used, jnp.arange(n_items, dtype=I32), jnp.maximum(n_used - 1, 0)).astype(I32)
    return tok_of_slot, slot_of_assign.reshape(n, TOP_K), item_e, item_rows, item_blk


def _moe_kernel(ie_ref, ir_ref, ib_ref, x_ref, wg_ref, wu_ref, wd_ref, o_ref, wg_s, wu_s, wd_s, x_s):
    it = pl.program_id(0)
    f = pl.program_id(1)
    nrows = ir_ref[it]
    half = x_ref.shape[1]

    @pl.when(f == 0)
    def _():
        o_ref[...] = jnp.zeros_like(o_ref)
        pk = x_ref[...]
        x_s[:, :half] = pltpu.bitcast(pk & jnp.int32(-65536), F32).astype(BF16)
        x_s[:, half:] = pltpu.bitcast(lax.shift_left(pk, 16), F32).astype(BF16)

    @pl.when(nrows > 0)
    def _():
        wg_s[...] = wg_ref[0, 0].astype(BF16)
        wu_s[...] = wu_ref[0, 0].astype(BF16)
        wd_s[...] = wd_ref[0, 0].astype(BF16)
        nsb = (nrows + MOE_SUB - 1) // MOE_SUB

        def body(sb, carry):
            r0 = pl.multiple_of(sb * MOE_SUB, MOE_SUB)
            x = x_s[pl.ds(r0, MOE_SUB), :]
            g = jnp.dot(x, wg_s[...], preferred_element_type=F32)
            u = jnp.dot(x, wu_s[...], preferred_element_type=F32)
            hmid = ((g * jax.nn.sigmoid(g)) * u).astype(BF16)
            o_ref[pl.ds(r0, MOE_SUB), :] += jnp.dot(hmid, wd_s[...], preferred_element_type=F32)
            return carry

        lax.fori_loop(0, nsb, body, 0)


def _moe_experts(x_sorted, item_e, item_rows, item_blk, w_gate, w_up, w_down, layer):
    p, half = x_sorted.shape
    d = 2 * half
    n_items = p // MOE_ROWS
    dff = w_gate.shape[3]
    nf = dff // MOE_FT
    grid_spec = pltpu.PrefetchScalarGridSpec(
        num_scalar_prefetch=3,
        grid=(n_items, nf),
        in_specs=[
            pl.BlockSpec((MOE_ROWS, half), lambda i, f, ie, ir, ib: (ib[i], 0)),
            pl.BlockSpec((1, 1, d, MOE_FT),
                         lambda i, f, ie, ir, ib: (layer, ie[i], 0, jnp.where(ir[i] > 0, f, nf - 1))),
            pl.BlockSpec((1, 1, d, MOE_FT),
                         lambda i, f, ie, ir, ib: (layer, ie[i], 0, jnp.where(ir[i] > 0, f, nf - 1))),
            pl.BlockSpec((1, 1, MOE_FT, d),
                         lambda i, f, ie, ir, ib: (layer, ie[i], jnp.where(ir[i] > 0, f, nf - 1), 0)),
        ],
        out_specs=pl.BlockSpec((MOE_ROWS, d), lambda i, f, ie, ir, ib: (i, 0)),
        scratch_shapes=[pltpu.VMEM((d, MOE_FT), BF16), pltpu.VMEM((d, MOE_FT), BF16),
                        pltpu.VMEM((MOE_FT, d), BF16), pltpu.VMEM((MOE_ROWS, d), BF16)],
    )
    return pl.pallas_call(
        _moe_kernel,
        out_shape=jax.ShapeDtypeStruct((p, d), F32),
        grid_spec=grid_spec,
        compiler_params=_cparams(("arbitrary", "arbitrary")),
        name="moe_experts",
    )(item_e, item_rows, item_blk, x_sorted, w_gate, w_up, w_down)


def _moe(hn2, logits, router_b, w_gate, w_up, w_down, layer):
    experts, gates = _route(logits, router_b)
    tok_of_slot, slot_of_assign, item_e, item_rows, item_blk = _slot_tables(experts)
    x_sorted = jnp.take(hn2, tok_of_slot, axis=0)
    y_slot = _moe_experts(x_sorted, item_e, item_rows, item_blk, w_gate, w_up, w_down, layer)
    y0 = jnp.take(y_slot, slot_of_assign[:, 0], axis=0)
    y1 = jnp.take(y_slot, slot_of_assign[:, 1], axis=0)
    return y0, y1, gates.T


def _moe_combine(h1, y0, y1, gates, mod):
    return h1 + mod[5:6] * (gates[:, 0:1] * y0 + gates[:, 1:2] * y1)


def _bproj_kernel(h1_ref, y0_ref, y1_ref, gt_ref, modp_ref, mod_ref, n1g_ref, w_ref, kvg_ref, kig_ref,
                  wuk_ref, h_out, qa_out, ckv_out, qi_out, ki_out, wi_out):
    tm = h1_ref.shape[0]
    h = _moe_combine(h1_ref[...], y0_ref[...], y1_ref[...], gt_ref[...], modp_ref[0])
    h_out[...] = h
    mod = mod_ref[0]
    hn = _mod_rmsnorm(h, n1g_ref[...], mod[1:2], mod[0:1]).astype(BF16)
    proj = jnp.dot(hn, w_ref[...], preferred_element_type=F32)
    o1 = B_HEADS * B_HEAD_DIM
    o2 = o1 + B_KV_LATENT
    o3 = o2 + B_IDX_HEADS * B_IDX_DIM
    ckv = proj[:, o1:o2]
    ckv = ckv * lax.rsqrt(jnp.mean(ckv * ckv, axis=-1, keepdims=True) + EPS) * kvg_ref[...]
    ckv_out[...] = ckv.astype(BF16)
    tail = proj[:, o3:o3 + 128]
    ki = tail[:, :B_IDX_DIM]
    ki = ki * lax.rsqrt(jnp.mean(ki * ki, axis=-1, keepdims=True) + EPS) * kig_ref[...]
    ki_out[...] = ki.astype(BF16)
    wi = tail[:, B_IDX_DIM:B_IDX_DIM + B_IDX_HEADS] * (B_IDX_HEADS ** -0.5 * B_IDX_DIM ** -0.5)
    scale = B_HEAD_DIM ** -0.5
    for blk in range(tm // QBLK):
        r0, r1 = blk * QBLK, (blk + 1) * QBLK
        for hh in range(B_HEADS):
            qh = proj[r0:r1, hh * B_HEAD_DIM:(hh + 1) * B_HEAD_DIM].astype(BF16)
            qa = jnp.dot(qh, wuk_ref[hh], preferred_element_type=F32) * scale
            qa_out[blk, hh * QBLK:(hh + 1) * QBLK, :] = qa.astype(BF16)
        for hh in range(B_IDX_HEADS):
            qi_out[blk, hh * QBLK:(hh + 1) * QBLK, :] = proj[r0:r1, o2 + hh * B_IDX_DIM:
                                                            o2 + (hh + 1) * B_IDX_DIM].astype(BF16)
            wi_out[blk, hh * QBLK:(hh + 1) * QBLK, :] = wi[r0:r1, hh:hh + 1]


def _bproj(h1, y0, y1, gates, mod_prev, mod_l, n1g, b_w_in, kv_g, w_uk, kidx_g, seq):
    n, d = h1.shape
    tm = ROW_TILE
    per_b = seq // tm
    nq = n // QBLK
    qpb = tm // QBLK
    o1 = B_HEADS * B_HEAD_DIM
    o2 = o1 + B_KV_LATENT
    o3 = o2 + B_IDX_HEADS * B_IDX_DIM
    wcat = jnp.zeros((d, o3 + 128), F32).at[:, :b_w_in.shape[1]].set(b_w_in).astype(BF16)
    wuk_t = jnp.transpose(w_uk, (1, 2, 0)).astype(BF16)
    row = lambda i: (i, 0)
    modm = lambda i: (i // per_b, 0, 0)
    blk3 = lambda i: (i, 0, 0)
    return pl.pallas_call(
        _bproj_kernel,
        out_shape=(
            jax.ShapeDtypeStruct((n, d), F32),
            jax.ShapeDtypeStruct((nq, B_HEADS * QBLK, B_KV_LATENT), BF16),
            jax.ShapeDtypeStruct((n, B_KV_LATENT), BF16),
            jax.ShapeDtypeStruct((nq, B_IDX_HEADS * QBLK, B_IDX_DIM), BF16),
            jax.ShapeDtypeStruct((n, B_IDX_DIM), BF16),
            jax.ShapeDtypeStruct((nq, B_IDX_HEADS * QBLK, 1), F32),
        ),
        grid=(n // tm,),
        in_specs=[
            pl.BlockSpec((tm, d), row), pl.BlockSpec((tm, d), row), pl.BlockSpec((tm, d), row),
            pl.BlockSpec((tm, TOP_K), row),
            pl.BlockSpec((1, 6, d), modm), pl.BlockSpec((1, 6, d), modm),
            _const_spec((1, d)),
            _const_spec((d, o3 + 128)),
            _const_spec((1, B_KV_LATENT)),
            _const_spec((1, B_IDX_DIM)),
            _const_spec((B_HEADS, B_HEAD_DIM, B_KV_LATENT)),
        ],
        out_specs=(
            pl.BlockSpec((tm, d), row),
            pl.BlockSpec((qpb, B_HEADS * QBLK, B_KV_LATENT), blk3),
            pl.BlockSpec((tm, B_KV_LATENT), row),
            pl.BlockSpec((qpb, B_IDX_HEADS * QBLK, B_IDX_DIM), blk3),
            pl.BlockSpec((tm, B_IDX_DIM), row),
            pl.BlockSpec((qpb, B_IDX_HEADS * QBLK, 1), blk3),
        ),
        compiler_params=_cparams(("arbitrary",)),
        name="dsa_proj",
    )(h1, y0, y1, gates, mod_prev, mod_l, n1g.reshape(1, d), wcat, kv_g.reshape(1, -1),
      kidx_g.reshape(1, -1), wuk_t)


def _t5_bucket(dist):
    n = jnp.maximum(dist, 0)
    exact = REL_BUCKETS // 2
    nf = jnp.maximum(n, 1).astype(F32)
    large = exact + (jnp.log(nf / exact) / math.log(REL_MAX_DIST / exact)
                     * (REL_BUCKETS - exact)).astype(I32)
    large = jnp.minimum(large, REL_BUCKETS - 1)
    return jnp.where(n < exact, n, large)


def _bias_tables(rel_bias):
    t = jnp.arange(QBLK, dtype=I32)[:, None]
    s = jnp.arange(QBLK, dtype=I32)[None, :]
    far = rel_bias[REL_BUCKETS - 1]
    diag = rel_bias[_t5_bucket(t - s)] - far
    prev = rel_bias[_t5_bucket(t - s + QBLK)] - far
    diag = jnp.transpose(diag, (2, 0, 1))
    prev = jnp.transpose(prev, (2, 0, 1))
    zero = jnp.zeros_like(diag)
    cat = lambda a, b: jnp.concatenate([a, b], axis=-1)
    return jnp.stack([cat(zero, zero), cat(zero, prev), cat(prev, diag), cat(diag, zero)])


def _attn_kernel(qa_ref, qi_ref, wi_ref, ckv_ref, ki_ref, bt_ref, o_ref, kbuf, m_scr, l_scr, a_scr, acc_scr,
                 s_scr, p_scr):
    i = pl.program_id(1)
    nt = i // 2 + 1
    t_row = i * QBLK + lax.broadcasted_iota(I32, (QBLK, KTILE), 0)
    lane = lax.broadcasted_iota(I32, (QBLK, KTILE), 1)

    qi = qi_ref[0]
    wcol = wi_ref[0]

    def score_body(j, carry):
        k0 = pl.multiple_of(j * KTILE, KTILE)
        kt = ki_ref[0, pl.ds(k0, KTILE), :]
        p = lax.dot_general(qi, kt, (((1,), (1,)), ((), ())), preferred_element_type=F32)
        p = jnp.maximum(p, 0.0) * wcol
        sc = p[0:QBLK]
        for hh in range(1, B_IDX_HEADS):
            sc = sc + p[hh * QBLK:(hh + 1) * QBLK]
        bits = pltpu.bitcast(sc + 0.0, I32)
        key = jnp.where(bits < 0, bits ^ jnp.int32(0x7FFFFFFF), bits)
        key = jnp.where(k0 + lane <= t_row, key, jnp.int32(INT_MIN))
        kbuf[:, pl.ds(k0, KTILE)] = key
        return carry

    lax.fori_loop(0, nt, score_body, 0)

    def count_ge(cand):
        def body(j, acc):
            k0 = pl.multiple_of(j * KTILE, KTILE)
            return acc + jnp.where(kbuf[:, pl.ds(k0, KTILE)] >= cand, 1, 0)
        acc = lax.fori_loop(0, nt, body, jnp.zeros((QBLK, KTILE), I32))
        return jnp.sum(acc, axis=-1, keepdims=True)

    def bit_cond(c):
        return (c[0] < 32) & (c[3] == 0)

    def bit_body(c):
        b, lo, n_ge, _ = c
        cand = lo + lax.shift_left(jnp.int32(1), 31 - b)
        cnt = count_ge(cand)
        take = cnt >= B_TOPK_MAX
        n_ge = jnp.where(take, cnt, n_ge)
        done = (jnp.max(n_ge) == B_TOPK_MAX).astype(I32)
        return b + 1, jnp.where(take, cand, lo), n_ge, done

    _, tau, n_ge, _ = lax.while_loop(
        bit_cond, bit_body,
        (jnp.int32(0), jnp.full((QBLK, 1), INT_MIN, I32), jnp.full((QBLK, 1), nt * KTILE, I32),
         (i < 2).astype(I32)))
    tau = jnp.maximum(tau, jnp.int32(INT_MIN + 1))
    excess = (i >= 2) & (jnp.max(n_ge) > B_TOPK_MAX)

    @pl.when(excess)
    def _():
        n_gt = count_ge(tau + 1)
        need = B_TOPK_MAX - n_gt

        def count_eq_before(pos):
            def body(j, acc):
                k0 = pl.multiple_of(j * KTILE, KTILE)
                hit = (kbuf[:, pl.ds(k0, KTILE)] == tau) & (k0 + lane < pos)
                return acc + jnp.where(hit, 1, 0)
            acc = lax.fori_loop(0, nt, body, jnp.zeros((QBLK, KTILE), I32))
            return jnp.sum(acc, axis=-1, keepdims=True)

        def pos_body(b, pos):
            cand = pos + lax.shift_left(jnp.int32(1), 12 - b)
            return jnp.where(count_eq_before(cand) < need, cand, pos)

        pos = lax.fori_loop(0, 13, pos_body, jnp.zeros((QBLK, 1), I32))

        def drop_body(j, carry):
            k0 = pl.multiple_of(j * KTILE, KTILE)
            kk = kbuf[:, pl.ds(k0, KTILE)]
            drop = (kk == tau) & (k0 + lane > pos) & (n_ge > B_TOPK_MAX)
            kbuf[:, pl.ds(k0, KTILE)] = jnp.where(drop, jnp.int32(INT_MIN), kk)
            return carry

        lax.fori_loop(0, nt, drop_body, 0)

    m_scr[...] = jnp.full_like(m_scr, NEG_BIG)
    l_scr[...] = jnp.zeros_like(l_scr)
    acc_scr[...] = jnp.zeros_like(acc_scr)

    def attend(j, bias_idx):
        k0 = pl.multiple_of(j * KTILE, KTILE)
        kv = ckv_ref[0, pl.ds(k0, KTILE), :]
        s_scr[...] = lax.dot_general(qa_ref[0], kv, (((1,), (1,)), ((), ())), preferred_element_type=F32)
        madd = jnp.where(kbuf[:, pl.ds(k0, KTILE)] >= tau, 0.0, NEG_BIG)
        for hh in range(B_HEADS):
            r0, r1 = hh * QBLK, (hh + 1) * QBLK
            s = s_scr[r0:r1, :] + madd
            if bias_idx is not None:
                s = s + bt_ref[bias_idx, hh]
            m_old = m_scr[r0:r1, :]
            m_new = jnp.maximum(m_old, jnp.max(s, axis=-1, keepdims=True))
            p = jnp.exp(s - pltpu.repeat(m_new, KTILE // 128, axis=1))
            alpha = jnp.exp(m_old - m_new)
            l_scr[r0:r1, :] = alpha * l_scr[r0:r1, :] + jnp.sum(p, axis=-1, keepdims=True)
            m_scr[r0:r1, :] = m_new
            a_scr[r0:r1, :] = alpha
            p_scr[r0:r1, :] = p.astype(BF16)
        pv = jnp.dot(p_scr[...], kv, preferred_element_type=F32)
        acc_scr[...] = pltpu.repeat(a_scr[...], B_KV_LATENT // 128, axis=1) * acc_scr[...] + pv

    def far_body(j, carry):
        attend(j, None)
        return carry

    lax.fori_loop(0, jnp.maximum(nt - 2, 0), far_body, 0)
    odd = i % 2

    @pl.when(nt >= 2)
    def _():
        attend(nt - 2, 1 - odd)

    attend(nt - 1, 3 - odd)
    inv_l = 1.0 / l_scr[...]
    o_ref[0] = (acc_scr[...] * pltpu.repeat(inv_l, B_KV_LATENT // 128, axis=1)).astype(BF16)


def _attention(qa, qi, wi, ckv, ki, btab, bsz, seq):
    nqb = seq // QBLK
    gq = lambda b, i: (b * nqb + i, 0, 0)
    gb = lambda b, i: (b, 0, 0)
    return pl.pallas_call(
        _attn_kernel,
        out_shape=jax.ShapeDtypeStruct(qa.shape, BF16),
        grid=(bsz, nqb),
        in_specs=[
            pl.BlockSpec((1, B_HEADS * QBLK, B_KV_LATENT), gq),
            pl.BlockSpec((1, B_IDX_HEADS * QBLK, B_IDX_DIM), gq),
            pl.BlockSpec((1, B_IDX_HEADS * QBLK, 1), gq),
            pl.BlockSpec((1, seq, B_KV_LATENT), gb),
            pl.BlockSpec((1, seq, B_IDX_DIM), gb),
            pl.BlockSpec((4, B_HEADS, QBLK, KTILE), lambda b, i: (0, 0, 0, 0), pipeline_mode=pl.Buffered(1)),
        ],
        out_specs=pl.BlockSpec((1, B_HEADS * QBLK, B_KV_LATENT), gq),
        scratch_shapes=[
            pltpu.VMEM((QBLK, seq), I32),
            pltpu.VMEM((B_HEADS * QBLK, 128), F32),
            pltpu.VMEM((B_HEADS * QBLK, 128), F32),
            pltpu.VMEM((B_HEADS * QBLK, 128), F32),
            pltpu.VMEM((B_HEADS * QBLK, B_KV_LATENT), F32),
            pltpu.VMEM((B_HEADS * QBLK, KTILE), F32),
            pltpu.VMEM((B_HEADS * QBLK, KTILE), BF16),
        ],
        compiler_params=_cparams(("arbitrary", "arbitrary")),
        name="dsa_attention",
    )(qa, qi, wi, ckv.reshape(bsz, seq, -1), ki.reshape(bsz, seq, -1), btab)


def _bout_kernel(h_ref, ol_ref, mod_ref, wuv_ref, wout_ref, n2g_ref, rw_ref, h_out, hn_out, lg_out, o_scr):
    mod = mod_ref[0]
    for blk in range(ol_ref.shape[0]):
        for hh in range(B_HEADS):
            oh = jnp.dot(ol_ref[blk, hh * QBLK:(hh + 1) * QBLK, :], wuv_ref[hh], preferred_element_type=F32)
            o_scr[blk * QBLK:(blk + 1) * QBLK, hh * B_V_DIM:(hh + 1) * B_V_DIM] = oh.astype(BF16)
    y = jnp.dot(o_scr[...], wout_ref[...], preferred_element_type=F32)
    _residual_epilogue(h_ref[...], y, mod, n2g_ref[...], rw_ref[...], h_out, hn_out, lg_out)


def _bout(h, o_lat, mod_l, w_uv, w_out, n2g, router_w, seq):
    n, d = h.shape
    tm = ROW_TILE
    per_b = seq // tm
    qpb = tm // QBLK
    ne = router_w.shape[1]
    wuv_t = jnp.transpose(w_uv, (1, 0, 2)).astype(BF16)
    row = lambda i: (i, 0)
    return pl.pallas_call(
        _bout_kernel,
        out_shape=(jax.ShapeDtypeStruct((n, d), F32), jax.ShapeDtypeStruct((n, d // 2), I32),
                   jax.ShapeDtypeStruct((n, ne), F32)),
        grid=(n // tm,),
        in_specs=[
            pl.BlockSpec((tm, d), row),
            pl.BlockSpec((qpb, B_HEADS * QBLK, B_KV_LATENT), lambda i: (i, 0, 0)),
            pl.BlockSpec((1, 6, d), lambda i: (i // per_b, 0, 0)),
            _const_spec((B_HEADS, B_KV_LATENT, B_V_DIM)),
            _const_spec((B_HEADS * B_V_DIM, d)),
            _const_spec((1, d)),
            _const_spec((d, ne)),
        ],
        out_specs=(pl.BlockSpec((tm, d), row), pl.BlockSpec((tm, d // 2), row), pl.BlockSpec((tm, ne), row)),
        scratch_shapes=[pltpu.VMEM((tm, B_HEADS * B_V_DIM), BF16)],
        compiler_params=_cparams(("arbitrary",)),
        name="dsa_out",
    )(h, o_lat, mod_l, wuv_t, w_out.astype(BF16), n2g.reshape(1, d), router_w)


def _final_kernel(h1_ref, y0_ref, y1_ref, gt_ref, mod_ref, g_ref, o_ref):
    h = _moe_combine(h1_ref[...], y0_ref[...], y1_ref[...], gt_ref[...], mod_ref[0])
    o_ref[...] = h * lax.rsqrt(jnp.mean(h * h, axis=-1, keepdims=True) + EPS) * g_ref[...]


def _final(h1, y0, y1, gates, mod_l, final_g, seq):
    n, d = h1.shape
    tm = 512
    per_b = seq // tm
    row = lambda i: (i, 0)
    return pl.pallas_call(
        _final_kernel,
        out_shape=jax.ShapeDtypeStruct((n, d), F32),
        grid=(n // tm,),
        in_specs=[pl.BlockSpec((tm, d), row), pl.BlockSpec((tm, d), row), pl.BlockSpec((tm, d), row),
                  pl.BlockSpec((tm, TOP_K), row), pl.BlockSpec((1, 6, d), lambda i: (i // per_b, 0, 0)),
                  _const_spec((1, d))],
        out_specs=pl.BlockSpec((tm, d), row),
        compiler_params=_cparams(("arbitrary",)),
        name="final_norm",
    )(h1, y0, y1, gates, mod_l, final_g.reshape(1, d))


def kernel(x, c, ada_w, ada_b, norm1_g, norm2_g, a_w_in, a_ln_g, a_ln_b, a_w_sp, a_b_sp, a_w_out, b_w_in,
           b_kv_norm_g, b_w_uk, b_w_uv, b_kidx_g, b_w_out, rel_bias, router_w, router_b, moe_w_gate,
           moe_w_up, moe_w_down, final_g):
    bsz, seq, d = x.shape
    n = bsz * seq
    mod = _adaln(c, ada_w, ada_b).reshape(ada_w.shape[0], bsz, 6, d)
    h = x.reshape(n, d)

    h1, hn2, logits = _gmlp_layer(h, mod[0], norm1_g[0], a_w_in[0], a_ln_g[0], a_ln_b[0], a_w_sp[0],
                                  a_b_sp[0], a_w_out[0], norm2_g[0], router_w, seq)
    y0, y1, gates = _moe(hn2, logits, router_b, moe_w_gate, moe_w_up, moe_w_down, 0)

    h, qa, ckv, qi, ki, wi = _bproj(h1, y0, y1, gates, mod[0], mod[1], norm1_g[1], b_w_in[0],
                                    b_kv_norm_g[0], b_w_uk[0], b_kidx_g[0], seq)
    o_lat = _attention(qa, qi, wi, ckv, ki, _bias_tables(rel_bias), bsz, seq)
    h1, hn2, logits = _bout(h, o_lat, mod[1], b_w_uv[0], b_w_out[0], norm2_g[1], router_w, seq)
    y0, y1, gates = _moe(hn2, logits, router_b, moe_w_gate, moe_w_up, moe_w_down, 1)

    out = _final(h1, y0, y1, gates, mod[1], final_g, seq)
    return out.reshape(bsz, seq, d)
```

```python
import functools
import math

import jax
import jax.numpy as jnp
from jax import lax
from jax.experimental import pallas as pl
from jax.experimental.pallas import tpu as pltpu

F32 = jnp.float32
BF16 = jnp.bfloat16
I32 = jnp.int32
HIGHEST = lax.Precision.HIGHEST

EPS = 1e-6
A_CHUNK = 128
A_GROUPS = 8
B_HEADS = 16
B_HEAD_DIM = 64
B_V_DIM = 64
B_KV_LATENT = 256
B_IDX_HEADS = 8
B_IDX_DIM = 64
B_TOPK_MAX = 256
QBLK = 128
KTILE = 256
REL_BUCKETS = 32
REL_MAX_DIST = 128
N_EXPERTS = 16
N_GROUPS = 4
EXPERTS_PER_GROUP = 4
TOP_K = 2
MOE_ROWS = 1024
MOE_SUB = 256
MOE_FT = 512

ROW_TILE = 256
VMEM_LIMIT = 56 * 1024 * 1024

INT_MIN = -2 ** 31
NEG_BIG = -1e30


def _cparams(sem):
    return pltpu.CompilerParams(dimension_semantics=sem, vmem_limit_bytes=VMEM_LIMIT)


def _mod_rmsnorm(h, g, scale, shift):
    ms = jnp.mean(h * h, axis=-1, keepdims=True)
    return (h * lax.rsqrt(ms + EPS) * g) * (1.0 + scale) + shift


def _gelu_tanh(x):
    c = math.sqrt(2.0 / math.pi)
    return 0.5 * x * (1.0 + jnp.tanh(c * (x + 0.044715 * (x * x * x))))


def _adaln_kernel(c_ref, w_ref, b_ref, o_ref):
    c = c_ref[...]
    sc = c * jax.nn.sigmoid(c)
    o_ref[0] = jnp.dot(sc, w_ref[0], precision=HIGHEST, preferred_element_type=F32) + b_ref[0]


def _adaln(c, ada_w, ada_b):
    depth, d, e = ada_w.shape
    bsz = c.shape[0]
    bp = 8
    c_pad = jnp.zeros((bp, d), F32).at[:bsz].set(c)
    tn = 1024
    out = pl.pallas_call(
        _adaln_kernel,
        out_shape=jax.ShapeDtypeStruct((depth, bp, e), F32),
        grid=(depth, e // tn),
        in_specs=[
            pl.BlockSpec((bp, d), lambda l, j: (0, 0)),
            pl.BlockSpec((1, d, tn), lambda l, j: (l, 0, j)),
            pl.BlockSpec((1, 1, tn), lambda l, j: (l, 0, j)),
        ],
        out_specs=pl.BlockSpec((1, bp, tn), lambda l, j: (l, 0, j)),
        compiler_params=_cparams(("arbitrary", "arbitrary")),
        name="adaln",
    )(c_pad, ada_w, ada_b.reshape(depth, 1, e))
    return out[:, :bsz]


def _residual_epilogue(h, y, mod, n2g, rw, h_out, hn_out, lg_out):
    h1 = h + mod[2:3] * y
    h_out[...] = h1
    hn2 = _mod_rmsnorm(h1, n2g, mod[4:5], mod[3:4])
    bits = pltpu.bitcast(hn2.astype(BF16).astype(F32), I32)
    half = bits.shape[1] // 2
    hn_out[...] = bits[:, :half] | lax.shift_right_logical(bits[:, half:], 16)
    lg_out[...] = jnp.dot(hn2, rw, precision=HIGHEST, preferred_element_type=F32)


def _gmlp_kernel(h_ref, mod_ref, n1g_ref, win_ref, lng_ref, lnb_ref, wsp_ref, bsp_ref, wout_ref,
                 n2g_ref, rw_ref, h_out, hn_out, lg_out, u_scr, v_scr, s_scr):
    tm = h_ref.shape[0]
    inner = u_scr.shape[1]
    gw = inner // A_GROUPS
    tn = 512
    h = h_ref[...]
    mod = mod_ref[0]
    hn = _mod_rmsnorm(h, n1g_ref[...], mod[1:2], mod[0:1]).astype(BF16)
    for j in range(2 * inner // tn):
        z = _gelu_tanh(jnp.dot(hn, win_ref[:, j * tn:(j + 1) * tn], preferred_element_type=F32))
        if j < inner // tn:
            u_scr[:, j * tn:(j + 1) * tn] = z
        else:
            jj = j - inner // tn
            v_scr[:, jj * tn:(jj + 1) * tn] = z
    vsum = jnp.zeros((tm, 1), F32)
    for j in range(inner // tn):
        vsum = vsum + jnp.sum(v_scr[:, j * tn:(j + 1) * tn], axis=-1, keepdims=True)
    mu = vsum * (1.0 / inner)
    vsq = jnp.zeros((tm, 1), F32)
    for j in range(inner // tn):
        d = v_scr[:, j * tn:(j + 1) * tn] - mu
        vsq = vsq + jnp.sum(d * d, axis=-1, keepdims=True)
    rstd = lax.rsqrt(vsq * (1.0 / inner) + EPS)
    row = lax.broadcasted_iota(I32, (A_CHUNK, A_CHUNK), 0)
    col = lax.broadcasted_iota(I32, (A_CHUNK, A_CHUNK), 1)
    tril = row >= col
    for g in range(A_GROUPS):
        ws = jnp.where(tril, wsp_ref[g], 0.0).astype(BF16)
        bcol = bsp_ref[:, g:g + 1]
        lg = lng_ref[:, g * gw:(g + 1) * gw]
        lb = lnb_ref[:, g * gw:(g + 1) * gw]
        for c in range(tm // A_CHUNK):
            r0, r1 = c * A_CHUNK, (c + 1) * A_CHUNK
            vt = v_scr[r0:r1, g * gw:(g + 1) * gw]
            vn = ((vt - mu[r0:r1]) * rstd[r0:r1]) * lg + lb
            fv = jnp.dot(ws, vn.astype(BF16), preferred_element_type=F32) + bcol
            s_scr[r0:r1, g * gw:(g + 1) * gw] = (u_scr[r0:r1, g * gw:(g + 1) * gw] * fv).astype(BF16)
    y = jnp.dot(s_scr[...], wout_ref[...], preferred_element_type=F32)
    _residual_epilogue(h, y, mod, n2g_ref[...], rw_ref[...], h_out, hn_out, lg_out)


def _const_spec(shape):
    nd = len(shape)
    return pl.BlockSpec(shape, lambda i, _nd=nd: (0,) * _nd, pipeline_mode=pl.Buffered(1))


def _gmlp_layer(h, mod_l, n1g, w_in, ln_g, ln_b, w_sp, b_sp, w_out, n2g, router_w, seq):
    n, d = h.shape
    inner = w_out.shape[0]
    tm = ROW_TILE
    per_b = seq // tm
    ne = router_w.shape[1]
    return pl.pallas_call(
        _gmlp_kernel,
        out_shape=(jax.ShapeDtypeStruct((n, d), F32), jax.ShapeDtypeStruct((n, d // 2), I32),
                   jax.ShapeDtypeStruct((n, ne), F32)),
        grid=(n // tm,),
        in_specs=[
            pl.BlockSpec((tm, d), lambda i: (i, 0)),
            pl.BlockSpec((1, 6, d), lambda i: (i // per_b, 0, 0)),
            _const_spec((1, d)),
            _const_spec((d, 2 * inner)),
            _const_spec((1, inner)),
            _const_spec((1, inner)),
            _const_spec((A_GROUPS, A_CHUNK, A_CHUNK)),
            _const_spec((A_CHUNK, A_GROUPS)),
            _const_spec((inner, d)),
            _const_spec((1, d)),
            _const_spec((d, ne)),
        ],
        out_specs=(pl.BlockSpec((tm, d), lambda i: (i, 0)), pl.BlockSpec((tm, d // 2), lambda i: (i, 0)),
                   pl.BlockSpec((tm, ne), lambda i: (i, 0))),
        scratch_shapes=[pltpu.VMEM((tm, inner), F32), pltpu.VMEM((tm, inner), F32),
                        pltpu.VMEM((tm, inner), BF16)],
        compiler_params=_cparams(("arbitrary",)),
        name="gmlp_layer",
    )(h, mod_l, n1g.reshape(1, d), w_in.astype(BF16), ln_g.reshape(1, inner), ln_b.reshape(1, inner),
      w_sp, b_sp.T, w_out.astype(BF16), n2g.reshape(1, d), router_w)


def _first_max4(rows):
    m = jnp.maximum(jnp.maximum(rows[0], rows[1]), jnp.maximum(rows[2], rows[3]))
    idx = jnp.where(rows[0] == m, 0, jnp.where(rows[1] == m, 1, jnp.where(rows[2] == m, 2, 3)))
    return m, idx.astype(I32)


def _route_kernel(lt_ref, rb_ref, e_ref, g_ref):
    aff = jax.nn.sigmoid(lt_ref[...])
    sel = aff + rb_ref[...]
    neg = jnp.float32(-jnp.inf)
    g_score, g_i1, g_i2 = [], [], []
    for g in range(N_GROUPS):
        rows = [sel[4 * g + k:4 * g + k + 1, :] for k in range(EXPERTS_PER_GROUP)]
        m1, i1 = _first_max4(rows)
        rest = [jnp.where(i1 == k, neg, rows[k]) for k in range(EXPERTS_PER_GROUP)]
        m2, i2 = _first_max4(rest)
        g_score.append(m1 + m2)
        g_i1.append(i1)
        g_i2.append(i2)
    _, grp = _first_max4(g_score)
    l1 = jnp.zeros_like(grp)
    l2 = jnp.zeros_like(grp)
    for g in range(N_GROUPS):
        l1 = jnp.where(grp == g, g_i1[g], l1)
        l2 = jnp.where(grp == g, g_i2[g], l2)
    e1 = grp * EXPERTS_PER_GROUP + l1
    e2 = grp * EXPERTS_PER_GROUP + l2
    a1 = jnp.zeros_like(g_score[0])
    a2 = jnp.zeros_like(g_score[0])
    for e in range(N_EXPERTS):
        ae = aff[e:e + 1, :]
        a1 = jnp.where(e1 == e, ae, a1)
        a2 = jnp.where(e2 == e, ae, a2)
    tot = a1 + a2
    e_ref[0:1, :] = e1
    e_ref[1:2, :] = e2
    g_ref[0:1, :] = a1 / tot
    g_ref[1:2, :] = a2 / tot


def _route(logits, router_b):
    n, ne = logits.shape
    tn = min(2048, n)
    return pl.pallas_call(
        _route_kernel,
        out_shape=(jax.ShapeDtypeStruct((TOP_K, n), I32), jax.ShapeDtypeStruct((TOP_K, n), F32)),
        grid=(n // tn,),
        in_specs=[pl.BlockSpec((ne, tn), lambda i: (0, i)), pl.BlockSpec((ne, 1), lambda i: (0, 0))],
        out_specs=(pl.BlockSpec((TOP_K, tn), lambda i: (0, i)), pl.BlockSpec((TOP_K, tn), lambda i: (0, i))),
        compiler_params=_cparams(("arbitrary",)),
        name="route",
    )(logits.T, router_b.reshape(ne, 1))


def _slot_kernel(e_ref, slot_ref, cnt_ref):
    n = e_ref.shape[1]
    nblk = n // 128
    eid = lax.broadcasted_iota(I32, (N_EXPERTS, 128), 0)
    r = lax.broadcasted_iota(I32, (128, 128), 0)
    c = lax.broadcasted_iota(I32, (128, 128), 1)
    upper = jnp.where(r <= c, 1.0, 0.0).astype(BF16)

    def hits(b):
        o = pl.multiple_of(b * 128, 128)
        t1 = e_ref[0:1, pl.ds(o, 128)] == eid
        t2 = e_ref[1:2, pl.ds(o, 128)] == eid
        return o, t1, t2, jnp.where(t1, 1.0, 0.0) + jnp.where(t2, 1.0, 0.0)

    def count_body(b, acc):
        return acc + hits(b)[3]

    cnt = jnp.sum(lax.fori_loop(0, nblk, count_body, jnp.zeros((N_EXPERTS, 128), F32)),
                  axis=-1, keepdims=True)
    cnt_ref[...] = cnt.astype(I32)
    padded = jnp.floor((cnt + (MOE_ROWS - 1)) * (1.0 / MOE_ROWS)) * MOE_ROWS
    er = lax.broadcasted_iota(I32, (N_EXPERTS, N_EXPERTS), 0)
    ec = lax.broadcasted_iota(I32, (N_EXPERTS, N_EXPERTS), 1)
    padded_row = jnp.sum(jnp.where(er == ec, padded, 0.0), axis=0, keepdims=True)
    start = jnp.sum(jnp.where(ec < er, padded_row, 0.0), axis=-1, keepdims=True)

    def slot_body(b, carry):
        o, t1, t2, t = hits(b)
        incl = jnp.dot(t.astype(BF16), upper, preferred_element_type=F32)
        pos = carry + incl - t
        slot_ref[0:1, pl.ds(o, 128)] = jnp.sum(jnp.where(t1, pos, 0.0), axis=0, keepdims=True).astype(I32)
        slot_ref[1:2, pl.ds(o, 128)] = jnp.sum(jnp.where(t2, pos, 0.0), axis=0, keepdims=True).astype(I32)
        return carry + incl[:, 127:128]

    lax.fori_loop(0, nblk, slot_body, start)


def _slot_tables(experts):
    n = experts.shape[1]
    a = n * TOP_K
    slot, counts = pl.pallas_call(
        _slot_kernel,
        out_shape=(jax.ShapeDtypeStruct((TOP_K, n), I32), jax.ShapeDtypeStruct((N_EXPERTS, 1), I32)),
        compiler_params=pltpu.CompilerParams(vmem_limit_bytes=VMEM_LIMIT),
        name="slots",
    )(experts)
    counts = counts.reshape(N_EXPERTS)
    padded = ((counts + MOE_ROWS - 1) // MOE_ROWS) * MOE_ROWS
    pad_end = jnp.cumsum(padded)
    pad_start = pad_end - padded
    p = a + N_EXPERTS * MOE_ROWS
    n_items = p // MOE_ROWS
    tok = jnp.broadcast_to(jnp.arange(n, dtype=I32)[None, :], (TOP_K, n))
    tok_of_slot = jnp.zeros((p,), I32).at[slot.reshape(a)].set(tok.reshape(a))
    slot_of_assign = slot.T
    item_start = jnp.arange(n_items, dtype=I32) * MOE_ROWS
    n_used = pad_end[-1] // MOE_ROWS
    item_e_raw = jnp.minimum(jnp.searchsorted(pad_end, item_start, side='right'), N_EXPERTS - 1).astype(I32)
    used = item_start < pad_end[-1]
    last_e = item_e_raw[jnp.maximum(n_used - 1, 0)]
    item_e = jnp.where(used, item_e_raw, last_e).astype(I32)
    valid = jnp.clip(counts[item_e_raw] - (item_start - pad_start[item_e_raw]), 0, MOE_ROWS)
    item_rows = jnp.where(used, valid, 0).astype(I32)
    item_blk = jnp.where(used, jnp.arange(n_items, dtype=I32), jnp.maximum(n_used - 1, 0)).astype(I32)
    return tok_of_slot, slot_of_assign, item_e, item_rows, item_blk


def _moe_kernel(ie_ref, ir_ref, ib_ref, x_ref, wg_ref, wu_ref, wd_ref, o_ref, wg_s, wu_s, wd_s, x_s):
    it = pl.program_id(0)
    f = pl.program_id(1)
    nrows = ir_ref[it]
    half = x_ref.shape[1]

    @pl.when(f == 0)
    def _():
        o_ref[...] = jnp.zeros_like(o_ref)
        pk = x_ref[...]
        x_s[:, :half] = pltpu.bitcast(pk & jnp.int32(-65536), F32).astype(BF16)
        x_s[:, half:] = pltpu.bitcast(lax.shift_left(pk, 16), F32).astype(BF16)

    @pl.when(nrows > 0)
    def _():
        wg_s[...] = wg_ref[0, 0].astype(BF16)
        wu_s[...] = wu_ref[0, 0].astype(BF16)
        wd_s[...] = wd_ref[0, 0].astype(BF16)
        nsb = (nrows + MOE_SUB - 1) // MOE_SUB

        def body(sb, carry):
            r0 = pl.multiple_of(sb * MOE_SUB, MOE_SUB)
            x = x_s[pl.ds(r0, MOE_SUB), :]
            g = jnp.dot(x, wg_s[...], preferred_element_type=F32)
            u = jnp.dot(x, wu_s[...], preferred_element_type=F32)
            hmid = ((g * jax.nn.sigmoid(g)) * u).astype(BF16)
            o_ref[pl.ds(r0, MOE_SUB), :] += jnp.dot(hmid, wd_s[...], preferred_element_type=F32)
            return carry

        lax.fori_loop(0, nsb, body, 0)


def _moe_experts(x_sorted, item_e, item_rows, item_blk, w_gate, w_up, w_down, layer):
    p, half = x_sorted.shape
    d = 2 * half
    n_items = p // MOE_ROWS
    dff = w_gate.shape[3]
    nf = dff // MOE_FT
    grid_spec = pltpu.PrefetchScalarGridSpec(
        num_scalar_prefetch=3,
        grid=(n_items, nf),
        in_specs=[
            pl.BlockSpec((MOE_ROWS, half), lambda i, f, ie, ir, ib: (ib[i], 0)),
            pl.BlockSpec((1, 1, d, MOE_FT),
                         lambda i, f, ie, ir, ib: (layer, ie[i], 0, jnp.where(ir[i] > 0, f, nf - 1))),
            pl.BlockSpec((1, 1, d, MOE_FT),
                         lambda i, f, ie, ir, ib: (layer, ie[i], 0, jnp.where(ir[i] > 0, f, nf - 1))),
            pl.BlockSpec((1, 1, MOE_FT, d),
                         lambda i, f, ie, ir, ib: (layer, ie[i], jnp.where(ir[i] > 0, f, nf - 1), 0)),
        ],
        out_specs=pl.BlockSpec((MOE_ROWS, d), lambda i, f, ie, ir, ib: (i, 0)),
        scratch_shapes=[pltpu.VMEM((d, MOE_FT), BF16), pltpu.VMEM((d, MOE_FT), BF16),
                        pltpu.VMEM((MOE_FT, d), BF16), pltpu.VMEM((MOE_ROWS, d), BF16)],
    )
    return pl.pallas_call(
        _moe_kernel,
        out_shape=jax.ShapeDtypeStruct((p, d), F32),
        grid_spec=grid_spec,
        compiler_params=_cparams(("arbitrary", "arbitrary")),
        name="moe_experts",
    )(item_e, item_rows, item_blk, x_sorted, w_gate, w_up, w_down)


def _moe(hn2, logits, router_b, w_gate, w_up, w_down, layer):
    experts, gates = _route(logits, router_b)
    tok_of_slot, slot_of_assign, item_e, item_rows, item_blk = _slot_tables(experts)
    x_sorted = jnp.take(hn2, tok_of_slot, axis=0)
    y_slot = _moe_experts(x_sorted, item_e, item_rows, item_blk, w_gate, w_up, w_down, layer)
    y0 = jnp.take(y_slot, slot_of_assign[:, 0], axis=0)
    y1 = jnp.take(y_slot, slot_of_assign[:, 1], axis=0)
    return y0, y1, gates.T


def _moe_combine(h1, y0, y1, gates, mod):
    return h1 + mod[5:6] * (gates[:, 0:1] * y0 + gates[:, 1:2] * y1)


def _bproj_kernel(h1_ref, y0_ref, y1_ref, gt_ref, modp_ref, mod_ref, n1g_ref, w_ref, kvg_ref, kig_ref,
                  wuk_ref, h_out, qa_out, ckv_out, qi_out, ki_out, wi_out):
    tm = h1_ref.shape[0]
    h = _moe_combine(h1_ref[...], y0_ref[...], y1_ref[...], gt_ref[...], modp_ref[0])
    h_out[...] = h
    mod = mod_ref[0]
    hn = _mod_rmsnorm(h, n1g_ref[...], mod[1:2], mod[0:1]).astype(BF16)
    proj = jnp.dot(hn, w_ref[...], preferred_element_type=F32)
    o1 = B_HEADS * B_HEAD_DIM
    o2 = o1 + B_KV_LATENT
    o3 = o2 + B_IDX_HEADS * B_IDX_DIM
    ckv = proj[:, o1:o2]
    ckv = ckv * lax.rsqrt(jnp.mean(ckv * ckv, axis=-1, keepdims=True) + EPS) * kvg_ref[...]
    ckv_out[...] = ckv.astype(BF16)
    tail = proj[:, o3:o3 + 128]
    ki = tail[:, :B_IDX_DIM]
    ki = ki * lax.rsqrt(jnp.mean(ki * ki, axis=-1, keepdims=True) + EPS) * kig_ref[...]
    ki_out[...] = ki.astype(BF16)
    wi = tail[:, B_IDX_DIM:B_IDX_DIM + B_IDX_HEADS] * (B_IDX_HEADS ** -0.5 * B_IDX_DIM ** -0.5)
    scale = B_HEAD_DIM ** -0.5
    for blk in range(tm // QBLK):
        r0, r1 = blk * QBLK, (blk + 1) * QBLK
        for hh in range(B_HEADS):
            qh = proj[r0:r1, hh * B_HEAD_DIM:(hh + 1) * B_HEAD_DIM].astype(BF16)
            qa = jnp.dot(qh, wuk_ref[hh], preferred_element_type=F32) * scale
            qa_out[blk, hh * QBLK:(hh + 1) * QBLK, :] = qa.astype(BF16)
        for hh in range(B_IDX_HEADS):
            qi_out[blk, hh * QBLK:(hh + 1) * QBLK, :] = proj[r0:r1, o2 + hh * B_IDX_DIM:
                                                            o2 + (hh + 1) * B_IDX_DIM].astype(BF16)
            wi_out[blk, hh * QBLK:(hh + 1) * QBLK, :] = wi[r0:r1, hh:hh + 1]


def _bproj(h1, y0, y1, gates, mod_prev, mod_l, n1g, b_w_in, kv_g, w_uk, kidx_g, seq):
    n, d = h1.shape
    tm = ROW_TILE
    per_b = seq // tm
    nq = n // QBLK
    qpb = tm // QBLK
    o1 = B_HEADS * B_HEAD_DIM
    o2 = o1 + B_KV_LATENT
    o3 = o2 + B_IDX_HEADS * B_IDX_DIM
    wcat = jnp.zeros((d, o3 + 128), F32).at[:, :b_w_in.shape[1]].set(b_w_in).astype(BF16)
    wuk_t = jnp.transpose(w_uk, (1, 2, 0)).astype(BF16)
    row = lambda i: (i, 0)
    modm = lambda i: (i // per_b, 0, 0)
    blk3 = lambda i: (i, 0, 0)
    return pl.pallas_call(
        _bproj_kernel,
        out_shape=(
            jax.ShapeDtypeStruct((n, d), F32),
            jax.ShapeDtypeStruct((nq, B_HEADS * QBLK, B_KV_LATENT), BF16),
            jax.ShapeDtypeStruct((n, B_KV_LATENT), BF16),
            jax.ShapeDtypeStruct((nq, B_IDX_HEADS * QBLK, B_IDX_DIM), BF16),
            jax.ShapeDtypeStruct((n, B_IDX_DIM), BF16),
            jax.ShapeDtypeStruct((nq, B_IDX_HEADS * QBLK, 1), F32),
        ),
        grid=(n // tm,),
        in_specs=[
            pl.BlockSpec((tm, d), row), pl.BlockSpec((tm, d), row), pl.BlockSpec((tm, d), row),
            pl.BlockSpec((tm, TOP_K), row),
            pl.BlockSpec((1, 6, d), modm), pl.BlockSpec((1, 6, d), modm),
            _const_spec((1, d)),
            _const_spec((d, o3 + 128)),
            _const_spec((1, B_KV_LATENT)),
            _const_spec((1, B_IDX_DIM)),
            _const_spec((B_HEADS, B_HEAD_DIM, B_KV_LATENT)),
        ],
        out_specs=(
            pl.BlockSpec((tm, d), row),
            pl.BlockSpec((qpb, B_HEADS * QBLK, B_KV_LATENT), blk3),
            pl.BlockSpec((tm, B_KV_LATENT), row),
            pl.BlockSpec((qpb, B_IDX_HEADS * QBLK, B_IDX_DIM), blk3),
            pl.BlockSpec((tm, B_IDX_DIM), row),
            pl.BlockSpec((qpb, B_IDX_HEADS * QBLK, 1), blk3),
        ),
        compiler_params=_cparams(("arbitrary",)),
        name="dsa_proj",
    )(h1, y0, y1, gates, mod_prev, mod_l, n1g.reshape(1, d), wcat, kv_g.reshape(1, -1),
      kidx_g.reshape(1, -1), wuk_t)


def _t5_bucket(dist):
    n = jnp.maximum(dist, 0)
    exact = REL_BUCKETS // 2
    nf = jnp.maximum(n, 1).astype(F32)
    large = exact + (jnp.log(nf / exact) / math.log(REL_MAX_DIST / exact)
                     * (REL_BUCKETS - exact)).astype(I32)
    large = jnp.minimum(large, REL_BUCKETS - 1)
    return jnp.where(n < exact, n, large)


def _bias_tables(rel_bias):
    t = jnp.arange(QBLK, dtype=I32)[:, None]
    s = jnp.arange(QBLK, dtype=I32)[None, :]
    far = rel_bias[REL_BUCKETS - 1]
    diag = rel_bias[_t5_bucket(t - s)] - far
    prev = rel_bias[_t5_bucket(t - s + QBLK)] - far
    diag = jnp.transpose(diag, (2, 0, 1))
    prev = jnp.transpose(prev, (2, 0, 1))
    zero = jnp.zeros_like(diag)
    cat = lambda a, b: jnp.concatenate([a, b], axis=-1)
    return jnp.stack([cat(zero, zero), cat(zero, prev), cat(prev, diag), cat(diag, zero)])


def _attn_kernel(qa_ref, qi_ref, wi_ref, ckv_ref, ki_ref, bt_ref, o_ref, kbuf, m_scr, l_scr, a_scr, acc_scr,
                 s_scr, p_scr):
    i = pl.program_id(1)
    nt = i // 2 + 1
    t_row = i * QBLK + lax.broadcasted_iota(I32, (QBLK, KTILE), 0)
    lane = lax.broadcasted_iota(I32, (QBLK, KTILE), 1)

    qi = qi_ref[0]
    wcol = wi_ref[0]

    def score_body(j, carry):
        k0 = pl.multiple_of(j * KTILE, KTILE)
        kt = ki_ref[0, pl.ds(k0, KTILE), :]
        p = lax.dot_general(qi, kt, (((1,), (1,)), ((), ())), preferred_element_type=F32)
        p = jnp.maximum(p, 0.0) * wcol
        sc = p[0:QBLK]
        for hh in range(1, B_IDX_HEADS):
            sc = sc + p[hh * QBLK:(hh + 1) * QBLK]
        bits = pltpu.bitcast(sc + 0.0, I32)
        key = jnp.where(bits < 0, bits ^ jnp.int32(0x7FFFFFFF), bits)
        key = jnp.where(k0 + lane <= t_row, key, jnp.int32(INT_MIN))
        kbuf[:, pl.ds(k0, KTILE)] = key
        return carry

    lax.fori_loop(0, nt, score_body, 0)

    def count_ge(cand):
        def body(j, acc):
            k0 = pl.multiple_of(j * KTILE, KTILE)
            return acc + jnp.where(kbuf[:, pl.ds(k0, KTILE)] >= cand, 1, 0)
        acc = lax.fori_loop(0, nt, body, jnp.zeros((QBLK, KTILE), I32))
        return jnp.sum(acc, axis=-1, keepdims=True)

    def bit_cond(c):
        return (c[0] < 32) & (c[3] == 0)

    def bit_body(c):
        b, lo, n_ge, _ = c
        cand = lo + lax.shift_left(jnp.int32(1), 31 - b)
        cnt = count_ge(cand)
        take = cnt >= B_TOPK_MAX
        n_ge = jnp.where(take, cnt, n_ge)
        done = (jnp.max(n_ge) == B_TOPK_MAX).astype(I32)
        return b + 1, jnp.where(take, cand, lo), n_ge, done

    _, tau, n_ge, _ = lax.while_loop(
        bit_cond, bit_body,
        (jnp.int32(0), jnp.full((QBLK, 1), INT_MIN, I32), jnp.full((QBLK, 1), nt * KTILE, I32),
         (i < 2).astype(I32)))
    tau = jnp.maximum(tau, jnp.int32(INT_MIN + 1))
    excess = (i >= 2) & (jnp.max(n_ge) > B_TOPK_MAX)

    @pl.when(excess)
    def _():
        n_gt = count_ge(tau + 1)
        need = B_TOPK_MAX - n_gt

        def count_eq_before(pos):
            def body(j, acc):
                k0 = pl.multiple_of(j * KTILE, KTILE)
                hit = (kbuf[:, pl.ds(k0, KTILE)] == tau) & (k0 + lane < pos)
                return acc + jnp.where(hit, 1, 0)
            acc = lax.fori_loop(0, nt, body, jnp.zeros((QBLK, KTILE), I32))
            return jnp.sum(acc, axis=-1, keepdims=True)

        def pos_body(b, pos):
            cand = pos + lax.shift_left(jnp.int32(1), 12 - b)
            return jnp.where(count_eq_before(cand) < need, cand, pos)

        pos = lax.fori_loop(0, 13, pos_body, jnp.zeros((QBLK, 1), I32))

        def drop_body(j, carry):
            k0 = pl.multiple_of(j * KTILE, KTILE)
            kk = kbuf[:, pl.ds(k0, KTILE)]
            drop = (kk == tau) & (k0 + lane > pos) & (n_ge > B_TOPK_MAX)
            kbuf[:, pl.ds(k0, KTILE)] = jnp.where(drop, jnp.int32(INT_MIN), kk)
            return carry

        lax.fori_loop(0, nt, drop_body, 0)

    m_scr[...] = jnp.full_like(m_scr, NEG_BIG)
    l_scr[...] = jnp.zeros_like(l_scr)
    acc_scr[...] = jnp.zeros_like(acc_scr)

    def attend(j, bias_idx):
        k0 = pl.multiple_of(j * KTILE, KTILE)
        kv = ckv_ref[0, pl.ds(k0, KTILE), :]
        s_scr[...] = lax.dot_general(qa_ref[0], kv, (((1,), (1,)), ((), ())), preferred_element_type=F32)
        madd = jnp.where(kbuf[:, pl.ds(k0, KTILE)] >= tau, 0.0, NEG_BIG)
        for hh in range(B_HEADS):
            r0, r1 = hh * QBLK, (hh + 1) * QBLK
            s = s_scr[r0:r1, :] + madd
            if bias_idx is not None:
                s = s + bt_ref[bias_idx, hh]
            m_old = m_scr[r0:r1, :]
            m_new = jnp.maximum(m_old, jnp.max(s, axis=-1, keepdims=True))
            p = jnp.exp(s - pltpu.repeat(m_new, KTILE // 128, axis=1))
            alpha = jnp.exp(m_old - m_new)
            l_scr[r0:r1, :] = alpha * l_scr[r0:r1, :] + jnp.sum(p, axis=-1, keepdims=True)
            m_scr[r0:r1, :] = m_new
            a_scr[r0:r1, :] = alpha
            p_scr[r0:r1, :] = p.astype(BF16)
        pv = jnp.dot(p_scr[...], kv, preferred_element_type=F32)
        acc_scr[...] = pltpu.repeat(a_scr[...], B_KV_LATENT // 128, axis=1) * acc_scr[...] + pv

    def far_body(j, carry):
        attend(j, None)
        return carry

    lax.fori_loop(0, jnp.maximum(nt - 2, 0), far_body, 0)
    odd = i % 2

    @pl.when(nt >= 2)
    def _():
        attend(nt - 2, 1 - odd)

    attend(nt - 1, 3 - odd)
    inv_l = 1.0 / l_scr[...]
    o_ref[0] = (acc_scr[...] * pltpu.repeat(inv_l, B_KV_LATENT // 128, axis=1)).astype(BF16)


def _attention(qa, qi, wi, ckv, ki, btab, bsz, seq):
    nqb = seq // QBLK
    gq = lambda b, i: (b * nqb + i, 0, 0)
    gb = lambda b, i: (b, 0, 0)
    return pl.pallas_call(
        _attn_kernel,
        out_shape=jax.ShapeDtypeStruct(qa.shape, BF16),
        grid=(bsz, nqb),
        in_specs=[
            pl.BlockSpec((1, B_HEADS * QBLK, B_KV_LATENT), gq),
            pl.BlockSpec((1, B_IDX_HEADS * QBLK, B_IDX_DIM), gq),
            pl.BlockSpec((1, B_IDX_HEADS * QBLK, 1), gq),
            pl.BlockSpec((1, seq, B_KV_LATENT), gb),
            pl.BlockSpec((1, seq, B_IDX_DIM), gb),
            pl.BlockSpec((4, B_HEADS, QBLK, KTILE), lambda b, i: (0, 0, 0, 0), pipeline_mode=pl.Buffered(1)),
        ],
        out_specs=pl.BlockSpec((1, B_HEADS * QBLK, B_KV_LATENT), gq),
        scratch_shapes=[
            pltpu.VMEM((QBLK, seq), I32),
            pltpu.VMEM((B_HEADS * QBLK, 128), F32),
            pltpu.VMEM((B_HEADS * QBLK, 128), F32),
            pltpu.VMEM((B_HEADS * QBLK, 128), F32),
            pltpu.VMEM((B_HEADS * QBLK, B_KV_LATENT), F32),
            pltpu.VMEM((B_HEADS * QBLK, KTILE), F32),
            pltpu.VMEM((B_HEADS * QBLK, KTILE), BF16),
        ],
        compiler_params=_cparams(("arbitrary", "arbitrary")),
        name="dsa_attention",
    )(qa, qi, wi, ckv.reshape(bsz, seq, -1), ki.reshape(bsz, seq, -1), btab)


def _bout_kernel(h_ref, ol_ref, mod_ref, wuv_ref, wout_ref, n2g_ref, rw_ref, h_out, hn_out, lg_out, o_scr):
    mod = mod_ref[0]
    for blk in range(ol_ref.shape[0]):
        for hh in range(B_HEADS):
            oh = jnp.dot(ol_ref[blk, hh * QBLK:(hh + 1) * QBLK, :], wuv_ref[hh], preferred_element_type=F32)
            o_scr[blk * QBLK:(blk + 1) * QBLK, hh * B_V_DIM:(hh + 1) * B_V_DIM] = oh.astype(BF16)
    y = jnp.dot(o_scr[...], wout_ref[...], preferred_element_type=F32)
    _residual_epilogue(h_ref[...], y, mod, n2g_ref[...], rw_ref[...], h_out, hn_out, lg_out)


def _bout(h, o_lat, mod_l, w_uv, w_out, n2g, router_w, seq):
    n, d = h.shape
    tm = ROW_TILE
    per_b = seq // tm
    qpb = tm // QBLK
    ne = router_w.shape[1]
    wuv_t = jnp.transpose(w_uv, (1, 0, 2)).astype(BF16)
    row = lambda i: (i, 0)
    return pl.pallas_call(
        _bout_kernel,
        out_shape=(jax.ShapeDtypeStruct((n, d), F32), jax.ShapeDtypeStruct((n, d // 2), I32),
                   jax.ShapeDtypeStruct((n, ne), F32)),
        grid=(n // tm,),
        in_specs=[
            pl.BlockSpec((tm, d), row),
            pl.BlockSpec((qpb, B_HEADS * QBLK, B_KV_LATENT), lambda i: (i, 0, 0)),
            pl.BlockSpec((1, 6, d), lambda i: (i // per_b, 0, 0)),
            _const_spec((B_HEADS, B_KV_LATENT, B_V_DIM)),
            _const_spec((B_HEADS * B_V_DIM, d)),
            _const_spec((1, d)),
            _const_spec((d, ne)),
        ],
        out_specs=(pl.BlockSpec((tm, d), row), pl.BlockSpec((tm, d // 2), row), pl.BlockSpec((tm, ne), row)),
        scratch_shapes=[pltpu.VMEM((tm, B_HEADS * B_V_DIM), BF16)],
        compiler_params=_cparams(("arbitrary",)),
        name="dsa_out",
    )(h, o_lat, mod_l, wuv_t, w_out.astype(BF16), n2g.reshape(1, d), router_w)


def _final_kernel(h1_ref, y0_ref, y1_ref, gt_ref, mod_ref, g_ref, o_ref):
    h = _moe_combine(h1_ref[...], y0_ref[...], y1_ref[...], gt_ref[...], mod_ref[0])
    o_ref[...] = h * lax.rsqrt(jnp.mean(h * h, axis=-1, keepdims=True) + EPS) * g_ref[...]


def _final(h1, y0, y1, gates, mod_l, final_g, seq):
    n, d = h1.shape
    tm = 512
    per_b = seq // tm
    row = lambda i: (i, 0)
    return pl.pallas_call(
        _final_kernel,
        out_shape=jax.ShapeDtypeStruct((n, d), F32),
        grid=(n // tm,),
        in_specs=[pl.BlockSpec((tm, d), row), pl.BlockSpec((tm, d), row), pl.BlockSpec((tm, d), row),
                  pl.BlockSpec((tm, TOP_K), row), pl.BlockSpec((1, 6, d), lambda i: (i // per_b, 0, 0)),
                  _const_spec((1, d))],
        out_specs=pl.BlockSpec((tm, d), row),
        compiler_params=_cparams(("arbitrary",)),
        name="final_norm",
    )(h1, y0, y1, gates, mod_l, final_g.reshape(1, d))


def kernel(x, c, ada_w, ada_b, norm1_g, norm2_g, a_w_in, a_ln_g, a_ln_b, a_w_sp, a_b_sp, a_w_out, b_w_in,
           b_kv_norm_g, b_w_uk, b_w_uv, b_kidx_g, b_w_out, rel_bias, router_w, router_b, moe_w_gate,
           moe_w_up, moe_w_down, final_g):
    bsz, seq, d = x.shape
    n = bsz * seq
    mod = _adaln(c, ada_w, ada_b).reshape(ada_w.shape[0], bsz, 6, d)
    h = x.reshape(n, d)

    h1, hn2, logits = _gmlp_layer(h, mod[0], norm1_g[0], a_w_in[0], a_ln_g[0], a_ln_b[0], a_w_sp[0],
                                  a_b_sp[0], a_w_out[0], norm2_g[0], router_w, seq)
    y0, y1, gates = _moe(hn2, logits, router_b, moe_w_gate, moe_w_up, moe_w_down, 0)

    h, qa, ckv, qi, ki, wi = _bproj(h1, y0, y1, gates, mod[0], mod[1], norm1_g[1], b_w_in[0],
                                    b_kv_norm_g[0], b_w_uk[0], b_kidx_g[0], seq)
    o_lat = _attention(qa, qi, wi, ckv, ki, _bias_tables(rel_bias), bsz, seq)
    h1, hn2, logits = _bout(h, o_lat, mod[1], b_w_uv[0], b_w_out[0], norm2_g[1], router_w, seq)
    y0, y1, gates = _moe(hn2, logits, router_b, moe_w_gate, moe_w_up, moe_w_down, 1)

    out = _final(h1, y0, y1, gates, mod[1], final_g, seq)
    return out.reshape(bsz, seq, d)
```

```python
import functools
import math

import jax
import jax.numpy as jnp
from jax import lax
from jax.experimental import pallas as pl
from jax.experimental.pallas import tpu as pltpu

F32 = jnp.float32
BF16 = jnp.bfloat16
I32 = jnp.int32
HIGHEST = lax.Precision.HIGHEST

EPS = 1e-6
A_CHUNK = 128
A_GROUPS = 8
B_HEADS = 16
B_HEAD_DIM = 64
B_V_DIM = 64
B_KV_LATENT = 256
B_IDX_HEADS = 8
B_IDX_DIM = 64
B_TOPK_MAX = 256
QBLK = 128
KTILE = 256
REL_BUCKETS = 32
REL_MAX_DIST = 128
N_EXPERTS = 16
N_GROUPS = 4
EXPERTS_PER_GROUP = 4
TOP_K = 2
MOE_ROWS = 1024
MOE_SUB = 256
MOE_FT = 512

ROW_TILE = 256
VMEM_LIMIT = 56 * 1024 * 1024

INT_MIN = -2 ** 31
NEG_BIG = -1e30


def _cparams(sem):
    return pltpu.CompilerParams(dimension_semantics=sem, vmem_limit_bytes=VMEM_LIMIT)


def _mod_rmsnorm(h, g, scale, shift):
    ms = jnp.mean(h * h, axis=-1, keepdims=True)
    return (h * lax.rsqrt(ms + EPS) * g) * (1.0 + scale) + shift


def _gelu_tanh(x):
    c = math.sqrt(2.0 / math.pi)
    return 0.5 * x * (1.0 + jnp.tanh(c * (x + 0.044715 * (x * x * x))))


def _adaln_kernel(c_ref, w_ref, b_ref, o_ref):
    c = c_ref[...]
    sc = c * jax.nn.sigmoid(c)
    o_ref[0] = jnp.dot(sc, w_ref[0], precision=HIGHEST, preferred_element_type=F32) + b_ref[0]


def _adaln(c, ada_w, ada_b):
    depth, d, e = ada_w.shape
    bsz = c.shape[0]
    bp = 8
    c_pad = jnp.zeros((bp, d), F32).at[:bsz].set(c)
    tn = 1024
    out = pl.pallas_call(
        _adaln_kernel,
        out_shape=jax.ShapeDtypeStruct((depth, bp, e), F32),
        grid=(depth, e // tn),
        in_specs=[
            pl.BlockSpec((bp, d), lambda l, j: (0, 0)),
            pl.BlockSpec((1, d, tn), lambda l, j: (l, 0, j)),
            pl.BlockSpec((1, 1, tn), lambda l, j: (l, 0, j)),
        ],
        out_specs=pl.BlockSpec((1, bp, tn), lambda l, j: (l, 0, j)),
        compiler_params=_cparams(("arbitrary", "arbitrary")),
        name="adaln",
    )(c_pad, ada_w, ada_b.reshape(depth, 1, e))
    return out[:, :bsz]


def _residual_epilogue(h, y, mod, n2g, rw, h_out, hn_out, lg_out):
    h1 = h + mod[2:3] * y
    h_out[...] = h1
    hn2 = _mod_rmsnorm(h1, n2g, mod[4:5], mod[3:4])
    bits = pltpu.bitcast(hn2.astype(BF16).astype(F32), I32)
    half = bits.shape[1] // 2
    hn_out[...] = bits[:, :half] | lax.shift_right_logical(bits[:, half:], 16)
    lg_out[...] = jnp.dot(hn2, rw, precision=HIGHEST, preferred_element_type=F32)


def _gmlp_kernel(h_ref, mod_ref, n1g_ref, win_ref, lng_ref, lnb_ref, wsp_ref, bsp_ref, wout_ref,
                 n2g_ref, rw_ref, h_out, hn_out, lg_out, u_scr, v_scr, s_scr):
    tm = h_ref.shape[0]
    inner = u_scr.shape[1]
    gw = inner // A_GROUPS
    tn = 512
    h = h_ref[...]
    mod = mod_ref[0]
    hn = _mod_rmsnorm(h, n1g_ref[...], mod[1:2], mod[0:1]).astype(BF16)
    for j in range(2 * inner // tn):
        z = _gelu_tanh(jnp.dot(hn, win_ref[:, j * tn:(j + 1) * tn], preferred_element_type=F32))
        if j < inner // tn:
            u_scr[:, j * tn:(j + 1) * tn] = z
        else:
            jj = j - inner // tn
            v_scr[:, jj * tn:(jj + 1) * tn] = z
    vsum = jnp.zeros((tm, 1), F32)
    for j in range(inner // tn):
        vsum = vsum + jnp.sum(v_scr[:, j * tn:(j + 1) * tn], axis=-1, keepdims=True)
    mu = vsum * (1.0 / inner)
    vsq = jnp.zeros((tm, 1), F32)
    for j in range(inner // tn):
        d = v_scr[:, j * tn:(j + 1) * tn] - mu
        vsq = vsq + jnp.sum(d * d, axis=-1, keepdims=True)
    rstd = lax.rsqrt(vsq * (1.0 / inner) + EPS)
    row = lax.broadcasted_iota(I32, (A_CHUNK, A_CHUNK), 0)
    col = lax.broadcasted_iota(I32, (A_CHUNK, A_CHUNK), 1)
    tril = row >= col
    for g in range(A_GROUPS):
        ws = jnp.where(tril, wsp_ref[g], 0.0).astype(BF16)
        bcol = bsp_ref[:, g:g + 1]
        lg = lng_ref[:, g * gw:(g + 1) * gw]
        lb = lnb_ref[:, g * gw:(g + 1) * gw]
        for c in range(tm // A_CHUNK):
            r0, r1 = c * A_CHUNK, (c + 1) * A_CHUNK
            vt = v_scr[r0:r1, g * gw:(g + 1) * gw]
            vn = ((vt - mu[r0:r1]) * rstd[r0:r1]) * lg + lb
            fv = jnp.dot(ws, vn.astype(BF16), preferred_element_type=F32) + bcol
            s_scr[r0:r1, g * gw:(g + 1) * gw] = (u_scr[r0:r1, g * gw:(g + 1) * gw] * fv).astype(BF16)
    y = jnp.dot(s_scr[...], wout_ref[...], preferred_element_type=F32)
    _residual_epilogue(h, y, mod, n2g_ref[...], rw_ref[...], h_out, hn_out, lg_out)


def _const_spec(shape):
    nd = len(shape)
    return pl.BlockSpec(shape, lambda i, _nd=nd: (0,) * _nd, pipeline_mode=pl.Buffered(1))


def _gmlp_layer(h, mod_l, n1g, w_in, ln_g, ln_b, w_sp, b_sp, w_out, n2g, router_w, seq):
    n, d = h.shape
    inner = w_out.shape[0]
    tm = ROW_TILE
    per_b = seq // tm
    ne = router_w.shape[1]
    return pl.pallas_call(
        _gmlp_kernel,
        out_shape=(jax.ShapeDtypeStruct((n, d), F32), jax.ShapeDtypeStruct((n, d // 2), I32),
                   jax.ShapeDtypeStruct((n, ne), F32)),
        grid=(n // tm,),
        in_specs=[
            pl.BlockSpec((tm, d), lambda i: (i, 0)),
            pl.BlockSpec((1, 6, d), lambda i: (i // per_b, 0, 0)),
            _const_spec((1, d)),
            _const_spec((d, 2 * inner)),
            _const_spec((1, inner)),
            _const_spec((1, inner)),
            _const_spec((A_GROUPS, A_CHUNK, A_CHUNK)),
            _const_spec((A_CHUNK, A_GROUPS)),
            _const_spec((inner, d)),
            _const_spec((1, d)),
            _const_spec((d, ne)),
        ],
        out_specs=(pl.BlockSpec((tm, d), lambda i: (i, 0)), pl.BlockSpec((tm, d // 2), lambda i: (i, 0)),
                   pl.BlockSpec((tm, ne), lambda i: (i, 0))),
        scratch_shapes=[pltpu.VMEM((tm, inner), F32), pltpu.VMEM((tm, inner), F32),
                        pltpu.VMEM((tm, inner), BF16)],
        compiler_params=_cparams(("arbitrary",)),
        name="gmlp_layer",
    )(h, mod_l, n1g.reshape(1, d), w_in.astype(BF16), ln_g.reshape(1, inner), ln_b.reshape(1, inner),
      w_sp, b_sp.T, w_out.astype(BF16), n2g.reshape(1, d), router_w)


def _first_max4(rows):
    m = jnp.maximum(jnp.maximum(rows[0], rows[1]), jnp.maximum(rows[2], rows[3]))
    idx = jnp.where(rows[0] == m, 0, jnp.where(rows[1] == m, 1, jnp.where(rows[2] == m, 2, 3)))
    return m, idx.astype(I32)


def _route_kernel(lt_ref, rb_ref, e_ref, g_ref):
    aff = jax.nn.sigmoid(lt_ref[...])
    sel = aff + rb_ref[...]
    neg = jnp.float32(-jnp.inf)
    g_score, g_i1, g_i2 = [], [], []
    for g in range(N_GROUPS):
        rows = [sel[4 * g + k:4 * g + k + 1, :] for k in range(EXPERTS_PER_GROUP)]
        m1, i1 = _first_max4(rows)
        rest = [jnp.where(i1 == k, neg, rows[k]) for k in range(EXPERTS_PER_GROUP)]
        m2, i2 = _first_max4(rest)
        g_score.append(m1 + m2)
        g_i1.append(i1)
        g_i2.append(i2)
    _, grp = _first_max4(g_score)
    l1 = jnp.zeros_like(grp)
    l2 = jnp.zeros_like(grp)
    for g in range(N_GROUPS):
        l1 = jnp.where(grp == g, g_i1[g], l1)
        l2 = jnp.where(grp == g, g_i2[g], l2)
    e1 = grp * EXPERTS_PER_GROUP + l1
    e2 = grp * EXPERTS_PER_GROUP + l2
    a1 = jnp.zeros_like(g_score[0])
    a2 = jnp.zeros_like(g_score[0])
    for e in range(N_EXPERTS):
        ae = aff[e:e + 1, :]
        a1 = jnp.where(e1 == e, ae, a1)
        a2 = jnp.where(e2 == e, ae, a2)
    tot = a1 + a2
    e_ref[0:1, :] = e1
    e_ref[1:2, :] = e2
    g_ref[0:1, :] = a1 / tot
    g_ref[1:2, :] = a2 / tot


def _route(logits, router_b):
    n, ne = logits.shape
    tn = min(2048, n)
    return pl.pallas_call(
        _route_kernel,
        out_shape=(jax.ShapeDtypeStruct((TOP_K, n), I32), jax.ShapeDtypeStruct((TOP_K, n), F32)),
        grid=(n // tn,),
        in_specs=[pl.BlockSpec((ne, tn), lambda i: (0, i)), pl.BlockSpec((ne, 1), lambda i: (0, 0))],
        out_specs=(pl.BlockSpec((TOP_K, tn), lambda i: (0, i)), pl.BlockSpec((TOP_K, tn), lambda i: (0, i))),
        compiler_params=_cparams(("arbitrary",)),
        name="route",
    )(logits.T, router_b.reshape(ne, 1))


def _slot_kernel(e_ref, slot_ref, cnt_ref):
    n = e_ref.shape[1]
    nblk = n // 128
    eid = lax.broadcasted_iota(I32, (N_EXPERTS, 128), 0)
    r = lax.broadcasted_iota(I32, (128, 128), 0)
    c = lax.broadcasted_iota(I32, (128, 128), 1)
    upper = jnp.where(r <= c, 1.0, 0.0).astype(BF16)

    def hits(b):
        o = pl.multiple_of(b * 128, 128)
        t1 = e_ref[0:1, pl.ds(o, 128)] == eid
        t2 = e_ref[1:2, pl.ds(o, 128)] == eid
        return o, t1, t2, jnp.where(t1, 1.0, 0.0) + jnp.where(t2, 1.0, 0.0)

    def count_body(b, acc):
        return acc + hits(b)[3]

    cnt = jnp.sum(lax.fori_loop(0, nblk, count_body, jnp.zeros((N_EXPERTS, 128), F32)),
                  axis=-1, keepdims=True)
    cnt_ref[...] = cnt.astype(I32)
    padded = jnp.floor((cnt + (MOE_ROWS - 1)) * (1.0 / MOE_ROWS)) * MOE_ROWS
    er = lax.broadcasted_iota(I32, (N_EXPERTS, N_EXPERTS), 0)
    ec = lax.broadcasted_iota(I32, (N_EXPERTS, N_EXPERTS), 1)
    padded_row = jnp.sum(jnp.where(er == ec, padded, 0.0), axis=0, keepdims=True)
    start = jnp.sum(jnp.where(ec < er, padded_row, 0.0), axis=-1, keepdims=True)

    def slot_body(b, carry):
        o, t1, t2, t = hits(b)
        incl = jnp.dot(t.astype(BF16), upper, preferred_element_type=F32)
        pos = carry + incl - t
        slot_ref[0:1, pl.ds(o, 128)] = jnp.sum(jnp.where(t1, pos, 0.0), axis=0, keepdims=True).astype(I32)
        slot_ref[1:2, pl.ds(o, 128)] = jnp.sum(jnp.where(t2, pos, 0.0), axis=0, keepdims=True).astype(I32)
        return carry + incl[:, 127:128]

    lax.fori_loop(0, nblk, slot_body, start)


def _slot_tables(experts):
    n = experts.shape[1]
    a = n * TOP_K
    slot, counts = pl.pallas_call(
        _slot_kernel,
        out_shape=(jax.ShapeDtypeStruct((TOP_K, n), I32), jax.ShapeDtypeStruct((N_EXPERTS, 1), I32)),
        compiler_params=pltpu.CompilerParams(vmem_limit_bytes=VMEM_LIMIT),
        name="slots",
    )(experts)
    counts = counts.reshape(N_EXPERTS)
    padded = ((counts + MOE_ROWS - 1) // MOE_ROWS) * MOE_ROWS
    pad_end = jnp.cumsum(padded)
    pad_start = pad_end - padded
    p = a + N_EXPERTS * MOE_ROWS
    n_items = p // MOE_ROWS
    tok = jnp.broadcast_to(jnp.arange(n, dtype=I32)[None, :], (TOP_K, n))
    tok_of_slot = jnp.zeros((p,), I32).at[slot.reshape(a)].set(tok.reshape(a))
    slot_of_assign = slot.T
    item_start = jnp.arange(n_items, dtype=I32) * MOE_ROWS
    n_used = pad_end[-1] // MOE_ROWS
    item_e_raw = jnp.minimum(jnp.searchsorted(pad_end, item_start, side='right'), N_EXPERTS - 1).astype(I32)
    used = item_start < pad_end[-1]
    last_e = item_e_raw[jnp.maximum(n_used - 1, 0)]
    item_e = jnp.where(used, item_e_raw, last_e).astype(I32)
    valid = jnp.clip(counts[item_e_raw] - (item_start - pad_start[item_e_raw]), 0, MOE_ROWS)
    item_rows = jnp.where(used, valid, 0).astype(I32)
    item_blk = jnp.where(used, jnp.arange(n_items, dtype=I32), jnp.maximum(n_used - 1, 0)).astype(I32)
    return tok_of_slot, slot_of_assign, item_e, item_rows, item_blk


def _moe_kernel(ie_ref, ir_ref, ib_ref, x_ref, wg_ref, wu_ref, wd_ref, o_ref, wg_s, wu_s, wd_s, x_s):
    it = pl.program_id(0)
    f = pl.program_id(1)
    nrows = ir_ref[it]
    half = x_ref.shape[1]

    @pl.when(f == 0)
    def _():
        o_ref[...] = jnp.zeros_like(o_ref)
        pk = x_ref[...]
        x_s[:, :half] = pltpu.bitcast(pk & jnp.int32(-65536), F32).astype(BF16)
        x_s[:, half:] = pltpu.bitcast(lax.shift_left(pk, 16), F32).astype(BF16)

    @pl.when(nrows > 0)
    def _():
        wg_s[...] = wg_ref[0, 0].astype(BF16)
        wu_s[...] = wu_ref[0, 0].astype(BF16)
        wd_s[...] = wd_ref[0, 0].astype(BF16)
        nsb = (nrows + MOE_SUB - 1) // MOE_SUB

        def sub_block(sb):
            r0 = pl.multiple_of(sb * MOE_SUB, MOE_SUB)
            x = x_s[pl.ds(r0, MOE_SUB), :]
            g = jnp.dot(x, wg_s[...], preferred_element_type=F32)
            u = jnp.dot(x, wu_s[...], preferred_element_type=F32)
            hmid = ((g * jax.nn.sigmoid(g)) * u).astype(BF16)
            o_ref[pl.ds(r0, MOE_SUB), :] += jnp.dot(hmid, wd_s[...], preferred_element_type=F32)

        def pair_body(pi, carry):
            sub_block(2 * pi)
            sub_block(2 * pi + 1)
            return carry

        lax.fori_loop(0, nsb // 2, pair_body, 0)

        @pl.when(nsb % 2 == 1)
        def _():
            sub_block(nsb - 1)


def _moe_experts(x_sorted, item_e, item_rows, item_blk, w_gate, w_up, w_down, layer):
    p, half = x_sorted.shape
    d = 2 * half
    n_items = p // MOE_ROWS
    dff = w_gate.shape[3]
    nf = dff // MOE_FT
    grid_spec = pltpu.PrefetchScalarGridSpec(
        num_scalar_prefetch=3,
        grid=(n_items, nf),
        in_specs=[
            pl.BlockSpec((MOE_ROWS, half), lambda i, f, ie, ir, ib: (ib[i], 0)),
            pl.BlockSpec((1, 1, d, MOE_FT),
                         lambda i, f, ie, ir, ib: (layer, ie[i], 0, jnp.where(ir[i] > 0, f, nf - 1))),
            pl.BlockSpec((1, 1, d, MOE_FT),
                         lambda i, f, ie, ir, ib: (layer, ie[i], 0, jnp.where(ir[i] > 0, f, nf - 1))),
            pl.BlockSpec((1, 1, MOE_FT, d),
                         lambda i, f, ie, ir, ib: (layer, ie[i], jnp.where(ir[i] > 0, f, nf - 1), 0)),
        ],
        out_specs=pl.BlockSpec((MOE_ROWS, d), lambda i, f, ie, ir, ib: (i, 0)),
        scratch_shapes=[pltpu.VMEM((d, MOE_FT), BF16), pltpu.VMEM((d, MOE_FT), BF16),
                        pltpu.VMEM((MOE_FT, d), BF16), pltpu.VMEM((MOE_ROWS, d), BF16)],
    )
    return pl.pallas_call(
        _moe_kernel,
        out_shape=jax.ShapeDtypeStruct((p, d), F32),
        grid_spec=grid_spec,
        compiler_params=_cparams(("arbitrary", "arbitrary")),
        name="moe_experts",
    )(item_e, item_rows, item_blk, x_sorted, w_gate, w_up, w_down)


def _moe(hn2, logits, router_b, w_gate, w_up, w_down, layer):
    experts, gates = _route(logits, router_b)
    tok_of_slot, slot_of_assign, item_e, item_rows, item_blk = _slot_tables(experts)
    x_sorted = hn2.at[tok_of_slot].get(mode='promise_in_bounds')
    y_slot = _moe_experts(x_sorted, item_e, item_rows, item_blk, w_gate, w_up, w_down, layer)
    y0 = y_slot.at[slot_of_assign[:, 0]].get(mode='promise_in_bounds')
    y1 = y_slot.at[slot_of_assign[:, 1]].get(mode='promise_in_bounds')
    return y0, y1, gates.T


def _moe_combine(h1, y0, y1, gates, mod):
    return h1 + mod[5:6] * (gates[:, 0:1] * y0 + gates[:, 1:2] * y1)


def _bproj_kernel(h1_ref, y0_ref, y1_ref, gt_ref, modp_ref, mod_ref, n1g_ref, w_ref, kvg_ref, kig_ref,
                  wuk_ref, h_out, qa_out, ckv_out, qi_out, ki_out, wi_out):
    tm = h1_ref.shape[0]
    h = _moe_combine(h1_ref[...], y0_ref[...], y1_ref[...], gt_ref[...], modp_ref[0])
    h_out[...] = h
    mod = mod_ref[0]
    hn = _mod_rmsnorm(h, n1g_ref[...], mod[1:2], mod[0:1]).astype(BF16)
    proj = jnp.dot(hn, w_ref[...], preferred_element_type=F32)
    o1 = B_HEADS * B_HEAD_DIM
    o2 = o1 + B_KV_LATENT
    o3 = o2 + B_IDX_HEADS * B_IDX_DIM
    ckv = proj[:, o1:o2]
    ckv = ckv * lax.rsqrt(jnp.mean(ckv * ckv, axis=-1, keepdims=True) + EPS) * kvg_ref[...]
    ckv_out[...] = ckv.astype(BF16)
    tail = proj[:, o3:o3 + 128]
    ki = tail[:, :B_IDX_DIM]
    ki = ki * lax.rsqrt(jnp.mean(ki * ki, axis=-1, keepdims=True) + EPS) * kig_ref[...]
    ki_out[...] = ki.astype(BF16)
    wi = tail[:, B_IDX_DIM:B_IDX_DIM + B_IDX_HEADS] * (B_IDX_HEADS ** -0.5 * B_IDX_DIM ** -0.5)
    scale = B_HEAD_DIM ** -0.5
    for blk in range(tm // QBLK):
        r0, r1 = blk * QBLK, (blk + 1) * QBLK
        for hh in range(B_HEADS):
            qh = proj[r0:r1, hh * B_HEAD_DIM:(hh + 1) * B_HEAD_DIM].astype(BF16)
            qa = jnp.dot(qh, wuk_ref[hh], preferred_element_type=F32) * scale
            qa_out[blk, hh * QBLK:(hh + 1) * QBLK, :] = qa.astype(BF16)
        for hh in range(B_IDX_HEADS):
            qi_out[blk, hh * QBLK:(hh + 1) * QBLK, :] = proj[r0:r1, o2 + hh * B_IDX_DIM:
                                                            o2 + (hh + 1) * B_IDX_DIM].astype(BF16)
            wi_out[blk, hh * QBLK:(hh + 1) * QBLK, :] = wi[r0:r1, hh:hh + 1]


def _bproj(h1, y0, y1, gates, mod_prev, mod_l, n1g, b_w_in, kv_g, w_uk, kidx_g, seq):
    n, d = h1.shape
    tm = ROW_TILE
    per_b = seq // tm
    nq = n // QBLK
    qpb = tm // QBLK
    o1 = B_HEADS * B_HEAD_DIM
    o2 = o1 + B_KV_LATENT
    o3 = o2 + B_IDX_HEADS * B_IDX_DIM
    wcat = jnp.zeros((d, o3 + 128), F32).at[:, :b_w_in.shape[1]].set(b_w_in).astype(BF16)
    wuk_t = jnp.transpose(w_uk, (1, 2, 0)).astype(BF16)
    row = lambda i: (i, 0)
    modm = lambda i: (i // per_b, 0, 0)
    blk3 = lambda i: (i, 0, 0)
    return pl.pallas_call(
        _bproj_kernel,
        out_shape=(
            jax.ShapeDtypeStruct((n, d), F32),
            jax.ShapeDtypeStruct((nq, B_HEADS * QBLK, B_KV_LATENT), BF16),
            jax.ShapeDtypeStruct((n, B_KV_LATENT), BF16),
            jax.ShapeDtypeStruct((nq, B_IDX_HEADS * QBLK, B_IDX_DIM), BF16),
            jax.ShapeDtypeStruct((n, B_IDX_DIM), BF16),
            jax.ShapeDtypeStruct((nq, B_IDX_HEADS * QBLK, 1), F32),
        ),
        grid=(n // tm,),
        in_specs=[
            pl.BlockSpec((tm, d), row), pl.BlockSpec((tm, d), row), pl.BlockSpec((tm, d), row),
            pl.BlockSpec((tm, TOP_K), row),
            pl.BlockSpec((1, 6, d), modm), pl.BlockSpec((1, 6, d), modm),
            _const_spec((1, d)),
            _const_spec((d, o3 + 128)),
            _const_spec((1, B_KV_LATENT)),
            _const_spec((1, B_IDX_DIM)),
            _const_spec((B_HEADS, B_HEAD_DIM, B_KV_LATENT)),
        ],
        out_specs=(
            pl.BlockSpec((tm, d), row),
            pl.BlockSpec((qpb, B_HEADS * QBLK, B_KV_LATENT), blk3),
            pl.BlockSpec((tm, B_KV_LATENT), row),
            pl.BlockSpec((qpb, B_IDX_HEADS * QBLK, B_IDX_DIM), blk3),
            pl.BlockSpec((tm, B_IDX_DIM), row),
            pl.BlockSpec((qpb, B_IDX_HEADS * QBLK, 1), blk3),
        ),
        compiler_params=_cparams(("arbitrary",)),
        name="dsa_proj",
    )(h1, y0, y1, gates, mod_prev, mod_l, n1g.reshape(1, d), wcat, kv_g.reshape(1, -1),
      kidx_g.reshape(1, -1), wuk_t)


def _t5_bucket(dist):
    n = jnp.maximum(dist, 0)
    exact = REL_BUCKETS // 2
    nf = jnp.maximum(n, 1).astype(F32)
    large = exact + (jnp.log(nf / exact) / math.log(REL_MAX_DIST / exact)
                     * (REL_BUCKETS - exact)).astype(I32)
    large = jnp.minimum(large, REL_BUCKETS - 1)
    return jnp.where(n < exact, n, large)


def _bias_tables(rel_bias):
    t = jnp.arange(QBLK, dtype=I32)[:, None]
    s = jnp.arange(QBLK, dtype=I32)[None, :]
    far = rel_bias[REL_BUCKETS - 1]
    diag = rel_bias[_t5_bucket(t - s)] - far
    prev = rel_bias[_t5_bucket(t - s + QBLK)] - far
    diag = jnp.transpose(diag, (2, 0, 1))
    prev = jnp.transpose(prev, (2, 0, 1))
    zero = jnp.zeros_like(diag)
    cat = lambda a, b: jnp.concatenate([a, b], axis=-1)
    return jnp.stack([cat(zero, zero), cat(zero, prev), cat(prev, diag), cat(diag, zero)])


def _attn_kernel(qa_ref, qi_ref, wi_ref, ckv_ref, ki_ref, bt_ref, o_ref, kbuf, m_scr, l_scr, a_scr, acc_scr,
                 s_scr, p_scr):
    i = pl.program_id(1)
    nt = i // 2 + 1
    t_row = i * QBLK + lax.broadcasted_iota(I32, (QBLK, KTILE), 0)
    lane = lax.broadcasted_iota(I32, (QBLK, KTILE), 1)

    qi = qi_ref[0]
    wcol = wi_ref[0]

    def score_body(j, carry):
        k0 = pl.multiple_of(j * KTILE, KTILE)
        kt = ki_ref[0, pl.ds(k0, KTILE), :]
        p = lax.dot_general(qi, kt, (((1,), (1,)), ((), ())), preferred_element_type=F32)
        p = jnp.maximum(p, 0.0) * wcol
        sc = p[0:QBLK]
        for hh in range(1, B_IDX_HEADS):
            sc = sc + p[hh * QBLK:(hh + 1) * QBLK]
        bits = pltpu.bitcast(sc + 0.0, I32)
        key = jnp.where(bits < 0, bits ^ jnp.int32(0x7FFFFFFF), bits)
        key = jnp.where(k0 + lane <= t_row, key, jnp.int32(INT_MIN))
        kbuf[:, pl.ds(k0, KTILE)] = key
        return carry

    lax.fori_loop(0, nt, score_body, 0)

    def count_ge(cand):
        def body(j, acc):
            k0 = pl.multiple_of(j * KTILE, KTILE)
            return acc + jnp.where(kbuf[:, pl.ds(k0, KTILE)] >= cand, 1, 0)
        acc = lax.fori_loop(0, nt, body, jnp.zeros((QBLK, KTILE), I32))
        return jnp.sum(acc, axis=-1, keepdims=True)

    def bit_cond(c):
        return (c[0] < 32) & (c[3] == 0)

    def bit_body(c):
        b, lo, n_ge, _ = c
        cand = lo + lax.shift_left(jnp.int32(1), 31 - b)
        cnt = count_ge(cand)
        take = cnt >= B_TOPK_MAX
        n_ge = jnp.where(take, cnt, n_ge)
        done = (jnp.max(n_ge) == B_TOPK_MAX).astype(I32)
        return b + 1, jnp.where(take, cand, lo), n_ge, done

    _, tau, n_ge, _ = lax.while_loop(
        bit_cond, bit_body,
        (jnp.int32(0), jnp.full((QBLK, 1), INT_MIN, I32), jnp.full((QBLK, 1), nt * KTILE, I32),
         (i < 2).astype(I32)))
    tau = jnp.maximum(tau, jnp.int32(INT_MIN + 1))
    excess = (i >= 2) & (jnp.max(n_ge) > B_TOPK_MAX)

    @pl.when(excess)
    def _():
        n_gt = count_ge(tau + 1)
        need = B_TOPK_MAX - n_gt

        def count_eq_before(pos):
            def body(j, acc):
                k0 = pl.multiple_of(j * KTILE, KTILE)
                hit = (kbuf[:, pl.ds(k0, KTILE)] == tau) & (k0 + lane < pos)
                return acc + jnp.where(hit, 1, 0)
            acc = lax.fori_loop(0, nt, body, jnp.zeros((QBLK, KTILE), I32))
            return jnp.sum(acc, axis=-1, keepdims=True)

        def pos_body(b, pos):
            cand = pos + lax.shift_left(jnp.int32(1), 12 - b)
            return jnp.where(count_eq_before(cand) < need, cand, pos)

        pos = lax.fori_loop(0, 13, pos_body, jnp.zeros((QBLK, 1), I32))

        def drop_body(j, carry):
            k0 = pl.multiple_of(j * KTILE, KTILE)
            kk = kbuf[:, pl.ds(k0, KTILE)]
            drop = (kk == tau) & (k0 + lane > pos) & (n_ge > B_TOPK_MAX)
            kbuf[:, pl.ds(k0, KTILE)] = jnp.where(drop, jnp.int32(INT_MIN), kk)
            return carry

        lax.fori_loop(0, nt, drop_body, 0)

    m_scr[...] = jnp.full_like(m_scr, NEG_BIG)
    l_scr[...] = jnp.zeros_like(l_scr)
    acc_scr[...] = jnp.zeros_like(acc_scr)

    def attend(j, bias_idx):
        k0 = pl.multiple_of(j * KTILE, KTILE)
        kv = ckv_ref[0, pl.ds(k0, KTILE), :]
        s_scr[...] = lax.dot_general(qa_ref[0], kv, (((1,), (1,)), ((), ())), preferred_element_type=F32)
        madd = jnp.where(kbuf[:, pl.ds(k0, KTILE)] >= tau, 0.0, NEG_BIG)
        for hh in range(B_HEADS):
            r0, r1 = hh * QBLK, (hh + 1) * QBLK
            s = s_scr[r0:r1, :] + madd
            if bias_idx is not None:
                s = s + bt_ref[bias_idx, hh]
            m_old = m_scr[r0:r1, :]
            m_new = jnp.maximum(m_old, jnp.max(s, axis=-1, keepdims=True))
            p = jnp.exp(s - pltpu.repeat(m_new, KTILE // 128, axis=1))
            alpha = jnp.exp(m_old - m_new)
            l_scr[r0:r1, :] = alpha * l_scr[r0:r1, :] + jnp.sum(p, axis=-1, keepdims=True)
            m_scr[r0:r1, :] = m_new
            a_scr[r0:r1, :] = alpha
            p_scr[r0:r1, :] = p.astype(BF16)
        pv = jnp.dot(p_scr[...], kv, preferred_element_type=F32)
        acc_scr[...] = pltpu.repeat(a_scr[...], B_KV_LATENT // 128, axis=1) * acc_scr[...] + pv

    def far_body(j, carry):
        attend(j, None)
        return carry

    lax.fori_loop(0, jnp.maximum(nt - 2, 0), far_body, 0)
    odd = i % 2

    @pl.when(nt >= 2)
    def _():
        attend(nt - 2, 1 - odd)

    attend(nt - 1, 3 - odd)
    inv_l = 1.0 / l_scr[...]
    o_ref[0] = (acc_scr[...] * pltpu.repeat(inv_l, B_KV_LATENT // 128, axis=1)).astype(BF16)


def _attention(qa, qi, wi, ckv, ki, btab, bsz, seq):
    nqb = seq // QBLK
    gq = lambda b, i: (b * nqb + i, 0, 0)
    gb = lambda b, i: (b, 0, 0)
    return pl.pallas_call(
        _attn_kernel,
        out_shape=jax.ShapeDtypeStruct(qa.shape, BF16),
        grid=(bsz, nqb),
        in_specs=[
            pl.BlockSpec((1, B_HEADS * QBLK, B_KV_LATENT), gq),
            pl.BlockSpec((1, B_IDX_HEADS * QBLK, B_IDX_DIM), gq),
            pl.BlockSpec((1, B_IDX_HEADS * QBLK, 1), gq),
            pl.BlockSpec((1, seq, B_KV_LATENT), gb),
            pl.BlockSpec((1, seq, B_IDX_DIM), gb),
            pl.BlockSpec((4, B_HEADS, QBLK, KTILE), lambda b, i: (0, 0, 0, 0), pipeline_mode=pl.Buffered(1)),
        ],
        out_specs=pl.BlockSpec((1, B_HEADS * QBLK, B_KV_LATENT), gq),
        scratch_shapes=[
            pltpu.VMEM((QBLK, seq), I32),
            pltpu.VMEM((B_HEADS * QBLK, 128), F32),
            pltpu.VMEM((B_HEADS * QBLK, 128), F32),
            pltpu.VMEM((B_HEADS * QBLK, 128), F32),
            pltpu.VMEM((B_HEADS * QBLK, B_KV_LATENT), F32),
            pltpu.VMEM((B_HEADS * QBLK, KTILE), F32),
            pltpu.VMEM((B_HEADS * QBLK, KTILE), BF16),
        ],
        compiler_params=_cparams(("arbitrary", "arbitrary")),
        name="dsa_attention",
    )(qa, qi, wi, ckv.reshape(bsz, seq, -1), ki.reshape(bsz, seq, -1), btab)


def _bout_kernel(h_ref, ol_ref, mod_ref, wuv_ref, wout_ref, n2g_ref, rw_ref, h_out, hn_out, lg_out, o_scr):
    mod = mod_ref[0]
    for blk in range(ol_ref.shape[0]):
        for hh in range(B_HEADS):
            oh = jnp.dot(ol_ref[blk, hh * QBLK:(hh + 1) * QBLK, :], wuv_ref[hh], preferred_element_type=F32)
            o_scr[blk * QBLK:(blk + 1) * QBLK, hh * B_V_DIM:(hh + 1) * B_V_DIM] = oh.astype(BF16)
    y = jnp.dot(o_scr[...], wout_ref[...], preferred_element_type=F32)
    _residual_epilogue(h_ref[...], y, mod, n2g_ref[...], rw_ref[...], h_out, hn_out, lg_out)


def _bout(h, o_lat, mod_l, w_uv, w_out, n2g, router_w, seq):
    n, d = h.shape
    tm = ROW_TILE
    per_b = seq // tm
    qpb = tm // QBLK
    ne = router_w.shape[1]
    wuv_t = jnp.transpose(w_uv, (1, 0, 2)).astype(BF16)
    row = lambda i: (i, 0)
    return pl.pallas_call(
        _bout_kernel,
        out_shape=(jax.ShapeDtypeStruct((n, d), F32), jax.ShapeDtypeStruct((n, d // 2), I32),
                   jax.ShapeDtypeStruct((n, ne), F32)),
        grid=(n // tm,),
        in_specs=[
            pl.BlockSpec((tm, d), row),
            pl.BlockSpec((qpb, B_HEADS * QBLK, B_KV_LATENT), lambda i: (i, 0, 0)),
            pl.BlockSpec((1, 6, d), lambda i: (i // per_b, 0, 0)),
            _const_spec((B_HEADS, B_KV_LATENT, B_V_DIM)),
            _const_spec((B_HEADS * B_V_DIM, d)),
            _const_spec((1, d)),
            _const_spec((d, ne)),
        ],
        out_specs=(pl.BlockSpec((tm, d), row), pl.BlockSpec((tm, d // 2), row), pl.BlockSpec((tm, ne), row)),
        scratch_shapes=[pltpu.VMEM((tm, B_HEADS * B_V_DIM), BF16)],
        compiler_params=_cparams(("arbitrary",)),
        name="dsa_out",
    )(h, o_lat, mod_l, wuv_t, w_out.astype(BF16), n2g.reshape(1, d), router_w)


def _final_kernel(h1_ref, y0_ref, y1_ref, gt_ref, mod_ref, g_ref, o_ref):
    h = _moe_combine(h1_ref[...], y0_ref[...], y1_ref[...], gt_ref[...], mod_ref[0])
    o_ref[...] = h * lax.rsqrt(jnp.mean(h * h, axis=-1, keepdims=True) + EPS) * g_ref[...]


def _final(h1, y0, y1, gates, mod_l, final_g, seq):
    n, d = h1.shape
    tm = 512
    per_b = seq // tm
    row = lambda i: (i, 0)
    return pl.pallas_call(
        _final_kernel,
        out_shape=jax.ShapeDtypeStruct((n, d), F32),
        grid=(n // tm,),
        in_specs=[pl.BlockSpec((tm, d), row), pl.BlockSpec((tm, d), row), pl.BlockSpec((tm, d), row),
                  pl.BlockSpec((tm, TOP_K), row), pl.BlockSpec((1, 6, d), lambda i: (i // per_b, 0, 0)),
                  _const_spec((1, d))],
        out_specs=pl.BlockSpec((tm, d), row),
        compiler_params=_cparams(("arbitrary",)),
        name="final_norm",
    )(h1, y0, y1, gates, mod_l, final_g.reshape(1, d))


def kernel(x, c, ada_w, ada_b, norm1_g, norm2_g, a_w_in, a_ln_g, a_ln_b, a_w_sp, a_b_sp, a_w_out, b_w_in,
           b_kv_norm_g, b_w_uk, b_w_uv, b_kidx_g, b_w_out, rel_bias, router_w, router_b, moe_w_gate,
           moe_w_up, moe_w_down, final_g):
    bsz, seq, d = x.shape
    n = bsz * seq
    mod = _adaln(c, ada_w, ada_b).reshape(ada_w.shape[0], bsz, 6, d)
    h = x.reshape(n, d)

    h1, hn2, logits = _gmlp_layer(h, mod[0], norm1_g[0], a_w_in[0], a_ln_g[0], a_ln_b[0], a_w_sp[0],
                                  a_b_sp[0], a_w_out[0], norm2_g[0], router_w, seq)
    y0, y1, gates = _moe(hn2, logits, router_b, moe_w_gate, moe_w_up, moe_w_down, 0)

    h, qa, ckv, qi, ki, wi = _bproj(h1, y0, y1, gates, mod[0], mod[1], norm1_g[1], b_w_in[0],
                                    b_kv_norm_g[0], b_w_uk[0], b_kidx_g[0], seq)
    o_lat = _attention(qa, qi, wi, ckv, ki, _bias_tables(rel_bias), bsz, seq)
    h1, hn2, logits = _bout(h, o_lat, mod[1], b_w_uv[0], b_w_out[0], norm2_g[1], router_w, seq)
    y0, y1, gates = _moe(hn2, logits, router_b, moe_w_gate, moe_w_up, moe_w_down, 1)

    out = _final(h1, y0, y1, gates, mod[1], final_g, seq)
    return out.reshape(bsz, seq, d)
```

```python
import functools
import math

import jax
import jax.numpy as jnp
from jax import lax
from jax.experimental import pallas as pl
from jax.experimental.pallas import tpu as pltpu

F32 = jnp.float32
BF16 = jnp.bfloat16
I32 = jnp.int32
HIGHEST = lax.Precision.HIGHEST

EPS = 1e-6
A_CHUNK = 128
A_GROUPS = 8
B_HEADS = 16
B_HEAD_DIM = 64
B_V_DIM = 64
B_KV_LATENT = 256
B_IDX_HEADS = 8
B_IDX_DIM = 64
B_TOPK_MAX = 256
QBLK = 256
KTILE = 256
SM_ROWS = 128
REL_BUCKETS = 32
REL_MAX_DIST = 128
N_EXPERTS = 16
N_GROUPS = 4
EXPERTS_PER_GROUP = 4
TOP_K = 2
MOE_ROWS = 1024
MOE_SUB = 256
MOE_FT = 512

ROW_TILE = 256
VMEM_LIMIT = 56 * 1024 * 1024

INT_MIN = -2 ** 31
NEG_BIG = -1e30


def _cparams(sem):
    return pltpu.CompilerParams(dimension_semantics=sem, vmem_limit_bytes=VMEM_LIMIT)


def _mod_rmsnorm(h, g, scale, shift):
    ms = jnp.mean(h * h, axis=-1, keepdims=True)
    return (h * lax.rsqrt(ms + EPS) * g) * (1.0 + scale) + shift


def _gelu_tanh(x):
    c = math.sqrt(2.0 / math.pi)
    return 0.5 * x * (1.0 + jnp.tanh(c * (x + 0.044715 * (x * x * x))))


def _adaln_kernel(c_ref, w_ref, b_ref, o_ref):
    c = c_ref[...]
    sc = c * jax.nn.sigmoid(c)
    o_ref[0] = jnp.dot(sc, w_ref[0], precision=HIGHEST, preferred_element_type=F32) + b_ref[0]


def _adaln(c, ada_w, ada_b):
    depth, d, e = ada_w.shape
    bsz = c.shape[0]
    bp = 8
    c_pad = jnp.zeros((bp, d), F32).at[:bsz].set(c)
    tn = 1024
    out = pl.pallas_call(
        _adaln_kernel,
        out_shape=jax.ShapeDtypeStruct((depth, bp, e), F32),
        grid=(depth, e // tn),
        in_specs=[
            pl.BlockSpec((bp, d), lambda l, j: (0, 0)),
            pl.BlockSpec((1, d, tn), lambda l, j: (l, 0, j)),
            pl.BlockSpec((1, 1, tn), lambda l, j: (l, 0, j)),
        ],
        out_specs=pl.BlockSpec((1, bp, tn), lambda l, j: (l, 0, j)),
        compiler_params=_cparams(("arbitrary", "arbitrary")),
        name="adaln",
    )(c_pad, ada_w, ada_b.reshape(depth, 1, e))
    return out[:, :bsz]


def _residual_epilogue(h, y, mod, n2g, rw, h_out, hn_out, lg_out):
    h1 = h + mod[2:3] * y
    h_out[...] = h1
    hn2 = _mod_rmsnorm(h1, n2g, mod[4:5], mod[3:4])
    bits = pltpu.bitcast(hn2.astype(BF16).astype(F32), I32)
    half = bits.shape[1] // 2
    hn_out[...] = bits[:, :half] | lax.shift_right_logical(bits[:, half:], 16)
    lg_out[...] = jnp.dot(hn2, rw, precision=HIGHEST, preferred_element_type=F32)


def _gmlp_kernel(h_ref, mod_ref, n1g_ref, win_ref, lng_ref, lnb_ref, wsp_ref, bsp_ref, wout_ref,
                 n2g_ref, rw_ref, h_out, hn_out, lg_out, u_scr, v_scr, s_scr):
    tm = h_ref.shape[0]
    inner = u_scr.shape[1]
    gw = inner // A_GROUPS
    tn = 512
    h = h_ref[...]
    mod = mod_ref[0]
    hn = _mod_rmsnorm(h, n1g_ref[...], mod[1:2], mod[0:1]).astype(BF16)
    for j in range(2 * inner // tn):
        z = _gelu_tanh(jnp.dot(hn, win_ref[:, j * tn:(j + 1) * tn], preferred_element_type=F32))
        if j < inner // tn:
            u_scr[:, j * tn:(j + 1) * tn] = z
        else:
            jj = j - inner // tn
            v_scr[:, jj * tn:(jj + 1) * tn] = z
    vsum = jnp.zeros((tm, 1), F32)
    for j in range(inner // tn):
        vsum = vsum + jnp.sum(v_scr[:, j * tn:(j + 1) * tn], axis=-1, keepdims=True)
    mu = vsum * (1.0 / inner)
    vsq = jnp.zeros((tm, 1), F32)
    for j in range(inner // tn):
        d = v_scr[:, j * tn:(j + 1) * tn] - mu
        vsq = vsq + jnp.sum(d * d, axis=-1, keepdims=True)
    rstd = lax.rsqrt(vsq * (1.0 / inner) + EPS)
    row = lax.broadcasted_iota(I32, (A_CHUNK, A_CHUNK), 0)
    col = lax.broadcasted_iota(I32, (A_CHUNK, A_CHUNK), 1)
    tril = row >= col
    for g in range(A_GROUPS):
        ws = jnp.where(tril, wsp_ref[g], 0.0).astype(BF16)
        bcol = bsp_ref[:, g:g + 1]
        lg = lng_ref[:, g * gw:(g + 1) * gw]
        lb = lnb_ref[:, g * gw:(g + 1) * gw]
        for c in range(tm // A_CHUNK):
            r0, r1 = c * A_CHUNK, (c + 1) * A_CHUNK
            vt = v_scr[r0:r1, g * gw:(g + 1) * gw]
            vn = ((vt - mu[r0:r1]) * rstd[r0:r1]) * lg + lb
            fv = jnp.dot(ws, vn.astype(BF16), preferred_element_type=F32) + bcol
            s_scr[r0:r1, g * gw:(g + 1) * gw] = (u_scr[r0:r1, g * gw:(g + 1) * gw] * fv).astype(BF16)
    y = jnp.dot(s_scr[...], wout_ref[...], preferred_element_type=F32)
    _residual_epilogue(h, y, mod, n2g_ref[...], rw_ref[...], h_out, hn_out, lg_out)


def _const_spec(shape):
    nd = len(shape)
    return pl.BlockSpec(shape, lambda i, _nd=nd: (0,) * _nd, pipeline_mode=pl.Buffered(1))


def _gmlp_layer(h, mod_l, n1g, w_in, ln_g, ln_b, w_sp, b_sp, w_out, n2g, router_w, seq):
    n, d = h.shape
    inner = w_out.shape[0]
    tm = ROW_TILE
    per_b = seq // tm
    ne = router_w.shape[1]
    return pl.pallas_call(
        _gmlp_kernel,
        out_shape=(jax.ShapeDtypeStruct((n, d), F32), jax.ShapeDtypeStruct((n, d // 2), I32),
                   jax.ShapeDtypeStruct((n, ne), F32)),
        grid=(n // tm,),
        in_specs=[
            pl.BlockSpec((tm, d), lambda i: (i, 0)),
            pl.BlockSpec((1, 6, d), lambda i: (i // per_b, 0, 0)),
            _const_spec((1, d)),
            _const_spec((d, 2 * inner)),
            _const_spec((1, inner)),
            _const_spec((1, inner)),
            _const_spec((A_GROUPS, A_CHUNK, A_CHUNK)),
            _const_spec((A_CHUNK, A_GROUPS)),
            _const_spec((inner, d)),
            _const_spec((1, d)),
            _const_spec((d, ne)),
        ],
        out_specs=(pl.BlockSpec((tm, d), lambda i: (i, 0)), pl.BlockSpec((tm, d // 2), lambda i: (i, 0)),
                   pl.BlockSpec((tm, ne), lambda i: (i, 0))),
        scratch_shapes=[pltpu.VMEM((tm, inner), F32), pltpu.VMEM((tm, inner), F32),
                        pltpu.VMEM((tm, inner), BF16)],
        compiler_params=_cparams(("arbitrary",)),
        name="gmlp_layer",
    )(h, mod_l, n1g.reshape(1, d), w_in.astype(BF16), ln_g.reshape(1, inner), ln_b.reshape(1, inner),
      w_sp, b_sp.T, w_out.astype(BF16), n2g.reshape(1, d), router_w)


def _first_max4(rows):
    m = jnp.maximum(jnp.maximum(rows[0], rows[1]), jnp.maximum(rows[2], rows[3]))
    idx = jnp.where(rows[0] == m, 0, jnp.where(rows[1] == m, 1, jnp.where(rows[2] == m, 2, 3)))
    return m, idx.astype(I32)


def _route_kernel(lt_ref, rb_ref, e_ref, g_ref):
    aff = jax.nn.sigmoid(lt_ref[...])
    sel = aff + rb_ref[...]
    neg = jnp.float32(-jnp.inf)
    g_score, g_i1, g_i2 = [], [], []
    for g in range(N_GROUPS):
        rows = [sel[4 * g + k:4 * g + k + 1, :] for k in range(EXPERTS_PER_GROUP)]
        m1, i1 = _first_max4(rows)
        rest = [jnp.where(i1 == k, neg, rows[k]) for k in range(EXPERTS_PER_GROUP)]
        m2, i2 = _first_max4(rest)
        g_score.append(m1 + m2)
        g_i1.append(i1)
        g_i2.append(i2)
    _, grp = _first_max4(g_score)
    l1 = jnp.zeros_like(grp)
    l2 = jnp.zeros_like(grp)
    for g in range(N_GROUPS):
        l1 = jnp.where(grp == g, g_i1[g], l1)
        l2 = jnp.where(grp == g, g_i2[g], l2)
    e1 = grp * EXPERTS_PER_GROUP + l1
    e2 = grp * EXPERTS_PER_GROUP + l2
    a1 = jnp.zeros_like(g_score[0])
    a2 = jnp.zeros_like(g_score[0])
    for e in range(N_EXPERTS):
        ae = aff[e:e + 1, :]
        a1 = jnp.where(e1 == e, ae, a1)
        a2 = jnp.where(e2 == e, ae, a2)
    tot = a1 + a2
    e_ref[0:1, :] = e1
    e_ref[1:2, :] = e2
    g_ref[0:1, :] = a1 / tot
    g_ref[1:2, :] = a2 / tot


def _route(logits, router_b):
    n, ne = logits.shape
    tn = min(2048, n)
    return pl.pallas_call(
        _route_kernel,
        out_shape=(jax.ShapeDtypeStruct((TOP_K, n), I32), jax.ShapeDtypeStruct((TOP_K, n), F32)),
        grid=(n // tn,),
        in_specs=[pl.BlockSpec((ne, tn), lambda i: (0, i)), pl.BlockSpec((ne, 1), lambda i: (0, 0))],
        out_specs=(pl.BlockSpec((TOP_K, tn), lambda i: (0, i)), pl.BlockSpec((TOP_K, tn), lambda i: (0, i))),
        compiler_params=_cparams(("arbitrary",)),
        name="route",
    )(logits.T, router_b.reshape(ne, 1))


def _slot_kernel(e_ref, slot_ref, cnt_ref):
    n = e_ref.shape[1]
    nblk = n // 128
    eid = lax.broadcasted_iota(I32, (N_EXPERTS, 128), 0)
    r = lax.broadcasted_iota(I32, (128, 128), 0)
    c = lax.broadcasted_iota(I32, (128, 128), 1)
    upper = jnp.where(r <= c, 1.0, 0.0).astype(BF16)

    def hits(b):
        o = pl.multiple_of(b * 128, 128)
        t1 = e_ref[0:1, pl.ds(o, 128)] == eid
        t2 = e_ref[1:2, pl.ds(o, 128)] == eid
        return o, t1, t2, jnp.where(t1, 1.0, 0.0) + jnp.where(t2, 1.0, 0.0)

    def count_body(b, acc):
        return acc + hits(b)[3]

    cnt = jnp.sum(lax.fori_loop(0, nblk, count_body, jnp.zeros((N_EXPERTS, 128), F32)),
                  axis=-1, keepdims=True)
    cnt_ref[...] = cnt.astype(I32)
    padded = jnp.floor((cnt + (MOE_ROWS - 1)) * (1.0 / MOE_ROWS)) * MOE_ROWS
    er = lax.broadcasted_iota(I32, (N_EXPERTS, N_EXPERTS), 0)
    ec = lax.broadcasted_iota(I32, (N_EXPERTS, N_EXPERTS), 1)
    padded_row = jnp.sum(jnp.where(er == ec, padded, 0.0), axis=0, keepdims=True)
    start = jnp.sum(jnp.where(ec < er, padded_row, 0.0), axis=-1, keepdims=True)

    def slot_body(b, carry):
        o, t1, t2, t = hits(b)
        incl = jnp.dot(t.astype(BF16), upper, preferred_element_type=F32)
        pos = carry + incl - t
        slot_ref[0:1, pl.ds(o, 128)] = jnp.sum(jnp.where(t1, pos, 0.0), axis=0, keepdims=True).astype(I32)
        slot_ref[1:2, pl.ds(o, 128)] = jnp.sum(jnp.where(t2, pos, 0.0), axis=0, keepdims=True).astype(I32)
        return carry + incl[:, 127:128]

    lax.fori_loop(0, nblk, slot_body, start)


def _slot_tables(experts):
    n = experts.shape[1]
    a = n * TOP_K
    slot, counts = pl.pallas_call(
        _slot_kernel,
        out_shape=(jax.ShapeDtypeStruct((TOP_K, n), I32), jax.ShapeDtypeStruct((N_EXPERTS, 1), I32)),
        compiler_params=pltpu.CompilerParams(vmem_limit_bytes=VMEM_LIMIT),
        name="slots",
    )(experts)
    counts = counts.reshape(N_EXPERTS)
    padded = ((counts + MOE_ROWS - 1) // MOE_ROWS) * MOE_ROWS
    pad_end = jnp.cumsum(padded)
    pad_start = pad_end - padded
    p = a + N_EXPERTS * MOE_ROWS
    n_items = p // MOE_ROWS
    tok = jnp.broadcast_to(jnp.arange(n, dtype=I32)[None, :], (TOP_K, n))
    tok_of_slot = jnp.zeros((p,), I32).at[slot.reshape(a)].set(tok.reshape(a))
    slot_of_assign = slot.T
    item_start = jnp.arange(n_items, dtype=I32) * MOE_ROWS
    n_used = pad_end[-1] // MOE_ROWS
    item_e_raw = jnp.minimum(jnp.searchsorted(pad_end, item_start, side='right'), N_EXPERTS - 1).astype(I32)
    used = item_start < pad_end[-1]
    last_e = item_e_raw[jnp.maximum(n_used - 1, 0)]
    item_e = jnp.where(used, item_e_raw, last_e).astype(I32)
    valid = jnp.clip(counts[item_e_raw] - (item_start - pad_start[item_e_raw]), 0, MOE_ROWS)
    item_rows = jnp.where(used, valid, 0).astype(I32)
    item_blk = jnp.where(used, jnp.arange(n_items, dtype=I32), jnp.maximum(n_used - 1, 0)).astype(I32)
    return tok_of_slot, slot_of_assign, item_e, item_rows, item_blk


def _moe_kernel(ie_ref, ir_ref, ib_ref, x_ref, wg_ref, wu_ref, wd_ref, o_ref, wg_s, wu_s, wd_s, x_s):
    it = pl.program_id(0)
    f = pl.program_id(1)
    nrows = ir_ref[it]
    half = x_ref.shape[1]

    @pl.when(f == 0)
    def _():
        o_ref[...] = jnp.zeros_like(o_ref)
        pk = x_ref[...]
        x_s[:, :half] = pltpu.bitcast(pk & jnp.int32(-65536), F32).astype(BF16)
        x_s[:, half:] = pltpu.bitcast(lax.shift_left(pk, 16), F32).astype(BF16)

    @pl.when(nrows > 0)
    def _():
        wg_s[...] = wg_ref[0, 0].astype(BF16)
        wu_s[...] = wu_ref[0, 0].astype(BF16)
        wd_s[...] = wd_ref[0, 0].astype(BF16)
        nsb = (nrows + MOE_SUB - 1) // MOE_SUB

        def sub_block(sb):
            r0 = pl.multiple_of(sb * MOE_SUB, MOE_SUB)
            x = x_s[pl.ds(r0, MOE_SUB), :]
            g = jnp.dot(x, wg_s[...], preferred_element_type=F32)
            u = jnp.dot(x, wu_s[...], preferred_element_type=F32)
            hmid = ((g * jax.nn.sigmoid(g)) * u).astype(BF16)
            o_ref[pl.ds(r0, MOE_SUB), :] += jnp.dot(hmid, wd_s[...], preferred_element_type=F32)

        def pair_body(pi, carry):
            sub_block(2 * pi)
            sub_block(2 * pi + 1)
            return carry

        lax.fori_loop(0, nsb // 2, pair_body, 0)

        @pl.when(nsb % 2 == 1)
        def _():
            sub_block(nsb - 1)


def _moe_experts(x_sorted, item_e, item_rows, item_blk, w_gate, w_up, w_down, layer):
    p, half = x_sorted.shape
    d = 2 * half
    n_items = p // MOE_ROWS
    dff = w_gate.shape[3]
    nf = dff // MOE_FT
    grid_spec = pltpu.PrefetchScalarGridSpec(
        num_scalar_prefetch=3,
        grid=(n_items, nf),
        in_specs=[
            pl.BlockSpec((MOE_ROWS, half), lambda i, f, ie, ir, ib: (ib[i], 0)),
            pl.BlockSpec((1, 1, d, MOE_FT),
                         lambda i, f, ie, ir, ib: (layer, ie[i], 0, jnp.where(ir[i] > 0, f, nf - 1))),
            pl.BlockSpec((1, 1, d, MOE_FT),
                         lambda i, f, ie, ir, ib: (layer, ie[i], 0, jnp.where(ir[i] > 0, f, nf - 1))),
            pl.BlockSpec((1, 1, MOE_FT, d),
                         lambda i, f, ie, ir, ib: (layer, ie[i], jnp.where(ir[i] > 0, f, nf - 1), 0)),
        ],
        out_specs=pl.BlockSpec((MOE_ROWS, d), lambda i, f, ie, ir, ib: (i, 0)),
        scratch_shapes=[pltpu.VMEM((d, MOE_FT), BF16), pltpu.VMEM((d, MOE_FT), BF16),
                        pltpu.VMEM((MOE_FT, d), BF16), pltpu.VMEM((MOE_ROWS, d), BF16)],
    )
    return pl.pallas_call(
        _moe_kernel,
        out_shape=jax.ShapeDtypeStruct((p, d), F32),
        grid_spec=grid_spec,
        compiler_params=_cparams(("arbitrary", "arbitrary")),
        name="moe_experts",
    )(item_e, item_rows, item_blk, x_sorted, w_gate, w_up, w_down)


def _moe(hn2, logits, router_b, w_gate, w_up, w_down, layer):
    experts, gates = _route(logits, router_b)
    tok_of_slot, slot_of_assign, item_e, item_rows, item_blk = _slot_tables(experts)
    x_sorted = hn2.at[tok_of_slot].get(mode='promise_in_bounds')
    y_slot = _moe_experts(x_sorted, item_e, item_rows, item_blk, w_gate, w_up, w_down, layer)
    y0 = y_slot.at[slot_of_assign[:, 0]].get(mode='promise_in_bounds')
    y1 = y_slot.at[slot_of_assign[:, 1]].get(mode='promise_in_bounds')
    return y0, y1, gates.T


def _moe_combine(h1, y0, y1, gates, mod):
    return h1 + mod[5:6] * (gates[:, 0:1] * y0 + gates[:, 1:2] * y1)


def _bproj_kernel(h1_ref, y0_ref, y1_ref, gt_ref, modp_ref, mod_ref, n1g_ref, w_ref, kvg_ref, kig_ref,
                  wuk_ref, h_out, qa_out, ckv_out, qi_out, ki_out, wi_out):
    tm = h1_ref.shape[0]
    h = _moe_combine(h1_ref[...], y0_ref[...], y1_ref[...], gt_ref[...], modp_ref[0])
    h_out[...] = h
    mod = mod_ref[0]
    hn = _mod_rmsnorm(h, n1g_ref[...], mod[1:2], mod[0:1]).astype(BF16)
    proj = jnp.dot(hn, w_ref[...], preferred_element_type=F32)
    o1 = B_HEADS * B_HEAD_DIM
    o2 = o1 + B_KV_LATENT
    o3 = o2 + B_IDX_HEADS * B_IDX_DIM
    ckv = proj[:, o1:o2]
    ckv = ckv * lax.rsqrt(jnp.mean(ckv * ckv, axis=-1, keepdims=True) + EPS) * kvg_ref[...]
    ckv_out[...] = ckv.astype(BF16)
    tail = proj[:, o3:o3 + 128]
    ki = tail[:, :B_IDX_DIM]
    ki = ki * lax.rsqrt(jnp.mean(ki * ki, axis=-1, keepdims=True) + EPS) * kig_ref[...]
    ki_out[...] = ki.astype(BF16)
    wi = tail[:, B_IDX_DIM:B_IDX_DIM + B_IDX_HEADS] * (B_IDX_HEADS ** -0.5 * B_IDX_DIM ** -0.5)
    scale = B_HEAD_DIM ** -0.5
    for blk in range(tm // QBLK):
        r0, r1 = blk * QBLK, (blk + 1) * QBLK
        for hh in range(B_HEADS):
            qh = proj[r0:r1, hh * B_HEAD_DIM:(hh + 1) * B_HEAD_DIM].astype(BF16)
            qa = jnp.dot(qh, wuk_ref[hh], preferred_element_type=F32) * scale
            qa_out[blk, hh * QBLK:(hh + 1) * QBLK, :] = qa.astype(BF16)
        for hh in range(B_IDX_HEADS):
            qi_out[blk, hh * QBLK:(hh + 1) * QBLK, :] = proj[r0:r1, o2 + hh * B_IDX_DIM:
                                                            o2 + (hh + 1) * B_IDX_DIM].astype(BF16)
            wi_out[blk, hh * QBLK:(hh + 1) * QBLK, :] = wi[r0:r1, hh:hh + 1]


def _bproj(h1, y0, y1, gates, mod_prev, mod_l, n1g, b_w_in, kv_g, w_uk, kidx_g, seq):
    n, d = h1.shape
    tm = ROW_TILE
    per_b = seq // tm
    nq = n // QBLK
    qpb = tm // QBLK
    o1 = B_HEADS * B_HEAD_DIM
    o2 = o1 + B_KV_LATENT
    o3 = o2 + B_IDX_HEADS * B_IDX_DIM
    wcat = jnp.zeros((d, o3 + 128), F32).at[:, :b_w_in.shape[1]].set(b_w_in).astype(BF16)
    wuk_t = jnp.transpose(w_uk, (1, 2, 0)).astype(BF16)
    row = lambda i: (i, 0)
    modm = lambda i: (i // per_b, 0, 0)
    blk3 = lambda i: (i, 0, 0)
    return pl.pallas_call(
        _bproj_kernel,
        out_shape=(
            jax.ShapeDtypeStruct((n, d), F32),
            jax.ShapeDtypeStruct((nq, B_HEADS * QBLK, B_KV_LATENT), BF16),
            jax.ShapeDtypeStruct((n, B_KV_LATENT), BF16),
            jax.ShapeDtypeStruct((nq, B_IDX_HEADS * QBLK, B_IDX_DIM), BF16),
            jax.ShapeDtypeStruct((n, B_IDX_DIM), BF16),
            jax.ShapeDtypeStruct((nq, B_IDX_HEADS * QBLK, 1), F32),
        ),
        grid=(n // tm,),
        in_specs=[
            pl.BlockSpec((tm, d), row), pl.BlockSpec((tm, d), row), pl.BlockSpec((tm, d), row),
            pl.BlockSpec((tm, TOP_K), row),
            pl.BlockSpec((1, 6, d), modm), pl.BlockSpec((1, 6, d), modm),
            _const_spec((1, d)),
            _const_spec((d, o3 + 128)),
            _const_spec((1, B_KV_LATENT)),
            _const_spec((1, B_IDX_DIM)),
            _const_spec((B_HEADS, B_HEAD_DIM, B_KV_LATENT)),
        ],
        out_specs=(
            pl.BlockSpec((tm, d), row),
            pl.BlockSpec((qpb, B_HEADS * QBLK, B_KV_LATENT), blk3),
            pl.BlockSpec((tm, B_KV_LATENT), row),
            pl.BlockSpec((qpb, B_IDX_HEADS * QBLK, B_IDX_DIM), blk3),
            pl.BlockSpec((tm, B_IDX_DIM), row),
            pl.BlockSpec((qpb, B_IDX_HEADS * QBLK, 1), blk3),
        ),
        compiler_params=_cparams(("arbitrary",)),
        name="dsa_proj",
    )(h1, y0, y1, gates, mod_prev, mod_l, n1g.reshape(1, d), wcat, kv_g.reshape(1, -1),
      kidx_g.reshape(1, -1), wuk_t)


def _t5_bucket(dist):
    n = jnp.maximum(dist, 0)
    exact = REL_BUCKETS // 2
    nf = jnp.maximum(n, 1).astype(F32)
    large = exact + (jnp.log(nf / exact) / math.log(REL_MAX_DIST / exact)
                     * (REL_BUCKETS - exact)).astype(I32)
    large = jnp.minimum(large, REL_BUCKETS - 1)
    return jnp.where(n < exact, n, large)


def _bias_tables(rel_bias):
    assert QBLK == KTILE and QBLK >= REL_MAX_DIST
    t = jnp.arange(QBLK, dtype=I32)[:, None]
    s = jnp.arange(KTILE, dtype=I32)[None, :]
    far = rel_bias[REL_BUCKETS - 1]
    diag = rel_bias[_t5_bucket(t - s)] - far
    prev = rel_bias[_t5_bucket(t - s + QBLK)] - far
    return jnp.stack([jnp.transpose(diag, (2, 0, 1)), jnp.transpose(prev, (2, 0, 1))])


def _attn_kernel(qa_ref, qi_ref, wi_ref, ckv_ref, ki_ref, bt_ref, o_ref, kbuf, m_scr, l_scr, a_scr, acc_scr,
                 s_scr, p_scr, madd_scr):
    i = pl.program_id(1)
    nt = i + 1
    t_row = i * QBLK + lax.broadcasted_iota(I32, (QBLK, KTILE), 0)
    lane = lax.broadcasted_iota(I32, (QBLK, KTILE), 1)

    qi = qi_ref[0]
    wcol = wi_ref[0]

    def score_body(j, carry):
        k0 = pl.multiple_of(j * KTILE, KTILE)
        kt = ki_ref[0, pl.ds(k0, KTILE), :]
        p = lax.dot_general(qi, kt, (((1,), (1,)), ((), ())), preferred_element_type=F32)
        p = jnp.maximum(p, 0.0) * wcol
        sc = p[0:QBLK]
        for hh in range(1, B_IDX_HEADS):
            sc = sc + p[hh * QBLK:(hh + 1) * QBLK]
        bits = pltpu.bitcast(sc + 0.0, I32)
        key = jnp.where(bits < 0, bits ^ jnp.int32(0x7FFFFFFF), bits)
        key = jnp.where(k0 + lane <= t_row, key, jnp.int32(INT_MIN))
        kbuf[:, pl.ds(k0, KTILE)] = key
        return carry

    lax.fori_loop(0, nt, score_body, 0)

    def fold_lanes(x):
        out = x[:, 0:128]
        for k in range(1, KTILE // 128):
            out = out + x[:, k * 128:(k + 1) * 128]
        return out

    def count_ge(cand):
        def body(j, acc):
            k0 = pl.multiple_of(j * KTILE, KTILE)
            return acc + fold_lanes(jnp.where(kbuf[:, pl.ds(k0, KTILE)] >= cand, 1, 0))
        acc = lax.fori_loop(0, nt, body, jnp.zeros((QBLK, 128), I32))
        return jnp.sum(acc, axis=-1, keepdims=True)

    def bit_step(b, lo, n_ge):
        cand = lo + lax.shift_left(jnp.int32(1), 31 - b)
        cnt = count_ge(cand)
        take = cnt >= B_TOPK_MAX
        return jnp.where(take, cand, lo), jnp.where(take, cnt, n_ge)

    def bit_cond(c):
        return (c[0] < 32) & (c[3] == 0)

    def bit_body(c):
        b, lo, n_ge, _ = c
        lo, n_ge = bit_step(b, lo, n_ge)
        lo, n_ge = bit_step(b + 1, lo, n_ge)
        done = (jnp.max(n_ge) == B_TOPK_MAX).astype(I32)
        return b + 2, lo, n_ge, done

    _, tau, n_ge, _ = lax.while_loop(
        bit_cond, bit_body,
        (jnp.int32(0), jnp.full((QBLK, 1), INT_MIN, I32), jnp.full((QBLK, 1), nt * KTILE, I32),
         (i < 1).astype(I32)))
    tau = jnp.maximum(tau, jnp.int32(INT_MIN + 1))
    excess = (i >= 1) & (jnp.max(n_ge) > B_TOPK_MAX)

    @pl.when(excess)
    def _():
        n_gt = count_ge(tau + 1)
        need = B_TOPK_MAX - n_gt

        def count_eq_before(pos):
            def body(j, acc):
                k0 = pl.multiple_of(j * KTILE, KTILE)
                hit = (kbuf[:, pl.ds(k0, KTILE)] == tau) & (k0 + lane < pos)
                return acc + fold_lanes(jnp.where(hit, 1, 0))
            acc = lax.fori_loop(0, nt, body, jnp.zeros((QBLK, 128), I32))
            return jnp.sum(acc, axis=-1, keepdims=True)

        def pos_body(b, pos):
            cand = pos + lax.shift_left(jnp.int32(1), 12 - b)
            return jnp.where(count_eq_before(cand) < need, cand, pos)

        pos = lax.fori_loop(0, 13, pos_body, jnp.zeros((QBLK, 1), I32))

        def drop_body(j, carry):
            k0 = pl.multiple_of(j * KTILE, KTILE)
            kk = kbuf[:, pl.ds(k0, KTILE)]
            drop = (kk == tau) & (k0 + lane > pos) & (n_ge > B_TOPK_MAX)
            kbuf[:, pl.ds(k0, KTILE)] = jnp.where(drop, jnp.int32(INT_MIN), kk)
            return carry

        lax.fori_loop(0, nt, drop_body, 0)

    m_scr[...] = jnp.full_like(m_scr, NEG_BIG)
    l_scr[...] = jnp.zeros_like(l_scr)
    acc_scr[...] = jnp.zeros_like(acc_scr)

    def attend(j, bias_idx):
        k0 = pl.multiple_of(j * KTILE, KTILE)
        kv = ckv_ref[0, pl.ds(k0, KTILE), :]
        s_scr[...] = lax.dot_general(qa_ref[0], kv, (((1,), (1,)), ((), ())), preferred_element_type=F32)
        madd_scr[...] = jnp.where(kbuf[:, pl.ds(k0, KTILE)] >= tau, 0.0, NEG_BIG)
        for ch in range(B_HEADS * QBLK // SM_ROWS):
            r0, r1 = ch * SM_ROWS, (ch + 1) * SM_ROWS
            q0 = r0 % QBLK
            s = s_scr[r0:r1, :] + madd_scr[q0:q0 + SM_ROWS, :]
            if bias_idx is not None:
                s = s + bt_ref[bias_idx, r0 // QBLK, q0:q0 + SM_ROWS, :]
            m_old = m_scr[r0:r1, :]
            m_new = jnp.maximum(m_old, jnp.max(s, axis=-1, keepdims=True))
            p = jnp.exp(s - pltpu.repeat(m_new, KTILE // 128, axis=1))
            alpha = jnp.exp(m_old - m_new)
            l_scr[r0:r1, :] = alpha * l_scr[r0:r1, :] + jnp.sum(p, axis=-1, keepdims=True)
            m_scr[r0:r1, :] = m_new
            a_scr[r0:r1, :] = alpha
            p_scr[r0:r1, :] = p.astype(BF16)
        pv = jnp.dot(p_scr[...], kv, preferred_element_type=F32)
        acc_scr[...] = pltpu.repeat(a_scr[...], B_KV_LATENT // 128, axis=1) * acc_scr[...] + pv

    def far_body(j, carry):
        attend(j, None)
        return carry

    lax.fori_loop(0, jnp.maximum(i - 1, 0), far_body, 0)

    @pl.when(i >= 1)
    def _():
        attend(i - 1, 1)

    attend(i, 0)
    inv_l = 1.0 / l_scr[...]
    o_ref[0] = (acc_scr[...] * pltpu.repeat(inv_l, B_KV_LATENT // 128, axis=1)).astype(BF16)


def _attention(qa, qi, wi, ckv, ki, btab, bsz, seq):
    nqb = seq // QBLK
    gq = lambda b, i: (b * nqb + i, 0, 0)
    gb = lambda b, i: (b, 0, 0)
    return pl.pallas_call(
        _attn_kernel,
        out_shape=jax.ShapeDtypeStruct(qa.shape, BF16),
        grid=(bsz, nqb),
        in_specs=[
            pl.BlockSpec((1, B_HEADS * QBLK, B_KV_LATENT), gq),
            pl.BlockSpec((1, B_IDX_HEADS * QBLK, B_IDX_DIM), gq),
            pl.BlockSpec((1, B_IDX_HEADS * QBLK, 1), gq),
            pl.BlockSpec((1, seq, B_KV_LATENT), gb),
            pl.BlockSpec((1, seq, B_IDX_DIM), gb),
            pl.BlockSpec((2, B_HEADS, QBLK, KTILE), lambda b, i: (0, 0, 0, 0), pipeline_mode=pl.Buffered(1)),
        ],
        out_specs=pl.BlockSpec((1, B_HEADS * QBLK, B_KV_LATENT), gq),
        scratch_shapes=[
            pltpu.VMEM((QBLK, seq), I32),
            pltpu.VMEM((B_HEADS * QBLK, 128), F32),
            pltpu.VMEM((B_HEADS * QBLK, 128), F32),
            pltpu.VMEM((B_HEADS * QBLK, 128), F32),
            pltpu.VMEM((B_HEADS * QBLK, B_KV_LATENT), F32),
            pltpu.VMEM((B_HEADS * QBLK, KTILE), F32),
            pltpu.VMEM((B_HEADS * QBLK, KTILE), BF16),
            pltpu.VMEM((QBLK, KTILE), F32),
        ],
        compiler_params=_cparams(("arbitrary", "arbitrary")),
        name="dsa_attention",
    )(qa, qi, wi, ckv.reshape(bsz, seq, -1), ki.reshape(bsz, seq, -1), btab)


def _bout_kernel(h_ref, ol_ref, mod_ref, wuv_ref, wout_ref, n2g_ref, rw_ref, h_out, hn_out, lg_out, o_scr):
    mod = mod_ref[0]
    for blk in range(ol_ref.shape[0]):
        for hh in range(B_HEADS):
            oh = jnp.dot(ol_ref[blk, hh * QBLK:(hh + 1) * QBLK, :], wuv_ref[hh], preferred_element_type=F32)
            o_scr[blk * QBLK:(blk + 1) * QBLK, hh * B_V_DIM:(hh + 1) * B_V_DIM] = oh.astype(BF16)
    y = jnp.dot(o_scr[...], wout_ref[...], preferred_element_type=F32)
    _residual_epilogue(h_ref[...], y, mod, n2g_ref[...], rw_ref[...], h_out, hn_out, lg_out)


def _bout(h, o_lat, mod_l, w_uv, w_out, n2g, router_w, seq):
    n, d = h.shape
    tm = ROW_TILE
    per_b = seq // tm
    qpb = tm // QBLK
    ne = router_w.shape[1]
    wuv_t = jnp.transpose(w_uv, (1, 0, 2)).astype(BF16)
    row = lambda i: (i, 0)
    return pl.pallas_call(
        _bout_kernel,
        out_shape=(jax.ShapeDtypeStruct((n, d), F32), jax.ShapeDtypeStruct((n, d // 2), I32),
                   jax.ShapeDtypeStruct((n, ne), F32)),
        grid=(n // tm,),
        in_specs=[
            pl.BlockSpec((tm, d), row),
            pl.BlockSpec((qpb, B_HEADS * QBLK, B_KV_LATENT), lambda i: (i, 0, 0)),
            pl.BlockSpec((1, 6, d), lambda i: (i // per_b, 0, 0)),
            _const_spec((B_HEADS, B_KV_LATENT, B_V_DIM)),
            _const_spec((B_HEADS * B_V_DIM, d)),
            _const_spec((1, d)),
            _const_spec((d, ne)),
        ],
        out_specs=(pl.BlockSpec((tm, d), row), pl.BlockSpec((tm, d // 2), row), pl.BlockSpec((tm, ne), row)),
        scratch_shapes=[pltpu.VMEM((tm, B_HEADS * B_V_DIM), BF16)],
        compiler_params=_cparams(("arbitrary",)),
        name="dsa_out",
    )(h, o_lat, mod_l, wuv_t, w_out.astype(BF16), n2g.reshape(1, d), router_w)


def _final_kernel(h1_ref, y0_ref, y1_ref, gt_ref, mod_ref, g_ref, o_ref):
    h = _moe_combine(h1_ref[...], y0_ref[...], y1_ref[...], gt_ref[...], mod_ref[0])
    o_ref[...] = h * lax.rsqrt(jnp.mean(h * h, axis=-1, keepdims=True) + EPS) * g_ref[...]


def _final(h1, y0, y1, gates, mod_l, final_g, seq):
    n, d = h1.shape
    tm = 512
    per_b = seq // tm
    row = lambda i: (i, 0)
    return pl.pallas_call(
        _final_kernel,
        out_shape=jax.ShapeDtypeStruct((n, d), F32),
        grid=(n // tm,),
        in_specs=[pl.BlockSpec((tm, d), row), pl.BlockSpec((tm, d), row), pl.BlockSpec((tm, d), row),
                  pl.BlockSpec((tm, TOP_K), row), pl.BlockSpec((1, 6, d), lambda i: (i // per_b, 0, 0)),
                  _const_spec((1, d))],
        out_specs=pl.BlockSpec((tm, d), row),
        compiler_params=_cparams(("arbitrary",)),
        name="final_norm",
    )(h1, y0, y1, gates, mod_l, final_g.reshape(1, d))


def kernel(x, c, ada_w, ada_b, norm1_g, norm2_g, a_w_in, a_ln_g, a_ln_b, a_w_sp, a_b_sp, a_w_out, b_w_in,
           b_kv_norm_g, b_w_uk, b_w_uv, b_kidx_g, b_w_out, rel_bias, router_w, router_b, moe_w_gate,
           moe_w_up, moe_w_down, final_g):
    bsz, seq, d = x.shape
    n = bsz * seq
    mod = _adaln(c, ada_w, ada_b).reshape(ada_w.shape[0], bsz, 6, d)
    h = x.reshape(n, d)

    h1, hn2, logits = _gmlp_layer(h, mod[0], norm1_g[0], a_w_in[0], a_ln_g[0], a_ln_b[0], a_w_sp[0],
                                  a_b_sp[0], a_w_out[0], norm2_g[0], router_w, seq)
    y0, y1, gates = _moe(hn2, logits, router_b, moe_w_gate, moe_w_up, moe_w_down, 0)

    h, qa, ckv, qi, ki, wi = _bproj(h1, y0, y1, gates, mod[0], mod[1], norm1_g[1], b_w_in[0],
                                    b_kv_norm_g[0], b_w_uk[0], b_kidx_g[0], seq)
    o_lat = _attention(qa, qi, wi, ckv, ki, _bias_tables(rel_bias), bsz, seq)
    h1, hn2, logits = _bout(h, o_lat, mod[1], b_w_uv[0], b_w_out[0], norm2_g[1], router_w, seq)
    y0, y1, gates = _moe(hn2, logits, router_b, moe_w_gate, moe_w_up, moe_w_down, 1)

    out = _final(h1, y0, y1, gates, mod[1], final_g, seq)
    return out.reshape(bsz, seq, d)
```

```python
import functools
import math

import jax
import jax.numpy as jnp
from jax import lax
from jax.experimental import pallas as pl
from jax.experimental.pallas import tpu as pltpu

F32 = jnp.float32
BF16 = jnp.bfloat16
I32 = jnp.int32
HIGHEST = lax.Precision.HIGHEST

EPS = 1e-6
A_CHUNK = 128
A_GROUPS = 8
B_HEADS = 16
B_HEAD_DIM = 64
B_V_DIM = 64
B_KV_LATENT = 256
B_IDX_HEADS = 8
B_IDX_DIM = 64
B_TOPK_MAX = 256
QBLK = 256
KTILE = 256
REL_BUCKETS = 32
REL_MAX_DIST = 128
N_EXPERTS = 16
N_GROUPS = 4
EXPERTS_PER_GROUP = 4
TOP_K = 2
MOE_ROWS = 1024
MOE_SUB = 256
MOE_FT = 512

ROW_TILE = 256
VMEM_LIMIT = 56 * 1024 * 1024

INT_MIN = -2 ** 31
NEG_BIG = -1e30


def _cparams(sem):
    return pltpu.CompilerParams(dimension_semantics=sem, vmem_limit_bytes=VMEM_LIMIT)


def _mod_rmsnorm(h, g, scale, shift):
    ms = jnp.mean(h * h, axis=-1, keepdims=True)
    return (h * lax.rsqrt(ms + EPS) * g) * (1.0 + scale) + shift


def _gelu_tanh(x):
    c = math.sqrt(2.0 / math.pi)
    return 0.5 * x * (1.0 + jnp.tanh(c * (x + 0.044715 * (x * x * x))))


def _adaln_kernel(c_ref, w_ref, b_ref, o_ref):
    c = c_ref[...]
    sc = c * jax.nn.sigmoid(c)
    o_ref[0] = jnp.dot(sc, w_ref[0], precision=HIGHEST, preferred_element_type=F32) + b_ref[0]


def _adaln(c, ada_w, ada_b):
    depth, d, e = ada_w.shape
    bsz = c.shape[0]
    bp = 8
    c_pad = jnp.zeros((bp, d), F32).at[:bsz].set(c)
    tn = 1024
    out = pl.pallas_call(
        _adaln_kernel,
        out_shape=jax.ShapeDtypeStruct((depth, bp, e), F32),
        grid=(depth, e // tn),
        in_specs=[
            pl.BlockSpec((bp, d), lambda l, j: (0, 0)),
            pl.BlockSpec((1, d, tn), lambda l, j: (l, 0, j)),
            pl.BlockSpec((1, 1, tn), lambda l, j: (l, 0, j)),
        ],
        out_specs=pl.BlockSpec((1, bp, tn), lambda l, j: (l, 0, j)),
        compiler_params=_cparams(("arbitrary", "arbitrary")),
        name="adaln",
    )(c_pad, ada_w, ada_b.reshape(depth, 1, e))
    return out[:, :bsz]


def _residual_epilogue(h, y, mod, n2g, rw, h_out, hn_out, lg_out):
    h1 = h + mod[2:3] * y
    h_out[...] = h1
    hn2 = _mod_rmsnorm(h1, n2g, mod[4:5], mod[3:4])
    bits = pltpu.bitcast(hn2.astype(BF16).astype(F32), I32)
    half = bits.shape[1] // 2
    hn_out[...] = bits[:, :half] | lax.shift_right_logical(bits[:, half:], 16)
    lg_out[...] = jnp.dot(hn2, rw, precision=HIGHEST, preferred_element_type=F32)


def _gmlp_kernel(h_ref, mod_ref, n1g_ref, win_ref, lng_ref, lnb_ref, wsp_ref, bsp_ref, wout_ref,
                 n2g_ref, rw_ref, h_out, hn_out, lg_out, u_scr, v_scr, s_scr):
    tm = h_ref.shape[0]
    inner = u_scr.shape[1]
    gw = inner // A_GROUPS
    tn = 512
    h = h_ref[...]
    mod = mod_ref[0]
    hn = _mod_rmsnorm(h, n1g_ref[...], mod[1:2], mod[0:1]).astype(BF16)
    for j in range(2 * inner // tn):
        z = _gelu_tanh(jnp.dot(hn, win_ref[:, j * tn:(j + 1) * tn], preferred_element_type=F32))
        if j < inner // tn:
            u_scr[:, j * tn:(j + 1) * tn] = z
        else:
            jj = j - inner // tn
            v_scr[:, jj * tn:(jj + 1) * tn] = z
    vsum = jnp.zeros((tm, 1), F32)
    for j in range(inner // tn):
        vsum = vsum + jnp.sum(v_scr[:, j * tn:(j + 1) * tn], axis=-1, keepdims=True)
    mu = vsum * (1.0 / inner)
    vsq = jnp.zeros((tm, 1), F32)
    for j in range(inner // tn):
        d = v_scr[:, j * tn:(j + 1) * tn] - mu
        vsq = vsq + jnp.sum(d * d, axis=-1, keepdims=True)
    rstd = lax.rsqrt(vsq * (1.0 / inner) + EPS)
    row = lax.broadcasted_iota(I32, (A_CHUNK, A_CHUNK), 0)
    col = lax.broadcasted_iota(I32, (A_CHUNK, A_CHUNK), 1)
    tril = row >= col
    for g in range(A_GROUPS):
        ws = jnp.where(tril, wsp_ref[g], 0.0).astype(BF16)
        bcol = bsp_ref[:, g:g + 1]
        lg = lng_ref[:, g * gw:(g + 1) * gw]
        lb = lnb_ref[:, g * gw:(g + 1) * gw]
        for c in range(tm // A_CHUNK):
            r0, r1 = c * A_CHUNK, (c + 1) * A_CHUNK
            vt = v_scr[r0:r1, g * gw:(g + 1) * gw]
            vn = ((vt - mu[r0:r1]) * rstd[r0:r1]) * lg + lb
            fv = jnp.dot(ws, vn.astype(BF16), preferred_element_type=F32) + bcol
            s_scr[r0:r1, g * gw:(g + 1) * gw] = (u_scr[r0:r1, g * gw:(g + 1) * gw] * fv).astype(BF16)
    y = jnp.dot(s_scr[...], wout_ref[...], preferred_element_type=F32)
    _residual_epilogue(h, y, mod, n2g_ref[...], rw_ref[...], h_out, hn_out, lg_out)


def _const_spec(shape):
    nd = len(shape)
    return pl.BlockSpec(shape, lambda i, _nd=nd: (0,) * _nd, pipeline_mode=pl.Buffered(1))


def _gmlp_layer(h, mod_l, n1g, w_in, ln_g, ln_b, w_sp, b_sp, w_out, n2g, router_w, seq):
    n, d = h.shape
    inner = w_out.shape[0]
    tm = ROW_TILE
    per_b = seq // tm
    ne = router_w.shape[1]
    return pl.pallas_call(
        _gmlp_kernel,
        out_shape=(jax.ShapeDtypeStruct((n, d), F32), jax.ShapeDtypeStruct((n, d // 2), I32),
                   jax.ShapeDtypeStruct((n, ne), F32)),
        grid=(n // tm,),
        in_specs=[
            pl.BlockSpec((tm, d), lambda i: (i, 0)),
            pl.BlockSpec((1, 6, d), lambda i: (i // per_b, 0, 0)),
            _const_spec((1, d)),
            _const_spec((d, 2 * inner)),
            _const_spec((1, inner)),
            _const_spec((1, inner)),
            _const_spec((A_GROUPS, A_CHUNK, A_CHUNK)),
            _const_spec((A_CHUNK, A_GROUPS)),
            _const_spec((inner, d)),
            _const_spec((1, d)),
            _const_spec((d, ne)),
        ],
        out_specs=(pl.BlockSpec((tm, d), lambda i: (i, 0)), pl.BlockSpec((tm, d // 2), lambda i: (i, 0)),
                   pl.BlockSpec((tm, ne), lambda i: (i, 0))),
        scratch_shapes=[pltpu.VMEM((tm, inner), F32), pltpu.VMEM((tm, inner), F32),
                        pltpu.VMEM((tm, inner), BF16)],
        compiler_params=_cparams(("arbitrary",)),
        name="gmlp_layer",
    )(h, mod_l, n1g.reshape(1, d), w_in.astype(BF16), ln_g.reshape(1, inner), ln_b.reshape(1, inner),
      w_sp, b_sp.T, w_out.astype(BF16), n2g.reshape(1, d), router_w)


def _first_max4(rows):
    m = jnp.maximum(jnp.maximum(rows[0], rows[1]), jnp.maximum(rows[2], rows[3]))
    idx = jnp.where(rows[0] == m, 0, jnp.where(rows[1] == m, 1, jnp.where(rows[2] == m, 2, 3)))
    return m, idx.astype(I32)


def _route_kernel(lt_ref, rb_ref, e_ref, g_ref):
    aff = jax.nn.sigmoid(lt_ref[...])
    sel = aff + rb_ref[...]
    neg = jnp.float32(-jnp.inf)
    g_score, g_i1, g_i2 = [], [], []
    for g in range(N_GROUPS):
        rows = [sel[4 * g + k:4 * g + k + 1, :] for k in range(EXPERTS_PER_GROUP)]
        m1, i1 = _first_max4(rows)
        rest = [jnp.where(i1 == k, neg, rows[k]) for k in range(EXPERTS_PER_GROUP)]
        m2, i2 = _first_max4(rest)
        g_score.append(m1 + m2)
        g_i1.append(i1)
        g_i2.append(i2)
    _, grp = _first_max4(g_score)
    l1 = jnp.zeros_like(grp)
    l2 = jnp.zeros_like(grp)
    for g in range(N_GROUPS):
        l1 = jnp.where(grp == g, g_i1[g], l1)
        l2 = jnp.where(grp == g, g_i2[g], l2)
    e1 = grp * EXPERTS_PER_GROUP + l1
    e2 = grp * EXPERTS_PER_GROUP + l2
    a1 = jnp.zeros_like(g_score[0])
    a2 = jnp.zeros_like(g_score[0])
    for e in range(N_EXPERTS):
        ae = aff[e:e + 1, :]
        a1 = jnp.where(e1 == e, ae, a1)
        a2 = jnp.where(e2 == e, ae, a2)
    tot = a1 + a2
    e_ref[0:1, :] = e1
    e_ref[1:2, :] = e2
    g_ref[0:1, :] = a1 / tot
    g_ref[1:2, :] = a2 / tot


def _route(logits, router_b):
    n, ne = logits.shape
    tn = min(2048, n)
    return pl.pallas_call(
        _route_kernel,
        out_shape=(jax.ShapeDtypeStruct((TOP_K, n), I32), jax.ShapeDtypeStruct((TOP_K, n), F32)),
        grid=(n // tn,),
        in_specs=[pl.BlockSpec((ne, tn), lambda i: (0, i)), pl.BlockSpec((ne, 1), lambda i: (0, 0))],
        out_specs=(pl.BlockSpec((TOP_K, tn), lambda i: (0, i)), pl.BlockSpec((TOP_K, tn), lambda i: (0, i))),
        compiler_params=_cparams(("arbitrary",)),
        name="route",
    )(logits.T, router_b.reshape(ne, 1))


def _slot_kernel(e_ref, slot_ref, cnt_ref):
    n = e_ref.shape[1]
    nblk = n // 128
    eid = lax.broadcasted_iota(I32, (N_EXPERTS, 128), 0)
    r = lax.broadcasted_iota(I32, (128, 128), 0)
    c = lax.broadcasted_iota(I32, (128, 128), 1)
    upper = jnp.where(r <= c, 1.0, 0.0).astype(BF16)

    def hits(b):
        o = pl.multiple_of(b * 128, 128)
        t1 = e_ref[0:1, pl.ds(o, 128)] == eid
        t2 = e_ref[1:2, pl.ds(o, 128)] == eid
        return o, t1, t2, jnp.where(t1, 1.0, 0.0) + jnp.where(t2, 1.0, 0.0)

    def count_body(b, acc):
        return acc + hits(b)[3]

    cnt = jnp.sum(lax.fori_loop(0, nblk, count_body, jnp.zeros((N_EXPERTS, 128), F32)),
                  axis=-1, keepdims=True)
    cnt_ref[...] = cnt.astype(I32)
    padded = jnp.floor((cnt + (MOE_ROWS - 1)) * (1.0 / MOE_ROWS)) * MOE_ROWS
    er = lax.broadcasted_iota(I32, (N_EXPERTS, N_EXPERTS), 0)
    ec = lax.broadcasted_iota(I32, (N_EXPERTS, N_EXPERTS), 1)
    padded_row = jnp.sum(jnp.where(er == ec, padded, 0.0), axis=0, keepdims=True)
    start = jnp.sum(jnp.where(ec < er, padded_row, 0.0), axis=-1, keepdims=True)

    def slot_body(b, carry):
        o, t1, t2, t = hits(b)
        incl = jnp.dot(t.astype(BF16), upper, preferred_element_type=F32)
        pos = carry + incl - t
        slot_ref[0:1, pl.ds(o, 128)] = jnp.sum(jnp.where(t1, pos, 0.0), axis=0, keepdims=True).astype(I32)
        slot_ref[1:2, pl.ds(o, 128)] = jnp.sum(jnp.where(t2, pos, 0.0), axis=0, keepdims=True).astype(I32)
        return carry + incl[:, 127:128]

    lax.fori_loop(0, nblk, slot_body, start)


def _slot_tables(experts):
    n = experts.shape[1]
    a = n * TOP_K
    slot, counts = pl.pallas_call(
        _slot_kernel,
        out_shape=(jax.ShapeDtypeStruct((TOP_K, n), I32), jax.ShapeDtypeStruct((N_EXPERTS, 1), I32)),
        compiler_params=pltpu.CompilerParams(vmem_limit_bytes=VMEM_LIMIT),
        name="slots",
    )(experts)
    counts = counts.reshape(N_EXPERTS)
    padded = ((counts + MOE_ROWS - 1) // MOE_ROWS) * MOE_ROWS
    pad_end = jnp.cumsum(padded)
    pad_start = pad_end - padded
    p = a + N_EXPERTS * MOE_ROWS
    n_items = p // MOE_ROWS
    tok = jnp.broadcast_to(jnp.arange(n, dtype=I32)[None, :], (TOP_K, n))
    tok_of_slot = jnp.zeros((p,), I32).at[slot.reshape(a)].set(tok.reshape(a))
    slot_of_assign = slot.T
    item_start = jnp.arange(n_items, dtype=I32) * MOE_ROWS
    n_used = pad_end[-1] // MOE_ROWS
    item_e_raw = jnp.minimum(jnp.searchsorted(pad_end, item_start, side='right'), N_EXPERTS - 1).astype(I32)
    used = item_start < pad_end[-1]
    last_e = item_e_raw[jnp.maximum(n_used - 1, 0)]
    item_e = jnp.where(used, item_e_raw, last_e).astype(I32)
    valid = jnp.clip(counts[item_e_raw] - (item_start - pad_start[item_e_raw]), 0, MOE_ROWS)
    item_rows = jnp.where(used, valid, 0).astype(I32)
    item_blk = jnp.where(used, jnp.arange(n_items, dtype=I32), jnp.maximum(n_used - 1, 0)).astype(I32)
    return tok_of_slot, slot_of_assign, item_e, item_rows, item_blk


def _moe_kernel(ie_ref, ir_ref, ib_ref, x_ref, wg_ref, wu_ref, wd_ref, o_ref, wg_s, wu_s, wd_s, x_s):
    it = pl.program_id(0)
    f = pl.program_id(1)
    nrows = ir_ref[it]
    half = x_ref.shape[1]

    @pl.when(f == 0)
    def _():
        o_ref[...] = jnp.zeros_like(o_ref)
        pk = x_ref[...]
        x_s[:, :half] = pltpu.bitcast(pk & jnp.int32(-65536), F32).astype(BF16)
        x_s[:, half:] = pltpu.bitcast(lax.shift_left(pk, 16), F32).astype(BF16)

    @pl.when(nrows > 0)
    def _():
        wg_s[...] = wg_ref[0, 0].astype(BF16)
        wu_s[...] = wu_ref[0, 0].astype(BF16)
        wd_s[...] = wd_ref[0, 0].astype(BF16)
        nsb = (nrows + MOE_SUB - 1) // MOE_SUB

        def sub_block(sb):
            r0 = pl.multiple_of(sb * MOE_SUB, MOE_SUB)
            x = x_s[pl.ds(r0, MOE_SUB), :]
            g = jnp.dot(x, wg_s[...], preferred_element_type=F32)
            u = jnp.dot(x, wu_s[...], preferred_element_type=F32)
            hmid = ((g * jax.nn.sigmoid(g)) * u).astype(BF16)
            o_ref[pl.ds(r0, MOE_SUB), :] += jnp.dot(hmid, wd_s[...], preferred_element_type=F32)

        def pair_body(pi, carry):
            sub_block(2 * pi)
            sub_block(2 * pi + 1)
            return carry

        lax.fori_loop(0, nsb // 2, pair_body, 0)

        @pl.when(nsb % 2 == 1)
        def _():
            sub_block(nsb - 1)


def _moe_experts(x_sorted, item_e, item_rows, item_blk, w_gate, w_up, w_down, layer):
    p, half = x_sorted.shape
    d = 2 * half
    n_items = p // MOE_ROWS
    dff = w_gate.shape[3]
    nf = dff // MOE_FT
    grid_spec = pltpu.PrefetchScalarGridSpec(
        num_scalar_prefetch=3,
        grid=(n_items, nf),
        in_specs=[
            pl.BlockSpec((MOE_ROWS, half), lambda i, f, ie, ir, ib: (ib[i], 0)),
            pl.BlockSpec((1, 1, d, MOE_FT),
                         lambda i, f, ie, ir, ib: (layer, ie[i], 0, jnp.where(ir[i] > 0, f, nf - 1))),
            pl.BlockSpec((1, 1, d, MOE_FT),
                         lambda i, f, ie, ir, ib: (layer, ie[i], 0, jnp.where(ir[i] > 0, f, nf - 1))),
            pl.BlockSpec((1, 1, MOE_FT, d),
                         lambda i, f, ie, ir, ib: (layer, ie[i], jnp.where(ir[i] > 0, f, nf - 1), 0)),
        ],
        out_specs=pl.BlockSpec((MOE_ROWS, d), lambda i, f, ie, ir, ib: (i, 0)),
        scratch_shapes=[pltpu.VMEM((d, MOE_FT), BF16), pltpu.VMEM((d, MOE_FT), BF16),
                        pltpu.VMEM((MOE_FT, d), BF16), pltpu.VMEM((MOE_ROWS, d), BF16)],
    )
    return pl.pallas_call(
        _moe_kernel,
        out_shape=jax.ShapeDtypeStruct((p, d), F32),
        grid_spec=grid_spec,
        compiler_params=_cparams(("arbitrary", "arbitrary")),
        name="moe_experts",
    )(item_e, item_rows, item_blk, x_sorted, w_gate, w_up, w_down)


def _moe(hn2, logits, router_b, w_gate, w_up, w_down, layer):
    experts, gates = _route(logits, router_b)
    tok_of_slot, slot_of_assign, item_e, item_rows, item_blk = _slot_tables(experts)
    x_sorted = hn2.at[tok_of_slot].get(mode='promise_in_bounds')
    y_slot = _moe_experts(x_sorted, item_e, item_rows, item_blk, w_gate, w_up, w_down, layer)
    y0 = y_slot.at[slot_of_assign[:, 0]].get(mode='promise_in_bounds')
    y1 = y_slot.at[slot_of_assign[:, 1]].get(mode='promise_in_bounds')
    return y0, y1, gates.T


def _moe_combine(h1, y0, y1, gates, mod):
    return h1 + mod[5:6] * (gates[:, 0:1] * y0 + gates[:, 1:2] * y1)


def _bproj_kernel(h1_ref, y0_ref, y1_ref, gt_ref, modp_ref, mod_ref, n1g_ref, w_ref, kvg_ref, kig_ref,
                  wuk_ref, h_out, qa_out, ckv_out, qi_out, ki_out, wi_out):
    tm = h1_ref.shape[0]
    h = _moe_combine(h1_ref[...], y0_ref[...], y1_ref[...], gt_ref[...], modp_ref[0])
    h_out[...] = h
    mod = mod_ref[0]
    hn = _mod_rmsnorm(h, n1g_ref[...], mod[1:2], mod[0:1]).astype(BF16)
    proj = jnp.dot(hn, w_ref[...], preferred_element_type=F32)
    o1 = B_HEADS * B_HEAD_DIM
    o2 = o1 + B_KV_LATENT
    o3 = o2 + B_IDX_HEADS * B_IDX_DIM
    ckv = proj[:, o1:o2]
    ckv = ckv * lax.rsqrt(jnp.mean(ckv * ckv, axis=-1, keepdims=True) + EPS) * kvg_ref[...]
    ckv_out[...] = ckv.astype(BF16)
    tail = proj[:, o3:o3 + 128]
    ki = tail[:, :B_IDX_DIM]
    ki = ki * lax.rsqrt(jnp.mean(ki * ki, axis=-1, keepdims=True) + EPS) * kig_ref[...]
    ki_out[...] = ki.astype(BF16)
    wi = tail[:, B_IDX_DIM:B_IDX_DIM + B_IDX_HEADS] * (B_IDX_HEADS ** -0.5 * B_IDX_DIM ** -0.5)
    scale = B_HEAD_DIM ** -0.5
    for blk in range(tm // QBLK):
        r0, r1 = blk * QBLK, (blk + 1) * QBLK
        for hh in range(B_HEADS):
            qh = proj[r0:r1, hh * B_HEAD_DIM:(hh + 1) * B_HEAD_DIM].astype(BF16)
            qa = jnp.dot(qh, wuk_ref[hh], preferred_element_type=F32) * scale
            qa_out[blk, hh * QBLK:(hh + 1) * QBLK, :] = qa.astype(BF16)
        for hh in range(B_IDX_HEADS):
            qi_out[blk, hh * QBLK:(hh + 1) * QBLK, :] = proj[r0:r1, o2 + hh * B_IDX_DIM:
                                                            o2 + (hh + 1) * B_IDX_DIM].astype(BF16)
            wi_out[blk, hh * QBLK:(hh + 1) * QBLK, :] = wi[r0:r1, hh:hh + 1]


def _bproj(h1, y0, y1, gates, mod_prev, mod_l, n1g, b_w_in, kv_g, w_uk, kidx_g, seq):
    n, d = h1.shape
    tm = ROW_TILE
    per_b = seq // tm
    nq = n // QBLK
    qpb = tm // QBLK
    o1 = B_HEADS * B_HEAD_DIM
    o2 = o1 + B_KV_LATENT
    o3 = o2 + B_IDX_HEADS * B_IDX_DIM
    wcat = jnp.zeros((d, o3 + 128), F32).at[:, :b_w_in.shape[1]].set(b_w_in).astype(BF16)
    wuk_t = jnp.transpose(w_uk, (1, 2, 0)).astype(BF16)
    row = lambda i: (i, 0)
    modm = lambda i: (i // per_b, 0, 0)
    blk3 = lambda i: (i, 0, 0)
    return pl.pallas_call(
        _bproj_kernel,
        out_shape=(
            jax.ShapeDtypeStruct((n, d), F32),
            jax.ShapeDtypeStruct((nq, B_HEADS * QBLK, B_KV_LATENT), BF16),
            jax.ShapeDtypeStruct((n, B_KV_LATENT), BF16),
            jax.ShapeDtypeStruct((nq, B_IDX_HEADS * QBLK, B_IDX_DIM), BF16),
            jax.ShapeDtypeStruct((n, B_IDX_DIM), BF16),
            jax.ShapeDtypeStruct((nq, B_IDX_HEADS * QBLK, 1), F32),
        ),
        grid=(n // tm,),
        in_specs=[
            pl.BlockSpec((tm, d), row), pl.BlockSpec((tm, d), row), pl.BlockSpec((tm, d), row),
            pl.BlockSpec((tm, TOP_K), row),
            pl.BlockSpec((1, 6, d), modm), pl.BlockSpec((1, 6, d), modm),
            _const_spec((1, d)),
            _const_spec((d, o3 + 128)),
            _const_spec((1, B_KV_LATENT)),
            _const_spec((1, B_IDX_DIM)),
            _const_spec((B_HEADS, B_HEAD_DIM, B_KV_LATENT)),
        ],
        out_specs=(
            pl.BlockSpec((tm, d), row),
            pl.BlockSpec((qpb, B_HEADS * QBLK, B_KV_LATENT), blk3),
            pl.BlockSpec((tm, B_KV_LATENT), row),
            pl.BlockSpec((qpb, B_IDX_HEADS * QBLK, B_IDX_DIM), blk3),
            pl.BlockSpec((tm, B_IDX_DIM), row),
            pl.BlockSpec((qpb, B_IDX_HEADS * QBLK, 1), blk3),
        ),
        compiler_params=_cparams(("arbitrary",)),
        name="dsa_proj",
    )(h1, y0, y1, gates, mod_prev, mod_l, n1g.reshape(1, d), wcat, kv_g.reshape(1, -1),
      kidx_g.reshape(1, -1), wuk_t)


def _t5_bucket(dist):
    n = jnp.maximum(dist, 0)
    exact = REL_BUCKETS // 2
    nf = jnp.maximum(n, 1).astype(F32)
    large = exact + (jnp.log(nf / exact) / math.log(REL_MAX_DIST / exact)
                     * (REL_BUCKETS - exact)).astype(I32)
    large = jnp.minimum(large, REL_BUCKETS - 1)
    return jnp.where(n < exact, n, large)


def _bias_tables(rel_bias):
    assert REL_MAX_DIST <= 128
    t = jnp.arange(128, dtype=I32)[:, None]
    s = jnp.arange(128, dtype=I32)[None, :]
    far = rel_bias[REL_BUCKETS - 1]
    diag = rel_bias[_t5_bucket(t - s)] - far
    prev = rel_bias[_t5_bucket(t - s + 128)] - far
    return jnp.stack([jnp.transpose(diag, (2, 0, 1)), jnp.transpose(prev, (2, 0, 1))])


def _attn_kernel(qa_ref, qi_ref, wi_ref, ckv_ref, ki_ref, bt_ref, o_ref, kbuf, m_scr, l_scr, a_scr, acc_scr,
                 s_scr, p_scr, madd_scr):
    i = pl.program_id(1)
    nt = i + 1
    t_row = i * QBLK + lax.broadcasted_iota(I32, (QBLK, KTILE), 0)
    lane = lax.broadcasted_iota(I32, (QBLK, KTILE), 1)

    qi = qi_ref[0]
    wcol = wi_ref[0]

    def score_body(j, carry):
        k0 = pl.multiple_of(j * KTILE, KTILE)
        kt = ki_ref[0, pl.ds(k0, KTILE), :]
        p = lax.dot_general(qi, kt, (((1,), (1,)), ((), ())), preferred_element_type=F32)
        p = jnp.maximum(p, 0.0) * wcol
        sc = p[0:QBLK]
        for hh in range(1, B_IDX_HEADS):
            sc = sc + p[hh * QBLK:(hh + 1) * QBLK]
        bits = pltpu.bitcast(sc + 0.0, I32)
        key = jnp.where(bits < 0, bits ^ jnp.int32(0x7FFFFFFF), bits)
        key = jnp.where(k0 + lane <= t_row, key, jnp.int32(INT_MIN))
        kbuf[:, pl.ds(k0, KTILE)] = key
        return carry

    lax.fori_loop(0, nt, score_body, 0)

    def fold_lanes(x):
        out = x[:, 0:128]
        for k in range(1, KTILE // 128):
            out = out + x[:, k * 128:(k + 1) * 128]
        return out

    def count_ge(cand):
        def body(j, acc):
            k0 = pl.multiple_of(j * KTILE, KTILE)
            return acc + fold_lanes(jnp.where(kbuf[:, pl.ds(k0, KTILE)] >= cand, 1, 0))
        acc = lax.fori_loop(0, nt, body, jnp.zeros((QBLK, 128), I32))
        return jnp.sum(acc, axis=-1, keepdims=True)

    def bit_step(b, lo, n_ge):
        cand = lo + lax.shift_left(jnp.int32(1), 31 - b)
        cnt = count_ge(cand)
        take = cnt >= B_TOPK_MAX
        return jnp.where(take, cand, lo), jnp.where(take, cnt, n_ge)

    def bit_cond(c):
        return (c[0] < 32) & (c[3] == 0)

    def bit_body(c):
        b, lo, n_ge, _ = c
        lo, n_ge = bit_step(b, lo, n_ge)
        lo, n_ge = bit_step(b + 1, lo, n_ge)
        done = (jnp.max(n_ge) == B_TOPK_MAX).astype(I32)
        return b + 2, lo, n_ge, done

    _, tau, n_ge, _ = lax.while_loop(
        bit_cond, bit_body,
        (jnp.int32(0), jnp.full((QBLK, 1), INT_MIN, I32), jnp.full((QBLK, 1), nt * KTILE, I32),
         (i < 1).astype(I32)))
    tau = jnp.maximum(tau, jnp.int32(INT_MIN + 1))
    excess = (i >= 1) & (jnp.max(n_ge) > B_TOPK_MAX)

    @pl.when(excess)
    def _():
        n_gt = count_ge(tau + 1)
        need = B_TOPK_MAX - n_gt

        def count_eq_before(pos):
            def body(j, acc):
                k0 = pl.multiple_of(j * KTILE, KTILE)
                hit = (kbuf[:, pl.ds(k0, KTILE)] == tau) & (k0 + lane < pos)
                return acc + fold_lanes(jnp.where(hit, 1, 0))
            acc = lax.fori_loop(0, nt, body, jnp.zeros((QBLK, 128), I32))
            return jnp.sum(acc, axis=-1, keepdims=True)

        def pos_body(b, pos):
            cand = pos + lax.shift_left(jnp.int32(1), 12 - b)
            return jnp.where(count_eq_before(cand) < need, cand, pos)

        pos = lax.fori_loop(0, 13, pos_body, jnp.zeros((QBLK, 1), I32))

        def drop_body(j, carry):
            k0 = pl.multiple_of(j * KTILE, KTILE)
            kk = kbuf[:, pl.ds(k0, KTILE)]
            drop = (kk == tau) & (k0 + lane > pos) & (n_ge > B_TOPK_MAX)
            kbuf[:, pl.ds(k0, KTILE)] = jnp.where(drop, jnp.int32(INT_MIN), kk)
            return carry

        lax.fori_loop(0, nt, drop_body, 0)

    m_scr[...] = jnp.full_like(m_scr, NEG_BIG)
    l_scr[...] = jnp.zeros_like(l_scr)
    acc_scr[...] = jnp.zeros_like(acc_scr)

    def attend(j, tile_off):
        k0 = pl.multiple_of(j * KTILE, KTILE)
        kv = ckv_ref[0, pl.ds(k0, KTILE), :]
        s_scr[...] = lax.dot_general(qa_ref[0], kv, (((1,), (1,)), ((), ())), preferred_element_type=F32)
        madd_scr[...] = jnp.where(kbuf[:, pl.ds(k0, KTILE)] >= tau, 0.0, NEG_BIG)
        for ch in range(B_HEADS * QBLK // 128):
            r0, r1 = ch * 128, (ch + 1) * 128
            hh, q0 = r0 // QBLK, r0 % QBLK
            parts = []
            for kc in range(KTILE // 128):
                sp = s_scr[r0:r1, kc * 128:(kc + 1) * 128] + madd_scr[q0:q0 + 128, kc * 128:(kc + 1) * 128]
                if tile_off is not None:
                    rel = (QBLK // 128) * tile_off + q0 // 128 - kc
                    if rel in (0, 1):
                        sp = sp + bt_ref[rel, hh]
                parts.append(sp)
            smax = parts[0]
            for sp in parts[1:]:
                smax = jnp.maximum(smax, sp)
            m_old = m_scr[r0:r1, :]
            m_new = jnp.maximum(m_old, jnp.max(smax, axis=-1, keepdims=True))
            alpha = jnp.exp(m_old - m_new)
            psum = None
            for kc, sp in enumerate(parts):
                p = jnp.exp(sp - m_new)
                p_scr[r0:r1, kc * 128:(kc + 1) * 128] = p.astype(BF16)
                psum = p if psum is None else psum + p
            l_scr[r0:r1, :] = alpha * l_scr[r0:r1, :] + jnp.sum(psum, axis=-1, keepdims=True)
            m_scr[r0:r1, :] = m_new
            a_scr[r0:r1, :] = alpha
        pv = jnp.dot(p_scr[...], kv, preferred_element_type=F32)
        acc_scr[...] = pltpu.repeat(a_scr[...], B_KV_LATENT // 128, axis=1) * acc_scr[...] + pv

    def far_body(j, carry):
        attend(j, None)
        return carry

    lax.fori_loop(0, jnp.maximum(i - 1, 0), far_body, 0)

    @pl.when(i >= 1)
    def _():
        attend(i - 1, 1)

    attend(i, 0)
    inv_l = 1.0 / l_scr[...]
    o_ref[0] = (acc_scr[...] * pltpu.repeat(inv_l, B_KV_LATENT // 128, axis=1)).astype(BF16)


def _attention(qa, qi, wi, ckv, ki, btab, bsz, seq):
    nqb = seq // QBLK
    gq = lambda b, i: (b * nqb + i, 0, 0)
    gb = lambda b, i: (b, 0, 0)
    return pl.pallas_call(
        _attn_kernel,
        out_shape=jax.ShapeDtypeStruct(qa.shape, BF16),
        grid=(bsz, nqb),
        in_specs=[
            pl.BlockSpec((1, B_HEADS * QBLK, B_KV_LATENT), gq),
            pl.BlockSpec((1, B_IDX_HEADS * QBLK, B_IDX_DIM), gq),
            pl.BlockSpec((1, B_IDX_HEADS * QBLK, 1), gq),
            pl.BlockSpec((1, seq, B_KV_LATENT), gb),
            pl.BlockSpec((1, seq, B_IDX_DIM), gb),
            pl.BlockSpec((2, B_HEADS, 128, 128), lambda b, i: (0, 0, 0, 0), pipeline_mode=pl.Buffered(1)),
        ],
        out_specs=pl.BlockSpec((1, B_HEADS * QBLK, B_KV_LATENT), gq),
        scratch_shapes=[
            pltpu.VMEM((QBLK, seq), I32),
            pltpu.VMEM((B_HEADS * QBLK, 128), F32),
            pltpu.VMEM((B_HEADS * QBLK, 128), F32),
            pltpu.VMEM((B_HEADS * QBLK, 128), F32),
            pltpu.VMEM((B_HEADS * QBLK, B_KV_LATENT), F32),
            pltpu.VMEM((B_HEADS * QBLK, KTILE), F32),
            pltpu.VMEM((B_HEADS * QBLK, KTILE), BF16),
            pltpu.VMEM((QBLK, KTILE), F32),
        ],
        compiler_params=_cparams(("arbitrary", "arbitrary")),
        name="dsa_attention",
    )(qa, qi, wi, ckv.reshape(bsz, seq, -1), ki.reshape(bsz, seq, -1), btab)


def _bout_kernel(h_ref, ol_ref, mod_ref, wuv_ref, wout_ref, n2g_ref, rw_ref, h_out, hn_out, lg_out, o_scr):
    mod = mod_ref[0]
    for blk in range(ol_ref.shape[0]):
        for hh in range(B_HEADS):
            oh = jnp.dot(ol_ref[blk, hh * QBLK:(hh + 1) * QBLK, :], wuv_ref[hh], preferred_element_type=F32)
            o_scr[blk * QBLK:(blk + 1) * QBLK, hh * B_V_DIM:(hh + 1) * B_V_DIM] = oh.astype(BF16)
    y = jnp.dot(o_scr[...], wout_ref[...], preferred_element_type=F32)
    _residual_epilogue(h_ref[...], y, mod, n2g_ref[...], rw_ref[...], h_out, hn_out, lg_out)


def _bout(h, o_lat, mod_l, w_uv, w_out, n2g, router_w, seq):
    n, d = h.shape
    tm = ROW_TILE
    per_b = seq // tm
    qpb = tm // QBLK
    ne = router_w.shape[1]
    wuv_t = jnp.transpose(w_uv, (1, 0, 2)).astype(BF16)
    row = lambda i: (i, 0)
    return pl.pallas_call(
        _bout_kernel,
        out_shape=(jax.ShapeDtypeStruct((n, d), F32), jax.ShapeDtypeStruct((n, d // 2), I32),
                   jax.ShapeDtypeStruct((n, ne), F32)),
        grid=(n // tm,),
        in_specs=[
            pl.BlockSpec((tm, d), row),
            pl.BlockSpec((qpb, B_HEADS * QBLK, B_KV_LATENT), lambda i: (i, 0, 0)),
            pl.BlockSpec((1, 6, d), lambda i: (i // per_b, 0, 0)),
            _const_spec((B_HEADS, B_KV_LATENT, B_V_DIM)),
            _const_spec((B_HEADS * B_V_DIM, d)),
            _const_spec((1, d)),
            _const_spec((d, ne)),
        ],
        out_specs=(pl.BlockSpec((tm, d), row), pl.BlockSpec((tm, d // 2), row), pl.BlockSpec((tm, ne), row)),
        scratch_shapes=[pltpu.VMEM((tm, B_HEADS * B_V_DIM), BF16)],
        compiler_params=_cparams(("arbitrary",)),
        name="dsa_out",
    )(h, o_lat, mod_l, wuv_t, w_out.astype(BF16), n2g.reshape(1, d), router_w)


def _final_kernel(h1_ref, y0_ref, y1_ref, gt_ref, mod_ref, g_ref, o_ref):
    h = _moe_combine(h1_ref[...], y0_ref[...], y1_ref[...], gt_ref[...], mod_ref[0])
    o_ref[...] = h * lax.rsqrt(jnp.mean(h * h, axis=-1, keepdims=True) + EPS) * g_ref[...]


def _final(h1, y0, y1, gates, mod_l, final_g, seq):
    n, d = h1.shape
    tm = 512
    per_b = seq // tm
    row = lambda i: (i, 0)
    return pl.pallas_call(
        _final_kernel,
        out_shape=jax.ShapeDtypeStruct((n, d), F32),
        grid=(n // tm,),
        in_specs=[pl.BlockSpec((tm, d), row), pl.BlockSpec((tm, d), row), pl.BlockSpec((tm, d), row),
                  pl.BlockSpec((tm, TOP_K), row), pl.BlockSpec((1, 6, d), lambda i: (i // per_b, 0, 0)),
                  _const_spec((1, d))],
        out_specs=pl.BlockSpec((tm, d), row),
        compiler_params=_cparams(("arbitrary",)),
        name="final_norm",
    )(h1, y0, y1, gates, mod_l, final_g.reshape(1, d))


def kernel(x, c, ada_w, ada_b, norm1_g, norm2_g, a_w_in, a_ln_g, a_ln_b, a_w_sp, a_b_sp, a_w_out, b_w_in,
           b_kv_norm_g, b_w_uk, b_w_uv, b_kidx_g, b_w_out, rel_bias, router_w, router_b, moe_w_gate,
           moe_w_up, moe_w_down, final_g):
    bsz, seq, d = x.shape
    n = bsz * seq
    mod = _adaln(c, ada_w, ada_b).reshape(ada_w.shape[0], bsz, 6, d)
    h = x.reshape(n, d)

    h1, hn2, logits = _gmlp_layer(h, mod[0], norm1_g[0], a_w_in[0], a_ln_g[0], a_ln_b[0], a_w_sp[0],
                                  a_b_sp[0], a_w_out[0], norm2_g[0], router_w, seq)
    y0, y1, gates = _moe(hn2, logits, router_b, moe_w_gate, moe_w_up, moe_w_down, 0)

    h, qa, ckv, qi, ki, wi = _bproj(h1, y0, y1, gates, mod[0], mod[1], norm1_g[1], b_w_in[0],
                                    b_kv_norm_g[0], b_w_uk[0], b_kidx_g[0], seq)
    o_lat = _attention(qa, qi, wi, ckv, ki, _bias_tables(rel_bias), bsz, seq)
    h1, hn2, logits = _bout(h, o_lat, mod[1], b_w_uv[0], b_w_out[0], norm2_g[1], router_w, seq)
    y0, y1, gates = _moe(hn2, logits, router_b, moe_w_gate, moe_w_up, moe_w_down, 1)

    out = _final(h1, y0, y1, gates, mod[1], final_g, seq)
    return out.reshape(bsz, seq, d)
```

```python
import functools
import math

import jax
import jax.numpy as jnp
from jax import lax
from jax.experimental import pallas as pl
from jax.experimental.pallas import tpu as pltpu

F32 = jnp.float32
BF16 = jnp.bfloat16
I32 = jnp.int32
HIGHEST = lax.Precision.HIGHEST

EPS = 1e-6
A_CHUNK = 128
A_GROUPS = 8
B_HEADS = 16
B_HEAD_DIM = 64
B_V_DIM = 64
B_KV_LATENT = 256
B_IDX_HEADS = 8
B_IDX_DIM = 64
B_TOPK_MAX = 256
QBLK = 256
KTILE = 256
FAR_TILE = 512
SM_ROWS = 128
REL_BUCKETS = 32
REL_MAX_DIST = 128
N_EXPERTS = 16
N_GROUPS = 4
EXPERTS_PER_GROUP = 4
TOP_K = 2
MOE_ROWS = 1024
MOE_SUB = 256
MOE_FT = 512

ROW_TILE = 256
GMLP_ROWS = 512
VMEM_LIMIT = 60 * 1024 * 1024

INT_MIN = -2 ** 31
NEG_BIG = -1e30


def _cparams(sem):
    return pltpu.CompilerParams(dimension_semantics=sem, vmem_limit_bytes=VMEM_LIMIT)


def _mod_rmsnorm(h, g, scale, shift):
    ms = jnp.mean(h * h, axis=-1, keepdims=True)
    return (h * lax.rsqrt(ms + EPS) * g) * (1.0 + scale) + shift


def _gelu_tanh(x):
    c = math.sqrt(2.0 / math.pi)
    return 0.5 * x * (1.0 + jnp.tanh(c * (x + 0.044715 * (x * x * x))))


def _adaln_kernel(c_ref, w_ref, b_ref, o_ref):
    c = c_ref[...]
    sc = c * jax.nn.sigmoid(c)
    o_ref[0] = jnp.dot(sc, w_ref[0], precision=HIGHEST, preferred_element_type=F32) + b_ref[0]


def _adaln(c, ada_w, ada_b):
    depth, d, e = ada_w.shape
    bsz = c.shape[0]
    bp = 8
    c_pad = jnp.zeros((bp, d), F32).at[:bsz].set(c)
    tn = 1024
    out = pl.pallas_call(
        _adaln_kernel,
        out_shape=jax.ShapeDtypeStruct((depth, bp, e), F32),
        grid=(depth, e // tn),
        in_specs=[
            pl.BlockSpec((bp, d), lambda l, j: (0, 0)),
            pl.BlockSpec((1, d, tn), lambda l, j: (l, 0, j)),
            pl.BlockSpec((1, 1, tn), lambda l, j: (l, 0, j)),
        ],
        out_specs=pl.BlockSpec((1, bp, tn), lambda l, j: (l, 0, j)),
        compiler_params=_cparams(("arbitrary", "arbitrary")),
        name="adaln",
    )(c_pad, ada_w, ada_b.reshape(depth, 1, e))
    return out[:, :bsz]


def _residual_epilogue(h, y, mod, n2g, rw, h_out, hn_out, lg_out):
    h1 = h + mod[2:3] * y
    h_out[...] = h1
    hn2 = _mod_rmsnorm(h1, n2g, mod[4:5], mod[3:4])
    bits = pltpu.bitcast(hn2.astype(BF16).astype(F32), I32)
    half = bits.shape[1] // 2
    hn_out[...] = bits[:, :half] | lax.shift_right_logical(bits[:, half:], 16)
    lg_out[...] = jnp.dot(hn2, rw, precision=HIGHEST, preferred_element_type=F32)


def _gmlp_kernel(h_ref, mod_ref, n1g_ref, win_ref, lng_ref, lnb_ref, wsp_ref, bsp_ref, wout_ref,
                 n2g_ref, rw_ref, h_out, hn_out, lg_out, u_scr, v_scr, s_scr):
    tm = h_ref.shape[0]
    inner = u_scr.shape[1]
    gw = inner // A_GROUPS
    tn = 512
    h = h_ref[...]
    mod = mod_ref[0]
    hn = _mod_rmsnorm(h, n1g_ref[...], mod[1:2], mod[0:1]).astype(BF16)
    for j in range(2 * inner // tn):
        z = _gelu_tanh(jnp.dot(hn, win_ref[:, j * tn:(j + 1) * tn], preferred_element_type=F32))
        if j < inner // tn:
            u_scr[:, j * tn:(j + 1) * tn] = z
        else:
            jj = j - inner // tn
            v_scr[:, jj * tn:(jj + 1) * tn] = z
    vsum = jnp.zeros((tm, 1), F32)
    for j in range(inner // tn):
        vsum = vsum + jnp.sum(v_scr[:, j * tn:(j + 1) * tn], axis=-1, keepdims=True)
    mu = vsum * (1.0 / inner)
    vsq = jnp.zeros((tm, 1), F32)
    for j in range(inner // tn):
        d = v_scr[:, j * tn:(j + 1) * tn] - mu
        vsq = vsq + jnp.sum(d * d, axis=-1, keepdims=True)
    rstd = lax.rsqrt(vsq * (1.0 / inner) + EPS)
    row = lax.broadcasted_iota(I32, (A_CHUNK, A_CHUNK), 0)
    col = lax.broadcasted_iota(I32, (A_CHUNK, A_CHUNK), 1)
    tril = row >= col
    for g in range(A_GROUPS):
        ws = jnp.where(tril, wsp_ref[g], 0.0).astype(BF16)
        bcol = bsp_ref[:, g:g + 1]
        lg = lng_ref[:, g * gw:(g + 1) * gw]
        lb = lnb_ref[:, g * gw:(g + 1) * gw]
        for c in range(tm // A_CHUNK):
            r0, r1 = c * A_CHUNK, (c + 1) * A_CHUNK
            vt = v_scr[r0:r1, g * gw:(g + 1) * gw]
            vn = ((vt - mu[r0:r1]) * rstd[r0:r1]) * lg + lb
            fv = jnp.dot(ws, vn.astype(BF16), preferred_element_type=F32) + bcol
            s_scr[r0:r1, g * gw:(g + 1) * gw] = (u_scr[r0:r1, g * gw:(g + 1) * gw] * fv).astype(BF16)
    y = jnp.dot(s_scr[...], wout_ref[...], preferred_element_type=F32)
    _residual_epilogue(h, y, mod, n2g_ref[...], rw_ref[...], h_out, hn_out, lg_out)


def _const_spec(shape):
    nd = len(shape)
    return pl.BlockSpec(shape, lambda i, _nd=nd: (0,) * _nd, pipeline_mode=pl.Buffered(1))


def _gmlp_layer(h, mod_l, n1g, w_in, ln_g, ln_b, w_sp, b_sp, w_out, n2g, router_w, seq):
    n, d = h.shape
    inner = w_out.shape[0]
    tm = GMLP_ROWS
    per_b = seq // tm
    ne = router_w.shape[1]
    return pl.pallas_call(
        _gmlp_kernel,
        out_shape=(jax.ShapeDtypeStruct((n, d), F32), jax.ShapeDtypeStruct((n, d // 2), I32),
                   jax.ShapeDtypeStruct((n, ne), F32)),
        grid=(n // tm,),
        in_specs=[
            pl.BlockSpec((tm, d), lambda i: (i, 0)),
            pl.BlockSpec((1, 6, d), lambda i: (i // per_b, 0, 0)),
            _const_spec((1, d)),
            _const_spec((d, 2 * inner)),
            _const_spec((1, inner)),
            _const_spec((1, inner)),
            _const_spec((A_GROUPS, A_CHUNK, A_CHUNK)),
            _const_spec((A_CHUNK, A_GROUPS)),
            _const_spec((inner, d)),
            _const_spec((1, d)),
            _const_spec((d, ne)),
        ],
        out_specs=(pl.BlockSpec((tm, d), lambda i: (i, 0)), pl.BlockSpec((tm, d // 2), lambda i: (i, 0)),
                   pl.BlockSpec((tm, ne), lambda i: (i, 0))),
        scratch_shapes=[pltpu.VMEM((tm, inner), F32), pltpu.VMEM((tm, inner), F32),
                        pltpu.VMEM((tm, inner), BF16)],
        compiler_params=_cparams(("arbitrary",)),
        name="gmlp_layer",
    )(h, mod_l, n1g.reshape(1, d), w_in.astype(BF16), ln_g.reshape(1, inner), ln_b.reshape(1, inner),
      w_sp, b_sp.T, w_out.astype(BF16), n2g.reshape(1, d), router_w)


def _first_max4(rows):
    m = jnp.maximum(jnp.maximum(rows[0], rows[1]), jnp.maximum(rows[2], rows[3]))
    idx = jnp.where(rows[0] == m, 0, jnp.where(rows[1] == m, 1, jnp.where(rows[2] == m, 2, 3)))
    return m, idx.astype(I32)


def _route_kernel(lt_ref, rb_ref, e_ref, g_ref):
    aff = jax.nn.sigmoid(lt_ref[...])
    sel = aff + rb_ref[...]
    neg = jnp.float32(-jnp.inf)
    g_score, g_i1, g_i2 = [], [], []
    for g in range(N_GROUPS):
        rows = [sel[4 * g + k:4 * g + k + 1, :] for k in range(EXPERTS_PER_GROUP)]
        m1, i1 = _first_max4(rows)
        rest = [jnp.where(i1 == k, neg, rows[k]) for k in range(EXPERTS_PER_GROUP)]
        m2, i2 = _first_max4(rest)
        g_score.append(m1 + m2)
        g_i1.append(i1)
        g_i2.append(i2)
    _, grp = _first_max4(g_score)
    l1 = jnp.zeros_like(grp)
    l2 = jnp.zeros_like(grp)
    for g in range(N_GROUPS):
        l1 = jnp.where(grp == g, g_i1[g], l1)
        l2 = jnp.where(grp == g, g_i2[g], l2)
    e1 = grp * EXPERTS_PER_GROUP + l1
    e2 = grp * EXPERTS_PER_GROUP + l2
    a1 = jnp.zeros_like(g_score[0])
    a2 = jnp.zeros_like(g_score[0])
    for e in range(N_EXPERTS):
        ae = aff[e:e + 1, :]
        a1 = jnp.where(e1 == e, ae, a1)
        a2 = jnp.where(e2 == e, ae, a2)
    tot = a1 + a2
    e_ref[0:1, :] = e1
    e_ref[1:2, :] = e2
    g_ref[0:1, :] = a1 / tot
    g_ref[1:2, :] = a2 / tot


def _route(logits, router_b):
    n, ne = logits.shape
    tn = min(2048, n)
    return pl.pallas_call(
        _route_kernel,
        out_shape=(jax.ShapeDtypeStruct((TOP_K, n), I32), jax.ShapeDtypeStruct((TOP_K, n), F32)),
        grid=(n // tn,),
        in_specs=[pl.BlockSpec((ne, tn), lambda i: (0, i)), pl.BlockSpec((ne, 1), lambda i: (0, 0))],
        out_specs=(pl.BlockSpec((TOP_K, tn), lambda i: (0, i)), pl.BlockSpec((TOP_K, tn), lambda i: (0, i))),
        compiler_params=_cparams(("arbitrary",)),
        name="route",
    )(logits.T, router_b.reshape(ne, 1))


def _slot_kernel(e_ref, slot_ref, cnt_ref):
    n = e_ref.shape[1]
    nblk = n // 128
    eid = lax.broadcasted_iota(I32, (N_EXPERTS, 128), 0)
    r = lax.broadcasted_iota(I32, (128, 128), 0)
    c = lax.broadcasted_iota(I32, (128, 128), 1)
    upper = jnp.where(r <= c, 1.0, 0.0).astype(BF16)

    def hits(b):
        o = pl.multiple_of(b * 128, 128)
        t1 = e_ref[0:1, pl.ds(o, 128)] == eid
        t2 = e_ref[1:2, pl.ds(o, 128)] == eid
        return o, t1, t2, jnp.where(t1, 1.0, 0.0) + jnp.where(t2, 1.0, 0.0)

    def count_body(b, acc):
        return acc + hits(b)[3]

    cnt = jnp.sum(lax.fori_loop(0, nblk, count_body, jnp.zeros((N_EXPERTS, 128), F32)),
                  axis=-1, keepdims=True)
    cnt_ref[...] = cnt.astype(I32)
    padded = jnp.floor((cnt + (MOE_ROWS - 1)) * (1.0 / MOE_ROWS)) * MOE_ROWS
    er = lax.broadcasted_iota(I32, (N_EXPERTS, N_EXPERTS), 0)
    ec = lax.broadcasted_iota(I32, (N_EXPERTS, N_EXPERTS), 1)
    padded_row = jnp.sum(jnp.where(er == ec, padded, 0.0), axis=0, keepdims=True)
    start = jnp.sum(jnp.where(ec < er, padded_row, 0.0), axis=-1, keepdims=True)

    def slot_body(b, carry):
        o, t1, t2, t = hits(b)
        incl = jnp.dot(t.astype(BF16), upper, preferred_element_type=F32)
        pos = carry + incl - t
        slot_ref[0:1, pl.ds(o, 128)] = jnp.sum(jnp.where(t1, pos, 0.0), axis=0, keepdims=True).astype(I32)
        slot_ref[1:2, pl.ds(o, 128)] = jnp.sum(jnp.where(t2, pos, 0.0), axis=0, keepdims=True).astype(I32)
        return carry + incl[:, 127:128]

    lax.fori_loop(0, nblk, slot_body, start)


def _slot_tables(experts):
    n = experts.shape[1]
    a = n * TOP_K
    slot, counts = pl.pallas_call(
        _slot_kernel,
        out_shape=(jax.ShapeDtypeStruct((TOP_K, n), I32), jax.ShapeDtypeStruct((N_EXPERTS, 1), I32)),
        compiler_params=pltpu.CompilerParams(vmem_limit_bytes=VMEM_LIMIT),
        name="slots",
    )(experts)
    counts = counts.reshape(N_EXPERTS)
    padded = ((counts + MOE_ROWS - 1) // MOE_ROWS) * MOE_ROWS
    pad_end = jnp.cumsum(padded)
    pad_start = pad_end - padded
    p = a + N_EXPERTS * MOE_ROWS
    n_items = p // MOE_ROWS
    tok = jnp.broadcast_to(jnp.arange(n, dtype=I32)[None, :], (TOP_K, n))
    tok_of_slot = jnp.zeros((p,), I32).at[slot.reshape(a)].set(tok.reshape(a))
    slot_of_assign = slot.T
    item_start = jnp.arange(n_items, dtype=I32) * MOE_ROWS
    n_used = pad_end[-1] // MOE_ROWS
    item_e_raw = jnp.minimum(jnp.searchsorted(pad_end, item_start, side='right'), N_EXPERTS - 1).astype(I32)
    used = item_start < pad_end[-1]
    last_e = item_e_raw[jnp.maximum(n_used - 1, 0)]
    item_e = jnp.where(used, item_e_raw, last_e).astype(I32)
    valid = jnp.clip(counts[item_e_raw] - (item_start - pad_start[item_e_raw]), 0, MOE_ROWS)
    item_rows = jnp.where(used, valid, 0).astype(I32)
    item_blk = jnp.where(used, jnp.arange(n_items, dtype=I32), jnp.maximum(n_used - 1, 0)).astype(I32)
    return tok_of_slot, slot_of_assign, item_e, item_rows, item_blk


def _moe_kernel(ie_ref, ir_ref, ib_ref, x_ref, wg_ref, wu_ref, wd_ref, o_ref, wg_s, wu_s, wd_s, x_s):
    it = pl.program_id(0)
    f = pl.program_id(1)
    nrows = ir_ref[it]
    half = x_ref.shape[1]

    @pl.when(f == 0)
    def _():
        o_ref[...] = jnp.zeros_like(o_ref)
        pk = x_ref[...]
        x_s[:, :half] = pltpu.bitcast(pk & jnp.int32(-65536), F32).astype(BF16)
        x_s[:, half:] = pltpu.bitcast(lax.shift_left(pk, 16), F32).astype(BF16)

    @pl.when(nrows > 0)
    def _():
        wg_s[...] = wg_ref[0, 0].astype(BF16)
        wu_s[...] = wu_ref[0, 0].astype(BF16)
        wd_s[...] = wd_ref[0, 0].astype(BF16)
        nsb = (nrows + MOE_SUB - 1) // MOE_SUB

        def sub_block(sb):
            r0 = pl.multiple_of(sb * MOE_SUB, MOE_SUB)
            x = x_s[pl.ds(r0, MOE_SUB), :]
            g = jnp.dot(x, wg_s[...], preferred_element_type=F32)
            u = jnp.dot(x, wu_s[...], preferred_element_type=F32)
            hmid = ((g * jax.nn.sigmoid(g)) * u).astype(BF16)
            o_ref[pl.ds(r0, MOE_SUB), :] += jnp.dot(hmid, wd_s[...], preferred_element_type=F32)

        def pair_body(pi, carry):
            sub_block(2 * pi)
            sub_block(2 * pi + 1)
            return carry

        lax.fori_loop(0, nsb // 2, pair_body, 0)

        @pl.when(nsb % 2 == 1)
        def _():
            sub_block(nsb - 1)


def _moe_experts(x_sorted, item_e, item_rows, item_blk, w_gate, w_up, w_down, layer):
    p, half = x_sorted.shape
    d = 2 * half
    n_items = p // MOE_ROWS
    dff = w_gate.shape[3]
    nf = dff // MOE_FT
    grid_spec = pltpu.PrefetchScalarGridSpec(
        num_scalar_prefetch=3,
        grid=(n_items, nf),
        in_specs=[
            pl.BlockSpec((MOE_ROWS, half), lambda i, f, ie, ir, ib: (ib[i], 0)),
            pl.BlockSpec((1, 1, d, MOE_FT),
                         lambda i, f, ie, ir, ib: (layer, ie[i], 0, jnp.where(ir[i] > 0, f, nf - 1))),
            pl.BlockSpec((1, 1, d, MOE_FT),
                         lambda i, f, ie, ir, ib: (layer, ie[i], 0, jnp.where(ir[i] > 0, f, nf - 1))),
            pl.BlockSpec((1, 1, MOE_FT, d),
                         lambda i, f, ie, ir, ib: (layer, ie[i], jnp.where(ir[i] > 0, f, nf - 1), 0)),
        ],
        out_specs=pl.BlockSpec((MOE_ROWS, d), lambda i, f, ie, ir, ib: (i, 0)),
        scratch_shapes=[pltpu.VMEM((d, MOE_FT), BF16), pltpu.VMEM((d, MOE_FT), BF16),
                        pltpu.VMEM((MOE_FT, d), BF16), pltpu.VMEM((MOE_ROWS, d), BF16)],
    )
    return pl.pallas_call(
        _moe_kernel,
        out_shape=jax.ShapeDtypeStruct((p, d), F32),
        grid_spec=grid_spec,
        compiler_params=_cparams(("arbitrary", "arbitrary")),
        name="moe_experts",
    )(item_e, item_rows, item_blk, x_sorted, w_gate, w_up, w_down)


def _moe(hn2, logits, router_b, w_gate, w_up, w_down, layer):
    experts, gates = _route(logits, router_b)
    tok_of_slot, slot_of_assign, item_e, item_rows, item_blk = _slot_tables(experts)
    x_sorted = hn2.at[tok_of_slot].get(mode='promise_in_bounds')
    y_slot = _moe_experts(x_sorted, item_e, item_rows, item_blk, w_gate, w_up, w_down, layer)
    y0 = y_slot.at[slot_of_assign[:, 0]].get(mode='promise_in_bounds')
    y1 = y_slot.at[slot_of_assign[:, 1]].get(mode='promise_in_bounds')
    return y0, y1, gates.T


def _moe_combine(h1, y0, y1, gates, mod):
    return h1 + mod[5:6] * (gates[:, 0:1] * y0 + gates[:, 1:2] * y1)


def _bproj_kernel(h1_ref, y0_ref, y1_ref, gt_ref, modp_ref, mod_ref, n1g_ref, w_ref, kvg_ref, kig_ref,
                  wuk_ref, h_out, qa_out, ckv_out, qi_out, ki_out, wi_out):
    tm = h1_ref.shape[0]
    h = _moe_combine(h1_ref[...], y0_ref[...], y1_ref[...], gt_ref[...], modp_ref[0])
    h_out[...] = h
    mod = mod_ref[0]
    hn = _mod_rmsnorm(h, n1g_ref[...], mod[1:2], mod[0:1]).astype(BF16)
    proj = jnp.dot(hn, w_ref[...], preferred_element_type=F32)
    o1 = B_HEADS * B_HEAD_DIM
    o2 = o1 + B_KV_LATENT
    o3 = o2 + B_IDX_HEADS * B_IDX_DIM
    ckv = proj[:, o1:o2]
    ckv = ckv * lax.rsqrt(jnp.mean(ckv * ckv, axis=-1, keepdims=True) + EPS) * kvg_ref[...]
    ckv_out[...] = ckv.astype(BF16)
    tail = proj[:, o3:o3 + 128]
    ki = tail[:, :B_IDX_DIM]
    ki = ki * lax.rsqrt(jnp.mean(ki * ki, axis=-1, keepdims=True) + EPS) * kig_ref[...]
    ki_out[...] = ki.astype(BF16)
    wi = tail[:, B_IDX_DIM:B_IDX_DIM + B_IDX_HEADS] * (B_IDX_HEADS ** -0.5 * B_IDX_DIM ** -0.5)
    scale = B_HEAD_DIM ** -0.5
    for blk in range(tm // QBLK):
        r0, r1 = blk * QBLK, (blk + 1) * QBLK
        for hh in range(B_HEADS):
            qh = proj[r0:r1, hh * B_HEAD_DIM:(hh + 1) * B_HEAD_DIM].astype(BF16)
            qa = jnp.dot(qh, wuk_ref[hh], preferred_element_type=F32) * scale
            qa_out[blk, hh * QBLK:(hh + 1) * QBLK, :] = qa.astype(BF16)
        for hh in range(B_IDX_HEADS):
            qi_out[blk, hh * QBLK:(hh + 1) * QBLK, :] = proj[r0:r1, o2 + hh * B_IDX_DIM:
                                                            o2 + (hh + 1) * B_IDX_DIM].astype(BF16)
            wi_out[blk, hh * QBLK:(hh + 1) * QBLK, :] = wi[r0:r1, hh:hh + 1]


def _bproj(h1, y0, y1, gates, mod_prev, mod_l, n1g, b_w_in, kv_g, w_uk, kidx_g, seq):
    n, d = h1.shape
    tm = ROW_TILE
    per_b = seq // tm
    nq = n // QBLK
    qpb = tm // QBLK
    o1 = B_HEADS * B_HEAD_DIM
    o2 = o1 + B_KV_LATENT
    o3 = o2 + B_IDX_HEADS * B_IDX_DIM
    wcat = jnp.zeros((d, o3 + 128), F32).at[:, :b_w_in.shape[1]].set(b_w_in).astype(BF16)
    wuk_t = jnp.transpose(w_uk, (1, 2, 0)).astype(BF16)
    row = lambda i: (i, 0)
    modm = lambda i: (i // per_b, 0, 0)
    blk3 = lambda i: (i, 0, 0)
    return pl.pallas_call(
        _bproj_kernel,
        out_shape=(
            jax.ShapeDtypeStruct((n, d), F32),
            jax.ShapeDtypeStruct((nq, B_HEADS * QBLK, B_KV_LATENT), BF16),
            jax.ShapeDtypeStruct((n, B_KV_LATENT), BF16),
            jax.ShapeDtypeStruct((nq, B_IDX_HEADS * QBLK, B_IDX_DIM), BF16),
            jax.ShapeDtypeStruct((n, B_IDX_DIM), BF16),
            jax.ShapeDtypeStruct((nq, B_IDX_HEADS * QBLK, 1), F32),
        ),
        grid=(n // tm,),
        in_specs=[
            pl.BlockSpec((tm, d), row), pl.BlockSpec((tm, d), row), pl.BlockSpec((tm, d), row),
            pl.BlockSpec((tm, TOP_K), row),
            pl.BlockSpec((1, 6, d), modm), pl.BlockSpec((1, 6, d), modm),
            _const_spec((1, d)),
            _const_spec((d, o3 + 128)),
            _const_spec((1, B_KV_LATENT)),
            _const_spec((1, B_IDX_DIM)),
            _const_spec((B_HEADS, B_HEAD_DIM, B_KV_LATENT)),
        ],
        out_specs=(
            pl.BlockSpec((tm, d), row),
            pl.BlockSpec((qpb, B_HEADS * QBLK, B_KV_LATENT), blk3),
            pl.BlockSpec((tm, B_KV_LATENT), row),
            pl.BlockSpec((qpb, B_IDX_HEADS * QBLK, B_IDX_DIM), blk3),
            pl.BlockSpec((tm, B_IDX_DIM), row),
            pl.BlockSpec((qpb, B_IDX_HEADS * QBLK, 1), blk3),
        ),
        compiler_params=_cparams(("arbitrary",)),
        name="dsa_proj",
    )(h1, y0, y1, gates, mod_prev, mod_l, n1g.reshape(1, d), wcat, kv_g.reshape(1, -1),
      kidx_g.reshape(1, -1), wuk_t)


def _t5_bucket(dist):
    n = jnp.maximum(dist, 0)
    exact = REL_BUCKETS // 2
    nf = jnp.maximum(n, 1).astype(F32)
    large = exact + (jnp.log(nf / exact) / math.log(REL_MAX_DIST / exact)
                     * (REL_BUCKETS - exact)).astype(I32)
    large = jnp.minimum(large, REL_BUCKETS - 1)
    return jnp.where(n < exact, n, large)


def _bias_tables(rel_bias):
    assert REL_MAX_DIST <= 128
    t = jnp.arange(128, dtype=I32)[:, None]
    s = jnp.arange(128, dtype=I32)[None, :]
    far = rel_bias[REL_BUCKETS - 1]
    diag = rel_bias[_t5_bucket(t - s)] - far
    prev = rel_bias[_t5_bucket(t - s + 128)] - far
    return jnp.stack([jnp.transpose(diag, (2, 0, 1)), jnp.transpose(prev, (2, 0, 1))])


def _attn_kernel(qa_ref, qi_ref, wi_ref, ckv_ref, ki_ref, bt_ref, o_ref, kbuf, m_scr, l_scr, a_scr, acc_scr,
                 s_scr, p_scr, madd_scr):
    i = pl.program_id(1)
    nt = i + 1
    t_row = i * QBLK + lax.broadcasted_iota(I32, (QBLK, KTILE), 0)
    lane = lax.broadcasted_iota(I32, (QBLK, KTILE), 1)

    qi = qi_ref[0]
    wcol = wi_ref[0]

    def score_body(j, carry):
        k0 = pl.multiple_of(j * KTILE, KTILE)
        kt = ki_ref[0, pl.ds(k0, KTILE), :]
        p = lax.dot_general(qi, kt, (((1,), (1,)), ((), ())), preferred_element_type=F32)
        p = jnp.maximum(p, 0.0) * wcol
        sc = p[0:QBLK]
        for hh in range(1, B_IDX_HEADS):
            sc = sc + p[hh * QBLK:(hh + 1) * QBLK]
        bits = pltpu.bitcast(sc + 0.0, I32)
        key = jnp.where(bits < 0, bits ^ jnp.int32(0x7FFFFFFF), bits)
        key = jnp.where(k0 + lane <= t_row, key, jnp.int32(INT_MIN))
        kbuf[:, pl.ds(k0, KTILE)] = key
        return carry

    lax.fori_loop(0, nt, score_body, 0)

    def fold_lanes(x):
        out = x[:, 0:128]
        for k in range(1, KTILE // 128):
            out = out + x[:, k * 128:(k + 1) * 128]
        return out

    def count_ge(cand):
        def body(j, acc):
            k0 = pl.multiple_of(j * KTILE, KTILE)
            return acc + fold_lanes(jnp.where(kbuf[:, pl.ds(k0, KTILE)] >= cand, 1, 0))
        acc = lax.fori_loop(0, nt, body, jnp.zeros((QBLK, 128), I32))
        return jnp.sum(acc, axis=-1, keepdims=True)

    def bit_step(b, lo, n_ge):
        cand = lo + lax.shift_left(jnp.int32(1), 31 - b)
        cnt = count_ge(cand)
        take = cnt >= B_TOPK_MAX
        return jnp.where(take, cand, lo), jnp.where(take, cnt, n_ge)

    def bit_cond(c):
        return (c[0] < 32) & (c[3] == 0)

    def bit_body(c):
        b, lo, n_ge, _ = c
        lo, n_ge = bit_step(b, lo, n_ge)
        lo, n_ge = bit_step(b + 1, lo, n_ge)
        done = (jnp.max(n_ge) == B_TOPK_MAX).astype(I32)
        return b + 2, lo, n_ge, done

    _, tau, n_ge, _ = lax.while_loop(
        bit_cond, bit_body,
        (jnp.int32(0), jnp.full((QBLK, 1), INT_MIN, I32), jnp.full((QBLK, 1), nt * KTILE, I32),
         (i < 1).astype(I32)))
    tau = jnp.maximum(tau, jnp.int32(INT_MIN + 1))
    excess = (i >= 1) & (jnp.max(n_ge) > B_TOPK_MAX)

    @pl.when(excess)
    def _():
        n_gt = count_ge(tau + 1)
        need = B_TOPK_MAX - n_gt

        def count_eq_before(pos):
            def body(j, acc):
                k0 = pl.multiple_of(j * KTILE, KTILE)
                hit = (kbuf[:, pl.ds(k0, KTILE)] == tau) & (k0 + lane < pos)
                return acc + fold_lanes(jnp.where(hit, 1, 0))
            acc = lax.fori_loop(0, nt, body, jnp.zeros((QBLK, 128), I32))
            return jnp.sum(acc, axis=-1, keepdims=True)

        def pos_body(b, pos):
            cand = pos + lax.shift_left(jnp.int32(1), 12 - b)
            return jnp.where(count_eq_before(cand) < need, cand, pos)

        pos = lax.fori_loop(0, 13, pos_body, jnp.zeros((QBLK, 1), I32))

        def drop_body(j, carry):
            k0 = pl.multiple_of(j * KTILE, KTILE)
            kk = kbuf[:, pl.ds(k0, KTILE)]
            drop = (kk == tau) & (k0 + lane > pos) & (n_ge > B_TOPK_MAX)
            kbuf[:, pl.ds(k0, KTILE)] = jnp.where(drop, jnp.int32(INT_MIN), kk)
            return carry

        lax.fori_loop(0, nt, drop_body, 0)

    m_scr[...] = jnp.full_like(m_scr, NEG_BIG)
    l_scr[...] = jnp.zeros_like(l_scr)
    acc_scr[...] = jnp.zeros_like(acc_scr)

    def attend(k0, width, tile_off):
        kv = ckv_ref[0, pl.ds(k0, width), :]
        s_scr[:, :width] = lax.dot_general(qa_ref[0], kv, (((1,), (1,)), ((), ())),
                                           preferred_element_type=F32)
        madd_scr[:, :width] = jnp.where(kbuf[:, pl.ds(k0, width)] >= tau, 0.0, NEG_BIG)
        for r0 in range(0, B_HEADS * QBLK, SM_ROWS):
            r1 = r0 + SM_ROWS
            hh, q0 = r0 // QBLK, r0 % QBLK
            parts = []
            for kc in range(width // 128):
                sp = s_scr[r0:r1, kc * 128:(kc + 1) * 128] + madd_scr[q0:q0 + SM_ROWS, kc * 128:(kc + 1) * 128]
                if tile_off is not None:
                    rel = (QBLK // 128) * tile_off + q0 // 128 - kc
                    if rel in (0, 1):
                        sp = sp + bt_ref[rel, hh, q0 % 128:q0 % 128 + SM_ROWS, :]
                parts.append(sp)
            smax = parts[0]
            for sp in parts[1:]:
                smax = jnp.maximum(smax, sp)
            m_old = m_scr[r0:r1, :]
            m_new = jnp.maximum(m_old, jnp.max(smax, axis=-1, keepdims=True))
            alpha = jnp.exp(m_old - m_new)
            psum = None
            for kc, sp in enumerate(parts):
                p = jnp.exp(sp - m_new)
                p_scr[r0:r1, kc * 128:(kc + 1) * 128] = p.astype(BF16)
                psum = p if psum is None else psum + p
            l_scr[r0:r1, :] = alpha * l_scr[r0:r1, :] + jnp.sum(psum, axis=-1, keepdims=True)
            m_scr[r0:r1, :] = m_new
            a_scr[r0:r1, :] = alpha
        pv = jnp.dot(p_scr[:, :width], kv, preferred_element_type=F32)
        acc_scr[...] = pltpu.repeat(a_scr[...], B_KV_LATENT // 128, axis=1) * acc_scr[...] + pv

    n_far = jnp.maximum(i - 1, 0)
    per_far = FAR_TILE // KTILE

    def far_body(jf, carry):
        attend(pl.multiple_of(jf * FAR_TILE, FAR_TILE), FAR_TILE, None)
        return carry

    lax.fori_loop(0, n_far // per_far, far_body, 0)

    @pl.when(n_far % per_far == 1)
    def _():
        attend(pl.multiple_of((n_far - 1) * KTILE, KTILE), KTILE, None)

    @pl.when(i >= 1)
    def _():
        attend(pl.multiple_of((i - 1) * KTILE, KTILE), KTILE, 1)

    attend(pl.multiple_of(i * KTILE, KTILE), KTILE, 0)
    inv_l = 1.0 / l_scr[...]
    o_ref[0] = (acc_scr[...] * pltpu.repeat(inv_l, B_KV_LATENT // 128, axis=1)).astype(BF16)


def _attention(qa, qi, wi, ckv, ki, btab, bsz, seq):
    nqb = seq // QBLK
    gq = lambda b, i: (b * nqb + i, 0, 0)
    gb = lambda b, i: (b, 0, 0)
    return pl.pallas_call(
        _attn_kernel,
        out_shape=jax.ShapeDtypeStruct(qa.shape, BF16),
        grid=(bsz, nqb),
        in_specs=[
            pl.BlockSpec((1, B_HEADS * QBLK, B_KV_LATENT), gq),
            pl.BlockSpec((1, B_IDX_HEADS * QBLK, B_IDX_DIM), gq),
            pl.BlockSpec((1, B_IDX_HEADS * QBLK, 1), gq),
            pl.BlockSpec((1, seq, B_KV_LATENT), gb, pipeline_mode=pl.Buffered(1)),
            pl.BlockSpec((1, seq, B_IDX_DIM), gb, pipeline_mode=pl.Buffered(1)),
            pl.BlockSpec((2, B_HEADS, 128, 128), lambda b, i: (0, 0, 0, 0), pipeline_mode=pl.Buffered(1)),
        ],
        out_specs=pl.BlockSpec((1, B_HEADS * QBLK, B_KV_LATENT), gq),
        scratch_shapes=[
            pltpu.VMEM((QBLK, seq), I32),
            pltpu.VMEM((B_HEADS * QBLK, 128), F32),
            pltpu.VMEM((B_HEADS * QBLK, 128), F32),
            pltpu.VMEM((B_HEADS * QBLK, 128), F32),
            pltpu.VMEM((B_HEADS * QBLK, B_KV_LATENT), F32),
            pltpu.VMEM((B_HEADS * QBLK, FAR_TILE), F32),
            pltpu.VMEM((B_HEADS * QBLK, FAR_TILE), BF16),
            pltpu.VMEM((QBLK, FAR_TILE), F32),
        ],
        compiler_params=_cparams(("arbitrary", "arbitrary")),
        name="dsa_attention",
    )(qa, qi, wi, ckv.reshape(bsz, seq, -1), ki.reshape(bsz, seq, -1), btab)


def _bout_kernel(h_ref, ol_ref, mod_ref, wuv_ref, wout_ref, n2g_ref, rw_ref, h_out, hn_out, lg_out, o_scr):
    mod = mod_ref[0]
    for blk in range(ol_ref.shape[0]):
        for hh in range(B_HEADS):
            oh = jnp.dot(ol_ref[blk, hh * QBLK:(hh + 1) * QBLK, :], wuv_ref[hh], preferred_element_type=F32)
            o_scr[blk * QBLK:(blk + 1) * QBLK, hh * B_V_DIM:(hh + 1) * B_V_DIM] = oh.astype(BF16)
    y = jnp.dot(o_scr[...], wout_ref[...], preferred_element_type=F32)
    _residual_epilogue(h_ref[...], y, mod, n2g_ref[...], rw_ref[...], h_out, hn_out, lg_out)


def _bout(h, o_lat, mod_l, w_uv, w_out, n2g, router_w, seq):
    n, d = h.shape
    tm = ROW_TILE
    per_b = seq // tm
    qpb = tm // QBLK
    ne = router_w.shape[1]
    wuv_t = jnp.transpose(w_uv, (1, 0, 2)).astype(BF16)
    row = lambda i: (i, 0)
    return pl.pallas_call(
        _bout_kernel,
        out_shape=(jax.ShapeDtypeStruct((n, d), F32), jax.ShapeDtypeStruct((n, d // 2), I32),
                   jax.ShapeDtypeStruct((n, ne), F32)),
        grid=(n // tm,),
        in_specs=[
            pl.BlockSpec((tm, d), row),
            pl.BlockSpec((qpb, B_HEADS * QBLK, B_KV_LATENT), lambda i: (i, 0, 0)),
            pl.BlockSpec((1, 6, d), lambda i: (i // per_b, 0, 0)),
            _const_spec((B_HEADS, B_KV_LATENT, B_V_DIM)),
            _const_spec((B_HEADS * B_V_DIM, d)),
            _const_spec((1, d)),
            _const_spec((d, ne)),
        ],
        out_specs=(pl.BlockSpec((tm, d), row), pl.BlockSpec((tm, d // 2), row), pl.BlockSpec((tm, ne), row)),
        scratch_shapes=[pltpu.VMEM((tm, B_HEADS * B_V_DIM), BF16)],
        compiler_params=_cparams(("arbitrary",)),
        name="dsa_out",
    )(h, o_lat, mod_l, wuv_t, w_out.astype(BF16), n2g.reshape(1, d), router_w)


def _final_kernel(h1_ref, y0_ref, y1_ref, gt_ref, mod_ref, g_ref, o_ref):
    h = _moe_combine(h1_ref[...], y0_ref[...], y1_ref[...], gt_ref[...], mod_ref[0])
    o_ref[...] = h * lax.rsqrt(jnp.mean(h * h, axis=-1, keepdims=True) + EPS) * g_ref[...]


def _final(h1, y0, y1, gates, mod_l, final_g, seq):
    n, d = h1.shape
    tm = 512
    per_b = seq // tm
    row = lambda i: (i, 0)
    return pl.pallas_call(
        _final_kernel,
        out_shape=jax.ShapeDtypeStruct((n, d), F32),
        grid=(n // tm,),
        in_specs=[pl.BlockSpec((tm, d), row), pl.BlockSpec((tm, d), row), pl.BlockSpec((tm, d), row),
                  pl.BlockSpec((tm, TOP_K), row), pl.BlockSpec((1, 6, d), lambda i: (i // per_b, 0, 0)),
                  _const_spec((1, d))],
        out_specs=pl.BlockSpec((tm, d), row),
        compiler_params=_cparams(("arbitrary",)),
        name="final_norm",
    )(h1, y0, y1, gates, mod_l, final_g.reshape(1, d))


def kernel(x, c, ada_w, ada_b, norm1_g, norm2_g, a_w_in, a_ln_g, a_ln_b, a_w_sp, a_b_sp, a_w_out, b_w_in,
           b_kv_norm_g, b_w_uk, b_w_uv, b_kidx_g, b_w_out, rel_bias, router_w, router_b, moe_w_gate,
           moe_w_up, moe_w_down, final_g):
    bsz, seq, d = x.shape
    n = bsz * seq
    mod = _adaln(c, ada_w, ada_b).reshape(ada_w.shape[0], bsz, 6, d)
    h = x.reshape(n, d)

    h1, hn2, logits = _gmlp_layer(h, mod[0], norm1_g[0], a_w_in[0], a_ln_g[0], a_ln_b[0], a_w_sp[0],
                                  a_b_sp[0], a_w_out[0], norm2_g[0], router_w, seq)
    y0, y1, gates = _moe(hn2, logits, router_b, moe_w_gate, moe_w_up, moe_w_down, 0)

    h, qa, ckv, qi, ki, wi = _bproj(h1, y0, y1, gates, mod[0], mod[1], norm1_g[1], b_w_in[0],
                                    b_kv_norm_g[0], b_w_uk[0], b_kidx_g[0], seq)
    o_lat = _attention(qa, qi, wi, ckv, ki, _bias_tables(rel_bias), bsz, seq)
    h1, hn2, logits = _bout(h, o_lat, mod[1], b_w_uv[0], b_w_out[0], norm2_g[1], router_w, seq)
    y0, y1, gates = _moe(hn2, logits, router_b, moe_w_gate, moe_w_up, moe_w_down, 1)

    out = _final(h1, y0, y1, gates, mod[1], final_g, seq)
    return out.reshape(bsz, seq, d)
```

```python
import functools
import math

import jax
import jax.numpy as jnp
from jax import lax
from jax.experimental import pallas as pl
from jax.experimental.pallas import tpu as pltpu

F32 = jnp.float32
BF16 = jnp.bfloat16
I32 = jnp.int32
HIGHEST = lax.Precision.HIGHEST

EPS = 1e-6
A_CHUNK = 128
A_GROUPS = 8
B_HEADS = 16
B_HEAD_DIM = 64
B_V_DIM = 64
B_KV_LATENT = 256
B_IDX_HEADS = 8
B_IDX_DIM = 64
B_TOPK_MAX = 256
QBLK = 256
KTILE = 256
FAR_TILE = 512
SM_ROWS = 128
REL_BUCKETS = 32
REL_MAX_DIST = 128
N_EXPERTS = 16
N_GROUPS = 4
EXPERTS_PER_GROUP = 4
TOP_K = 2
MOE_ROWS = 1024
MOE_SUB = 256
MOE_FT = 512

ROW_TILE = 256
GMLP_ROWS = 512
VMEM_LIMIT = 60 * 1024 * 1024

INT_MIN = -2 ** 31
NEG_BIG = -1e30


def _cparams(sem):
    return pltpu.CompilerParams(dimension_semantics=sem, vmem_limit_bytes=VMEM_LIMIT)


def _mod_rmsnorm(h, g, scale, shift):
    ms = jnp.mean(h * h, axis=-1, keepdims=True)
    return (h * lax.rsqrt(ms + EPS) * g) * (1.0 + scale) + shift


def _gelu_tanh(x):
    c = math.sqrt(2.0 / math.pi)
    return 0.5 * x * (1.0 + jnp.tanh(c * (x + 0.044715 * (x * x * x))))


def _adaln_kernel(c_ref, w_ref, b_ref, o_ref):
    c = c_ref[...]
    sc = c * jax.nn.sigmoid(c)
    o_ref[0] = jnp.dot(sc, w_ref[0], precision=HIGHEST, preferred_element_type=F32) + b_ref[0]


def _adaln(c, ada_w, ada_b):
    depth, d, e = ada_w.shape
    bsz = c.shape[0]
    bp = 8
    c_pad = jnp.zeros((bp, d), F32).at[:bsz].set(c)
    tn = 1024
    out = pl.pallas_call(
        _adaln_kernel,
        out_shape=jax.ShapeDtypeStruct((depth, bp, e), F32),
        grid=(depth, e // tn),
        in_specs=[
            pl.BlockSpec((bp, d), lambda l, j: (0, 0)),
            pl.BlockSpec((1, d, tn), lambda l, j: (l, 0, j)),
            pl.BlockSpec((1, 1, tn), lambda l, j: (l, 0, j)),
        ],
        out_specs=pl.BlockSpec((1, bp, tn), lambda l, j: (l, 0, j)),
        compiler_params=_cparams(("arbitrary", "arbitrary")),
        name="adaln",
    )(c_pad, ada_w, ada_b.reshape(depth, 1, e))
    return out[:, :bsz]


def _residual_epilogue(h, y, mod, n2g, rw, h_out, hn_out, lg_out):
    h1 = h + mod[2:3] * y
    h_out[...] = h1
    hn2 = _mod_rmsnorm(h1, n2g, mod[4:5], mod[3:4])
    bits = pltpu.bitcast(hn2.astype(BF16).astype(F32), I32)
    half = bits.shape[1] // 2
    hn_out[...] = bits[:, :half] | lax.shift_right_logical(bits[:, half:], 16)
    lg_out[...] = jnp.dot(hn2, rw, precision=HIGHEST, preferred_element_type=F32)


def _gmlp_kernel(h_ref, mod_ref, n1g_ref, win_ref, lng_ref, lnb_ref, wsp_ref, bsp_ref, wout_ref,
                 n2g_ref, rw_ref, h_out, hn_out, lg_out, u_scr, v_scr, s_scr):
    tm = h_ref.shape[0]
    inner = u_scr.shape[1]
    gw = inner // A_GROUPS
    tn = 512
    h = h_ref[...]
    mod = mod_ref[0]
    hn = _mod_rmsnorm(h, n1g_ref[...], mod[1:2], mod[0:1]).astype(BF16)
    for j in range(2 * inner // tn):
        z = _gelu_tanh(jnp.dot(hn, win_ref[:, j * tn:(j + 1) * tn], preferred_element_type=F32))
        if j < inner // tn:
            u_scr[:, j * tn:(j + 1) * tn] = z
        else:
            jj = j - inner // tn
            v_scr[:, jj * tn:(jj + 1) * tn] = z
    vsum = jnp.zeros((tm, 1), F32)
    for j in range(inner // tn):
        vsum = vsum + jnp.sum(v_scr[:, j * tn:(j + 1) * tn], axis=-1, keepdims=True)
    mu = vsum * (1.0 / inner)
    vsq = jnp.zeros((tm, 1), F32)
    for j in range(inner // tn):
        d = v_scr[:, j * tn:(j + 1) * tn] - mu
        vsq = vsq + jnp.sum(d * d, axis=-1, keepdims=True)
    rstd = lax.rsqrt(vsq * (1.0 / inner) + EPS)
    row = lax.broadcasted_iota(I32, (A_CHUNK, A_CHUNK), 0)
    col = lax.broadcasted_iota(I32, (A_CHUNK, A_CHUNK), 1)
    tril = row >= col
    for g in range(A_GROUPS):
        ws = jnp.where(tril, wsp_ref[g], 0.0).astype(BF16)
        bcol = bsp_ref[:, g:g + 1]
        lg = lng_ref[:, g * gw:(g + 1) * gw]
        lb = lnb_ref[:, g * gw:(g + 1) * gw]
        for c in range(tm // A_CHUNK):
            r0, r1 = c * A_CHUNK, (c + 1) * A_CHUNK
            vt = v_scr[r0:r1, g * gw:(g + 1) * gw]
            vn = ((vt - mu[r0:r1]) * rstd[r0:r1]) * lg + lb
            fv = jnp.dot(ws, vn.astype(BF16), preferred_element_type=F32) + bcol
            s_scr[r0:r1, g * gw:(g + 1) * gw] = (u_scr[r0:r1, g * gw:(g + 1) * gw] * fv).astype(BF16)
    y = jnp.dot(s_scr[...], wout_ref[...], preferred_element_type=F32)
    _residual_epilogue(h, y, mod, n2g_ref[...], rw_ref[...], h_out, hn_out, lg_out)


def _const_spec(shape):
    nd = len(shape)
    return pl.BlockSpec(shape, lambda i, _nd=nd: (0,) * _nd, pipeline_mode=pl.Buffered(1))


def _gmlp_layer(h, mod_l, n1g, w_in, ln_g, ln_b, w_sp, b_sp, w_out, n2g, router_w, seq):
    n, d = h.shape
    inner = w_out.shape[0]
    tm = GMLP_ROWS
    per_b = seq // tm
    ne = router_w.shape[1]
    return pl.pallas_call(
        _gmlp_kernel,
        out_shape=(jax.ShapeDtypeStruct((n, d), F32), jax.ShapeDtypeStruct((n, d // 2), I32),
                   jax.ShapeDtypeStruct((n, ne), F32)),
        grid=(n // tm,),
        in_specs=[
            pl.BlockSpec((tm, d), lambda i: (i, 0)),
            pl.BlockSpec((1, 6, d), lambda i: (i // per_b, 0, 0)),
            _const_spec((1, d)),
            _const_spec((d, 2 * inner)),
            _const_spec((1, inner)),
            _const_spec((1, inner)),
            _const_spec((A_GROUPS, A_CHUNK, A_CHUNK)),
            _const_spec((A_CHUNK, A_GROUPS)),
            _const_spec((inner, d)),
            _const_spec((1, d)),
            _const_spec((d, ne)),
        ],
        out_specs=(pl.BlockSpec((tm, d), lambda i: (i, 0)), pl.BlockSpec((tm, d // 2), lambda i: (i, 0)),
                   pl.BlockSpec((tm, ne), lambda i: (i, 0))),
        scratch_shapes=[pltpu.VMEM((tm, inner), F32), pltpu.VMEM((tm, inner), F32),
                        pltpu.VMEM((tm, inner), BF16)],
        compiler_params=_cparams(("arbitrary",)),
        name="gmlp_layer",
    )(h, mod_l, n1g.reshape(1, d), w_in.astype(BF16), ln_g.reshape(1, inner), ln_b.reshape(1, inner),
      w_sp, b_sp.T, w_out.astype(BF16), n2g.reshape(1, d), router_w)


def _first_max4(rows):
    m = jnp.maximum(jnp.maximum(rows[0], rows[1]), jnp.maximum(rows[2], rows[3]))
    idx = jnp.where(rows[0] == m, 0, jnp.where(rows[1] == m, 1, jnp.where(rows[2] == m, 2, 3)))
    return m, idx.astype(I32)


def _route_kernel(lt_ref, rb_ref, e_ref, g_ref):
    aff = jax.nn.sigmoid(lt_ref[...])
    sel = aff + rb_ref[...]
    neg = jnp.float32(-jnp.inf)
    g_score, g_i1, g_i2 = [], [], []
    for g in range(N_GROUPS):
        rows = [sel[4 * g + k:4 * g + k + 1, :] for k in range(EXPERTS_PER_GROUP)]
        m1, i1 = _first_max4(rows)
        rest = [jnp.where(i1 == k, neg, rows[k]) for k in range(EXPERTS_PER_GROUP)]
        m2, i2 = _first_max4(rest)
        g_score.append(m1 + m2)
        g_i1.append(i1)
        g_i2.append(i2)
    _, grp = _first_max4(g_score)
    l1 = jnp.zeros_like(grp)
    l2 = jnp.zeros_like(grp)
    for g in range(N_GROUPS):
        l1 = jnp.where(grp == g, g_i1[g], l1)
        l2 = jnp.where(grp == g, g_i2[g], l2)
    e1 = grp * EXPERTS_PER_GROUP + l1
    e2 = grp * EXPERTS_PER_GROUP + l2
    a1 = jnp.zeros_like(g_score[0])
    a2 = jnp.zeros_like(g_score[0])
    for e in range(N_EXPERTS):
        ae = aff[e:e + 1, :]
        a1 = jnp.where(e1 == e, ae, a1)
        a2 = jnp.where(e2 == e, ae, a2)
    tot = a1 + a2
    e_ref[0:1, :] = e1
    e_ref[1:2, :] = e2
    g_ref[0:1, :] = a1 / tot
    g_ref[1:2, :] = a2 / tot


def _route(logits, router_b):
    n, ne = logits.shape[0], N_EXPERTS
    tn = min(2048, n)
    return pl.pallas_call(
        _route_kernel,
        out_shape=(jax.ShapeDtypeStruct((TOP_K, n), I32), jax.ShapeDtypeStruct((TOP_K, n), F32)),
        grid=(n // tn,),
        in_specs=[pl.BlockSpec((ne, tn), lambda i: (0, i)), pl.BlockSpec((ne, 1), lambda i: (0, 0))],
        out_specs=(pl.BlockSpec((TOP_K, tn), lambda i: (0, i)), pl.BlockSpec((TOP_K, tn), lambda i: (0, i))),
        compiler_params=_cparams(("arbitrary",)),
        name="route",
    )(logits[:, :ne].T, router_b.reshape(ne, 1))


def _slot_kernel(e_ref, slot_ref, cnt_ref):
    n = e_ref.shape[1]
    nblk = n // 128
    eid = lax.broadcasted_iota(I32, (N_EXPERTS, 128), 0)
    r = lax.broadcasted_iota(I32, (128, 128), 0)
    c = lax.broadcasted_iota(I32, (128, 128), 1)
    upper = jnp.where(r <= c, 1.0, 0.0).astype(BF16)

    def hits(b):
        o = pl.multiple_of(b * 128, 128)
        t1 = e_ref[0:1, pl.ds(o, 128)] == eid
        t2 = e_ref[1:2, pl.ds(o, 128)] == eid
        return o, t1, t2, jnp.where(t1, 1.0, 0.0) + jnp.where(t2, 1.0, 0.0)

    def count_body(b, acc):
        return acc + hits(b)[3]

    cnt = jnp.sum(lax.fori_loop(0, nblk, count_body, jnp.zeros((N_EXPERTS, 128), F32)),
                  axis=-1, keepdims=True)
    cnt_ref[...] = cnt.astype(I32)
    padded = jnp.floor((cnt + (MOE_ROWS - 1)) * (1.0 / MOE_ROWS)) * MOE_ROWS
    er = lax.broadcasted_iota(I32, (N_EXPERTS, N_EXPERTS), 0)
    ec = lax.broadcasted_iota(I32, (N_EXPERTS, N_EXPERTS), 1)
    padded_row = jnp.sum(jnp.where(er == ec, padded, 0.0), axis=0, keepdims=True)
    start = jnp.sum(jnp.where(ec < er, padded_row, 0.0), axis=-1, keepdims=True)

    def slot_body(b, carry):
        o, t1, t2, t = hits(b)
        incl = jnp.dot(t.astype(BF16), upper, preferred_element_type=F32)
        pos = carry + incl - t
        slot_ref[0:1, pl.ds(o, 128)] = jnp.sum(jnp.where(t1, pos, 0.0), axis=0, keepdims=True).astype(I32)
        slot_ref[1:2, pl.ds(o, 128)] = jnp.sum(jnp.where(t2, pos, 0.0), axis=0, keepdims=True).astype(I32)
        return carry + incl[:, 127:128]

    lax.fori_loop(0, nblk, slot_body, start)


def _slot_tables(experts):
    n = experts.shape[1]
    a = n * TOP_K
    slot, counts = pl.pallas_call(
        _slot_kernel,
        out_shape=(jax.ShapeDtypeStruct((TOP_K, n), I32), jax.ShapeDtypeStruct((N_EXPERTS, 1), I32)),
        compiler_params=pltpu.CompilerParams(vmem_limit_bytes=VMEM_LIMIT),
        name="slots",
    )(experts)
    counts = counts.reshape(N_EXPERTS)
    padded = ((counts + MOE_ROWS - 1) // MOE_ROWS) * MOE_ROWS
    pad_end = jnp.cumsum(padded)
    pad_start = pad_end - padded
    p = a + N_EXPERTS * MOE_ROWS
    n_items = p // MOE_ROWS
    tok = jnp.broadcast_to(jnp.arange(n, dtype=I32)[None, :], (TOP_K, n))
    tok_of_slot = jnp.zeros((p,), I32).at[slot.reshape(a)].set(tok.reshape(a))
    slot_of_assign = slot.T
    item_start = jnp.arange(n_items, dtype=I32) * MOE_ROWS
    n_used = pad_end[-1] // MOE_ROWS
    item_e_raw = jnp.minimum(jnp.searchsorted(pad_end, item_start, side='right'), N_EXPERTS - 1).astype(I32)
    used = item_start < pad_end[-1]
    last_e = item_e_raw[jnp.maximum(n_used - 1, 0)]
    item_e = jnp.where(used, item_e_raw, last_e).astype(I32)
    valid = jnp.clip(counts[item_e_raw] - (item_start - pad_start[item_e_raw]), 0, MOE_ROWS)
    item_rows = jnp.where(used, valid, 0).astype(I32)
    item_blk = jnp.where(used, jnp.arange(n_items, dtype=I32), jnp.maximum(n_used - 1, 0)).astype(I32)
    return tok_of_slot, slot_of_assign, item_e, item_rows, item_blk


def _moe_kernel(ie_ref, ir_ref, ib_ref, x_ref, wg_ref, wu_ref, wd_ref, o_ref, wg_s, wu_s, wd_s, x_s):
    it = pl.program_id(0)
    f = pl.program_id(1)
    nrows = ir_ref[it]
    half = x_ref.shape[1]

    @pl.when(f == 0)
    def _():
        o_ref[...] = jnp.zeros_like(o_ref)
        pk = x_ref[...]
        x_s[:, :half] = pltpu.bitcast(pk & jnp.int32(-65536), F32).astype(BF16)
        x_s[:, half:] = pltpu.bitcast(lax.shift_left(pk, 16), F32).astype(BF16)

    @pl.when(nrows > 0)
    def _():
        wg_s[...] = wg_ref[0, 0].astype(BF16)
        wu_s[...] = wu_ref[0, 0].astype(BF16)
        wd_s[...] = wd_ref[0, 0].astype(BF16)
        nsb = (nrows + MOE_SUB - 1) // MOE_SUB

        def sub_block(sb):
            r0 = pl.multiple_of(sb * MOE_SUB, MOE_SUB)
            x = x_s[pl.ds(r0, MOE_SUB), :]
            g = jnp.dot(x, wg_s[...], preferred_element_type=F32)
            u = jnp.dot(x, wu_s[...], preferred_element_type=F32)
            hmid = ((g * jax.nn.sigmoid(g)) * u).astype(BF16)
            o_ref[pl.ds(r0, MOE_SUB), :] += jnp.dot(hmid, wd_s[...], preferred_element_type=F32)

        def pair_body(pi, carry):
            sub_block(2 * pi)
            sub_block(2 * pi + 1)
            return carry

        lax.fori_loop(0, nsb // 2, pair_body, 0)

        @pl.when(nsb % 2 == 1)
        def _():
            sub_block(nsb - 1)


def _moe_experts(x_sorted, item_e, item_rows, item_blk, w_gate, w_up, w_down, layer):
    p, half = x_sorted.shape
    d = 2 * half
    n_items = p // MOE_ROWS
    dff = w_gate.shape[3]
    nf = dff // MOE_FT
    grid_spec = pltpu.PrefetchScalarGridSpec(
        num_scalar_prefetch=3,
        grid=(n_items, nf),
        in_specs=[
            pl.BlockSpec((MOE_ROWS, half), lambda i, f, ie, ir, ib: (ib[i], 0)),
            pl.BlockSpec((1, 1, d, MOE_FT),
                         lambda i, f, ie, ir, ib: (layer, ie[i], 0, jnp.where(ir[i] > 0, f, nf - 1))),
            pl.BlockSpec((1, 1, d, MOE_FT),
                         lambda i, f, ie, ir, ib: (layer, ie[i], 0, jnp.where(ir[i] > 0, f, nf - 1))),
            pl.BlockSpec((1, 1, MOE_FT, d),
                         lambda i, f, ie, ir, ib: (layer, ie[i], jnp.where(ir[i] > 0, f, nf - 1), 0)),
        ],
        out_specs=pl.BlockSpec((MOE_ROWS, d), lambda i, f, ie, ir, ib: (i, 0)),
        scratch_shapes=[pltpu.VMEM((d, MOE_FT), BF16), pltpu.VMEM((d, MOE_FT), BF16),
                        pltpu.VMEM((MOE_FT, d), BF16), pltpu.VMEM((MOE_ROWS, d), BF16)],
    )
    return pl.pallas_call(
        _moe_kernel,
        out_shape=jax.ShapeDtypeStruct((p, d), F32),
        grid_spec=grid_spec,
        compiler_params=_cparams(("arbitrary", "arbitrary")),
        name="moe_experts",
    )(item_e, item_rows, item_blk, x_sorted, w_gate, w_up, w_down)


def _moe(hn2, logits, router_b, w_gate, w_up, w_down, layer):
    experts, gates = _route(logits, router_b)
    tok_of_slot, slot_of_assign, item_e, item_rows, item_blk = _slot_tables(experts)
    x_sorted = hn2.at[tok_of_slot].get(mode='promise_in_bounds')
    y_slot = _moe_experts(x_sorted, item_e, item_rows, item_blk, w_gate, w_up, w_down, layer)
    y0 = y_slot.at[slot_of_assign[:, 0]].get(mode='promise_in_bounds')
    y1 = y_slot.at[slot_of_assign[:, 1]].get(mode='promise_in_bounds')
    n = hn2.shape[0]
    gates_rep = jnp.concatenate([jnp.broadcast_to(gates[k][:, None], (n, 128)) for k in range(TOP_K)], axis=1)
    return y0, y1, gates_rep


def _moe_combine(h1, y0, y1, gates, mod):
    reps = h1.shape[1] // 128
    g0 = pltpu.repeat(gates[:, :128], reps, axis=1)
    g1 = pltpu.repeat(gates[:, 128:], reps, axis=1)
    return h1 + mod[5:6] * (g0 * y0 + g1 * y1)


def _bproj_kernel(h1_ref, y0_ref, y1_ref, gt_ref, modp_ref, mod_ref, n1g_ref, w_ref, kvg_ref, kig_ref,
                  wuk_ref, h_out, qa_out, ckv_out, qi_out, ki_out, wi_out):
    tm = h1_ref.shape[0]
    h = _moe_combine(h1_ref[...], y0_ref[...], y1_ref[...], gt_ref[...], modp_ref[0])
    h_out[...] = h
    mod = mod_ref[0]
    hn = _mod_rmsnorm(h, n1g_ref[...], mod[1:2], mod[0:1]).astype(BF16)
    proj = jnp.dot(hn, w_ref[...], preferred_element_type=F32)
    o1 = B_HEADS * B_HEAD_DIM
    o2 = o1 + B_KV_LATENT
    o3 = o2 + B_IDX_HEADS * B_IDX_DIM
    ckv = proj[:, o1:o2]
    ckv = ckv * lax.rsqrt(jnp.mean(ckv * ckv, axis=-1, keepdims=True) + EPS) * kvg_ref[...]
    ckv_out[...] = ckv.astype(BF16)
    tail = proj[:, o3:o3 + 128]
    ki = tail[:, :B_IDX_DIM]
    ki = ki * lax.rsqrt(jnp.mean(ki * ki, axis=-1, keepdims=True) + EPS) * kig_ref[...]
    ki_out[...] = ki.astype(BF16)
    wi = tail[:, B_IDX_DIM:B_IDX_DIM + B_IDX_HEADS] * (B_IDX_HEADS ** -0.5 * B_IDX_DIM ** -0.5)
    scale = B_HEAD_DIM ** -0.5
    for blk in range(tm // QBLK):
        r0, r1 = blk * QBLK, (blk + 1) * QBLK
        for hh in range(B_HEADS):
            qh = proj[r0:r1, hh * B_HEAD_DIM:(hh + 1) * B_HEAD_DIM].astype(BF16)
            qa = jnp.dot(qh, wuk_ref[hh], preferred_element_type=F32) * scale
            qa_out[blk, hh * QBLK:(hh + 1) * QBLK, :] = qa.astype(BF16)
        for hh in range(B_IDX_HEADS):
            qi_out[blk, hh * QBLK:(hh + 1) * QBLK, :] = proj[r0:r1, o2 + hh * B_IDX_DIM:
                                                            o2 + (hh + 1) * B_IDX_DIM].astype(BF16)
            wi_out[blk, hh * QBLK:(hh + 1) * QBLK, :] = jnp.broadcast_to(wi[r0:r1, hh:hh + 1], (QBLK, 128))


def _bproj(h1, y0, y1, gates, mod_prev, mod_l, n1g, b_w_in, kv_g, w_uk, kidx_g, seq):
    n, d = h1.shape
    tm = ROW_TILE
    per_b = seq // tm
    nq = n // QBLK
    qpb = tm // QBLK
    o1 = B_HEADS * B_HEAD_DIM
    o2 = o1 + B_KV_LATENT
    o3 = o2 + B_IDX_HEADS * B_IDX_DIM
    wcat = jnp.zeros((d, o3 + 128), F32).at[:, :b_w_in.shape[1]].set(b_w_in).astype(BF16)
    wuk_t = jnp.transpose(w_uk, (1, 2, 0)).astype(BF16)
    row = lambda i: (i, 0)
    modm = lambda i: (i // per_b, 0, 0)
    blk3 = lambda i: (i, 0, 0)
    return pl.pallas_call(
        _bproj_kernel,
        out_shape=(
            jax.ShapeDtypeStruct((n, d), F32),
            jax.ShapeDtypeStruct((nq, B_HEADS * QBLK, B_KV_LATENT), BF16),
            jax.ShapeDtypeStruct((n, B_KV_LATENT), BF16),
            jax.ShapeDtypeStruct((nq, B_IDX_HEADS * QBLK, B_IDX_DIM), BF16),
            jax.ShapeDtypeStruct((n, B_IDX_DIM), BF16),
            jax.ShapeDtypeStruct((nq, B_IDX_HEADS * QBLK, 128), F32),
        ),
        grid=(n // tm,),
        in_specs=[
            pl.BlockSpec((tm, d), row), pl.BlockSpec((tm, d), row), pl.BlockSpec((tm, d), row),
            pl.BlockSpec((tm, TOP_K * 128), row),
            pl.BlockSpec((1, 6, d), modm), pl.BlockSpec((1, 6, d), modm),
            _const_spec((1, d)),
            _const_spec((d, o3 + 128)),
            _const_spec((1, B_KV_LATENT)),
            _const_spec((1, B_IDX_DIM)),
            _const_spec((B_HEADS, B_HEAD_DIM, B_KV_LATENT)),
        ],
        out_specs=(
            pl.BlockSpec((tm, d), row),
            pl.BlockSpec((qpb, B_HEADS * QBLK, B_KV_LATENT), blk3),
            pl.BlockSpec((tm, B_KV_LATENT), row),
            pl.BlockSpec((qpb, B_IDX_HEADS * QBLK, B_IDX_DIM), blk3),
            pl.BlockSpec((tm, B_IDX_DIM), row),
            pl.BlockSpec((qpb, B_IDX_HEADS * QBLK, 128), blk3),
        ),
        compiler_params=_cparams(("arbitrary",)),
        name="dsa_proj",
    )(h1, y0, y1, gates, mod_prev, mod_l, n1g.reshape(1, d), wcat, kv_g.reshape(1, -1),
      kidx_g.reshape(1, -1), wuk_t)


def _t5_bucket(dist):
    n = jnp.maximum(dist, 0)
    exact = REL_BUCKETS // 2
    nf = jnp.maximum(n, 1).astype(F32)
    large = exact + (jnp.log(nf / exact) / math.log(REL_MAX_DIST / exact)
                     * (REL_BUCKETS - exact)).astype(I32)
    large = jnp.minimum(large, REL_BUCKETS - 1)
    return jnp.where(n < exact, n, large)


def _bias_tables(rel_bias):
    assert REL_MAX_DIST <= 128
    t = jnp.arange(128, dtype=I32)[:, None]
    s = jnp.arange(128, dtype=I32)[None, :]
    far = rel_bias[REL_BUCKETS - 1]
    diag = rel_bias[_t5_bucket(t - s)] - far
    prev = rel_bias[_t5_bucket(t - s + 128)] - far
    return jnp.stack([jnp.transpose(diag, (2, 0, 1)), jnp.transpose(prev, (2, 0, 1))])


def _attn_kernel(qa_ref, qi_ref, wi_ref, ckv_ref, ki_ref, bt_ref, o_ref, kbuf, m_scr, l_scr, a_scr, acc_scr,
                 s_scr, p_scr, madd_scr):
    i = pl.program_id(1)
    nt = i + 1
    t_row = i * QBLK + lax.broadcasted_iota(I32, (QBLK, KTILE), 0)
    lane = lax.broadcasted_iota(I32, (QBLK, KTILE), 1)

    qi = qi_ref[0]

    def score_body(j, carry):
        k0 = pl.multiple_of(j * KTILE, KTILE)
        kt = ki_ref[0, pl.ds(k0, KTILE), :]
        p = lax.dot_general(qi, kt, (((1,), (1,)), ((), ())), preferred_element_type=F32)
        p = jnp.maximum(p, 0.0) * pltpu.repeat(wi_ref[0], KTILE // 128, axis=1)
        sc = p[0:QBLK]
        for hh in range(1, B_IDX_HEADS):
            sc = sc + p[hh * QBLK:(hh + 1) * QBLK]
        bits = pltpu.bitcast(sc + 0.0, I32)
        key = jnp.where(bits < 0, bits ^ jnp.int32(0x7FFFFFFF), bits)
        key = jnp.where(k0 + lane <= t_row, key, jnp.int32(INT_MIN))
        kbuf[:, pl.ds(k0, KTILE)] = key
        return carry

    lax.fori_loop(0, nt, score_body, 0)

    def fold_lanes(x):
        out = x[:, 0:128]
        for k in range(1, KTILE // 128):
            out = out + x[:, k * 128:(k + 1) * 128]
        return out

    def count_ge(cand):
        def body(j, acc):
            k0 = pl.multiple_of(j * KTILE, KTILE)
            return acc + fold_lanes(jnp.where(kbuf[:, pl.ds(k0, KTILE)] >= cand, 1, 0))
        acc = lax.fori_loop(0, nt, body, jnp.zeros((QBLK, 128), I32))
        return jnp.sum(acc, axis=-1, keepdims=True)

    def bit_step(b, lo, n_ge):
        cand = lo + lax.shift_left(jnp.int32(1), 31 - b)
        cnt = count_ge(cand)
        take = cnt >= B_TOPK_MAX
        return jnp.where(take, cand, lo), jnp.where(take, cnt, n_ge)

    def bit_cond(c):
        return (c[0] < 32) & (c[3] == 0)

    def bit_body(c):
        b, lo, n_ge, _ = c
        lo, n_ge = bit_step(b, lo, n_ge)
        lo, n_ge = bit_step(b + 1, lo, n_ge)
        done = (jnp.max(n_ge) == B_TOPK_MAX).astype(I32)
        return b + 2, lo, n_ge, done

    _, tau, n_ge, _ = lax.while_loop(
        bit_cond, bit_body,
        (jnp.int32(0), jnp.full((QBLK, 1), INT_MIN, I32), jnp.full((QBLK, 1), nt * KTILE, I32),
         (i < 1).astype(I32)))
    tau = jnp.maximum(tau, jnp.int32(INT_MIN + 1))
    excess = (i >= 1) & (jnp.max(n_ge) > B_TOPK_MAX)

    @pl.when(excess)
    def _():
        n_gt = count_ge(tau + 1)
        need = B_TOPK_MAX - n_gt

        def count_eq_before(pos):
            def body(j, acc):
                k0 = pl.multiple_of(j * KTILE, KTILE)
                hit = (kbuf[:, pl.ds(k0, KTILE)] == tau) & (k0 + lane < pos)
                return acc + fold_lanes(jnp.where(hit, 1, 0))
            acc = lax.fori_loop(0, nt, body, jnp.zeros((QBLK, 128), I32))
            return jnp.sum(acc, axis=-1, keepdims=True)

        def pos_body(b, pos):
            cand = pos + lax.shift_left(jnp.int32(1), 12 - b)
            return jnp.where(count_eq_before(cand) < need, cand, pos)

        pos = lax.fori_loop(0, 13, pos_body, jnp.zeros((QBLK, 1), I32))

        def drop_body(j, carry):
            k0 = pl.multiple_of(j * KTILE, KTILE)
            kk = kbuf[:, pl.ds(k0, KTILE)]
            drop = (kk == tau) & (k0 + lane > pos) & (n_ge > B_TOPK_MAX)
            kbuf[:, pl.ds(k0, KTILE)] = jnp.where(drop, jnp.int32(INT_MIN), kk)
            return carry

        lax.fori_loop(0, nt, drop_body, 0)

    m_scr[...] = jnp.full_like(m_scr, NEG_BIG)
    l_scr[...] = jnp.zeros_like(l_scr)
    acc_scr[...] = jnp.zeros_like(acc_scr)

    def attend(k0, width, tile_off):
        kv = ckv_ref[0, pl.ds(k0, width), :]
        s_scr[:, :width] = lax.dot_general(qa_ref[0], kv, (((1,), (1,)), ((), ())),
                                           preferred_element_type=F32)
        madd_scr[:, :width] = jnp.where(kbuf[:, pl.ds(k0, width)] >= tau, 0.0, NEG_BIG)
        for r0 in range(0, B_HEADS * QBLK, SM_ROWS):
            r1 = r0 + SM_ROWS
            hh, q0 = r0 // QBLK, r0 % QBLK
            parts = []
            for kc in range(width // 128):
                sp = s_scr[r0:r1, kc * 128:(kc + 1) * 128] + madd_scr[q0:q0 + SM_ROWS, kc * 128:(kc + 1) * 128]
                if tile_off is not None:
                    rel = (QBLK // 128) * tile_off + q0 // 128 - kc
                    if rel in (0, 1):
                        sp = sp + bt_ref[rel, hh, q0 % 128:q0 % 128 + SM_ROWS, :]
                parts.append(sp)
            smax = parts[0]
            for sp in parts[1:]:
                smax = jnp.maximum(smax, sp)
            m_old = m_scr[r0:r1, :]
            m_new = jnp.maximum(m_old, jnp.max(smax, axis=-1, keepdims=True))
            alpha = jnp.exp(m_old - m_new)
            psum = None
            for kc, sp in enumerate(parts):
                p = jnp.exp(sp - m_new)
                p_scr[r0:r1, kc * 128:(kc + 1) * 128] = p.astype(BF16)
                psum = p if psum is None else psum + p
            l_scr[r0:r1, :] = alpha * l_scr[r0:r1, :] + jnp.sum(psum, axis=-1, keepdims=True)
            m_scr[r0:r1, :] = m_new
            a_scr[r0:r1, :] = alpha
        pv = jnp.dot(p_scr[:, :width], kv, preferred_element_type=F32)
        acc_scr[...] = pltpu.repeat(a_scr[...], B_KV_LATENT // 128, axis=1) * acc_scr[...] + pv

    n_far = jnp.maximum(i - 1, 0)
    per_far = FAR_TILE // KTILE

    def far_body(jf, carry):
        attend(pl.multiple_of(jf * FAR_TILE, FAR_TILE), FAR_TILE, None)
        return carry

    lax.fori_loop(0, n_far // per_far, far_body, 0)

    @pl.when(n_far % per_far == 1)
    def _():
        attend(pl.multiple_of((n_far - 1) * KTILE, KTILE), KTILE, None)

    @pl.when(i >= 1)
    def _():
        attend(pl.multiple_of((i - 1) * KTILE, KTILE), KTILE, 1)

    attend(pl.multiple_of(i * KTILE, KTILE), KTILE, 0)
    inv_l = 1.0 / l_scr[...]
    o_ref[0] = (acc_scr[...] * pltpu.repeat(inv_l, B_KV_LATENT // 128, axis=1)).astype(BF16)


def _attention(qa, qi, wi, ckv, ki, btab, bsz, seq):
    nqb = seq // QBLK
    gq = lambda b, i: (b * nqb + i, 0, 0)
    gb = lambda b, i: (b, 0, 0)
    return pl.pallas_call(
        _attn_kernel,
        out_shape=jax.ShapeDtypeStruct(qa.shape, BF16),
        grid=(bsz, nqb),
        in_specs=[
            pl.BlockSpec((1, B_HEADS * QBLK, B_KV_LATENT), gq),
            pl.BlockSpec((1, B_IDX_HEADS * QBLK, B_IDX_DIM), gq),
            pl.BlockSpec((1, B_IDX_HEADS * QBLK, 128), gq),
            pl.BlockSpec((1, seq, B_KV_LATENT), gb, pipeline_mode=pl.Buffered(1)),
            pl.BlockSpec((1, seq, B_IDX_DIM), gb, pipeline_mode=pl.Buffered(1)),
            pl.BlockSpec((2, B_HEADS, 128, 128), lambda b, i: (0, 0, 0, 0), pipeline_mode=pl.Buffered(1)),
        ],
        out_specs=pl.BlockSpec((1, B_HEADS * QBLK, B_KV_LATENT), gq),
        scratch_shapes=[
            pltpu.VMEM((QBLK, seq), I32),
            pltpu.VMEM((B_HEADS * QBLK, 128), F32),
            pltpu.VMEM((B_HEADS * QBLK, 128), F32),
            pltpu.VMEM((B_HEADS * QBLK, 128), F32),
            pltpu.VMEM((B_HEADS * QBLK, B_KV_LATENT), F32),
            pltpu.VMEM((B_HEADS * QBLK, FAR_TILE), F32),
            pltpu.VMEM((B_HEADS * QBLK, FAR_TILE), BF16),
            pltpu.VMEM((QBLK, FAR_TILE), F32),
        ],
        compiler_params=_cparams(("arbitrary", "arbitrary")),
        name="dsa_attention",
    )(qa, qi, wi, ckv.reshape(bsz, seq, -1), ki.reshape(bsz, seq, -1), btab)


def _bout_kernel(h_ref, ol_ref, mod_ref, wuv_ref, wout_ref, n2g_ref, rw_ref, h_out, hn_out, lg_out, o_scr):
    mod = mod_ref[0]
    for blk in range(ol_ref.shape[0]):
        for hh in range(B_HEADS):
            oh = jnp.dot(ol_ref[blk, hh * QBLK:(hh + 1) * QBLK, :], wuv_ref[hh], preferred_element_type=F32)
            o_scr[blk * QBLK:(blk + 1) * QBLK, hh * B_V_DIM:(hh + 1) * B_V_DIM] = oh.astype(BF16)
    y = jnp.dot(o_scr[...], wout_ref[...], preferred_element_type=F32)
    _residual_epilogue(h_ref[...], y, mod, n2g_ref[...], rw_ref[...], h_out, hn_out, lg_out)


def _bout(h, o_lat, mod_l, w_uv, w_out, n2g, router_w, seq):
    n, d = h.shape
    tm = ROW_TILE
    per_b = seq // tm
    qpb = tm // QBLK
    ne = router_w.shape[1]
    wuv_t = jnp.transpose(w_uv, (1, 0, 2)).astype(BF16)
    row = lambda i: (i, 0)
    return pl.pallas_call(
        _bout_kernel,
        out_shape=(jax.ShapeDtypeStruct((n, d), F32), jax.ShapeDtypeStruct((n, d // 2), I32),
                   jax.ShapeDtypeStruct((n, ne), F32)),
        grid=(n // tm,),
        in_specs=[
            pl.BlockSpec((tm, d), row),
            pl.BlockSpec((qpb, B_HEADS * QBLK, B_KV_LATENT), lambda i: (i, 0, 0)),
            pl.BlockSpec((1, 6, d), lambda i: (i // per_b, 0, 0)),
            _const_spec((B_HEADS, B_KV_LATENT, B_V_DIM)),
            _const_spec((B_HEADS * B_V_DIM, d)),
            _const_spec((1, d)),
            _const_spec((d, ne)),
        ],
        out_specs=(pl.BlockSpec((tm, d), row), pl.BlockSpec((tm, d // 2), row), pl.BlockSpec((tm, ne), row)),
        scratch_shapes=[pltpu.VMEM((tm, B_HEADS * B_V_DIM), BF16)],
        compiler_params=_cparams(("arbitrary",)),
        name="dsa_out",
    )(h, o_lat, mod_l, wuv_t, w_out.astype(BF16), n2g.reshape(1, d), router_w)


def _final_kernel(h1_ref, y0_ref, y1_ref, gt_ref, mod_ref, g_ref, o_ref):
    h = _moe_combine(h1_ref[...], y0_ref[...], y1_ref[...], gt_ref[...], mod_ref[0])
    o_ref[...] = h * lax.rsqrt(jnp.mean(h * h, axis=-1, keepdims=True) + EPS) * g_ref[...]


def _final(h1, y0, y1, gates, mod_l, final_g, seq):
    n, d = h1.shape
    tm = 512
    per_b = seq // tm
    row = lambda i: (i, 0)
    return pl.pallas_call(
        _final_kernel,
        out_shape=jax.ShapeDtypeStruct((n, d), F32),
        grid=(n // tm,),
        in_specs=[pl.BlockSpec((tm, d), row), pl.BlockSpec((tm, d), row), pl.BlockSpec((tm, d), row),
                  pl.BlockSpec((tm, TOP_K * 128), row), pl.BlockSpec((1, 6, d), lambda i: (i // per_b, 0, 0)),
                  _const_spec((1, d))],
        out_specs=pl.BlockSpec((tm, d), row),
        compiler_params=_cparams(("arbitrary",)),
        name="final_norm",
    )(h1, y0, y1, gates, mod_l, final_g.reshape(1, d))


def kernel(x, c, ada_w, ada_b, norm1_g, norm2_g, a_w_in, a_ln_g, a_ln_b, a_w_sp, a_b_sp, a_w_out, b_w_in,
           b_kv_norm_g, b_w_uk, b_w_uv, b_kidx_g, b_w_out, rel_bias, router_w, router_b, moe_w_gate,
           moe_w_up, moe_w_down, final_g):
    bsz, seq, d = x.shape
    n = bsz * seq
    mod = _adaln(c, ada_w, ada_b).reshape(ada_w.shape[0], bsz, 6, d)
    rw_pad = jnp.zeros((d, 128), F32).at[:, :N_EXPERTS].set(router_w)
    h = x.reshape(n, d)

    h1, hn2, logits = _gmlp_layer(h, mod[0], norm1_g[0], a_w_in[0], a_ln_g[0], a_ln_b[0], a_w_sp[0],
                                  a_b_sp[0], a_w_out[0], norm2_g[0], rw_pad, seq)
    y0, y1, gates = _moe(hn2, logits, router_b, moe_w_gate, moe_w_up, moe_w_down, 0)

    h, qa, ckv, qi, ki, wi = _bproj(h1, y0, y1, gates, mod[0], mod[1], norm1_g[1], b_w_in[0],
                                    b_kv_norm_g[0], b_w_uk[0], b_kidx_g[0], seq)
    o_lat = _attention(qa, qi, wi, ckv, ki, _bias_tables(rel_bias), bsz, seq)
    h1, hn2, logits = _bout(h, o_lat, mod[1], b_w_uv[0], b_w_out[0], norm2_g[1], rw_pad, seq)
    y0, y1, gates = _moe(hn2, logits, router_b, moe_w_gate, moe_w_up, moe_w_down, 1)

    out = _final(h1, y0, y1, gates, mod[1], final_g, seq)
    return out.reshape(bsz, seq, d)
```

```python
import functools
import math

import jax
import jax.numpy as jnp
from jax import lax
from jax.experimental import pallas as pl
from jax.experimental.pallas import tpu as pltpu

F32 = jnp.float32
BF16 = jnp.bfloat16
I32 = jnp.int32
HIGHEST = lax.Precision.HIGHEST

EPS = 1e-6
A_CHUNK = 128
A_GROUPS = 8
B_HEADS = 16
B_HEAD_DIM = 64
B_V_DIM = 64
B_KV_LATENT = 256
B_IDX_HEADS = 8
B_IDX_DIM = 64
B_TOPK_MAX = 256
QBLK = 256
KTILE = 256
FAR_TILE = 256
SM_ROWS = 128
REL_BUCKETS = 32
REL_MAX_DIST = 128
N_EXPERTS = 16
N_GROUPS = 4
EXPERTS_PER_GROUP = 4
TOP_K = 2
MOE_ROWS = 1024
MOE_SUB = 256
MOE_FT = 512

ROW_TILE = 256
GMLP_ROWS = 512
VMEM_LIMIT = 60 * 1024 * 1024

INT_MIN = -2 ** 31
NEG_BIG = -1e30


def _cparams(sem):
    return pltpu.CompilerParams(dimension_semantics=sem, vmem_limit_bytes=VMEM_LIMIT)


def _mod_rmsnorm(h, g, scale, shift):
    ms = jnp.mean(h * h, axis=-1, keepdims=True)
    return (h * lax.rsqrt(ms + EPS) * g) * (1.0 + scale) + shift


def _gelu_tanh(x):
    c = math.sqrt(2.0 / math.pi)
    return 0.5 * x * (1.0 + jnp.tanh(c * (x + 0.044715 * (x * x * x))))


def _adaln_kernel(c_ref, w_ref, b_ref, o_ref):
    c = c_ref[...]
    sc = c * jax.nn.sigmoid(c)
    o_ref[0] = jnp.dot(sc, w_ref[0], precision=HIGHEST, preferred_element_type=F32) + b_ref[0]


def _adaln(c, ada_w, ada_b):
    depth, d, e = ada_w.shape
    bsz = c.shape[0]
    bp = 8
    c_pad = jnp.zeros((bp, d), F32).at[:bsz].set(c)
    tn = 1024
    out = pl.pallas_call(
        _adaln_kernel,
        out_shape=jax.ShapeDtypeStruct((depth, bp, e), F32),
        grid=(depth, e // tn),
        in_specs=[
            pl.BlockSpec((bp, d), lambda l, j: (0, 0)),
            pl.BlockSpec((1, d, tn), lambda l, j: (l, 0, j)),
            pl.BlockSpec((1, 1, tn), lambda l, j: (l, 0, j)),
        ],
        out_specs=pl.BlockSpec((1, bp, tn), lambda l, j: (l, 0, j)),
        compiler_params=_cparams(("arbitrary", "arbitrary")),
        name="adaln",
    )(c_pad, ada_w, ada_b.reshape(depth, 1, e))
    return out[:, :bsz]


def _residual_epilogue(h, y, mod, n2g, rw, h_out, hn_out, lg_out):
    h1 = h + mod[2:3] * y
    h_out[...] = h1
    hn2 = _mod_rmsnorm(h1, n2g, mod[4:5], mod[3:4])
    bits = pltpu.bitcast(hn2.astype(BF16).astype(F32), I32)
    half = bits.shape[1] // 2
    hn_out[...] = bits[:, :half] | lax.shift_right_logical(bits[:, half:], 16)
    lg_out[...] = jnp.dot(hn2, rw, precision=HIGHEST, preferred_element_type=F32)


def _gmlp_kernel(h_ref, mod_ref, n1g_ref, win_ref, lng_ref, lnb_ref, wsp_ref, bsp_ref, wout_ref,
                 n2g_ref, rw_ref, h_out, hn_out, lg_out, u_scr, v_scr, s_scr):
    tm = h_ref.shape[0]
    inner = u_scr.shape[1]
    gw = inner // A_GROUPS
    tn = 512
    h = h_ref[...]
    mod = mod_ref[0]
    hn = _mod_rmsnorm(h, n1g_ref[...], mod[1:2], mod[0:1]).astype(BF16)
    for j in range(2 * inner // tn):
        z = _gelu_tanh(jnp.dot(hn, win_ref[:, j * tn:(j + 1) * tn], preferred_element_type=F32))
        if j < inner // tn:
            u_scr[:, j * tn:(j + 1) * tn] = z
        else:
            jj = j - inner // tn
            v_scr[:, jj * tn:(jj + 1) * tn] = z
    vsum = jnp.zeros((tm, 1), F32)
    for j in range(inner // tn):
        vsum = vsum + jnp.sum(v_scr[:, j * tn:(j + 1) * tn], axis=-1, keepdims=True)
    mu = vsum * (1.0 / inner)
    vsq = jnp.zeros((tm, 1), F32)
    for j in range(inner // tn):
        d = v_scr[:, j * tn:(j + 1) * tn] - mu
        vsq = vsq + jnp.sum(d * d, axis=-1, keepdims=True)
    rstd = lax.rsqrt(vsq * (1.0 / inner) + EPS)
    row = lax.broadcasted_iota(I32, (A_CHUNK, A_CHUNK), 0)
    col = lax.broadcasted_iota(I32, (A_CHUNK, A_CHUNK), 1)
    tril = row >= col
    for g in range(A_GROUPS):
        ws = jnp.where(tril, wsp_ref[g], 0.0).astype(BF16)
        bcol = bsp_ref[:, g:g + 1]
        lg = lng_ref[:, g * gw:(g + 1) * gw]
        lb = lnb_ref[:, g * gw:(g + 1) * gw]
        for c in range(tm // A_CHUNK):
            r0, r1 = c * A_CHUNK, (c + 1) * A_CHUNK
            vt = v_scr[r0:r1, g * gw:(g + 1) * gw]
            vn = ((vt - mu[r0:r1]) * rstd[r0:r1]) * lg + lb
            fv = jnp.dot(ws, vn.astype(BF16), preferred_element_type=F32) + bcol
            s_scr[r0:r1, g * gw:(g + 1) * gw] = (u_scr[r0:r1, g * gw:(g + 1) * gw] * fv).astype(BF16)
    y = jnp.dot(s_scr[...], wout_ref[...], preferred_element_type=F32)
    _residual_epilogue(h, y, mod, n2g_ref[...], rw_ref[...], h_out, hn_out, lg_out)


def _const_spec(shape):
    nd = len(shape)
    return pl.BlockSpec(shape, lambda i, _nd=nd: (0,) * _nd, pipeline_mode=pl.Buffered(1))


def _gmlp_layer(h, mod_l, n1g, w_in, ln_g, ln_b, w_sp, b_sp, w_out, n2g, router_w, seq):
    n, d = h.shape
    inner = w_out.shape[0]
    tm = GMLP_ROWS
    per_b = seq // tm
    ne = router_w.shape[1]
    return pl.pallas_call(
        _gmlp_kernel,
        out_shape=(jax.ShapeDtypeStruct((n, d), F32), jax.ShapeDtypeStruct((n, d // 2), I32),
                   jax.ShapeDtypeStruct((n, ne), F32)),
        grid=(n // tm,),
        in_specs=[
            pl.BlockSpec((tm, d), lambda i: (i, 0)),
            pl.BlockSpec((1, 6, d), lambda i: (i // per_b, 0, 0)),
            _const_spec((1, d)),
            _const_spec((d, 2 * inner)),
            _const_spec((1, inner)),
            _const_spec((1, inner)),
            _const_spec((A_GROUPS, A_CHUNK, A_CHUNK)),
            _const_spec((A_CHUNK, A_GROUPS)),
            _const_spec((inner, d)),
            _const_spec((1, d)),
            _const_spec((d, ne)),
        ],
        out_specs=(pl.BlockSpec((tm, d), lambda i: (i, 0)), pl.BlockSpec((tm, d // 2), lambda i: (i, 0)),
                   pl.BlockSpec((tm, ne), lambda i: (i, 0))),
        scratch_shapes=[pltpu.VMEM((tm, inner), F32), pltpu.VMEM((tm, inner), F32),
                        pltpu.VMEM((tm, inner), BF16)],
        compiler_params=_cparams(("arbitrary",)),
        name="gmlp_layer",
    )(h, mod_l, n1g.reshape(1, d), w_in.astype(BF16), ln_g.reshape(1, inner), ln_b.reshape(1, inner),
      w_sp, b_sp.T, w_out.astype(BF16), n2g.reshape(1, d), router_w)


def _first_max4(rows):
    m = jnp.maximum(jnp.maximum(rows[0], rows[1]), jnp.maximum(rows[2], rows[3]))
    idx = jnp.where(rows[0] == m, 0, jnp.where(rows[1] == m, 1, jnp.where(rows[2] == m, 2, 3)))
    return m, idx.astype(I32)


def _route_kernel(lt_ref, rb_ref, e_ref, g_ref):
    aff = jax.nn.sigmoid(lt_ref[...])
    sel = aff + rb_ref[...]
    neg = jnp.float32(-jnp.inf)
    g_score, g_i1, g_i2 = [], [], []
    for g in range(N_GROUPS):
        rows = [sel[4 * g + k:4 * g + k + 1, :] for k in range(EXPERTS_PER_GROUP)]
        m1, i1 = _first_max4(rows)
        rest = [jnp.where(i1 == k, neg, rows[k]) for k in range(EXPERTS_PER_GROUP)]
        m2, i2 = _first_max4(rest)
        g_score.append(m1 + m2)
        g_i1.append(i1)
        g_i2.append(i2)
    _, grp = _first_max4(g_score)
    l1 = jnp.zeros_like(grp)
    l2 = jnp.zeros_like(grp)
    for g in range(N_GROUPS):
        l1 = jnp.where(grp == g, g_i1[g], l1)
        l2 = jnp.where(grp == g, g_i2[g], l2)
    e1 = grp * EXPERTS_PER_GROUP + l1
    e2 = grp * EXPERTS_PER_GROUP + l2
    a1 = jnp.zeros_like(g_score[0])
    a2 = jnp.zeros_like(g_score[0])
    for e in range(N_EXPERTS):
        ae = aff[e:e + 1, :]
        a1 = jnp.where(e1 == e, ae, a1)
        a2 = jnp.where(e2 == e, ae, a2)
    tot = a1 + a2
    e_ref[0:1, :] = e1
    e_ref[1:2, :] = e2
    g_ref[0:1, :] = a1 / tot
    g_ref[1:2, :] = a2 / tot


def _route(logits, router_b):
    n, ne = logits.shape[0], N_EXPERTS
    tn = min(2048, n)
    return pl.pallas_call(
        _route_kernel,
        out_shape=(jax.ShapeDtypeStruct((TOP_K, n), I32), jax.ShapeDtypeStruct((TOP_K, n), F32)),
        grid=(n // tn,),
        in_specs=[pl.BlockSpec((ne, tn), lambda i: (0, i)), pl.BlockSpec((ne, 1), lambda i: (0, 0))],
        out_specs=(pl.BlockSpec((TOP_K, tn), lambda i: (0, i)), pl.BlockSpec((TOP_K, tn), lambda i: (0, i))),
        compiler_params=_cparams(("arbitrary",)),
        name="route",
    )(logits[:, :ne].T, router_b.reshape(ne, 1))


def _slot_kernel(e_ref, slot_ref, cnt_ref):
    n = e_ref.shape[1]
    nblk = n // 128
    eid = lax.broadcasted_iota(I32, (N_EXPERTS, 128), 0)
    r = lax.broadcasted_iota(I32, (128, 128), 0)
    c = lax.broadcasted_iota(I32, (128, 128), 1)
    upper = jnp.where(r <= c, 1.0, 0.0).astype(BF16)

    def hits(b):
        o = pl.multiple_of(b * 128, 128)
        t1 = e_ref[0:1, pl.ds(o, 128)] == eid
        t2 = e_ref[1:2, pl.ds(o, 128)] == eid
        return o, t1, t2, jnp.where(t1, 1.0, 0.0) + jnp.where(t2, 1.0, 0.0)

    def count_body(b, acc):
        return acc + hits(b)[3]

    cnt = jnp.sum(lax.fori_loop(0, nblk, count_body, jnp.zeros((N_EXPERTS, 128), F32)),
                  axis=-1, keepdims=True)
    cnt_ref[...] = cnt.astype(I32)
    padded = jnp.floor((cnt + (MOE_ROWS - 1)) * (1.0 / MOE_ROWS)) * MOE_ROWS
    er = lax.broadcasted_iota(I32, (N_EXPERTS, N_EXPERTS), 0)
    ec = lax.broadcasted_iota(I32, (N_EXPERTS, N_EXPERTS), 1)
    padded_row = jnp.sum(jnp.where(er == ec, padded, 0.0), axis=0, keepdims=True)
    start = jnp.sum(jnp.where(ec < er, padded_row, 0.0), axis=-1, keepdims=True)

    def slot_body(b, carry):
        o, t1, t2, t = hits(b)
        incl = jnp.dot(t.astype(BF16), upper, preferred_element_type=F32)
        pos = carry + incl - t
        slot_ref[0:1, pl.ds(o, 128)] = jnp.sum(jnp.where(t1, pos, 0.0), axis=0, keepdims=True).astype(I32)
        slot_ref[1:2, pl.ds(o, 128)] = jnp.sum(jnp.where(t2, pos, 0.0), axis=0, keepdims=True).astype(I32)
        return carry + incl[:, 127:128]

    lax.fori_loop(0, nblk, slot_body, start)


def _slot_tables(experts):
    n = experts.shape[1]
    a = n * TOP_K
    slot, counts = pl.pallas_call(
        _slot_kernel,
        out_shape=(jax.ShapeDtypeStruct((TOP_K, n), I32), jax.ShapeDtypeStruct((N_EXPERTS, 1), I32)),
        compiler_params=pltpu.CompilerParams(vmem_limit_bytes=VMEM_LIMIT),
        name="slots",
    )(experts)
    counts = counts.reshape(N_EXPERTS)
    padded = ((counts + MOE_ROWS - 1) // MOE_ROWS) * MOE_ROWS
    pad_end = jnp.cumsum(padded)
    pad_start = pad_end - padded
    p = a + N_EXPERTS * MOE_ROWS
    n_items = p // MOE_ROWS
    tok = jnp.broadcast_to(jnp.arange(n, dtype=I32)[None, :], (TOP_K, n))
    tok_of_slot = jnp.zeros((p,), I32).at[slot.reshape(a)].set(tok.reshape(a))
    slot_of_assign = slot.T
    item_start = jnp.arange(n_items, dtype=I32) * MOE_ROWS
    n_used = pad_end[-1] // MOE_ROWS
    item_e_raw = jnp.minimum(jnp.searchsorted(pad_end, item_start, side='right'), N_EXPERTS - 1).astype(I32)
    used = item_start < pad_end[-1]
    last_e = item_e_raw[jnp.maximum(n_used - 1, 0)]
    item_e = jnp.where(used, item_e_raw, last_e).astype(I32)
    valid = jnp.clip(counts[item_e_raw] - (item_start - pad_start[item_e_raw]), 0, MOE_ROWS)
    item_rows = jnp.where(used, valid, 0).astype(I32)
    item_blk = jnp.where(used, jnp.arange(n_items, dtype=I32), jnp.maximum(n_used - 1, 0)).astype(I32)
    return tok_of_slot, slot_of_assign, item_e, item_rows, item_blk


def _moe_kernel(ie_ref, ir_ref, ib_ref, x_ref, wg_ref, wu_ref, wd_ref, o_ref, wg_s, wu_s, wd_s, x_s):
    it = pl.program_id(0)
    f = pl.program_id(1)
    nrows = ir_ref[it]
    half = x_ref.shape[1]

    @pl.when(f == 0)
    def _():
        o_ref[...] = jnp.zeros_like(o_ref)
        pk = x_ref[...]
        x_s[:, :half] = pltpu.bitcast(pk & jnp.int32(-65536), F32).astype(BF16)
        x_s[:, half:] = pltpu.bitcast(lax.shift_left(pk, 16), F32).astype(BF16)

    @pl.when(nrows > 0)
    def _():
        wg_s[...] = wg_ref[0, 0].astype(BF16)
        wu_s[...] = wu_ref[0, 0].astype(BF16)
        wd_s[...] = wd_ref[0, 0].astype(BF16)
        nsb = (nrows + MOE_SUB - 1) // MOE_SUB

        def sub_block(sb):
            r0 = pl.multiple_of(sb * MOE_SUB, MOE_SUB)
            x = x_s[pl.ds(r0, MOE_SUB), :]
            g = jnp.dot(x, wg_s[...], preferred_element_type=F32)
            u = jnp.dot(x, wu_s[...], preferred_element_type=F32)
            hmid = ((g * jax.nn.sigmoid(g)) * u).astype(BF16)
            o_ref[pl.ds(r0, MOE_SUB), :] += jnp.dot(hmid, wd_s[...], preferred_element_type=F32)

        def pair_body(pi, carry):
            sub_block(2 * pi)
            sub_block(2 * pi + 1)
            return carry

        lax.fori_loop(0, nsb // 2, pair_body, 0)

        @pl.when(nsb % 2 == 1)
        def _():
            sub_block(nsb - 1)


def _moe_experts(x_sorted, item_e, item_rows, item_blk, w_gate, w_up, w_down, layer):
    p, half = x_sorted.shape
    d = 2 * half
    n_items = p // MOE_ROWS
    dff = w_gate.shape[3]
    nf = dff // MOE_FT
    grid_spec = pltpu.PrefetchScalarGridSpec(
        num_scalar_prefetch=3,
        grid=(n_items, nf),
        in_specs=[
            pl.BlockSpec((MOE_ROWS, half), lambda i, f, ie, ir, ib: (ib[i], 0)),
            pl.BlockSpec((1, 1, d, MOE_FT),
                         lambda i, f, ie, ir, ib: (layer, ie[i], 0, jnp.where(ir[i] > 0, f, nf - 1))),
            pl.BlockSpec((1, 1, d, MOE_FT),
                         lambda i, f, ie, ir, ib: (layer, ie[i], 0, jnp.where(ir[i] > 0, f, nf - 1))),
            pl.BlockSpec((1, 1, MOE_FT, d),
                         lambda i, f, ie, ir, ib: (layer, ie[i], jnp.where(ir[i] > 0, f, nf - 1), 0)),
        ],
        out_specs=pl.BlockSpec((MOE_ROWS, d), lambda i, f, ie, ir, ib: (i, 0)),
        scratch_shapes=[pltpu.VMEM((d, MOE_FT), BF16), pltpu.VMEM((d, MOE_FT), BF16),
                        pltpu.VMEM((MOE_FT, d), BF16), pltpu.VMEM((MOE_ROWS, d), BF16)],
    )
    return pl.pallas_call(
        _moe_kernel,
        out_shape=jax.ShapeDtypeStruct((p, d), F32),
        grid_spec=grid_spec,
        compiler_params=_cparams(("arbitrary", "arbitrary")),
        name="moe_experts",
    )(item_e, item_rows, item_blk, x_sorted, w_gate, w_up, w_down)


def _moe(hn2, logits, router_b, w_gate, w_up, w_down, layer):
    experts, gates = _route(logits, router_b)
    tok_of_slot, slot_of_assign, item_e, item_rows, item_blk = _slot_tables(experts)
    x_sorted = hn2.at[tok_of_slot].get(mode='promise_in_bounds')
    y_slot = _moe_experts(x_sorted, item_e, item_rows, item_blk, w_gate, w_up, w_down, layer)
    y0 = y_slot.at[slot_of_assign[:, 0]].get(mode='promise_in_bounds')
    y1 = y_slot.at[slot_of_assign[:, 1]].get(mode='promise_in_bounds')
    n = hn2.shape[0]
    gates_rep = jnp.concatenate([jnp.broadcast_to(gates[k][:, None], (n, 128)) for k in range(TOP_K)], axis=1)
    return y0, y1, gates_rep


def _moe_combine(h1, y0, y1, gates, mod):
    reps = h1.shape[1] // 128
    g0 = pltpu.repeat(gates[:, :128], reps, axis=1)
    g1 = pltpu.repeat(gates[:, 128:], reps, axis=1)
    return h1 + mod[5:6] * (g0 * y0 + g1 * y1)


def _bproj_kernel(h1_ref, y0_ref, y1_ref, gt_ref, modp_ref, mod_ref, n1g_ref, w_ref, kvg_ref, kig_ref,
                  wuk_ref, h_out, qa_out, ckv_out, qi_out, ki_out, wi_out):
    tm = h1_ref.shape[0]
    h = _moe_combine(h1_ref[...], y0_ref[...], y1_ref[...], gt_ref[...], modp_ref[0])
    h_out[...] = h
    mod = mod_ref[0]
    hn = _mod_rmsnorm(h, n1g_ref[...], mod[1:2], mod[0:1]).astype(BF16)
    proj = jnp.dot(hn, w_ref[...], preferred_element_type=F32)
    o1 = B_HEADS * B_HEAD_DIM
    o2 = o1 + B_KV_LATENT
    o3 = o2 + B_IDX_HEADS * B_IDX_DIM
    ckv = proj[:, o1:o2]
    ckv = ckv * lax.rsqrt(jnp.mean(ckv * ckv, axis=-1, keepdims=True) + EPS) * kvg_ref[...]
    ckv_out[...] = ckv.astype(BF16)
    tail = proj[:, o3:o3 + 128]
    ki = tail[:, :B_IDX_DIM]
    ki = ki * lax.rsqrt(jnp.mean(ki * ki, axis=-1, keepdims=True) + EPS) * kig_ref[...]
    ki_out[...] = ki.astype(BF16)
    wi = tail[:, B_IDX_DIM:B_IDX_DIM + B_IDX_HEADS] * (B_IDX_HEADS ** -0.5 * B_IDX_DIM ** -0.5)
    scale = B_HEAD_DIM ** -0.5
    for blk in range(tm // QBLK):
        r0, r1 = blk * QBLK, (blk + 1) * QBLK
        for hh in range(B_HEADS):
            qh = proj[r0:r1, hh * B_HEAD_DIM:(hh + 1) * B_HEAD_DIM].astype(BF16)
            qa = jnp.dot(qh, wuk_ref[hh], preferred_element_type=F32) * scale
            qa_out[blk, hh * QBLK:(hh + 1) * QBLK, :] = qa.astype(BF16)
        for hh in range(B_IDX_HEADS):
            qi_out[blk, hh * QBLK:(hh + 1) * QBLK, :] = proj[r0:r1, o2 + hh * B_IDX_DIM:
                                                            o2 + (hh + 1) * B_IDX_DIM].astype(BF16)
            wi_out[blk, hh * QBLK:(hh + 1) * QBLK, :] = jnp.broadcast_to(wi[r0:r1, hh:hh + 1], (QBLK, 128))


def _bproj(h1, y0, y1, gates, mod_prev, mod_l, n1g, b_w_in, kv_g, w_uk, kidx_g, seq):
    n, d = h1.shape
    tm = ROW_TILE
    per_b = seq // tm
    nq = n // QBLK
    qpb = tm // QBLK
    o1 = B_HEADS * B_HEAD_DIM
    o2 = o1 + B_KV_LATENT
    o3 = o2 + B_IDX_HEADS * B_IDX_DIM
    wcat = jnp.zeros((d, o3 + 128), F32).at[:, :b_w_in.shape[1]].set(b_w_in).astype(BF16)
    wuk_t = jnp.transpose(w_uk, (1, 2, 0)).astype(BF16)
    row = lambda i: (i, 0)
    modm = lambda i: (i // per_b, 0, 0)
    blk3 = lambda i: (i, 0, 0)
    return pl.pallas_call(
        _bproj_kernel,
        out_shape=(
            jax.ShapeDtypeStruct((n, d), F32),
            jax.ShapeDtypeStruct((nq, B_HEADS * QBLK, B_KV_LATENT), BF16),
            jax.ShapeDtypeStruct((n, B_KV_LATENT), BF16),
            jax.ShapeDtypeStruct((nq, B_IDX_HEADS * QBLK, B_IDX_DIM), BF16),
            jax.ShapeDtypeStruct((n, B_IDX_DIM), BF16),
            jax.ShapeDtypeStruct((nq, B_IDX_HEADS * QBLK, 128), F32),
        ),
        grid=(n // tm,),
        in_specs=[
            pl.BlockSpec((tm, d), row), pl.BlockSpec((tm, d), row), pl.BlockSpec((tm, d), row),
            pl.BlockSpec((tm, TOP_K * 128), row),
            pl.BlockSpec((1, 6, d), modm), pl.BlockSpec((1, 6, d), modm),
            _const_spec((1, d)),
            _const_spec((d, o3 + 128)),
            _const_spec((1, B_KV_LATENT)),
            _const_spec((1, B_IDX_DIM)),
            _const_spec((B_HEADS, B_HEAD_DIM, B_KV_LATENT)),
        ],
        out_specs=(
            pl.BlockSpec((tm, d), row),
            pl.BlockSpec((qpb, B_HEADS * QBLK, B_KV_LATENT), blk3),
            pl.BlockSpec((tm, B_KV_LATENT), row),
            pl.BlockSpec((qpb, B_IDX_HEADS * QBLK, B_IDX_DIM), blk3),
            pl.BlockSpec((tm, B_IDX_DIM), row),
            pl.BlockSpec((qpb, B_IDX_HEADS * QBLK, 128), blk3),
        ),
        compiler_params=_cparams(("arbitrary",)),
        name="dsa_proj",
    )(h1, y0, y1, gates, mod_prev, mod_l, n1g.reshape(1, d), wcat, kv_g.reshape(1, -1),
      kidx_g.reshape(1, -1), wuk_t)


def _t5_bucket(dist):
    n = jnp.maximum(dist, 0)
    exact = REL_BUCKETS // 2
    nf = jnp.maximum(n, 1).astype(F32)
    large = exact + (jnp.log(nf / exact) / math.log(REL_MAX_DIST / exact)
                     * (REL_BUCKETS - exact)).astype(I32)
    large = jnp.minimum(large, REL_BUCKETS - 1)
    return jnp.where(n < exact, n, large)


def _bias_tables(rel_bias):
    assert REL_MAX_DIST <= 128
    t = jnp.arange(128, dtype=I32)[:, None]
    s = jnp.arange(128, dtype=I32)[None, :]
    far = rel_bias[REL_BUCKETS - 1]
    diag = rel_bias[_t5_bucket(t - s)] - far
    prev = rel_bias[_t5_bucket(t - s + 128)] - far
    return jnp.stack([jnp.transpose(diag, (2, 0, 1)), jnp.transpose(prev, (2, 0, 1))])


def _attn_kernel(qa_ref, qi_ref, wi_ref, ckv_ref, ki_ref, bt_ref, o_ref, kbuf, m_scr, l_scr, a_scr, acc_scr,
                 s_scr, p_scr, madd_scr, kbuft, tau_scr):
    i = pl.program_id(1)
    nt = i + 1
    t_row = i * QBLK + lax.broadcasted_iota(I32, (QBLK, KTILE), 0)
    lane = lax.broadcasted_iota(I32, (QBLK, KTILE), 1)

    qi = qi_ref[0]

    def score_body(j, carry):
        k0 = pl.multiple_of(j * KTILE, KTILE)
        kt = ki_ref[0, pl.ds(k0, KTILE), :]
        p = lax.dot_general(qi, kt, (((1,), (1,)), ((), ())), preferred_element_type=F32)
        p = jnp.maximum(p, 0.0) * pltpu.repeat(wi_ref[0], KTILE // 128, axis=1)
        sc = p[0:QBLK]
        for hh in range(1, B_IDX_HEADS):
            sc = sc + p[hh * QBLK:(hh + 1) * QBLK]
        bits = pltpu.bitcast(sc + 0.0, I32)
        key = jnp.where(bits < 0, bits ^ jnp.int32(0x7FFFFFFF), bits)
        key = jnp.where(k0 + lane <= t_row, key, jnp.int32(INT_MIN))
        kbuf[:, pl.ds(k0, KTILE)] = key
        kbuft[pl.ds(k0, KTILE), :] = key.T
        return carry

    lax.fori_loop(0, nt, score_body, 0)

    def fold_lanes(x):
        out = x[:, 0:128]
        for k in range(1, KTILE // 128):
            out = out + x[:, k * 128:(k + 1) * 128]
        return out

    def count_ge(cand):
        def body(j, acc):
            k0 = pl.multiple_of(j * KTILE, KTILE)
            return acc + fold_lanes(jnp.where(kbuf[:, pl.ds(k0, KTILE)] >= cand, 1, 0))
        acc = lax.fori_loop(0, nt, body, jnp.zeros((QBLK, 128), I32))
        return jnp.sum(acc, axis=-1, keepdims=True)

    def count_ge_t(cand_row):
        def body(j, acc):
            k0 = pl.multiple_of(j * KTILE, KTILE)
            hit = jnp.where(kbuft[pl.ds(k0, KTILE), :] >= cand_row, 1, 0)
            return acc + jnp.sum(hit.reshape(KTILE // 8, 8, QBLK), axis=0)
        acc = lax.fori_loop(0, nt, body, jnp.zeros((8, QBLK), I32))
        return jnp.sum(acc, axis=0, keepdims=True)

    def bit_step(b, lo, n_ge):
        cand = lo + lax.shift_left(jnp.int32(1), 31 - b)
        cnt = count_ge_t(cand)
        take = cnt >= B_TOPK_MAX
        return jnp.where(take, cand, lo), jnp.where(take, cnt, n_ge)

    def bit_cond(c):
        return (c[0] < 32) & (c[3] == 0)

    def bit_body(c):
        b, lo, n_ge, _ = c
        lo, n_ge = bit_step(b, lo, n_ge)
        lo, n_ge = bit_step(b + 1, lo, n_ge)
        done = (jnp.max(n_ge) == B_TOPK_MAX).astype(I32)
        return b + 2, lo, n_ge, done

    _, tau_row, n_ge_row, _ = lax.while_loop(
        bit_cond, bit_body,
        (jnp.int32(0), jnp.full((1, QBLK), INT_MIN, I32), jnp.full((1, QBLK), nt * KTILE, I32),
         (i < 1).astype(I32)))
    tau_row = jnp.maximum(tau_row, jnp.int32(INT_MIN + 1))
    excess = (i >= 1) & (jnp.max(n_ge_row) > B_TOPK_MAX)

    def to_rows(row):
        return jnp.broadcast_to(row, (QBLK, QBLK)).T

    tau_scr[...] = to_rows(tau_row)

    @pl.when(excess)
    def _():
        tau = tau_scr[:, 0:1]
        n_ge = to_rows(n_ge_row)[:, 0:1]
        n_gt = count_ge(tau + 1)
        need = B_TOPK_MAX - n_gt

        def count_eq_before(pos):
            def body(j, acc):
                k0 = pl.multiple_of(j * KTILE, KTILE)
                hit = (kbuf[:, pl.ds(k0, KTILE)] == tau) & (k0 + lane < pos)
                return acc + fold_lanes(jnp.where(hit, 1, 0))
            acc = lax.fori_loop(0, nt, body, jnp.zeros((QBLK, 128), I32))
            return jnp.sum(acc, axis=-1, keepdims=True)

        def pos_body(b, pos):
            cand = pos + lax.shift_left(jnp.int32(1), 12 - b)
            return jnp.where(count_eq_before(cand) < need, cand, pos)

        pos = lax.fori_loop(0, 13, pos_body, jnp.zeros((QBLK, 1), I32))

        def drop_body(j, carry):
            k0 = pl.multiple_of(j * KTILE, KTILE)
            kk = kbuf[:, pl.ds(k0, KTILE)]
            drop = (kk == tau) & (k0 + lane > pos) & (n_ge > B_TOPK_MAX)
            kbuf[:, pl.ds(k0, KTILE)] = jnp.where(drop, jnp.int32(INT_MIN), kk)
            return carry

        lax.fori_loop(0, nt, drop_body, 0)

    m_scr[...] = jnp.full_like(m_scr, NEG_BIG)
    l_scr[...] = jnp.zeros_like(l_scr)
    acc_scr[...] = jnp.zeros_like(acc_scr)

    def attend(k0, width, tile_off):
        kv = ckv_ref[0, pl.ds(k0, width), :]
        s_scr[:, :width] = lax.dot_general(qa_ref[0], kv, (((1,), (1,)), ((), ())),
                                           preferred_element_type=F32)
        tau_t = tau_scr[...] if width == QBLK else pltpu.repeat(tau_scr[...], width // QBLK, axis=1)
        madd_scr[:, :width] = jnp.where(kbuf[:, pl.ds(k0, width)] >= tau_t, 0.0, NEG_BIG)
        for r0 in range(0, B_HEADS * QBLK, SM_ROWS):
            r1 = r0 + SM_ROWS
            hh, q0 = r0 // QBLK, r0 % QBLK
            parts = []
            for kc in range(width // 128):
                sp = s_scr[r0:r1, kc * 128:(kc + 1) * 128] + madd_scr[q0:q0 + SM_ROWS, kc * 128:(kc + 1) * 128]
                if tile_off is not None:
                    rel = (QBLK // 128) * tile_off + q0 // 128 - kc
                    if rel in (0, 1):
                        sp = sp + bt_ref[rel, hh, q0 % 128:q0 % 128 + SM_ROWS, :]
                parts.append(sp)
            smax = parts[0]
            for sp in parts[1:]:
                smax = jnp.maximum(smax, sp)
            m_old = m_scr[r0:r1, :]
            m_new = jnp.maximum(m_old, jnp.max(smax, axis=-1, keepdims=True))
            alpha = jnp.exp(m_old - m_new)
            psum = None
            for kc, sp in enumerate(parts):
                p = jnp.exp(sp - m_new)
                p_scr[r0:r1, kc * 128:(kc + 1) * 128] = p.astype(BF16)
                psum = p if psum is None else psum + p
            l_scr[r0:r1, :] = alpha * l_scr[r0:r1, :] + jnp.sum(psum, axis=-1, keepdims=True)
            m_scr[r0:r1, :] = m_new
            a_scr[r0:r1, :] = alpha
        pv = jnp.dot(p_scr[:, :width], kv, preferred_element_type=F32)
        acc_scr[...] = pltpu.repeat(a_scr[...], B_KV_LATENT // 128, axis=1) * acc_scr[...] + pv

    n_far = jnp.maximum(i - 1, 0)
    per_far = FAR_TILE // KTILE

    def far_body(jf, carry):
        attend(pl.multiple_of(jf * FAR_TILE, FAR_TILE), FAR_TILE, None)
        return carry

    lax.fori_loop(0, n_far // per_far, far_body, 0)

    if per_far > 1:
        @pl.when(n_far % per_far == 1)
        def _():
            attend(pl.multiple_of((n_far - 1) * KTILE, KTILE), KTILE, None)

    @pl.when(i >= 1)
    def _():
        attend(pl.multiple_of((i - 1) * KTILE, KTILE), KTILE, 1)

    attend(pl.multiple_of(i * KTILE, KTILE), KTILE, 0)
    inv_l = 1.0 / l_scr[...]
    o_ref[0] = (acc_scr[...] * pltpu.repeat(inv_l, B_KV_LATENT // 128, axis=1)).astype(BF16)


def _attention(qa, qi, wi, ckv, ki, btab, bsz, seq):
    nqb = seq // QBLK
    gq = lambda b, i: (b * nqb + i, 0, 0)
    gb = lambda b, i: (b, 0, 0)
    return pl.pallas_call(
        _attn_kernel,
        out_shape=jax.ShapeDtypeStruct(qa.shape, BF16),
        grid=(bsz, nqb),
        in_specs=[
            pl.BlockSpec((1, B_HEADS * QBLK, B_KV_LATENT), gq),
            pl.BlockSpec((1, B_IDX_HEADS * QBLK, B_IDX_DIM), gq),
            pl.BlockSpec((1, B_IDX_HEADS * QBLK, 128), gq),
            pl.BlockSpec((1, seq, B_KV_LATENT), gb, pipeline_mode=pl.Buffered(1)),
            pl.BlockSpec((1, seq, B_IDX_DIM), gb, pipeline_mode=pl.Buffered(1)),
            pl.BlockSpec((2, B_HEADS, 128, 128), lambda b, i: (0, 0, 0, 0), pipeline_mode=pl.Buffered(1)),
        ],
        out_specs=pl.BlockSpec((1, B_HEADS * QBLK, B_KV_LATENT), gq),
        scratch_shapes=[
            pltpu.VMEM((QBLK, seq), I32),
            pltpu.VMEM((B_HEADS * QBLK, 128), F32),
            pltpu.VMEM((B_HEADS * QBLK, 128), F32),
            pltpu.VMEM((B_HEADS * QBLK, 128), F32),
            pltpu.VMEM((B_HEADS * QBLK, B_KV_LATENT), F32),
            pltpu.VMEM((B_HEADS * QBLK, FAR_TILE), F32),
            pltpu.VMEM((B_HEADS * QBLK, FAR_TILE), BF16),
            pltpu.VMEM((QBLK, FAR_TILE), F32),
            pltpu.VMEM((seq, QBLK), I32),
            pltpu.VMEM((QBLK, QBLK), I32),
        ],
        compiler_params=_cparams(("arbitrary", "arbitrary")),
        name="dsa_attention",
    )(qa, qi, wi, ckv.reshape(bsz, seq, -1), ki.reshape(bsz, seq, -1), btab)


def _bout_kernel(h_ref, ol_ref, mod_ref, wuv_ref, wout_ref, n2g_ref, rw_ref, h_out, hn_out, lg_out, o_scr):
    mod = mod_ref[0]
    for blk in range(ol_ref.shape[0]):
        for hh in range(B_HEADS):
            oh = jnp.dot(ol_ref[blk, hh * QBLK:(hh + 1) * QBLK, :], wuv_ref[hh], preferred_element_type=F32)
            o_scr[blk * QBLK:(blk + 1) * QBLK, hh * B_V_DIM:(hh + 1) * B_V_DIM] = oh.astype(BF16)
    y = jnp.dot(o_scr[...], wout_ref[...], preferred_element_type=F32)
    _residual_epilogue(h_ref[...], y, mod, n2g_ref[...], rw_ref[...], h_out, hn_out, lg_out)


def _bout(h, o_lat, mod_l, w_uv, w_out, n2g, router_w, seq):
    n, d = h.shape
    tm = ROW_TILE
    per_b = seq // tm
    qpb = tm // QBLK
    ne = router_w.shape[1]
    wuv_t = jnp.transpose(w_uv, (1, 0, 2)).astype(BF16)
    row = lambda i: (i, 0)
    return pl.pallas_call(
        _bout_kernel,
        out_shape=(jax.ShapeDtypeStruct((n, d), F32), jax.ShapeDtypeStruct((n, d // 2), I32),
                   jax.ShapeDtypeStruct((n, ne), F32)),
        grid=(n // tm,),
        in_specs=[
            pl.BlockSpec((tm, d), row),
            pl.BlockSpec((qpb, B_HEADS * QBLK, B_KV_LATENT), lambda i: (i, 0, 0)),
            pl.BlockSpec((1, 6, d), lambda i: (i // per_b, 0, 0)),
            _const_spec((B_HEADS, B_KV_LATENT, B_V_DIM)),
            _const_spec((B_HEADS * B_V_DIM, d)),
            _const_spec((1, d)),
            _const_spec((d, ne)),
        ],
        out_specs=(pl.BlockSpec((tm, d), row), pl.BlockSpec((tm, d // 2), row), pl.BlockSpec((tm, ne), row)),
        scratch_shapes=[pltpu.VMEM((tm, B_HEADS * B_V_DIM), BF16)],
        compiler_params=_cparams(("arbitrary",)),
        name="dsa_out",
    )(h, o_lat, mod_l, wuv_t, w_out.astype(BF16), n2g.reshape(1, d), router_w)


def _final_kernel(h1_ref, y0_ref, y1_ref, gt_ref, mod_ref, g_ref, o_ref):
    h = _moe_combine(h1_ref[...], y0_ref[...], y1_ref[...], gt_ref[...], mod_ref[0])
    o_ref[...] = h * lax.rsqrt(jnp.mean(h * h, axis=-1, keepdims=True) + EPS) * g_ref[...]


def _final(h1, y0, y1, gates, mod_l, final_g, seq):
    n, d = h1.shape
    tm = 512
    per_b = seq // tm
    row = lambda i: (i, 0)
    return pl.pallas_call(
        _final_kernel,
        out_shape=jax.ShapeDtypeStruct((n, d), F32),
        grid=(n // tm,),
        in_specs=[pl.BlockSpec((tm, d), row), pl.BlockSpec((tm, d), row), pl.BlockSpec((tm, d), row),
                  pl.BlockSpec((tm, TOP_K * 128), row), pl.BlockSpec((1, 6, d), lambda i: (i // per_b, 0, 0)),
                  _const_spec((1, d))],
        out_specs=pl.BlockSpec((tm, d), row),
        compiler_params=_cparams(("arbitrary",)),
        name="final_norm",
    )(h1, y0, y1, gates, mod_l, final_g.reshape(1, d))


def kernel(x, c, ada_w, ada_b, norm1_g, norm2_g, a_w_in, a_ln_g, a_ln_b, a_w_sp, a_b_sp, a_w_out, b_w_in,
           b_kv_norm_g, b_w_uk, b_w_uv, b_kidx_g, b_w_out, rel_bias, router_w, router_b, moe_w_gate,
           moe_w_up, moe_w_down, final_g):
    bsz, seq, d = x.shape
    n = bsz * seq
    mod = _adaln(c, ada_w, ada_b).reshape(ada_w.shape[0], bsz, 6, d)
    rw_pad = jnp.zeros((d, 128), F32).at[:, :N_EXPERTS].set(router_w)
    h = x.reshape(n, d)

    h1, hn2, logits = _gmlp_layer(h, mod[0], norm1_g[0], a_w_in[0], a_ln_g[0], a_ln_b[0], a_w_sp[0],
                                  a_b_sp[0], a_w_out[0], norm2_g[0], rw_pad, seq)
    y0, y1, gates = _moe(hn2, logits, router_b, moe_w_gate, moe_w_up, moe_w_down, 0)

    h, qa, ckv, qi, ki, wi = _bproj(h1, y0, y1, gates, mod[0], mod[1], norm1_g[1], b_w_in[0],
                                    b_kv_norm_g[0], b_w_uk[0], b_kidx_g[0], seq)
    o_lat = _attention(qa, qi, wi, ckv, ki, _bias_tables(rel_bias), bsz, seq)
    h1, hn2, logits = _bout(h, o_lat, mod[1], b_w_uv[0], b_w_out[0], norm2_g[1], rw_pad, seq)
    y0, y1, gates = _moe(hn2, logits, router_b, moe_w_gate, moe_w_up, moe_w_down, 1)

    out = _final(h1, y0, y1, gates, mod[1], final_g, seq)
    return out.reshape(bsz, seq, d)
```

```python
import functools
import math

import jax
import jax.numpy as jnp
from jax import lax
from jax.experimental import pallas as pl
from jax.experimental.pallas import tpu as pltpu

F32 = jnp.float32
BF16 = jnp.bfloat16
I32 = jnp.int32
HIGHEST = lax.Precision.HIGHEST

EPS = 1e-6
A_CHUNK = 128
A_GROUPS = 8
B_HEADS = 16
B_HEAD_DIM = 64
B_V_DIM = 64
B_KV_LATENT = 256
B_IDX_HEADS = 8
B_IDX_DIM = 64
B_TOPK_MAX = 256
QBLK = 256
KTILE = 256
FAR_TILE = 256
SM_ROWS = 128
REL_BUCKETS = 32
REL_MAX_DIST = 128
N_EXPERTS = 16
N_GROUPS = 4
EXPERTS_PER_GROUP = 4
TOP_K = 2
MOE_ROWS = 1280
MOE_SUB = 256
MOE_FT = 512

ROW_TILE = 256
GMLP_ROWS = 512
VMEM_LIMIT = 60 * 1024 * 1024

INT_MIN = -2 ** 31
NEG_BIG = -1e30


def _cparams(sem):
    return pltpu.CompilerParams(dimension_semantics=sem, vmem_limit_bytes=VMEM_LIMIT)


def _mod_rmsnorm(h, g, scale, shift):
    ms = jnp.mean(h * h, axis=-1, keepdims=True)
    return (h * lax.rsqrt(ms + EPS) * g) * (1.0 + scale) + shift


def _gelu_tanh(x):
    c = math.sqrt(2.0 / math.pi)
    return 0.5 * x * (1.0 + jnp.tanh(c * (x + 0.044715 * (x * x * x))))


def _adaln_kernel(c_ref, w_ref, b_ref, o_ref):
    c = c_ref[...]
    sc = c * jax.nn.sigmoid(c)
    o_ref[0] = jnp.dot(sc, w_ref[0], precision=HIGHEST, preferred_element_type=F32) + b_ref[0]


def _adaln(c, ada_w, ada_b):
    depth, d, e = ada_w.shape
    bsz = c.shape[0]
    bp = 8
    c_pad = jnp.zeros((bp, d), F32).at[:bsz].set(c)
    tn = 1024
    out = pl.pallas_call(
        _adaln_kernel,
        out_shape=jax.ShapeDtypeStruct((depth, bp, e), F32),
        grid=(depth, e // tn),
        in_specs=[
            pl.BlockSpec((bp, d), lambda l, j: (0, 0)),
            pl.BlockSpec((1, d, tn), lambda l, j: (l, 0, j)),
            pl.BlockSpec((1, 1, tn), lambda l, j: (l, 0, j)),
        ],
        out_specs=pl.BlockSpec((1, bp, tn), lambda l, j: (l, 0, j)),
        compiler_params=_cparams(("arbitrary", "arbitrary")),
        name="adaln",
    )(c_pad, ada_w, ada_b.reshape(depth, 1, e))
    return out[:, :bsz]


def _residual_epilogue(h, y, mod, n2g, rw, h_out, hn_out, lg_out):
    h1 = h + mod[2:3] * y
    h_out[...] = h1
    hn2 = _mod_rmsnorm(h1, n2g, mod[4:5], mod[3:4])
    bits = pltpu.bitcast(hn2.astype(BF16).astype(F32), I32)
    half = bits.shape[1] // 2
    hn_out[...] = bits[:, :half] | lax.shift_right_logical(bits[:, half:], 16)
    lg_out[...] = jnp.dot(hn2, rw, precision=HIGHEST, preferred_element_type=F32)


def _gmlp_kernel(h_ref, mod_ref, n1g_ref, win_ref, lng_ref, lnb_ref, wsp_ref, bsp_ref, wout_ref,
                 n2g_ref, rw_ref, h_out, hn_out, lg_out, u_scr, v_scr, s_scr):
    tm = h_ref.shape[0]
    inner = u_scr.shape[1]
    gw = inner // A_GROUPS
    tn = 512
    h = h_ref[...]
    mod = mod_ref[0]
    hn = _mod_rmsnorm(h, n1g_ref[...], mod[1:2], mod[0:1]).astype(BF16)
    for j in range(2 * inner // tn):
        z = _gelu_tanh(jnp.dot(hn, win_ref[:, j * tn:(j + 1) * tn], preferred_element_type=F32))
        if j < inner // tn:
            u_scr[:, j * tn:(j + 1) * tn] = z
        else:
            jj = j - inner // tn
            v_scr[:, jj * tn:(jj + 1) * tn] = z
    vsum = jnp.zeros((tm, 1), F32)
    for j in range(inner // tn):
        vsum = vsum + jnp.sum(v_scr[:, j * tn:(j + 1) * tn], axis=-1, keepdims=True)
    mu = vsum * (1.0 / inner)
    vsq = jnp.zeros((tm, 1), F32)
    for j in range(inner // tn):
        d = v_scr[:, j * tn:(j + 1) * tn] - mu
        vsq = vsq + jnp.sum(d * d, axis=-1, keepdims=True)
    rstd = lax.rsqrt(vsq * (1.0 / inner) + EPS)
    row = lax.broadcasted_iota(I32, (A_CHUNK, A_CHUNK), 0)
    col = lax.broadcasted_iota(I32, (A_CHUNK, A_CHUNK), 1)
    tril = row >= col
    for g in range(A_GROUPS):
        ws = jnp.where(tril, wsp_ref[g], 0.0).astype(BF16)
        bcol = bsp_ref[:, g:g + 1]
        lg = lng_ref[:, g * gw:(g + 1) * gw]
        lb = lnb_ref[:, g * gw:(g + 1) * gw]
        for c in range(tm // A_CHUNK):
            r0, r1 = c * A_CHUNK, (c + 1) * A_CHUNK
            vt = v_scr[r0:r1, g * gw:(g + 1) * gw]
            vn = ((vt - mu[r0:r1]) * rstd[r0:r1]) * lg + lb
            fv = jnp.dot(ws, vn.astype(BF16), preferred_element_type=F32) + bcol
            s_scr[r0:r1, g * gw:(g + 1) * gw] = (u_scr[r0:r1, g * gw:(g + 1) * gw] * fv).astype(BF16)
    y = jnp.dot(s_scr[...], wout_ref[...], preferred_element_type=F32)
    _residual_epilogue(h, y, mod, n2g_ref[...], rw_ref[...], h_out, hn_out, lg_out)


def _const_spec(shape):
    nd = len(shape)
    return pl.BlockSpec(shape, lambda i, _nd=nd: (0,) * _nd, pipeline_mode=pl.Buffered(1))


def _gmlp_layer(h, mod_l, n1g, w_in, ln_g, ln_b, w_sp, b_sp, w_out, n2g, router_w, seq):
    n, d = h.shape
    inner = w_out.shape[0]
    tm = GMLP_ROWS
    per_b = seq // tm
    ne = router_w.shape[1]
    return pl.pallas_call(
        _gmlp_kernel,
        out_shape=(jax.ShapeDtypeStruct((n, d), F32), jax.ShapeDtypeStruct((n, d // 2), I32),
                   jax.ShapeDtypeStruct((n, ne), F32)),
        grid=(n // tm,),
        in_specs=[
            pl.BlockSpec((tm, d), lambda i: (i, 0)),
            pl.BlockSpec((1, 6, d), lambda i: (i // per_b, 0, 0)),
            _const_spec((1, d)),
            _const_spec((d, 2 * inner)),
            _const_spec((1, inner)),
            _const_spec((1, inner)),
            _const_spec((A_GROUPS, A_CHUNK, A_CHUNK)),
            _const_spec((A_CHUNK, A_GROUPS)),
            _const_spec((inner, d)),
            _const_spec((1, d)),
            _const_spec((d, ne)),
        ],
        out_specs=(pl.BlockSpec((tm, d), lambda i: (i, 0)), pl.BlockSpec((tm, d // 2), lambda i: (i, 0)),
                   pl.BlockSpec((tm, ne), lambda i: (i, 0))),
        scratch_shapes=[pltpu.VMEM((tm, inner), F32), pltpu.VMEM((tm, inner), F32),
                        pltpu.VMEM((tm, inner), BF16)],
        compiler_params=_cparams(("arbitrary",)),
        name="gmlp_layer",
    )(h, mod_l, n1g.reshape(1, d), w_in.astype(BF16), ln_g.reshape(1, inner), ln_b.reshape(1, inner),
      w_sp, b_sp.T, w_out.astype(BF16), n2g.reshape(1, d), router_w)


def _first_max4(rows):
    m = jnp.maximum(jnp.maximum(rows[0], rows[1]), jnp.maximum(rows[2], rows[3]))
    idx = jnp.where(rows[0] == m, 0, jnp.where(rows[1] == m, 1, jnp.where(rows[2] == m, 2, 3)))
    return m, idx.astype(I32)


def _route_kernel(lt_ref, rb_ref, e_ref, g_ref):
    aff = jax.nn.sigmoid(lt_ref[...])
    sel = aff + rb_ref[...]
    neg = jnp.float32(-jnp.inf)
    g_score, g_i1, g_i2 = [], [], []
    for g in range(N_GROUPS):
        rows = [sel[4 * g + k:4 * g + k + 1, :] for k in range(EXPERTS_PER_GROUP)]
        m1, i1 = _first_max4(rows)
        rest = [jnp.where(i1 == k, neg, rows[k]) for k in range(EXPERTS_PER_GROUP)]
        m2, i2 = _first_max4(rest)
        g_score.append(m1 + m2)
        g_i1.append(i1)
        g_i2.append(i2)
    _, grp = _first_max4(g_score)
    l1 = jnp.zeros_like(grp)
    l2 = jnp.zeros_like(grp)
    for g in range(N_GROUPS):
        l1 = jnp.where(grp == g, g_i1[g], l1)
        l2 = jnp.where(grp == g, g_i2[g], l2)
    e1 = grp * EXPERTS_PER_GROUP + l1
    e2 = grp * EXPERTS_PER_GROUP + l2
    a1 = jnp.zeros_like(g_score[0])
    a2 = jnp.zeros_like(g_score[0])
    for e in range(N_EXPERTS):
        ae = aff[e:e + 1, :]
        a1 = jnp.where(e1 == e, ae, a1)
        a2 = jnp.where(e2 == e, ae, a2)
    tot = a1 + a2
    e_ref[0:1, :] = e1
    e_ref[1:2, :] = e2
    g_ref[0:1, :] = a1 / tot
    g_ref[1:2, :] = a2 / tot


def _route(logits, router_b):
    n, ne = logits.shape[0], N_EXPERTS
    tn = min(2048, n)
    return pl.pallas_call(
        _route_kernel,
        out_shape=(jax.ShapeDtypeStruct((TOP_K, n), I32), jax.ShapeDtypeStruct((TOP_K, n), F32)),
        grid=(n // tn,),
        in_specs=[pl.BlockSpec((ne, tn), lambda i: (0, i)), pl.BlockSpec((ne, 1), lambda i: (0, 0))],
        out_specs=(pl.BlockSpec((TOP_K, tn), lambda i: (0, i)), pl.BlockSpec((TOP_K, tn), lambda i: (0, i))),
        compiler_params=_cparams(("arbitrary",)),
        name="route",
    )(logits[:, :ne].T, router_b.reshape(ne, 1))


def _slot_kernel(e_ref, slot_ref, cnt_ref):
    n = e_ref.shape[1]
    nblk = n // 128
    eid = lax.broadcasted_iota(I32, (N_EXPERTS, 128), 0)
    r = lax.broadcasted_iota(I32, (128, 128), 0)
    c = lax.broadcasted_iota(I32, (128, 128), 1)
    upper = jnp.where(r <= c, 1.0, 0.0).astype(BF16)

    def hits(b):
        o = pl.multiple_of(b * 128, 128)
        t1 = e_ref[0:1, pl.ds(o, 128)] == eid
        t2 = e_ref[1:2, pl.ds(o, 128)] == eid
        return o, t1, t2, jnp.where(t1, 1.0, 0.0) + jnp.where(t2, 1.0, 0.0)

    def count_body(b, acc):
        return acc + hits(b)[3]

    cnt = jnp.sum(lax.fori_loop(0, nblk, count_body, jnp.zeros((N_EXPERTS, 128), F32)),
                  axis=-1, keepdims=True)
    cnt_ref[...] = cnt.astype(I32)
    padded = jnp.zeros_like(cnt)
    for k in range(pl.cdiv(n * TOP_K, MOE_ROWS)):
        padded = padded + jnp.where(cnt > k * MOE_ROWS, float(MOE_ROWS), 0.0)
    er = lax.broadcasted_iota(I32, (N_EXPERTS, N_EXPERTS), 0)
    ec = lax.broadcasted_iota(I32, (N_EXPERTS, N_EXPERTS), 1)
    padded_row = jnp.sum(jnp.where(er == ec, padded, 0.0), axis=0, keepdims=True)
    start = jnp.sum(jnp.where(ec < er, padded_row, 0.0), axis=-1, keepdims=True)

    def slot_body(b, carry):
        o, t1, t2, t = hits(b)
        incl = jnp.dot(t.astype(BF16), upper, preferred_element_type=F32)
        pos = carry + incl - t
        slot_ref[0:1, pl.ds(o, 128)] = jnp.sum(jnp.where(t1, pos, 0.0), axis=0, keepdims=True).astype(I32)
        slot_ref[1:2, pl.ds(o, 128)] = jnp.sum(jnp.where(t2, pos, 0.0), axis=0, keepdims=True).astype(I32)
        return carry + incl[:, 127:128]

    lax.fori_loop(0, nblk, slot_body, start)


def _slot_tables(experts):
    n = experts.shape[1]
    a = n * TOP_K
    slot, counts = pl.pallas_call(
        _slot_kernel,
        out_shape=(jax.ShapeDtypeStruct((TOP_K, n), I32), jax.ShapeDtypeStruct((N_EXPERTS, 1), I32)),
        compiler_params=pltpu.CompilerParams(vmem_limit_bytes=VMEM_LIMIT),
        name="slots",
    )(experts)
    counts = counts.reshape(N_EXPERTS)
    padded = ((counts + MOE_ROWS - 1) // MOE_ROWS) * MOE_ROWS
    pad_end = jnp.cumsum(padded)
    pad_start = pad_end - padded
    n_items = pl.cdiv(a, MOE_ROWS) + N_EXPERTS
    p = n_items * MOE_ROWS
    tok = jnp.broadcast_to(jnp.arange(n, dtype=I32)[None, :], (TOP_K, n))
    tok_of_slot = jnp.zeros((p,), I32).at[slot.reshape(a)].set(tok.reshape(a))
    slot_of_assign = slot.T
    item_start = jnp.arange(n_items, dtype=I32) * MOE_ROWS
    n_used = pad_end[-1] // MOE_ROWS
    item_e_raw = jnp.minimum(jnp.searchsorted(pad_end, item_start, side='right'), N_EXPERTS - 1).astype(I32)
    used = item_start < pad_end[-1]
    last_e = item_e_raw[jnp.maximum(n_used - 1, 0)]
    item_e = jnp.where(used, item_e_raw, last_e).astype(I32)
    valid = jnp.clip(counts[item_e_raw] - (item_start - pad_start[item_e_raw]), 0, MOE_ROWS)
    item_rows = jnp.where(used, valid, 0).astype(I32)
    item_blk = jnp.where(used, jnp.arange(n_items, dtype=I32), jnp.maximum(n_used - 1, 0)).astype(I32)
    return tok_of_slot, slot_of_assign, item_e, item_rows, item_blk


def _moe_kernel(ie_ref, ir_ref, ib_ref, x_ref, wg_ref, wu_ref, wd_ref, o_ref, wg_s, wu_s, wd_s, x_s):
    it = pl.program_id(0)
    f = pl.program_id(1)
    nrows = ir_ref[it]
    half = x_ref.shape[1]

    @pl.when(f == 0)
    def _():
        o_ref[...] = jnp.zeros_like(o_ref)
        pk = x_ref[...]
        x_s[:, :half] = pltpu.bitcast(pk & jnp.int32(-65536), F32).astype(BF16)
        x_s[:, half:] = pltpu.bitcast(lax.shift_left(pk, 16), F32).astype(BF16)

    @pl.when(nrows > 0)
    def _():
        wg_s[...] = wg_ref[0, 0].astype(BF16)
        wu_s[...] = wu_ref[0, 0].astype(BF16)
        wd_s[...] = wd_ref[0, 0].astype(BF16)
        nsb = (nrows + MOE_SUB - 1) // MOE_SUB

        def sub_block(sb):
            r0 = pl.multiple_of(sb * MOE_SUB, MOE_SUB)
            x = x_s[pl.ds(r0, MOE_SUB), :]
            g = jnp.dot(x, wg_s[...], preferred_element_type=F32)
            u = jnp.dot(x, wu_s[...], preferred_element_type=F32)
            hmid = ((g * jax.nn.sigmoid(g)) * u).astype(BF16)
            o_ref[pl.ds(r0, MOE_SUB), :] += jnp.dot(hmid, wd_s[...], preferred_element_type=F32)

        def pair_body(pi, carry):
            sub_block(2 * pi)
            sub_block(2 * pi + 1)
            return carry

        lax.fori_loop(0, nsb // 2, pair_body, 0)

        @pl.when(nsb % 2 == 1)
        def _():
            sub_block(nsb - 1)


def _moe_experts(x_sorted, item_e, item_rows, item_blk, w_gate, w_up, w_down, layer):
    p, half = x_sorted.shape
    d = 2 * half
    n_items = p // MOE_ROWS
    dff = w_gate.shape[3]
    nf = dff // MOE_FT
    grid_spec = pltpu.PrefetchScalarGridSpec(
        num_scalar_prefetch=3,
        grid=(n_items, nf),
        in_specs=[
            pl.BlockSpec((MOE_ROWS, half), lambda i, f, ie, ir, ib: (ib[i], 0)),
            pl.BlockSpec((1, 1, d, MOE_FT),
                         lambda i, f, ie, ir, ib: (layer, ie[i], 0, jnp.where(ir[i] > 0, f, nf - 1))),
            pl.BlockSpec((1, 1, d, MOE_FT),
                         lambda i, f, ie, ir, ib: (layer, ie[i], 0, jnp.where(ir[i] > 0, f, nf - 1))),
            pl.BlockSpec((1, 1, MOE_FT, d),
                         lambda i, f, ie, ir, ib: (layer, ie[i], jnp.where(ir[i] > 0, f, nf - 1), 0)),
        ],
        out_specs=pl.BlockSpec((MOE_ROWS, d), lambda i, f, ie, ir, ib: (i, 0)),
        scratch_shapes=[pltpu.VMEM((d, MOE_FT), BF16), pltpu.VMEM((d, MOE_FT), BF16),
                        pltpu.VMEM((MOE_FT, d), BF16), pltpu.VMEM((MOE_ROWS, d), BF16)],
    )
    return pl.pallas_call(
        _moe_kernel,
        out_shape=jax.ShapeDtypeStruct((p, d), F32),
        grid_spec=grid_spec,
        compiler_params=_cparams(("arbitrary", "arbitrary")),
        name="moe_experts",
    )(item_e, item_rows, item_blk, x_sorted, w_gate, w_up, w_down)


def _moe(hn2, logits, router_b, w_gate, w_up, w_down, layer):
    experts, gates = _route(logits, router_b)
    tok_of_slot, slot_of_assign, item_e, item_rows, item_blk = _slot_tables(experts)
    x_sorted = hn2.at[tok_of_slot].get(mode='promise_in_bounds')
    y_slot = _moe_experts(x_sorted, item_e, item_rows, item_blk, w_gate, w_up, w_down, layer)
    y0 = y_slot.at[slot_of_assign[:, 0]].get(mode='promise_in_bounds')
    y1 = y_slot.at[slot_of_assign[:, 1]].get(mode='promise_in_bounds')
    n = hn2.shape[0]
    gates_rep = jnp.concatenate([jnp.broadcast_to(gates[k][:, None], (n, 128)) for k in range(TOP_K)], axis=1)
    return y0, y1, gates_rep


def _moe_combine(h1, y0, y1, gates, mod):
    reps = h1.shape[1] // 128
    g0 = pltpu.repeat(gates[:, :128], reps, axis=1)
    g1 = pltpu.repeat(gates[:, 128:], reps, axis=1)
    return h1 + mod[5:6] * (g0 * y0 + g1 * y1)


def _bproj_kernel(h1_ref, y0_ref, y1_ref, gt_ref, modp_ref, mod_ref, n1g_ref, w_ref, kvg_ref, kig_ref,
                  wuk_ref, h_out, qa_out, ckv_out, qi_out, ki_out, wi_out):
    tm = h1_ref.shape[0]
    h = _moe_combine(h1_ref[...], y0_ref[...], y1_ref[...], gt_ref[...], modp_ref[0])
    h_out[...] = h
    mod = mod_ref[0]
    hn = _mod_rmsnorm(h, n1g_ref[...], mod[1:2], mod[0:1]).astype(BF16)
    proj = jnp.dot(hn, w_ref[...], preferred_element_type=F32)
    o1 = B_HEADS * B_HEAD_DIM
    o2 = o1 + B_KV_LATENT
    o3 = o2 + B_IDX_HEADS * B_IDX_DIM
    ckv = proj[:, o1:o2]
    ckv = ckv * lax.rsqrt(jnp.mean(ckv * ckv, axis=-1, keepdims=True) + EPS) * kvg_ref[...]
    ckv_out[...] = ckv.astype(BF16)
    tail = proj[:, o3:o3 + 128]
    ki = tail[:, :B_IDX_DIM]
    ki = ki * lax.rsqrt(jnp.mean(ki * ki, axis=-1, keepdims=True) + EPS) * kig_ref[...]
    ki_out[...] = ki.astype(BF16)
    wi = tail[:, B_IDX_DIM:B_IDX_DIM + B_IDX_HEADS] * (B_IDX_HEADS ** -0.5 * B_IDX_DIM ** -0.5)
    scale = B_HEAD_DIM ** -0.5
    for blk in range(tm // QBLK):
        r0, r1 = blk * QBLK, (blk + 1) * QBLK
        for hh in range(B_HEADS):
            qh = proj[r0:r1, hh * B_HEAD_DIM:(hh + 1) * B_HEAD_DIM].astype(BF16)
            qa = jnp.dot(qh, wuk_ref[hh], preferred_element_type=F32) * scale
            qa_out[blk, hh * QBLK:(hh + 1) * QBLK, :] = qa.astype(BF16)
        for hh in range(B_IDX_HEADS):
            qi_out[blk, hh * QBLK:(hh + 1) * QBLK, :] = proj[r0:r1, o2 + hh * B_IDX_DIM:
                                                            o2 + (hh + 1) * B_IDX_DIM].astype(BF16)
            wi_out[blk, hh * QBLK:(hh + 1) * QBLK, :] = jnp.broadcast_to(wi[r0:r1, hh:hh + 1], (QBLK, 128))


def _bproj(h1, y0, y1, gates, mod_prev, mod_l, n1g, b_w_in, kv_g, w_uk, kidx_g, seq):
    n, d = h1.shape
    tm = ROW_TILE
    per_b = seq // tm
    nq = n // QBLK
    qpb = tm // QBLK
    o1 = B_HEADS * B_HEAD_DIM
    o2 = o1 + B_KV_LATENT
    o3 = o2 + B_IDX_HEADS * B_IDX_DIM
    wcat = jnp.zeros((d, o3 + 128), F32).at[:, :b_w_in.shape[1]].set(b_w_in).astype(BF16)
    wuk_t = jnp.transpose(w_uk, (1, 2, 0)).astype(BF16)
    row = lambda i: (i, 0)
    modm = lambda i: (i // per_b, 0, 0)
    blk3 = lambda i: (i, 0, 0)
    return pl.pallas_call(
        _bproj_kernel,
        out_shape=(
            jax.ShapeDtypeStruct((n, d), F32),
            jax.ShapeDtypeStruct((nq, B_HEADS * QBLK, B_KV_LATENT), BF16),
            jax.ShapeDtypeStruct((n, B_KV_LATENT), BF16),
            jax.ShapeDtypeStruct((nq, B_IDX_HEADS * QBLK, B_IDX_DIM), BF16),
            jax.ShapeDtypeStruct((n, B_IDX_DIM), BF16),
            jax.ShapeDtypeStruct((nq, B_IDX_HEADS * QBLK, 128), F32),
        ),
        grid=(n // tm,),
        in_specs=[
            pl.BlockSpec((tm, d), row), pl.BlockSpec((tm, d), row), pl.BlockSpec((tm, d), row),
            pl.BlockSpec((tm, TOP_K * 128), row),
            pl.BlockSpec((1, 6, d), modm), pl.BlockSpec((1, 6, d), modm),
            _const_spec((1, d)),
            _const_spec((d, o3 + 128)),
            _const_spec((1, B_KV_LATENT)),
            _const_spec((1, B_IDX_DIM)),
            _const_spec((B_HEADS, B_HEAD_DIM, B_KV_LATENT)),
        ],
        out_specs=(
            pl.BlockSpec((tm, d), row),
            pl.BlockSpec((qpb, B_HEADS * QBLK, B_KV_LATENT), blk3),
            pl.BlockSpec((tm, B_KV_LATENT), row),
            pl.BlockSpec((qpb, B_IDX_HEADS * QBLK, B_IDX_DIM), blk3),
            pl.BlockSpec((tm, B_IDX_DIM), row),
            pl.BlockSpec((qpb, B_IDX_HEADS * QBLK, 128), blk3),
        ),
        compiler_params=_cparams(("arbitrary",)),
        name="dsa_proj",
    )(h1, y0, y1, gates, mod_prev, mod_l, n1g.reshape(1, d), wcat, kv_g.reshape(1, -1),
      kidx_g.reshape(1, -1), wuk_t)


def _t5_bucket(dist):
    n = jnp.maximum(dist, 0)
    exact = REL_BUCKETS // 2
    nf = jnp.maximum(n, 1).astype(F32)
    large = exact + (jnp.log(nf / exact) / math.log(REL_MAX_DIST / exact)
                     * (REL_BUCKETS - exact)).astype(I32)
    large = jnp.minimum(large, REL_BUCKETS - 1)
    return jnp.where(n < exact, n, large)


def _bias_tables(rel_bias):
    assert REL_MAX_DIST <= 128
    t = jnp.arange(128, dtype=I32)[:, None]
    s = jnp.arange(128, dtype=I32)[None, :]
    far = rel_bias[REL_BUCKETS - 1]
    diag = rel_bias[_t5_bucket(t - s)] - far
    prev = rel_bias[_t5_bucket(t - s + 128)] - far
    return jnp.stack([jnp.transpose(diag, (2, 0, 1)), jnp.transpose(prev, (2, 0, 1))])


def _attn_kernel(qa_ref, qi_ref, wi_ref, ckv_ref, ki_ref, bt_ref, o_ref, kbuf, m_scr, l_scr, a_scr, acc_scr,
                 s_scr, p_scr, madd_scr, kbuft, tau_scr):
    i = pl.program_id(1)
    nt = i + 1
    t_row = i * QBLK + lax.broadcasted_iota(I32, (QBLK, KTILE), 0)
    lane = lax.broadcasted_iota(I32, (QBLK, KTILE), 1)

    qi = qi_ref[0]

    def score_body(j, carry):
        k0 = pl.multiple_of(j * KTILE, KTILE)
        kt = ki_ref[0, pl.ds(k0, KTILE), :]
        p = lax.dot_general(qi, kt, (((1,), (1,)), ((), ())), preferred_element_type=F32)
        p = jnp.maximum(p, 0.0) * pltpu.repeat(wi_ref[0], KTILE // 128, axis=1)
        sc = p[0:QBLK]
        for hh in range(1, B_IDX_HEADS):
            sc = sc + p[hh * QBLK:(hh + 1) * QBLK]
        bits = pltpu.bitcast(sc + 0.0, I32)
        key = jnp.where(bits < 0, bits ^ jnp.int32(0x7FFFFFFF), bits)
        key = jnp.where(k0 + lane <= t_row, key, jnp.int32(INT_MIN))
        kbuf[:, pl.ds(k0, KTILE)] = key
        kbuft[pl.ds(k0, KTILE), :] = key.T
        return carry

    lax.fori_loop(0, nt, score_body, 0)

    def fold_lanes(x):
        out = x[:, 0:128]
        for k in range(1, KTILE // 128):
            out = out + x[:, k * 128:(k + 1) * 128]
        return out

    def count_ge(cand):
        def body(j, acc):
            k0 = pl.multiple_of(j * KTILE, KTILE)
            return acc + fold_lanes(jnp.where(kbuf[:, pl.ds(k0, KTILE)] >= cand, 1, 0))
        acc = lax.fori_loop(0, nt, body, jnp.zeros((QBLK, 128), I32))
        return jnp.sum(acc, axis=-1, keepdims=True)

    def count_ge_t(cand_row):
        def body(j, acc):
            k0 = pl.multiple_of(j * KTILE, KTILE)
            hit = jnp.where(kbuft[pl.ds(k0, KTILE), :] >= cand_row, 1, 0)
            return acc + jnp.sum(hit.reshape(KTILE // 8, 8, QBLK), axis=0)
        acc = lax.fori_loop(0, nt, body, jnp.zeros((8, QBLK), I32))
        return jnp.sum(acc, axis=0, keepdims=True)

    def bit_step(b, lo, n_ge):
        cand = lo + lax.shift_left(jnp.int32(1), 31 - b)
        cnt = count_ge_t(cand)
        take = cnt >= B_TOPK_MAX
        return jnp.where(take, cand, lo), jnp.where(take, cnt, n_ge)

    def bit_cond(c):
        return (c[0] < 32) & (c[3] == 0)

    def bit_body(c):
        b, lo, n_ge, _ = c
        lo, n_ge = bit_step(b, lo, n_ge)
        lo, n_ge = bit_step(b + 1, lo, n_ge)
        done = (jnp.max(n_ge) == B_TOPK_MAX).astype(I32)
        return b + 2, lo, n_ge, done

    _, tau_row, n_ge_row, _ = lax.while_loop(
        bit_cond, bit_body,
        (jnp.int32(0), jnp.full((1, QBLK), INT_MIN, I32), jnp.full((1, QBLK), nt * KTILE, I32),
         (i < 1).astype(I32)))
    tau_row = jnp.maximum(tau_row, jnp.int32(INT_MIN + 1))
    excess = (i >= 1) & (jnp.max(n_ge_row) > B_TOPK_MAX)

    def to_rows(row):
        return jnp.broadcast_to(row, (QBLK, QBLK)).T

    tau_scr[...] = to_rows(tau_row)

    @pl.when(excess)
    def _():
        tau = tau_scr[:, 0:1]
        n_ge = to_rows(n_ge_row)[:, 0:1]
        n_gt = count_ge(tau + 1)
        need = B_TOPK_MAX - n_gt

        def count_eq_before(pos):
            def body(j, acc):
                k0 = pl.multiple_of(j * KTILE, KTILE)
                hit = (kbuf[:, pl.ds(k0, KTILE)] == tau) & (k0 + lane < pos)
                return acc + fold_lanes(jnp.where(hit, 1, 0))
            acc = lax.fori_loop(0, nt, body, jnp.zeros((QBLK, 128), I32))
            return jnp.sum(acc, axis=-1, keepdims=True)

        def pos_body(b, pos):
            cand = pos + lax.shift_left(jnp.int32(1), 12 - b)
            return jnp.where(count_eq_before(cand) < need, cand, pos)

        pos = lax.fori_loop(0, 13, pos_body, jnp.zeros((QBLK, 1), I32))

        def drop_body(j, carry):
            k0 = pl.multiple_of(j * KTILE, KTILE)
            kk = kbuf[:, pl.ds(k0, KTILE)]
            drop = (kk == tau) & (k0 + lane > pos) & (n_ge > B_TOPK_MAX)
            kbuf[:, pl.ds(k0, KTILE)] = jnp.where(drop, jnp.int32(INT_MIN), kk)
            return carry

        lax.fori_loop(0, nt, drop_body, 0)

    m_scr[...] = jnp.full_like(m_scr, NEG_BIG)
    l_scr[...] = jnp.zeros_like(l_scr)
    acc_scr[...] = jnp.zeros_like(acc_scr)

    def attend(k0, width, tile_off):
        kv = ckv_ref[0, pl.ds(k0, width), :]
        s_scr[:, :width] = lax.dot_general(qa_ref[0], kv, (((1,), (1,)), ((), ())),
                                           preferred_element_type=F32)
        tau_t = tau_scr[...] if width == QBLK else pltpu.repeat(tau_scr[...], width // QBLK, axis=1)
        madd_scr[:, :width] = jnp.where(kbuf[:, pl.ds(k0, width)] >= tau_t, 0.0, NEG_BIG)
        for r0 in range(0, B_HEADS * QBLK, SM_ROWS):
            r1 = r0 + SM_ROWS
            hh, q0 = r0 // QBLK, r0 % QBLK
            parts = []
            for kc in range(width // 128):
                sp = s_scr[r0:r1, kc * 128:(kc + 1) * 128] + madd_scr[q0:q0 + SM_ROWS, kc * 128:(kc + 1) * 128]
                if tile_off is not None:
                    rel = (QBLK // 128) * tile_off + q0 // 128 - kc
                    if rel in (0, 1):
                        sp = sp + bt_ref[rel, hh, q0 % 128:q0 % 128 + SM_ROWS, :]
                parts.append(sp)
            smax = parts[0]
            for sp in parts[1:]:
                smax = jnp.maximum(smax, sp)
            m_old = m_scr[r0:r1, :]
            m_new = jnp.maximum(m_old, jnp.max(smax, axis=-1, keepdims=True))
            alpha = jnp.exp(m_old - m_new)
            psum = None
            for kc, sp in enumerate(parts):
                p = jnp.exp(sp - m_new)
                p_scr[r0:r1, kc * 128:(kc + 1) * 128] = p.astype(BF16)
                psum = p if psum is None else psum + p
            l_scr[r0:r1, :] = alpha * l_scr[r0:r1, :] + jnp.sum(psum, axis=-1, keepdims=True)
            m_scr[r0:r1, :] = m_new
            a_scr[r0:r1, :] = alpha
        pv = jnp.dot(p_scr[:, :width], kv, preferred_element_type=F32)
        acc_scr[...] = pltpu.repeat(a_scr[...], B_KV_LATENT // 128, axis=1) * acc_scr[...] + pv

    n_far = jnp.maximum(i - 1, 0)
    per_far = FAR_TILE // KTILE

    def far_body(jf, carry):
        attend(pl.multiple_of(jf * FAR_TILE, FAR_TILE), FAR_TILE, None)
        return carry

    lax.fori_loop(0, n_far // per_far, far_body, 0)

    if per_far > 1:
        @pl.when(n_far % per_far == 1)
        def _():
            attend(pl.multiple_of((n_far - 1) * KTILE, KTILE), KTILE, None)

    @pl.when(i >= 1)
    def _():
        attend(pl.multiple_of((i - 1) * KTILE, KTILE), KTILE, 1)

    attend(pl.multiple_of(i * KTILE, KTILE), KTILE, 0)
    inv_l = 1.0 / l_scr[...]
    o_ref[0] = (acc_scr[...] * pltpu.repeat(inv_l, B_KV_LATENT // 128, axis=1)).astype(BF16)


def _attention(qa, qi, wi, ckv, ki, btab, bsz, seq):
    nqb = seq // QBLK
    gq = lambda b, i: (b * nqb + i, 0, 0)
    gb = lambda b, i: (b, 0, 0)
    return pl.pallas_call(
        _attn_kernel,
        out_shape=jax.ShapeDtypeStruct(qa.shape, BF16),
        grid=(bsz, nqb),
        in_specs=[
            pl.BlockSpec((1, B_HEADS * QBLK, B_KV_LATENT), gq),
            pl.BlockSpec((1, B_IDX_HEADS * QBLK, B_IDX_DIM), gq),
            pl.BlockSpec((1, B_IDX_HEADS * QBLK, 128), gq),
            pl.BlockSpec((1, seq, B_KV_LATENT), gb, pipeline_mode=pl.Buffered(1)),
            pl.BlockSpec((1, seq, B_IDX_DIM), gb, pipeline_mode=pl.Buffered(1)),
            pl.BlockSpec((2, B_HEADS, 128, 128), lambda b, i: (0, 0, 0, 0), pipeline_mode=pl.Buffered(1)),
        ],
        out_specs=pl.BlockSpec((1, B_HEADS * QBLK, B_KV_LATENT), gq),
        scratch_shapes=[
            pltpu.VMEM((QBLK, seq), I32),
            pltpu.VMEM((B_HEADS * QBLK, 128), F32),
            pltpu.VMEM((B_HEADS * QBLK, 128), F32),
            pltpu.VMEM((B_HEADS * QBLK, 128), F32),
            pltpu.VMEM((B_HEADS * QBLK, B_KV_LATENT), F32),
            pltpu.VMEM((B_HEADS * QBLK, FAR_TILE), F32),
            pltpu.VMEM((B_HEADS * QBLK, FAR_TILE), BF16),
            pltpu.VMEM((QBLK, FAR_TILE), F32),
            pltpu.VMEM((seq, QBLK), I32),
            pltpu.VMEM((QBLK, QBLK), I32),
        ],
        compiler_params=_cparams(("arbitrary", "arbitrary")),
        name="dsa_attention",
    )(qa, qi, wi, ckv.reshape(bsz, seq, -1), ki.reshape(bsz, seq, -1), btab)


def _bout_kernel(h_ref, ol_ref, mod_ref, wuv_ref, wout_ref, n2g_ref, rw_ref, h_out, hn_out, lg_out, o_scr):
    mod = mod_ref[0]
    for blk in range(ol_ref.shape[0]):
        for hh in range(B_HEADS):
            oh = jnp.dot(ol_ref[blk, hh * QBLK:(hh + 1) * QBLK, :], wuv_ref[hh], preferred_element_type=F32)
            o_scr[blk * QBLK:(blk + 1) * QBLK, hh * B_V_DIM:(hh + 1) * B_V_DIM] = oh.astype(BF16)
    y = jnp.dot(o_scr[...], wout_ref[...], preferred_element_type=F32)
    _residual_epilogue(h_ref[...], y, mod, n2g_ref[...], rw_ref[...], h_out, hn_out, lg_out)


def _bout(h, o_lat, mod_l, w_uv, w_out, n2g, router_w, seq):
    n, d = h.shape
    tm = ROW_TILE
    per_b = seq // tm
    qpb = tm // QBLK
    ne = router_w.shape[1]
    wuv_t = jnp.transpose(w_uv, (1, 0, 2)).astype(BF16)
    row = lambda i: (i, 0)
    return pl.pallas_call(
        _bout_kernel,
        out_shape=(jax.ShapeDtypeStruct((n, d), F32), jax.ShapeDtypeStruct((n, d // 2), I32),
                   jax.ShapeDtypeStruct((n, ne), F32)),
        grid=(n // tm,),
        in_specs=[
            pl.BlockSpec((tm, d), row),
            pl.BlockSpec((qpb, B_HEADS * QBLK, B_KV_LATENT), lambda i: (i, 0, 0)),
            pl.BlockSpec((1, 6, d), lambda i: (i // per_b, 0, 0)),
            _const_spec((B_HEADS, B_KV_LATENT, B_V_DIM)),
            _const_spec((B_HEADS * B_V_DIM, d)),
            _const_spec((1, d)),
            _const_spec((d, ne)),
        ],
        out_specs=(pl.BlockSpec((tm, d), row), pl.BlockSpec((tm, d // 2), row), pl.BlockSpec((tm, ne), row)),
        scratch_shapes=[pltpu.VMEM((tm, B_HEADS * B_V_DIM), BF16)],
        compiler_params=_cparams(("arbitrary",)),
        name="dsa_out",
    )(h, o_lat, mod_l, wuv_t, w_out.astype(BF16), n2g.reshape(1, d), router_w)


def _final_kernel(h1_ref, y0_ref, y1_ref, gt_ref, mod_ref, g_ref, o_ref):
    h = _moe_combine(h1_ref[...], y0_ref[...], y1_ref[...], gt_ref[...], mod_ref[0])
    o_ref[...] = h * lax.rsqrt(jnp.mean(h * h, axis=-1, keepdims=True) + EPS) * g_ref[...]


def _final(h1, y0, y1, gates, mod_l, final_g, seq):
    n, d = h1.shape
    tm = 512
    per_b = seq // tm
    row = lambda i: (i, 0)
    return pl.pallas_call(
        _final_kernel,
        out_shape=jax.ShapeDtypeStruct((n, d), F32),
        grid=(n // tm,),
        in_specs=[pl.BlockSpec((tm, d), row), pl.BlockSpec((tm, d), row), pl.BlockSpec((tm, d), row),
                  pl.BlockSpec((tm, TOP_K * 128), row), pl.BlockSpec((1, 6, d), lambda i: (i // per_b, 0, 0)),
                  _const_spec((1, d))],
        out_specs=pl.BlockSpec((tm, d), row),
        compiler_params=_cparams(("arbitrary",)),
        name="final_norm",
    )(h1, y0, y1, gates, mod_l, final_g.reshape(1, d))


def kernel(x, c, ada_w, ada_b, norm1_g, norm2_g, a_w_in, a_ln_g, a_ln_b, a_w_sp, a_b_sp, a_w_out, b_w_in,
           b_kv_norm_g, b_w_uk, b_w_uv, b_kidx_g, b_w_out, rel_bias, router_w, router_b, moe_w_gate,
           moe_w_up, moe_w_down, final_g):
    bsz, seq, d = x.shape
    n = bsz * seq
    mod = _adaln(c, ada_w, ada_b).reshape(ada_w.shape[0], bsz, 6, d)
    rw_pad = jnp.zeros((d, 128), F32).at[:, :N_EXPERTS].set(router_w)
    h = x.reshape(n, d)

    h1, hn2, logits = _gmlp_layer(h, mod[0], norm1_g[0], a_w_in[0], a_ln_g[0], a_ln_b[0], a_w_sp[0],
                                  a_b_sp[0], a_w_out[0], norm2_g[0], rw_pad, seq)
    y0, y1, gates = _moe(hn2, logits, router_b, moe_w_gate, moe_w_up, moe_w_down, 0)

    h, qa, ckv, qi, ki, wi = _bproj(h1, y0, y1, gates, mod[0], mod[1], norm1_g[1], b_w_in[0],
                                    b_kv_norm_g[0], b_w_uk[0], b_kidx_g[0], seq)
    o_lat = _attention(qa, qi, wi, ckv, ki, _bias_tables(rel_bias), bsz, seq)
    h1, hn2, logits = _bout(h, o_lat, mod[1], b_w_uv[0], b_w_out[0], norm2_g[1], rw_pad, seq)
    y0, y1, gates = _moe(hn2, logits, router_b, moe_w_gate, moe_w_up, moe_w_down, 1)

    out = _final(h1, y0, y1, gates, mod[1], final_g, seq)
    return out.reshape(bsz, seq, d)
```

```python
import functools
import math

import jax
import jax.numpy as jnp
from jax import lax
from jax.experimental import pallas as pl
from jax.experimental.pallas import tpu as pltpu

F32 = jnp.float32
BF16 = jnp.bfloat16
I32 = jnp.int32
HIGHEST = lax.Precision.HIGHEST

EPS = 1e-6
A_CHUNK = 128
A_GROUPS = 8
B_HEADS = 16
B_HEAD_DIM = 64
B_V_DIM = 64
B_KV_LATENT = 256
B_IDX_HEADS = 8
B_IDX_DIM = 64
B_TOPK_MAX = 256
QBLK = 256
KTILE = 256
FAR_TILE = 256
SM_ROWS = 128
REL_BUCKETS = 32
REL_MAX_DIST = 128
N_EXPERTS = 16
N_GROUPS = 4
EXPERTS_PER_GROUP = 4
TOP_K = 2
MOE_ROWS = 1280
MOE_SUB = 256
MOE_FT = 512

ROW_TILE = 256
GMLP_ROWS = 512
VMEM_LIMIT = 60 * 1024 * 1024

INT_MIN = -2 ** 31
NEG_BIG = -1e30


def _cparams(sem):
    return pltpu.CompilerParams(dimension_semantics=sem, vmem_limit_bytes=VMEM_LIMIT)


def _lane_tile(x, k):
    return x if k == 1 else jnp.concatenate([x] * k, axis=1)


def _mod_rmsnorm(h, g, scale, shift):
    ms = jnp.mean(h * h, axis=-1, keepdims=True)
    return (h * lax.rsqrt(ms + EPS) * g) * (1.0 + scale) + shift


def _gelu_tanh(x):
    c = math.sqrt(2.0 / math.pi)
    return 0.5 * x * (1.0 + jnp.tanh(c * (x + 0.044715 * (x * x * x))))


def _adaln_kernel(c_ref, w_ref, b_ref, o_ref):
    c = c_ref[...]
    sc = c * jax.nn.sigmoid(c)
    o_ref[0] = jnp.dot(sc, w_ref[0], precision=HIGHEST, preferred_element_type=F32) + b_ref[0]


def _adaln(c, ada_w, ada_b):
    depth, d, e = ada_w.shape
    bsz = c.shape[0]
    bp = 8
    c_pad = jnp.zeros((bp, d), F32).at[:bsz].set(c)
    tn = 1024
    out = pl.pallas_call(
        _adaln_kernel,
        out_shape=jax.ShapeDtypeStruct((depth, bp, e), F32),
        grid=(depth, e // tn),
        in_specs=[
            pl.BlockSpec((bp, d), lambda l, j: (0, 0)),
            pl.BlockSpec((1, d, tn), lambda l, j: (l, 0, j)),
            pl.BlockSpec((1, 1, tn), lambda l, j: (l, 0, j)),
        ],
        out_specs=pl.BlockSpec((1, bp, tn), lambda l, j: (l, 0, j)),
        compiler_params=_cparams(("arbitrary", "arbitrary")),
        name="adaln",
    )(c_pad, ada_w, ada_b.reshape(depth, 1, e))
    return out[:, :bsz]


def _residual_epilogue(h, y, mod, n2g, rw, h_out, hn_out, lg_out):
    h1 = h + mod[2:3] * y
    h_out[...] = h1
    hn2 = _mod_rmsnorm(h1, n2g, mod[4:5], mod[3:4])
    bits = pltpu.bitcast(hn2.astype(BF16).astype(F32), I32)
    half = bits.shape[1] // 2
    hn_out[...] = bits[:, :half] | lax.shift_right_logical(bits[:, half:], 16)
    lg_out[...] = jnp.dot(hn2, rw, precision=HIGHEST, preferred_element_type=F32)


def _gmlp_kernel(h_ref, mod_ref, n1g_ref, win_ref, lng_ref, lnb_ref, wsp_ref, bsp_ref, wout_ref,
                 n2g_ref, rw_ref, h_out, hn_out, lg_out, u_scr, v_scr, s_scr):
    tm = h_ref.shape[0]
    inner = u_scr.shape[1]
    gw = inner // A_GROUPS
    tn = 512
    h = h_ref[...]
    mod = mod_ref[0]
    hn = _mod_rmsnorm(h, n1g_ref[...], mod[1:2], mod[0:1]).astype(BF16)
    for j in range(2 * inner // tn):
        z = _gelu_tanh(jnp.dot(hn, win_ref[:, j * tn:(j + 1) * tn], preferred_element_type=F32))
        if j < inner // tn:
            u_scr[:, j * tn:(j + 1) * tn] = z
        else:
            jj = j - inner // tn
            v_scr[:, jj * tn:(jj + 1) * tn] = z
    vsum = jnp.zeros((tm, 1), F32)
    for j in range(inner // tn):
        vsum = vsum + jnp.sum(v_scr[:, j * tn:(j + 1) * tn], axis=-1, keepdims=True)
    mu = vsum * (1.0 / inner)
    vsq = jnp.zeros((tm, 1), F32)
    for j in range(inner // tn):
        d = v_scr[:, j * tn:(j + 1) * tn] - mu
        vsq = vsq + jnp.sum(d * d, axis=-1, keepdims=True)
    rstd = lax.rsqrt(vsq * (1.0 / inner) + EPS)
    row = lax.broadcasted_iota(I32, (A_CHUNK, A_CHUNK), 0)
    col = lax.broadcasted_iota(I32, (A_CHUNK, A_CHUNK), 1)
    tril = row >= col
    for g in range(A_GROUPS):
        ws = jnp.where(tril, wsp_ref[g], 0.0).astype(BF16)
        bcol = bsp_ref[:, g:g + 1]
        lg = lng_ref[:, g * gw:(g + 1) * gw]
        lb = lnb_ref[:, g * gw:(g + 1) * gw]
        for c in range(tm // A_CHUNK):
            r0, r1 = c * A_CHUNK, (c + 1) * A_CHUNK
            vt = v_scr[r0:r1, g * gw:(g + 1) * gw]
            vn = ((vt - mu[r0:r1]) * rstd[r0:r1]) * lg + lb
            fv = jnp.dot(ws, vn.astype(BF16), preferred_element_type=F32) + bcol
            s_scr[r0:r1, g * gw:(g + 1) * gw] = (u_scr[r0:r1, g * gw:(g + 1) * gw] * fv).astype(BF16)
    y = jnp.dot(s_scr[...], wout_ref[...], preferred_element_type=F32)
    _residual_epilogue(h, y, mod, n2g_ref[...], rw_ref[...], h_out, hn_out, lg_out)


def _const_spec(shape):
    nd = len(shape)
    return pl.BlockSpec(shape, lambda i, _nd=nd: (0,) * _nd, pipeline_mode=pl.Buffered(1))


def _gmlp_layer(h, mod_l, n1g, w_in, ln_g, ln_b, w_sp, b_sp, w_out, n2g, router_w, seq):
    n, d = h.shape
    inner = w_out.shape[0]
    tm = GMLP_ROWS
    per_b = seq // tm
    ne = router_w.shape[1]
    return pl.pallas_call(
        _gmlp_kernel,
        out_shape=(jax.ShapeDtypeStruct((n, d), F32), jax.ShapeDtypeStruct((n, d // 2), I32),
                   jax.ShapeDtypeStruct((n, ne), F32)),
        grid=(n // tm,),
        in_specs=[
            pl.BlockSpec((tm, d), lambda i: (i, 0)),
            pl.BlockSpec((1, 6, d), lambda i: (i // per_b, 0, 0)),
            _const_spec((1, d)),
            _const_spec((d, 2 * inner)),
            _const_spec((1, inner)),
            _const_spec((1, inner)),
            _const_spec((A_GROUPS, A_CHUNK, A_CHUNK)),
            _const_spec((A_CHUNK, A_GROUPS)),
            _const_spec((inner, d)),
            _const_spec((1, d)),
            _const_spec((d, ne)),
        ],
        out_specs=(pl.BlockSpec((tm, d), lambda i: (i, 0)), pl.BlockSpec((tm, d // 2), lambda i: (i, 0)),
                   pl.BlockSpec((tm, ne), lambda i: (i, 0))),
        scratch_shapes=[pltpu.VMEM((tm, inner), F32), pltpu.VMEM((tm, inner), F32),
                        pltpu.VMEM((tm, inner), BF16)],
        compiler_params=_cparams(("arbitrary",)),
        name="gmlp_layer",
    )(h, mod_l, n1g.reshape(1, d), w_in.astype(BF16), ln_g.reshape(1, inner), ln_b.reshape(1, inner),
      w_sp, b_sp.T, w_out.astype(BF16), n2g.reshape(1, d), router_w)


def _first_max4(rows):
    m = jnp.maximum(jnp.maximum(rows[0], rows[1]), jnp.maximum(rows[2], rows[3]))
    idx = jnp.where(rows[0] == m, 0, jnp.where(rows[1] == m, 1, jnp.where(rows[2] == m, 2, 3)))
    return m, idx.astype(I32)


def _route_kernel(lt_ref, rb_ref, e_ref, g_ref):
    aff = jax.nn.sigmoid(lt_ref[...])
    sel = aff + rb_ref[...]
    neg = jnp.float32(-jnp.inf)
    g_score, g_i1, g_i2 = [], [], []
    for g in range(N_GROUPS):
        rows = [sel[4 * g + k:4 * g + k + 1, :] for k in range(EXPERTS_PER_GROUP)]
        m1, i1 = _first_max4(rows)
        rest = [jnp.where(i1 == k, neg, rows[k]) for k in range(EXPERTS_PER_GROUP)]
        m2, i2 = _first_max4(rest)
        g_score.append(m1 + m2)
        g_i1.append(i1)
        g_i2.append(i2)
    _, grp = _first_max4(g_score)
    l1 = jnp.zeros_like(grp)
    l2 = jnp.zeros_like(grp)
    for g in range(N_GROUPS):
        l1 = jnp.where(grp == g, g_i1[g], l1)
        l2 = jnp.where(grp == g, g_i2[g], l2)
    e1 = grp * EXPERTS_PER_GROUP + l1
    e2 = grp * EXPERTS_PER_GROUP + l2
    a1 = jnp.zeros_like(g_score[0])
    a2 = jnp.zeros_like(g_score[0])
    for e in range(N_EXPERTS):
        ae = aff[e:e + 1, :]
        a1 = jnp.where(e1 == e, ae, a1)
        a2 = jnp.where(e2 == e, ae, a2)
    tot = a1 + a2
    e_ref[0:1, :] = e1
    e_ref[1:2, :] = e2
    g_ref[0:1, :] = a1 / tot
    g_ref[1:2, :] = a2 / tot


def _route(logits, router_b):
    n, ne = logits.shape[0], N_EXPERTS
    tn = min(2048, n)
    return pl.pallas_call(
        _route_kernel,
        out_shape=(jax.ShapeDtypeStruct((TOP_K, n), I32), jax.ShapeDtypeStruct((TOP_K, n), F32)),
        grid=(n // tn,),
        in_specs=[pl.BlockSpec((ne, tn), lambda i: (0, i)), pl.BlockSpec((ne, 1), lambda i: (0, 0))],
        out_specs=(pl.BlockSpec((TOP_K, tn), lambda i: (0, i)), pl.BlockSpec((TOP_K, tn), lambda i: (0, i))),
        compiler_params=_cparams(("arbitrary",)),
        name="route",
    )(logits[:, :ne].T, router_b.reshape(ne, 1))


def _slot_kernel(e_ref, slot_ref, cnt_ref):
    n = e_ref.shape[1]
    nblk = n // 128
    eid = lax.broadcasted_iota(I32, (N_EXPERTS, 128), 0)
    r = lax.broadcasted_iota(I32, (128, 128), 0)
    c = lax.broadcasted_iota(I32, (128, 128), 1)
    upper = jnp.where(r <= c, 1.0, 0.0).astype(BF16)

    def hits(b):
        o = pl.multiple_of(b * 128, 128)
        t1 = e_ref[0:1, pl.ds(o, 128)] == eid
        t2 = e_ref[1:2, pl.ds(o, 128)] == eid
        return o, t1, t2, jnp.where(t1, 1.0, 0.0) + jnp.where(t2, 1.0, 0.0)

    def count_body(b, acc):
        return acc + hits(b)[3]

    cnt = jnp.sum(lax.fori_loop(0, nblk, count_body, jnp.zeros((N_EXPERTS, 128), F32)),
                  axis=-1, keepdims=True)
    cnt_ref[...] = cnt.astype(I32)
    padded = jnp.zeros_like(cnt)
    for k in range(pl.cdiv(n * TOP_K, MOE_ROWS)):
        padded = padded + jnp.where(cnt > k * MOE_ROWS, float(MOE_ROWS), 0.0)
    er = lax.broadcasted_iota(I32, (N_EXPERTS, N_EXPERTS), 0)
    ec = lax.broadcasted_iota(I32, (N_EXPERTS, N_EXPERTS), 1)
    padded_row = jnp.sum(jnp.where(er == ec, padded, 0.0), axis=0, keepdims=True)
    start = jnp.sum(jnp.where(ec < er, padded_row, 0.0), axis=-1, keepdims=True)

    def slot_body(b, carry):
        o, t1, t2, t = hits(b)
        incl = jnp.dot(t.astype(BF16), upper, preferred_element_type=F32)
        pos = carry + incl - t
        slot_ref[0:1, pl.ds(o, 128)] = jnp.sum(jnp.where(t1, pos, 0.0), axis=0, keepdims=True).astype(I32)
        slot_ref[1:2, pl.ds(o, 128)] = jnp.sum(jnp.where(t2, pos, 0.0), axis=0, keepdims=True).astype(I32)
        return carry + incl[:, 127:128]

    lax.fori_loop(0, nblk, slot_body, start)


def _slot_tables(experts):
    n = experts.shape[1]
    a = n * TOP_K
    slot, counts = pl.pallas_call(
        _slot_kernel,
        out_shape=(jax.ShapeDtypeStruct((TOP_K, n), I32), jax.ShapeDtypeStruct((N_EXPERTS, 1), I32)),
        compiler_params=pltpu.CompilerParams(vmem_limit_bytes=VMEM_LIMIT),
        name="slots",
    )(experts)
    counts = counts.reshape(N_EXPERTS)
    padded = ((counts + MOE_ROWS - 1) // MOE_ROWS) * MOE_ROWS
    pad_end = jnp.cumsum(padded)
    pad_start = pad_end - padded
    n_items = pl.cdiv(a, MOE_ROWS) + N_EXPERTS
    p = n_items * MOE_ROWS
    tok = jnp.broadcast_to(jnp.arange(n, dtype=I32)[None, :], (TOP_K, n))
    tok_of_slot = jnp.zeros((p,), I32).at[slot.reshape(a)].set(tok.reshape(a))
    slot_of_assign = slot.T
    item_start = jnp.arange(n_items, dtype=I32) * MOE_ROWS
    n_used = pad_end[-1] // MOE_ROWS
    item_e_raw = jnp.minimum(jnp.searchsorted(pad_end, item_start, side='right'), N_EXPERTS - 1).astype(I32)
    used = item_start < pad_end[-1]
    last_e = item_e_raw[jnp.maximum(n_used - 1, 0)]
    item_e = jnp.where(used, item_e_raw, last_e).astype(I32)
    valid = jnp.clip(counts[item_e_raw] - (item_start - pad_start[item_e_raw]), 0, MOE_ROWS)
    item_rows = jnp.where(used, valid, 0).astype(I32)
    item_blk = jnp.where(used, jnp.arange(n_items, dtype=I32), jnp.maximum(n_used - 1, 0)).astype(I32)
    return tok_of_slot, slot_of_assign, item_e, item_rows, item_blk


def _moe_kernel(ie_ref, ir_ref, ib_ref, x_ref, wg_ref, wu_ref, wd_ref, o_ref, wg_s, wu_s, wd_s, x_s):
    it = pl.program_id(0)
    f = pl.program_id(1)
    nrows = ir_ref[it]
    half = x_ref.shape[1]

    @pl.when(f == 0)
    def _():
        o_ref[...] = jnp.zeros_like(o_ref)
        pk = x_ref[...]
        x_s[:, :half] = pltpu.bitcast(pk & jnp.int32(-65536), F32).astype(BF16)
        x_s[:, half:] = pltpu.bitcast(lax.shift_left(pk, 16), F32).astype(BF16)

    @pl.when(nrows > 0)
    def _():
        wg_s[...] = wg_ref[0, 0].astype(BF16)
        wu_s[...] = wu_ref[0, 0].astype(BF16)
        wd_s[...] = wd_ref[0, 0].astype(BF16)
        nsb = (nrows + MOE_SUB - 1) // MOE_SUB

        def sub_block(sb):
            r0 = pl.multiple_of(sb * MOE_SUB, MOE_SUB)
            x = x_s[pl.ds(r0, MOE_SUB), :]
            g = jnp.dot(x, wg_s[...], preferred_element_type=F32)
            u = jnp.dot(x, wu_s[...], preferred_element_type=F32)
            hmid = ((g * jax.nn.sigmoid(g)) * u).astype(BF16)
            o_ref[pl.ds(r0, MOE_SUB), :] += jnp.dot(hmid, wd_s[...], preferred_element_type=F32)

        def pair_body(pi, carry):
            sub_block(2 * pi)
            sub_block(2 * pi + 1)
            return carry

        lax.fori_loop(0, nsb // 2, pair_body, 0)

        @pl.when(nsb % 2 == 1)
        def _():
            sub_block(nsb - 1)


def _moe_experts(x_sorted, item_e, item_rows, item_blk, w_gate, w_up, w_down, layer):
    p, half = x_sorted.shape
    d = 2 * half
    n_items = p // MOE_ROWS
    dff = w_gate.shape[3]
    nf = dff // MOE_FT
    grid_spec = pltpu.PrefetchScalarGridSpec(
        num_scalar_prefetch=3,
        grid=(n_items, nf),
        in_specs=[
            pl.BlockSpec((MOE_ROWS, half), lambda i, f, ie, ir, ib: (ib[i], 0)),
            pl.BlockSpec((1, 1, d, MOE_FT),
                         lambda i, f, ie, ir, ib: (layer, ie[i], 0, jnp.where(ir[i] > 0, f, nf - 1))),
            pl.BlockSpec((1, 1, d, MOE_FT),
                         lambda i, f, ie, ir, ib: (layer, ie[i], 0, jnp.where(ir[i] > 0, f, nf - 1))),
            pl.BlockSpec((1, 1, MOE_FT, d),
                         lambda i, f, ie, ir, ib: (layer, ie[i], jnp.where(ir[i] > 0, f, nf - 1), 0)),
        ],
        out_specs=pl.BlockSpec((MOE_ROWS, d), lambda i, f, ie, ir, ib: (i, 0)),
        scratch_shapes=[pltpu.VMEM((d, MOE_FT), BF16), pltpu.VMEM((d, MOE_FT), BF16),
                        pltpu.VMEM((MOE_FT, d), BF16), pltpu.VMEM((MOE_ROWS, d), BF16)],
    )
    return pl.pallas_call(
        _moe_kernel,
        out_shape=jax.ShapeDtypeStruct((p, d), F32),
        grid_spec=grid_spec,
        compiler_params=_cparams(("arbitrary", "arbitrary")),
        name="moe_experts",
    )(item_e, item_rows, item_blk, x_sorted, w_gate, w_up, w_down)


def _moe(hn2, logits, router_b, w_gate, w_up, w_down, layer):
    experts, gates = _route(logits, router_b)
    tok_of_slot, slot_of_assign, item_e, item_rows, item_blk = _slot_tables(experts)
    x_sorted = hn2.at[tok_of_slot].get(mode='promise_in_bounds')
    y_slot = _moe_experts(x_sorted, item_e, item_rows, item_blk, w_gate, w_up, w_down, layer)
    y0 = y_slot.at[slot_of_assign[:, 0]].get(mode='promise_in_bounds')
    y1 = y_slot.at[slot_of_assign[:, 1]].get(mode='promise_in_bounds')
    n = hn2.shape[0]
    gates_rep = jnp.concatenate([jnp.broadcast_to(gates[k][:, None], (n, 128)) for k in range(TOP_K)], axis=1)
    return y0, y1, gates_rep


def _moe_combine(h1, y0, y1, gates, mod):
    reps = h1.shape[1] // 128
    g0 = _lane_tile(gates[:, :128], reps)
    g1 = _lane_tile(gates[:, 128:], reps)
    return h1 + mod[5:6] * (g0 * y0 + g1 * y1)


def _bproj_kernel(h1_ref, y0_ref, y1_ref, gt_ref, modp_ref, mod_ref, n1g_ref, w_ref, kvg_ref, kig_ref,
                  wuk_ref, h_out, qa_out, ckv_out, qi_out, ki_out, wi_out):
    tm = h1_ref.shape[0]
    h = _moe_combine(h1_ref[...], y0_ref[...], y1_ref[...], gt_ref[...], modp_ref[0])
    h_out[...] = h
    mod = mod_ref[0]
    hn = _mod_rmsnorm(h, n1g_ref[...], mod[1:2], mod[0:1]).astype(BF16)
    proj = jnp.dot(hn, w_ref[...], preferred_element_type=F32)
    o1 = B_HEADS * B_HEAD_DIM
    o2 = o1 + B_KV_LATENT
    o3 = o2 + B_IDX_HEADS * B_IDX_DIM
    ckv = proj[:, o1:o2]
    ckv = ckv * lax.rsqrt(jnp.mean(ckv * ckv, axis=-1, keepdims=True) + EPS) * kvg_ref[...]
    ckv_out[...] = ckv.astype(BF16)
    tail = proj[:, o3:o3 + 128]
    ki = tail[:, :B_IDX_DIM]
    ki = ki * lax.rsqrt(jnp.mean(ki * ki, axis=-1, keepdims=True) + EPS) * kig_ref[...]
    ki_out[...] = ki.astype(BF16)
    wi = tail[:, B_IDX_DIM:B_IDX_DIM + B_IDX_HEADS] * (B_IDX_HEADS ** -0.5 * B_IDX_DIM ** -0.5)
    scale = B_HEAD_DIM ** -0.5
    for blk in range(tm // QBLK):
        r0, r1 = blk * QBLK, (blk + 1) * QBLK
        for hh in range(B_HEADS):
            qh = proj[r0:r1, hh * B_HEAD_DIM:(hh + 1) * B_HEAD_DIM].astype(BF16)
            qa = jnp.dot(qh, wuk_ref[hh], preferred_element_type=F32) * scale
            qa_out[blk, hh * QBLK:(hh + 1) * QBLK, :] = qa.astype(BF16)
        for hh in range(B_IDX_HEADS):
            qi_out[blk, hh * QBLK:(hh + 1) * QBLK, :] = proj[r0:r1, o2 + hh * B_IDX_DIM:
                                                            o2 + (hh + 1) * B_IDX_DIM].astype(BF16)
            wi_out[blk, hh * QBLK:(hh + 1) * QBLK, :] = jnp.broadcast_to(wi[r0:r1, hh:hh + 1], (QBLK, 128))


def _bproj(h1, y0, y1, gates, mod_prev, mod_l, n1g, b_w_in, kv_g, w_uk, kidx_g, seq):
    n, d = h1.shape
    tm = ROW_TILE
    per_b = seq // tm
    nq = n // QBLK
    qpb = tm // QBLK
    o1 = B_HEADS * B_HEAD_DIM
    o2 = o1 + B_KV_LATENT
    o3 = o2 + B_IDX_HEADS * B_IDX_DIM
    wcat = jnp.zeros((d, o3 + 128), F32).at[:, :b_w_in.shape[1]].set(b_w_in).astype(BF16)
    wuk_t = jnp.transpose(w_uk, (1, 2, 0)).astype(BF16)
    row = lambda i: (i, 0)
    modm = lambda i: (i // per_b, 0, 0)
    blk3 = lambda i: (i, 0, 0)
    return pl.pallas_call(
        _bproj_kernel,
        out_shape=(
            jax.ShapeDtypeStruct((n, d), F32),
            jax.ShapeDtypeStruct((nq, B_HEADS * QBLK, B_KV_LATENT), BF16),
            jax.ShapeDtypeStruct((n, B_KV_LATENT), BF16),
            jax.ShapeDtypeStruct((nq, B_IDX_HEADS * QBLK, B_IDX_DIM), BF16),
            jax.ShapeDtypeStruct((n, B_IDX_DIM), BF16),
            jax.ShapeDtypeStruct((nq, B_IDX_HEADS * QBLK, 128), F32),
        ),
        grid=(n // tm,),
        in_specs=[
            pl.BlockSpec((tm, d), row), pl.BlockSpec((tm, d), row), pl.BlockSpec((tm, d), row),
            pl.BlockSpec((tm, TOP_K * 128), row),
            pl.BlockSpec((1, 6, d), modm), pl.BlockSpec((1, 6, d), modm),
            _const_spec((1, d)),
            _const_spec((d, o3 + 128)),
            _const_spec((1, B_KV_LATENT)),
            _const_spec((1, B_IDX_DIM)),
            _const_spec((B_HEADS, B_HEAD_DIM, B_KV_LATENT)),
        ],
        out_specs=(
            pl.BlockSpec((tm, d), row),
            pl.BlockSpec((qpb, B_HEADS * QBLK, B_KV_LATENT), blk3),
            pl.BlockSpec((tm, B_KV_LATENT), row),
            pl.BlockSpec((qpb, B_IDX_HEADS * QBLK, B_IDX_DIM), blk3),
            pl.BlockSpec((tm, B_IDX_DIM), row),
            pl.BlockSpec((qpb, B_IDX_HEADS * QBLK, 128), blk3),
        ),
        compiler_params=_cparams(("arbitrary",)),
        name="dsa_proj",
    )(h1, y0, y1, gates, mod_prev, mod_l, n1g.reshape(1, d), wcat, kv_g.reshape(1, -1),
      kidx_g.reshape(1, -1), wuk_t)


def _t5_bucket(dist):
    n = jnp.maximum(dist, 0)
    exact = REL_BUCKETS // 2
    nf = jnp.maximum(n, 1).astype(F32)
    large = exact + (jnp.log(nf / exact) / math.log(REL_MAX_DIST / exact)
                     * (REL_BUCKETS - exact)).astype(I32)
    large = jnp.minimum(large, REL_BUCKETS - 1)
    return jnp.where(n < exact, n, large)


def _bias_tables(rel_bias):
    assert REL_MAX_DIST <= 128
    t = jnp.arange(128, dtype=I32)[:, None]
    s = jnp.arange(128, dtype=I32)[None, :]
    far = rel_bias[REL_BUCKETS - 1]
    diag = rel_bias[_t5_bucket(t - s)] - far
    prev = rel_bias[_t5_bucket(t - s + 128)] - far
    return jnp.stack([jnp.transpose(diag, (2, 0, 1)), jnp.transpose(prev, (2, 0, 1))])


def _attn_kernel(qa_ref, qi_ref, wi_ref, ckv_ref, ki_ref, bt_ref, o_ref, kbuf, m_scr, l_scr, a_scr, acc_scr,
                 s_scr, p_scr, madd_scr, kbuft, tau_scr):
    i = pl.program_id(1)
    nt = i + 1
    t_row = i * QBLK + lax.broadcasted_iota(I32, (QBLK, KTILE), 0)
    lane = lax.broadcasted_iota(I32, (QBLK, KTILE), 1)

    qi = qi_ref[0]

    def score_body(j, carry):
        k0 = pl.multiple_of(j * KTILE, KTILE)
        kt = ki_ref[0, pl.ds(k0, KTILE), :]
        p = lax.dot_general(qi, kt, (((1,), (1,)), ((), ())), preferred_element_type=F32)
        p = jnp.maximum(p, 0.0) * _lane_tile(wi_ref[0], KTILE // 128)
        sc = p[0:QBLK]
        for hh in range(1, B_IDX_HEADS):
            sc = sc + p[hh * QBLK:(hh + 1) * QBLK]
        bits = pltpu.bitcast(sc + 0.0, I32)
        key = jnp.where(bits < 0, bits ^ jnp.int32(0x7FFFFFFF), bits)
        key = jnp.where(k0 + lane <= t_row, key, jnp.int32(INT_MIN))
        kbuf[:, pl.ds(k0, KTILE)] = key
        kbuft[pl.ds(k0, KTILE), :] = key.T
        return carry

    lax.fori_loop(0, nt, score_body, 0)

    def fold_lanes(x):
        out = x[:, 0:128]
        for k in range(1, KTILE // 128):
            out = out + x[:, k * 128:(k + 1) * 128]
        return out

    def count_ge(cand):
        def body(j, acc):
            k0 = pl.multiple_of(j * KTILE, KTILE)
            return acc + fold_lanes(jnp.where(kbuf[:, pl.ds(k0, KTILE)] >= cand, 1, 0))
        acc = lax.fori_loop(0, nt, body, jnp.zeros((QBLK, 128), I32))
        return jnp.sum(acc, axis=-1, keepdims=True)

    def count_ge_t(cand_row):
        def body(j, acc):
            k0 = pl.multiple_of(j * KTILE, KTILE)
            hit = jnp.where(kbuft[pl.ds(k0, KTILE), :] >= cand_row, 1, 0)
            return acc + jnp.sum(hit.reshape(KTILE // 8, 8, QBLK), axis=0)
        acc = lax.fori_loop(0, nt, body, jnp.zeros((8, QBLK), I32))
        return jnp.sum(acc, axis=0, keepdims=True)

    def bit_step(b, lo, n_ge):
        cand = lo + lax.shift_left(jnp.int32(1), 31 - b)
        cnt = count_ge_t(cand)
        take = cnt >= B_TOPK_MAX
        return jnp.where(take, cand, lo), jnp.where(take, cnt, n_ge)

    def bit_cond(c):
        return (c[0] < 32) & (c[3] == 0)

    def bit_body(c):
        b, lo, n_ge, _ = c
        lo, n_ge = bit_step(b, lo, n_ge)
        lo, n_ge = bit_step(b + 1, lo, n_ge)
        done = (jnp.max(n_ge) == B_TOPK_MAX).astype(I32)
        return b + 2, lo, n_ge, done

    _, tau_row, n_ge_row, _ = lax.while_loop(
        bit_cond, bit_body,
        (jnp.int32(0), jnp.full((1, QBLK), INT_MIN, I32), jnp.full((1, QBLK), nt * KTILE, I32),
         (i < 1).astype(I32)))
    tau_row = jnp.maximum(tau_row, jnp.int32(INT_MIN + 1))
    excess = (i >= 1) & (jnp.max(n_ge_row) > B_TOPK_MAX)

    def to_rows(row):
        return jnp.broadcast_to(row, (QBLK, QBLK)).T

    tau_scr[...] = to_rows(tau_row)

    @pl.when(excess)
    def _():
        tau = tau_scr[:, 0:1]
        n_ge = to_rows(n_ge_row)[:, 0:1]
        n_gt = count_ge(tau + 1)
        need = B_TOPK_MAX - n_gt

        def count_eq_before(pos):
            def body(j, acc):
                k0 = pl.multiple_of(j * KTILE, KTILE)
                hit = (kbuf[:, pl.ds(k0, KTILE)] == tau) & (k0 + lane < pos)
                return acc + fold_lanes(jnp.where(hit, 1, 0))
            acc = lax.fori_loop(0, nt, body, jnp.zeros((QBLK, 128), I32))
            return jnp.sum(acc, axis=-1, keepdims=True)

        def pos_body(b, pos):
            cand = pos + lax.shift_left(jnp.int32(1), 12 - b)
            return jnp.where(count_eq_before(cand) < need, cand, pos)

        pos = lax.fori_loop(0, 13, pos_body, jnp.zeros((QBLK, 1), I32))

        def drop_body(j, carry):
            k0 = pl.multiple_of(j * KTILE, KTILE)
            kk = kbuf[:, pl.ds(k0, KTILE)]
            drop = (kk == tau) & (k0 + lane > pos) & (n_ge > B_TOPK_MAX)
            kbuf[:, pl.ds(k0, KTILE)] = jnp.where(drop, jnp.int32(INT_MIN), kk)
            return carry

        lax.fori_loop(0, nt, drop_body, 0)

    m_scr[...] = jnp.full_like(m_scr, NEG_BIG)
    l_scr[...] = jnp.zeros_like(l_scr)
    acc_scr[...] = jnp.zeros_like(acc_scr)

    def attend(k0, width, tile_off):
        kv = ckv_ref[0, pl.ds(k0, width), :]
        s_scr[:, :width] = lax.dot_general(qa_ref[0], kv, (((1,), (1,)), ((), ())),
                                           preferred_element_type=F32)
        tau_t = _lane_tile(tau_scr[...], width // QBLK)
        madd_scr[:, :width] = jnp.where(kbuf[:, pl.ds(k0, width)] >= tau_t, 0.0, NEG_BIG)
        for r0 in range(0, B_HEADS * QBLK, SM_ROWS):
            r1 = r0 + SM_ROWS
            hh, q0 = r0 // QBLK, r0 % QBLK
            parts = []
            for kc in range(width // 128):
                sp = s_scr[r0:r1, kc * 128:(kc + 1) * 128] + madd_scr[q0:q0 + SM_ROWS, kc * 128:(kc + 1) * 128]
                if tile_off is not None:
                    rel = (QBLK // 128) * tile_off + q0 // 128 - kc
                    if rel in (0, 1):
                        sp = sp + bt_ref[rel, hh, q0 % 128:q0 % 128 + SM_ROWS, :]
                parts.append(sp)
            smax = parts[0]
            for sp in parts[1:]:
                smax = jnp.maximum(smax, sp)
            m_old = m_scr[r0:r1, :]
            m_new = jnp.maximum(m_old, jnp.max(smax, axis=-1, keepdims=True))
            alpha = jnp.exp(m_old - m_new)
            psum = None
            for kc, sp in enumerate(parts):
                p = jnp.exp(sp - m_new)
                p_scr[r0:r1, kc * 128:(kc + 1) * 128] = p.astype(BF16)
                psum = p if psum is None else psum + p
            l_scr[r0:r1, :] = alpha * l_scr[r0:r1, :] + jnp.sum(psum, axis=-1, keepdims=True)
            m_scr[r0:r1, :] = m_new
            a_scr[r0:r1, :] = alpha
        pv = jnp.dot(p_scr[:, :width], kv, preferred_element_type=F32)
        acc_scr[...] = _lane_tile(a_scr[...], B_KV_LATENT // 128) * acc_scr[...] + pv

    n_far = jnp.maximum(i - 1, 0)
    per_far = FAR_TILE // KTILE

    def far_body(jf, carry):
        attend(pl.multiple_of(jf * FAR_TILE, FAR_TILE), FAR_TILE, None)
        return carry

    lax.fori_loop(0, n_far // per_far, far_body, 0)

    if per_far > 1:
        @pl.when(n_far % per_far == 1)
        def _():
            attend(pl.multiple_of((n_far - 1) * KTILE, KTILE), KTILE, None)

    @pl.when(i >= 1)
    def _():
        attend(pl.multiple_of((i - 1) * KTILE, KTILE), KTILE, 1)

    attend(pl.multiple_of(i * KTILE, KTILE), KTILE, 0)
    inv_l = 1.0 / l_scr[...]
    o_ref[0] = (acc_scr[...] * _lane_tile(inv_l, B_KV_LATENT // 128)).astype(BF16)


def _attention(qa, qi, wi, ckv, ki, btab, bsz, seq):
    nqb = seq // QBLK
    gq = lambda b, i: (b * nqb + i, 0, 0)
    gb = lambda b, i: (b, 0, 0)
    return pl.pallas_call(
        _attn_kernel,
        out_shape=jax.ShapeDtypeStruct(qa.shape, BF16),
        grid=(bsz, nqb),
        in_specs=[
            pl.BlockSpec((1, B_HEADS * QBLK, B_KV_LATENT), gq),
            pl.BlockSpec((1, B_IDX_HEADS * QBLK, B_IDX_DIM), gq),
            pl.BlockSpec((1, B_IDX_HEADS * QBLK, 128), gq),
            pl.BlockSpec((1, seq, B_KV_LATENT), gb, pipeline_mode=pl.Buffered(1)),
            pl.BlockSpec((1, seq, B_IDX_DIM), gb, pipeline_mode=pl.Buffered(1)),
            pl.BlockSpec((2, B_HEADS, 128, 128), lambda b, i: (0, 0, 0, 0), pipeline_mode=pl.Buffered(1)),
        ],
        out_specs=pl.BlockSpec((1, B_HEADS * QBLK, B_KV_LATENT), gq),
        scratch_shapes=[
            pltpu.VMEM((QBLK, seq), I32),
            pltpu.VMEM((B_HEADS * QBLK, 128), F32),
            pltpu.VMEM((B_HEADS * QBLK, 128), F32),
            pltpu.VMEM((B_HEADS * QBLK, 128), F32),
            pltpu.VMEM((B_HEADS * QBLK, B_KV_LATENT), F32),
            pltpu.VMEM((B_HEADS * QBLK, FAR_TILE), F32),
            pltpu.VMEM((B_HEADS * QBLK, FAR_TILE), BF16),
            pltpu.VMEM((QBLK, FAR_TILE), F32),
            pltpu.VMEM((seq, QBLK), I32),
            pltpu.VMEM((QBLK, QBLK), I32),
        ],
        compiler_params=_cparams(("arbitrary", "arbitrary")),
        name="dsa_attention",
    )(qa, qi, wi, ckv.reshape(bsz, seq, -1), ki.reshape(bsz, seq, -1), btab)


def _bout_kernel(h_ref, ol_ref, mod_ref, wuv_ref, wout_ref, n2g_ref, rw_ref, h_out, hn_out, lg_out, o_scr):
    mod = mod_ref[0]
    for blk in range(ol_ref.shape[0]):
        for hh in range(B_HEADS):
            oh = jnp.dot(ol_ref[blk, hh * QBLK:(hh + 1) * QBLK, :], wuv_ref[hh], preferred_element_type=F32)
            o_scr[blk * QBLK:(blk + 1) * QBLK, hh * B_V_DIM:(hh + 1) * B_V_DIM] = oh.astype(BF16)
    y = jnp.dot(o_scr[...], wout_ref[...], preferred_element_type=F32)
    _residual_epilogue(h_ref[...], y, mod, n2g_ref[...], rw_ref[...], h_out, hn_out, lg_out)


def _bout(h, o_lat, mod_l, w_uv, w_out, n2g, router_w, seq):
    n, d = h.shape
    tm = ROW_TILE
    per_b = seq // tm
    qpb = tm // QBLK
    ne = router_w.shape[1]
    wuv_t = jnp.transpose(w_uv, (1, 0, 2)).astype(BF16)
    row = lambda i: (i, 0)
    return pl.pallas_call(
        _bout_kernel,
        out_shape=(jax.ShapeDtypeStruct((n, d), F32), jax.ShapeDtypeStruct((n, d // 2), I32),
                   jax.ShapeDtypeStruct((n, ne), F32)),
        grid=(n // tm,),
        in_specs=[
            pl.BlockSpec((tm, d), row),
            pl.BlockSpec((qpb, B_HEADS * QBLK, B_KV_LATENT), lambda i: (i, 0, 0)),
            pl.BlockSpec((1, 6, d), lambda i: (i // per_b, 0, 0)),
            _const_spec((B_HEADS, B_KV_LATENT, B_V_DIM)),
            _const_spec((B_HEADS * B_V_DIM, d)),
            _const_spec((1, d)),
            _const_spec((d, ne)),
        ],
        out_specs=(pl.BlockSpec((tm, d), row), pl.BlockSpec((tm, d // 2), row), pl.BlockSpec((tm, ne), row)),
        scratch_shapes=[pltpu.VMEM((tm, B_HEADS * B_V_DIM), BF16)],
        compiler_params=_cparams(("arbitrary",)),
        name="dsa_out",
    )(h, o_lat, mod_l, wuv_t, w_out.astype(BF16), n2g.reshape(1, d), router_w)


def _final_kernel(h1_ref, y0_ref, y1_ref, gt_ref, mod_ref, g_ref, o_ref):
    h = _moe_combine(h1_ref[...], y0_ref[...], y1_ref[...], gt_ref[...], mod_ref[0])
    o_ref[...] = h * lax.rsqrt(jnp.mean(h * h, axis=-1, keepdims=True) + EPS) * g_ref[...]


def _final(h1, y0, y1, gates, mod_l, final_g, seq):
    n, d = h1.shape
    tm = 512
    per_b = seq // tm
    row = lambda i: (i, 0)
    return pl.pallas_call(
        _final_kernel,
        out_shape=jax.ShapeDtypeStruct((n, d), F32),
        grid=(n // tm,),
        in_specs=[pl.BlockSpec((tm, d), row), pl.BlockSpec((tm, d), row), pl.BlockSpec((tm, d), row),
                  pl.BlockSpec((tm, TOP_K * 128), row), pl.BlockSpec((1, 6, d), lambda i: (i // per_b, 0, 0)),
                  _const_spec((1, d))],
        out_specs=pl.BlockSpec((tm, d), row),
        compiler_params=_cparams(("arbitrary",)),
        name="final_norm",
    )(h1, y0, y1, gates, mod_l, final_g.reshape(1, d))


def kernel(x, c, ada_w, ada_b, norm1_g, norm2_g, a_w_in, a_ln_g, a_ln_b, a_w_sp, a_b_sp, a_w_out, b_w_in,
           b_kv_norm_g, b_w_uk, b_w_uv, b_kidx_g, b_w_out, rel_bias, router_w, router_b, moe_w_gate,
           moe_w_up, moe_w_down, final_g):
    bsz, seq, d = x.shape
    n = bsz * seq
    mod = _adaln(c, ada_w, ada_b).reshape(ada_w.shape[0], bsz, 6, d)
    rw_pad = jnp.zeros((d, 128), F32).at[:, :N_EXPERTS].set(router_w)
    h = x.reshape(n, d)

    h1, hn2, logits = _gmlp_layer(h, mod[0], norm1_g[0], a_w_in[0], a_ln_g[0], a_ln_b[0], a_w_sp[0],
                                  a_b_sp[0], a_w_out[0], norm2_g[0], rw_pad, seq)
    y0, y1, gates = _moe(hn2, logits, router_b, moe_w_gate, moe_w_up, moe_w_down, 0)

    h, qa, ckv, qi, ki, wi = _bproj(h1, y0, y1, gates, mod[0], mod[1], norm1_g[1], b_w_in[0],
                                    b_kv_norm_g[0], b_w_uk[0], b_kidx_g[0], seq)
    o_lat = _attention(qa, qi, wi, ckv, ki, _bias_tables(rel_bias), bsz, seq)
    h1, hn2, logits = _bout(h, o_lat, mod[1], b_w_uv[0], b_w_out[0], norm2_g[1], rw_pad, seq)
    y0, y1, gates = _moe(hn2, logits, router_b, moe_w_gate, moe_w_up, moe_w_down, 1)

    out = _final(h1, y0, y1, gates, mod[1], final_g, seq)
    return out.reshape(bsz, seq, d)
```

```python
import functools
import math

import jax
import jax.numpy as jnp
from jax import lax
from jax.experimental import pallas as pl
from jax.experimental.pallas import tpu as pltpu

F32 = jnp.float32
BF16 = jnp.bfloat16
I32 = jnp.int32
HIGHEST = lax.Precision.HIGHEST

EPS = 1e-6
A_CHUNK = 128
A_GROUPS = 8
B_HEADS = 16
B_HEAD_DIM = 64
B_V_DIM = 64
B_KV_LATENT = 256
B_IDX_HEADS = 8
B_IDX_DIM = 64
B_TOPK_MAX = 256
QBLK = 256
KTILE = 256
FAR_TILE = 256
SM_ROWS = 128
REL_BUCKETS = 32
REL_MAX_DIST = 128
N_EXPERTS = 16
N_GROUPS = 4
EXPERTS_PER_GROUP = 4
TOP_K = 2
MOE_ROWS = 1280
MOE_SUB = 256
MOE_FT = 512

ROW_TILE = 256
GMLP_ROWS = 512
VMEM_LIMIT = 60 * 1024 * 1024

INT_MIN = -2 ** 31
NEG_BIG = -1e30


def _cparams(sem):
    return pltpu.CompilerParams(dimension_semantics=sem, vmem_limit_bytes=VMEM_LIMIT)


def _lane_tile(x, k):
    return x if k == 1 else jnp.concatenate([x] * k, axis=1)


def _mod_rmsnorm(h, g, scale, shift):
    ms = jnp.mean(h * h, axis=-1, keepdims=True)
    return (h * lax.rsqrt(ms + EPS) * g) * (1.0 + scale) + shift


def _gelu_tanh(x):
    c = math.sqrt(2.0 / math.pi)
    return 0.5 * x * (1.0 + jnp.tanh(c * (x + 0.044715 * (x * x * x))))


def _adaln_kernel(c_ref, w_ref, b_ref, o_ref):
    c = c_ref[...]
    sc = c * jax.nn.sigmoid(c)
    o_ref[0] = jnp.dot(sc, w_ref[0], precision=HIGHEST, preferred_element_type=F32) + b_ref[0]


def _adaln(c, ada_w, ada_b):
    depth, d, e = ada_w.shape
    bsz = c.shape[0]
    bp = 8
    c_pad = jnp.zeros((bp, d), F32).at[:bsz].set(c)
    tn = 1024
    out = pl.pallas_call(
        _adaln_kernel,
        out_shape=jax.ShapeDtypeStruct((depth, bp, e), F32),
        grid=(depth, e // tn),
        in_specs=[
            pl.BlockSpec((bp, d), lambda l, j: (0, 0)),
            pl.BlockSpec((1, d, tn), lambda l, j: (l, 0, j)),
            pl.BlockSpec((1, 1, tn), lambda l, j: (l, 0, j)),
        ],
        out_specs=pl.BlockSpec((1, bp, tn), lambda l, j: (l, 0, j)),
        compiler_params=_cparams(("arbitrary", "arbitrary")),
        name="adaln",
    )(c_pad, ada_w, ada_b.reshape(depth, 1, e))
    return out[:, :bsz]


def _residual_epilogue(h, y, mod, n2g, rw, h_out, hn_out, lg_out):
    h1 = h + mod[2:3] * y
    h_out[...] = h1
    hn2 = _mod_rmsnorm(h1, n2g, mod[4:5], mod[3:4])
    hn_hi = hn2.astype(BF16)
    bits = pltpu.bitcast(hn_hi.astype(F32), I32)
    half = bits.shape[1] // 2
    hn_out[...] = bits[:, :half] | lax.shift_right_logical(bits[:, half:], 16)
    hn_lo = (hn2 - hn_hi.astype(F32)).astype(BF16)
    rw_hi = rw.astype(BF16)
    rw_lo = (rw - rw_hi.astype(F32)).astype(BF16)
    lg_out[...] = (jnp.dot(hn_hi, rw_hi, preferred_element_type=F32)
                   + (jnp.dot(hn_lo, rw_hi, preferred_element_type=F32)
                      + jnp.dot(hn_hi, rw_lo, preferred_element_type=F32)))


def _gmlp_kernel(h_ref, mod_ref, n1g_ref, win_ref, lng_ref, lnb_ref, wsp_ref, bsp_ref, wout_ref,
                 n2g_ref, rw_ref, h_out, hn_out, lg_out, u_scr, v_scr, s_scr):
    tm = h_ref.shape[0]
    inner = u_scr.shape[1]
    gw = inner // A_GROUPS
    tn = 512
    h = h_ref[...]
    mod = mod_ref[0]
    hn = _mod_rmsnorm(h, n1g_ref[...], mod[1:2], mod[0:1]).astype(BF16)
    for j in range(2 * inner // tn):
        z = _gelu_tanh(jnp.dot(hn, win_ref[:, j * tn:(j + 1) * tn], preferred_element_type=F32))
        if j < inner // tn:
            u_scr[:, j * tn:(j + 1) * tn] = z
        else:
            jj = j - inner // tn
            v_scr[:, jj * tn:(jj + 1) * tn] = z
    vsum = jnp.zeros((tm, 1), F32)
    for j in range(inner // tn):
        vsum = vsum + jnp.sum(v_scr[:, j * tn:(j + 1) * tn], axis=-1, keepdims=True)
    mu = vsum * (1.0 / inner)
    vsq = jnp.zeros((tm, 1), F32)
    for j in range(inner // tn):
        d = v_scr[:, j * tn:(j + 1) * tn] - mu
        vsq = vsq + jnp.sum(d * d, axis=-1, keepdims=True)
    rstd = lax.rsqrt(vsq * (1.0 / inner) + EPS)
    row = lax.broadcasted_iota(I32, (A_CHUNK, A_CHUNK), 0)
    col = lax.broadcasted_iota(I32, (A_CHUNK, A_CHUNK), 1)
    tril = row >= col
    for g in range(A_GROUPS):
        ws = jnp.where(tril, wsp_ref[g], 0.0).astype(BF16)
        bcol = bsp_ref[:, g:g + 1]
        lg = lng_ref[:, g * gw:(g + 1) * gw]
        lb = lnb_ref[:, g * gw:(g + 1) * gw]
        for c in range(tm // A_CHUNK):
            r0, r1 = c * A_CHUNK, (c + 1) * A_CHUNK
            vt = v_scr[r0:r1, g * gw:(g + 1) * gw]
            vn = ((vt - mu[r0:r1]) * rstd[r0:r1]) * lg + lb
            fv = jnp.dot(ws, vn.astype(BF16), preferred_element_type=F32) + bcol
            s_scr[r0:r1, g * gw:(g + 1) * gw] = (u_scr[r0:r1, g * gw:(g + 1) * gw] * fv).astype(BF16)
    y = jnp.dot(s_scr[...], wout_ref[...], preferred_element_type=F32)
    _residual_epilogue(h, y, mod, n2g_ref[...], rw_ref[...], h_out, hn_out, lg_out)


def _const_spec(shape):
    nd = len(shape)
    return pl.BlockSpec(shape, lambda i, _nd=nd: (0,) * _nd, pipeline_mode=pl.Buffered(1))


def _gmlp_layer(h, mod_l, n1g, w_in, ln_g, ln_b, w_sp, b_sp, w_out, n2g, router_w, seq):
    n, d = h.shape
    inner = w_out.shape[0]
    tm = GMLP_ROWS
    per_b = seq // tm
    ne = router_w.shape[1]
    return pl.pallas_call(
        _gmlp_kernel,
        out_shape=(jax.ShapeDtypeStruct((n, d), F32), jax.ShapeDtypeStruct((n, d // 2), I32),
                   jax.ShapeDtypeStruct((n, ne), F32)),
        grid=(n // tm,),
        in_specs=[
            pl.BlockSpec((tm, d), lambda i: (i, 0)),
            pl.BlockSpec((1, 6, d), lambda i: (i // per_b, 0, 0)),
            _const_spec((1, d)),
            _const_spec((d, 2 * inner)),
            _const_spec((1, inner)),
            _const_spec((1, inner)),
            _const_spec((A_GROUPS, A_CHUNK, A_CHUNK)),
            _const_spec((A_CHUNK, A_GROUPS)),
            _const_spec((inner, d)),
            _const_spec((1, d)),
            _const_spec((d, ne)),
        ],
        out_specs=(pl.BlockSpec((tm, d), lambda i: (i, 0)), pl.BlockSpec((tm, d // 2), lambda i: (i, 0)),
                   pl.BlockSpec((tm, ne), lambda i: (i, 0))),
        scratch_shapes=[pltpu.VMEM((tm, inner), F32), pltpu.VMEM((tm, inner), F32),
                        pltpu.VMEM((tm, inner), BF16)],
        compiler_params=_cparams(("arbitrary",)),
        name="gmlp_layer",
    )(h, mod_l, n1g.reshape(1, d), w_in.astype(BF16), ln_g.reshape(1, inner), ln_b.reshape(1, inner),
      w_sp, b_sp.T, w_out.astype(BF16), n2g.reshape(1, d), router_w)


def _first_max4(rows):
    m = jnp.maximum(jnp.maximum(rows[0], rows[1]), jnp.maximum(rows[2], rows[3]))
    idx = jnp.where(rows[0] == m, 0, jnp.where(rows[1] == m, 1, jnp.where(rows[2] == m, 2, 3)))
    return m, idx.astype(I32)


def _route_kernel(lt_ref, rb_ref, e_ref, g_ref):
    aff = jax.nn.sigmoid(lt_ref[...])
    sel = aff + rb_ref[...]
    neg = jnp.float32(-jnp.inf)
    g_score, g_i1, g_i2 = [], [], []
    for g in range(N_GROUPS):
        rows = [sel[4 * g + k:4 * g + k + 1, :] for k in range(EXPERTS_PER_GROUP)]
        m1, i1 = _first_max4(rows)
        rest = [jnp.where(i1 == k, neg, rows[k]) for k in range(EXPERTS_PER_GROUP)]
        m2, i2 = _first_max4(rest)
        g_score.append(m1 + m2)
        g_i1.append(i1)
        g_i2.append(i2)
    _, grp = _first_max4(g_score)
    l1 = jnp.zeros_like(grp)
    l2 = jnp.zeros_like(grp)
    for g in range(N_GROUPS):
        l1 = jnp.where(grp == g, g_i1[g], l1)
        l2 = jnp.where(grp == g, g_i2[g], l2)
    e1 = grp * EXPERTS_PER_GROUP + l1
    e2 = grp * EXPERTS_PER_GROUP + l2
    a1 = jnp.zeros_like(g_score[0])
    a2 = jnp.zeros_like(g_score[0])
    for e in range(N_EXPERTS):
        ae = aff[e:e + 1, :]
        a1 = jnp.where(e1 == e, ae, a1)
        a2 = jnp.where(e2 == e, ae, a2)
    tot = a1 + a2
    e_ref[0:1, :] = e1
    e_ref[1:2, :] = e2
    g_ref[0:1, :] = a1 / tot
    g_ref[1:2, :] = a2 / tot


def _route(logits, router_b):
    n, ne = logits.shape[0], N_EXPERTS
    tn = min(2048, n)
    return pl.pallas_call(
        _route_kernel,
        out_shape=(jax.ShapeDtypeStruct((TOP_K, n), I32), jax.ShapeDtypeStruct((TOP_K, n), F32)),
        grid=(n // tn,),
        in_specs=[pl.BlockSpec((ne, tn), lambda i: (0, i)), pl.BlockSpec((ne, 1), lambda i: (0, 0))],
        out_specs=(pl.BlockSpec((TOP_K, tn), lambda i: (0, i)), pl.BlockSpec((TOP_K, tn), lambda i: (0, i))),
        compiler_params=_cparams(("arbitrary",)),
        name="route",
    )(logits[:, :ne].T, router_b.reshape(ne, 1))


def _slot_kernel(e_ref, slot_ref, cnt_ref):
    n = e_ref.shape[1]
    nblk = n // 128
    eid = lax.broadcasted_iota(I32, (N_EXPERTS, 128), 0)
    r = lax.broadcasted_iota(I32, (128, 128), 0)
    c = lax.broadcasted_iota(I32, (128, 128), 1)
    upper = jnp.where(r <= c, 1.0, 0.0).astype(BF16)

    def hits(b):
        o = pl.multiple_of(b * 128, 128)
        t1 = e_ref[0:1, pl.ds(o, 128)] == eid
        t2 = e_ref[1:2, pl.ds(o, 128)] == eid
        return o, t1, t2, jnp.where(t1, 1.0, 0.0) + jnp.where(t2, 1.0, 0.0)

    def count_body(b, acc):
        return acc + hits(b)[3]

    cnt = jnp.sum(lax.fori_loop(0, nblk, count_body, jnp.zeros((N_EXPERTS, 128), F32)),
                  axis=-1, keepdims=True)
    cnt_ref[...] = cnt.astype(I32)
    padded = jnp.zeros_like(cnt)
    for k in range(pl.cdiv(n * TOP_K, MOE_ROWS)):
        padded = padded + jnp.where(cnt > k * MOE_ROWS, float(MOE_ROWS), 0.0)
    er = lax.broadcasted_iota(I32, (N_EXPERTS, N_EXPERTS), 0)
    ec = lax.broadcasted_iota(I32, (N_EXPERTS, N_EXPERTS), 1)
    padded_row = jnp.sum(jnp.where(er == ec, padded, 0.0), axis=0, keepdims=True)
    start = jnp.sum(jnp.where(ec < er, padded_row, 0.0), axis=-1, keepdims=True)

    def slot_body(b, carry):
        o, t1, t2, t = hits(b)
        incl = jnp.dot(t.astype(BF16), upper, preferred_element_type=F32)
        pos = carry + incl - t
        slot_ref[0:1, pl.ds(o, 128)] = jnp.sum(jnp.where(t1, pos, 0.0), axis=0, keepdims=True).astype(I32)
        slot_ref[1:2, pl.ds(o, 128)] = jnp.sum(jnp.where(t2, pos, 0.0), axis=0, keepdims=True).astype(I32)
        return carry + incl[:, 127:128]

    lax.fori_loop(0, nblk, slot_body, start)


def _slot_tables(experts):
    n = experts.shape[1]
    a = n * TOP_K
    slot, counts = pl.pallas_call(
        _slot_kernel,
        out_shape=(jax.ShapeDtypeStruct((TOP_K, n), I32), jax.ShapeDtypeStruct((N_EXPERTS, 1), I32)),
        compiler_params=pltpu.CompilerParams(vmem_limit_bytes=VMEM_LIMIT),
        name="slots",
    )(experts)
    counts = counts.reshape(N_EXPERTS)
    padded = ((counts + MOE_ROWS - 1) // MOE_ROWS) * MOE_ROWS
    pad_end = jnp.cumsum(padded)
    pad_start = pad_end - padded
    n_items = pl.cdiv(a, MOE_ROWS) + N_EXPERTS
    p = n_items * MOE_ROWS
    tok = jnp.broadcast_to(jnp.arange(n, dtype=I32)[None, :], (TOP_K, n))
    tok_of_slot = jnp.zeros((p,), I32).at[slot.reshape(a)].set(tok.reshape(a))
    slot_of_assign = slot.T
    item_start = jnp.arange(n_items, dtype=I32) * MOE_ROWS
    n_used = pad_end[-1] // MOE_ROWS
    item_e_raw = jnp.minimum(jnp.searchsorted(pad_end, item_start, side='right'), N_EXPERTS - 1).astype(I32)
    used = item_start < pad_end[-1]
    last_e = item_e_raw[jnp.maximum(n_used - 1, 0)]
    item_e = jnp.where(used, item_e_raw, last_e).astype(I32)
    valid = jnp.clip(counts[item_e_raw] - (item_start - pad_start[item_e_raw]), 0, MOE_ROWS)
    item_rows = jnp.where(used, valid, 0).astype(I32)
    item_blk = jnp.where(used, jnp.arange(n_items, dtype=I32), jnp.maximum(n_used - 1, 0)).astype(I32)
    return tok_of_slot, slot_of_assign, item_e, item_rows, item_blk


def _moe_kernel(ie_ref, ir_ref, ib_ref, x_ref, wg_ref, wu_ref, wd_ref, o_ref, wg_s, wu_s, wd_s, x_s):
    it = pl.program_id(0)
    f = pl.program_id(1)
    nrows = ir_ref[it]
    half = x_ref.shape[1]

    @pl.when(f == 0)
    def _():
        o_ref[...] = jnp.zeros_like(o_ref)
        pk = x_ref[...]
        x_s[:, :half] = pltpu.bitcast(pk & jnp.int32(-65536), F32).astype(BF16)
        x_s[:, half:] = pltpu.bitcast(lax.shift_left(pk, 16), F32).astype(BF16)

    @pl.when(nrows > 0)
    def _():
        wg_s[...] = wg_ref[0, 0].astype(BF16)
        wu_s[...] = wu_ref[0, 0].astype(BF16)
        wd_s[...] = wd_ref[0, 0].astype(BF16)
        nsb = (nrows + MOE_SUB - 1) // MOE_SUB

        def sub_block(sb):
            r0 = pl.multiple_of(sb * MOE_SUB, MOE_SUB)
            x = x_s[pl.ds(r0, MOE_SUB), :]
            g = jnp.dot(x, wg_s[...], preferred_element_type=F32)
            u = jnp.dot(x, wu_s[...], preferred_element_type=F32)
            hmid = ((g * jax.nn.sigmoid(g)) * u).astype(BF16)
            o_ref[pl.ds(r0, MOE_SUB), :] += jnp.dot(hmid, wd_s[...], preferred_element_type=F32)

        def pair_body(pi, carry):
            sub_block(2 * pi)
            sub_block(2 * pi + 1)
            return carry

        lax.fori_loop(0, nsb // 2, pair_body, 0)

        @pl.when(nsb % 2 == 1)
        def _():
            sub_block(nsb - 1)


def _moe_experts(x_sorted, item_e, item_rows, item_blk, w_gate, w_up, w_down, layer):
    p, half = x_sorted.shape
    d = 2 * half
    n_items = p // MOE_ROWS
    dff = w_gate.shape[3]
    nf = dff // MOE_FT
    grid_spec = pltpu.PrefetchScalarGridSpec(
        num_scalar_prefetch=3,
        grid=(n_items, nf),
        in_specs=[
            pl.BlockSpec((MOE_ROWS, half), lambda i, f, ie, ir, ib: (ib[i], 0)),
            pl.BlockSpec((1, 1, d, MOE_FT),
                         lambda i, f, ie, ir, ib: (layer, ie[i], 0, jnp.where(ir[i] > 0, f, nf - 1))),
            pl.BlockSpec((1, 1, d, MOE_FT),
                         lambda i, f, ie, ir, ib: (layer, ie[i], 0, jnp.where(ir[i] > 0, f, nf - 1))),
            pl.BlockSpec((1, 1, MOE_FT, d),
                         lambda i, f, ie, ir, ib: (layer, ie[i], jnp.where(ir[i] > 0, f, nf - 1), 0)),
        ],
        out_specs=pl.BlockSpec((MOE_ROWS, d), lambda i, f, ie, ir, ib: (i, 0)),
        scratch_shapes=[pltpu.VMEM((d, MOE_FT), BF16), pltpu.VMEM((d, MOE_FT), BF16),
                        pltpu.VMEM((MOE_FT, d), BF16), pltpu.VMEM((MOE_ROWS, d), BF16)],
    )
    return pl.pallas_call(
        _moe_kernel,
        out_shape=jax.ShapeDtypeStruct((p, d), F32),
        grid_spec=grid_spec,
        compiler_params=_cparams(("arbitrary", "arbitrary")),
        name="moe_experts",
    )(item_e, item_rows, item_blk, x_sorted, w_gate, w_up, w_down)


def _moe(hn2, logits, router_b, w_gate, w_up, w_down, layer):
    experts, gates = _route(logits, router_b)
    tok_of_slot, slot_of_assign, item_e, item_rows, item_blk = _slot_tables(experts)
    x_sorted = hn2.at[tok_of_slot].get(mode='promise_in_bounds')
    y_slot = _moe_experts(x_sorted, item_e, item_rows, item_blk, w_gate, w_up, w_down, layer)
    y0 = y_slot.at[slot_of_assign[:, 0]].get(mode='promise_in_bounds')
    y1 = y_slot.at[slot_of_assign[:, 1]].get(mode='promise_in_bounds')
    n = hn2.shape[0]
    gates_rep = jnp.concatenate([jnp.broadcast_to(gates[k][:, None], (n, 128)) for k in range(TOP_K)], axis=1)
    return y0, y1, gates_rep


def _moe_combine(h1, y0, y1, gates, mod):
    reps = h1.shape[1] // 128
    g0 = _lane_tile(gates[:, :128], reps)
    g1 = _lane_tile(gates[:, 128:], reps)
    return h1 + mod[5:6] * (g0 * y0 + g1 * y1)


def _bproj_kernel(h1_ref, y0_ref, y1_ref, gt_ref, modp_ref, mod_ref, n1g_ref, w_ref, kvg_ref, kig_ref,
                  wuk_ref, h_out, qa_out, ckv_out, qi_out, ki_out, wi_out):
    tm = h1_ref.shape[0]
    h = _moe_combine(h1_ref[...], y0_ref[...], y1_ref[...], gt_ref[...], modp_ref[0])
    h_out[...] = h
    mod = mod_ref[0]
    hn = _mod_rmsnorm(h, n1g_ref[...], mod[1:2], mod[0:1]).astype(BF16)
    proj = jnp.dot(hn, w_ref[...], preferred_element_type=F32)
    o1 = B_HEADS * B_HEAD_DIM
    o2 = o1 + B_KV_LATENT
    o3 = o2 + B_IDX_HEADS * B_IDX_DIM
    ckv = proj[:, o1:o2]
    ckv = ckv * lax.rsqrt(jnp.mean(ckv * ckv, axis=-1, keepdims=True) + EPS) * kvg_ref[...]
    ckv_out[...] = ckv.astype(BF16)
    tail = proj[:, o3:o3 + 128]
    ki = tail[:, :B_IDX_DIM]
    ki = ki * lax.rsqrt(jnp.mean(ki * ki, axis=-1, keepdims=True) + EPS) * kig_ref[...]
    ki_out[...] = ki.astype(BF16)
    wi = tail[:, B_IDX_DIM:B_IDX_DIM + B_IDX_HEADS] * (B_IDX_HEADS ** -0.5 * B_IDX_DIM ** -0.5)
    scale = B_HEAD_DIM ** -0.5
    for blk in range(tm // QBLK):
        r0, r1 = blk * QBLK, (blk + 1) * QBLK
        for hh in range(B_HEADS):
            qh = proj[r0:r1, hh * B_HEAD_DIM:(hh + 1) * B_HEAD_DIM].astype(BF16)
            qa = jnp.dot(qh, wuk_ref[hh], preferred_element_type=F32) * scale
            qa_out[blk, hh * QBLK:(hh + 1) * QBLK, :] = qa.astype(BF16)
        for hh in range(B_IDX_HEADS):
            qi_out[blk, hh * QBLK:(hh + 1) * QBLK, :] = proj[r0:r1, o2 + hh * B_IDX_DIM:
                                                            o2 + (hh + 1) * B_IDX_DIM].astype(BF16)
            wi_out[blk, hh * QBLK:(hh + 1) * QBLK, :] = jnp.broadcast_to(wi[r0:r1, hh:hh + 1], (QBLK, 128))


def _bproj(h1, y0, y1, gates, mod_prev, mod_l, n1g, b_w_in, kv_g, w_uk, kidx_g, seq):
    n, d = h1.shape
    tm = ROW_TILE
    per_b = seq // tm
    nq = n // QBLK
    qpb = tm // QBLK
    o1 = B_HEADS * B_HEAD_DIM
    o2 = o1 + B_KV_LATENT
    o3 = o2 + B_IDX_HEADS * B_IDX_DIM
    wcat = jnp.zeros((d, o3 + 128), F32).at[:, :b_w_in.shape[1]].set(b_w_in).astype(BF16)
    wuk_t = jnp.transpose(w_uk, (1, 2, 0)).astype(BF16)
    row = lambda i: (i, 0)
    modm = lambda i: (i // per_b, 0, 0)
    blk3 = lambda i: (i, 0, 0)
    return pl.pallas_call(
        _bproj_kernel,
        out_shape=(
            jax.ShapeDtypeStruct((n, d), F32),
            jax.ShapeDtypeStruct((nq, B_HEADS * QBLK, B_KV_LATENT), BF16),
            jax.ShapeDtypeStruct((n, B_KV_LATENT), BF16),
            jax.ShapeDtypeStruct((nq, B_IDX_HEADS * QBLK, B_IDX_DIM), BF16),
            jax.ShapeDtypeStruct((n, B_IDX_DIM), BF16),
            jax.ShapeDtypeStruct((nq, B_IDX_HEADS * QBLK, 128), F32),
        ),
        grid=(n // tm,),
        in_specs=[
            pl.BlockSpec((tm, d), row), pl.BlockSpec((tm, d), row), pl.BlockSpec((tm, d), row),
            pl.BlockSpec((tm, TOP_K * 128), row),
            pl.BlockSpec((1, 6, d), modm), pl.BlockSpec((1, 6, d), modm),
            _const_spec((1, d)),
            _const_spec((d, o3 + 128)),
            _const_spec((1, B_KV_LATENT)),
            _const_spec((1, B_IDX_DIM)),
            _const_spec((B_HEADS, B_HEAD_DIM, B_KV_LATENT)),
        ],
        out_specs=(
            pl.BlockSpec((tm, d), row),
            pl.BlockSpec((qpb, B_HEADS * QBLK, B_KV_LATENT), blk3),
            pl.BlockSpec((tm, B_KV_LATENT), row),
            pl.BlockSpec((qpb, B_IDX_HEADS * QBLK, B_IDX_DIM), blk3),
            pl.BlockSpec((tm, B_IDX_DIM), row),
            pl.BlockSpec((qpb, B_IDX_HEADS * QBLK, 128), blk3),
        ),
        compiler_params=_cparams(("arbitrary",)),
        name="dsa_proj",
    )(h1, y0, y1, gates, mod_prev, mod_l, n1g.reshape(1, d), wcat, kv_g.reshape(1, -1),
      kidx_g.reshape(1, -1), wuk_t)


def _t5_bucket(dist):
    n = jnp.maximum(dist, 0)
    exact = REL_BUCKETS // 2
    nf = jnp.maximum(n, 1).astype(F32)
    large = exact + (jnp.log(nf / exact) / math.log(REL_MAX_DIST / exact)
                     * (REL_BUCKETS - exact)).astype(I32)
    large = jnp.minimum(large, REL_BUCKETS - 1)
    return jnp.where(n < exact, n, large)


def _bias_tables(rel_bias):
    assert REL_MAX_DIST <= 128
    t = jnp.arange(128, dtype=I32)[:, None]
    s = jnp.arange(128, dtype=I32)[None, :]
    far = rel_bias[REL_BUCKETS - 1]
    diag = rel_bias[_t5_bucket(t - s)] - far
    prev = rel_bias[_t5_bucket(t - s + 128)] - far
    return jnp.stack([jnp.transpose(diag, (2, 0, 1)), jnp.transpose(prev, (2, 0, 1))])


def _attn_kernel(qa_ref, qi_ref, wi_ref, ckv_ref, ki_ref, bt_ref, o_ref, kbuf, m_scr, l_scr, a_scr, acc_scr,
                 s_scr, p_scr, madd_scr, kbuft, tau_scr):
    i = pl.program_id(1)
    nt = i + 1
    t_row = i * QBLK + lax.broadcasted_iota(I32, (QBLK, KTILE), 0)
    lane = lax.broadcasted_iota(I32, (QBLK, KTILE), 1)

    qi = qi_ref[0]

    def score_body(j, carry):
        k0 = pl.multiple_of(j * KTILE, KTILE)
        kt = ki_ref[0, pl.ds(k0, KTILE), :]
        p = lax.dot_general(qi, kt, (((1,), (1,)), ((), ())), preferred_element_type=F32)
        p = jnp.maximum(p, 0.0) * _lane_tile(wi_ref[0], KTILE // 128)
        sc = p[0:QBLK]
        for hh in range(1, B_IDX_HEADS):
            sc = sc + p[hh * QBLK:(hh + 1) * QBLK]
        bits = pltpu.bitcast(sc + 0.0, I32)
        key = jnp.where(bits < 0, bits ^ jnp.int32(0x7FFFFFFF), bits)
        key = jnp.where(k0 + lane <= t_row, key, jnp.int32(INT_MIN))
        kbuf[:, pl.ds(k0, KTILE)] = key
        kbuft[pl.ds(k0, KTILE), :] = key.T
        return carry

    lax.fori_loop(0, nt, score_body, 0)

    def fold_lanes(x):
        out = x[:, 0:128]
        for k in range(1, KTILE // 128):
            out = out + x[:, k * 128:(k + 1) * 128]
        return out

    def count_ge(cand):
        def body(j, acc):
            k0 = pl.multiple_of(j * KTILE, KTILE)
            return acc + fold_lanes(jnp.where(kbuf[:, pl.ds(k0, KTILE)] >= cand, 1, 0))
        acc = lax.fori_loop(0, nt, body, jnp.zeros((QBLK, 128), I32))
        return jnp.sum(acc, axis=-1, keepdims=True)

    def count_ge_t(cand_row):
        def body(j, acc):
            k0 = pl.multiple_of(j * KTILE, KTILE)
            hit = jnp.where(kbuft[pl.ds(k0, KTILE), :] >= cand_row, 1, 0)
            return acc + jnp.sum(hit.reshape(KTILE // 8, 8, QBLK), axis=0)
        acc = lax.fori_loop(0, nt, body, jnp.zeros((8, QBLK), I32))
        return jnp.sum(acc, axis=0, keepdims=True)

    def bit_step(b, lo, n_ge):
        cand = lo + lax.shift_left(jnp.int32(1), 31 - b)
        cnt = count_ge_t(cand)
        take = cnt >= B_TOPK_MAX
        return jnp.where(take, cand, lo), jnp.where(take, cnt, n_ge)

    def bit_cond(c):
        return (c[0] < 32) & (c[3] == 0)

    def bit_body(c):
        b, lo, n_ge, _ = c
        lo, n_ge = bit_step(b, lo, n_ge)
        lo, n_ge = bit_step(b + 1, lo, n_ge)
        done = (jnp.max(n_ge) == B_TOPK_MAX).astype(I32)
        return b + 2, lo, n_ge, done

    _, tau_row, n_ge_row, _ = lax.while_loop(
        bit_cond, bit_body,
        (jnp.int32(0), jnp.full((1, QBLK), INT_MIN, I32), jnp.full((1, QBLK), nt * KTILE, I32),
         (i < 1).astype(I32)))
    tau_row = jnp.maximum(tau_row, jnp.int32(INT_MIN + 1))
    excess = (i >= 1) & (jnp.max(n_ge_row) > B_TOPK_MAX)

    def to_rows(row):
        return jnp.broadcast_to(row, (QBLK, QBLK)).T

    tau_scr[...] = to_rows(tau_row)

    @pl.when(excess)
    def _():
        tau = tau_scr[:, 0:1]
        n_ge = to_rows(n_ge_row)[:, 0:1]
        n_gt = count_ge(tau + 1)
        need = B_TOPK_MAX - n_gt

        def count_eq_before(pos):
            def body(j, acc):
                k0 = pl.multiple_of(j * KTILE, KTILE)
                hit = (kbuf[:, pl.ds(k0, KTILE)] == tau) & (k0 + lane < pos)
                return acc + fold_lanes(jnp.where(hit, 1, 0))
            acc = lax.fori_loop(0, nt, body, jnp.zeros((QBLK, 128), I32))
            return jnp.sum(acc, axis=-1, keepdims=True)

        def pos_body(b, pos):
            cand = pos + lax.shift_left(jnp.int32(1), 12 - b)
            return jnp.where(count_eq_before(cand) < need, cand, pos)

        pos = lax.fori_loop(0, 13, pos_body, jnp.zeros((QBLK, 1), I32))

        def drop_body(j, carry):
            k0 = pl.multiple_of(j * KTILE, KTILE)
            kk = kbuf[:, pl.ds(k0, KTILE)]
            drop = (kk == tau) & (k0 + lane > pos) & (n_ge > B_TOPK_MAX)
            kbuf[:, pl.ds(k0, KTILE)] = jnp.where(drop, jnp.int32(INT_MIN), kk)
            return carry

        lax.fori_loop(0, nt, drop_body, 0)

    m_scr[...] = jnp.full_like(m_scr, NEG_BIG)
    l_scr[...] = jnp.zeros_like(l_scr)
    acc_scr[...] = jnp.zeros_like(acc_scr)

    def attend(k0, width, tile_off):
        kv = ckv_ref[0, pl.ds(k0, width), :]
        s_scr[:, :width] = lax.dot_general(qa_ref[0], kv, (((1,), (1,)), ((), ())),
                                           preferred_element_type=F32)
        tau_t = _lane_tile(tau_scr[...], width // QBLK)
        madd_scr[:, :width] = jnp.where(kbuf[:, pl.ds(k0, width)] >= tau_t, 0.0, NEG_BIG)
        for r0 in range(0, B_HEADS * QBLK, SM_ROWS):
            r1 = r0 + SM_ROWS
            hh, q0 = r0 // QBLK, r0 % QBLK
            parts = []
            for kc in range(width // 128):
                sp = s_scr[r0:r1, kc * 128:(kc + 1) * 128] + madd_scr[q0:q0 + SM_ROWS, kc * 128:(kc + 1) * 128]
                if tile_off is not None:
                    rel = (QBLK // 128) * tile_off + q0 // 128 - kc
                    if rel in (0, 1):
                        sp = sp + bt_ref[rel, hh, q0 % 128:q0 % 128 + SM_ROWS, :]
                parts.append(sp)
            smax = parts[0]
            for sp in parts[1:]:
                smax = jnp.maximum(smax, sp)
            m_old = m_scr[r0:r1, :]
            m_new = jnp.maximum(m_old, jnp.max(smax, axis=-1, keepdims=True))
            alpha = jnp.exp(m_old - m_new)
            psum = None
            for kc, sp in enumerate(parts):
                p = jnp.exp(sp - m_new)
                p_scr[r0:r1, kc * 128:(kc + 1) * 128] = p.astype(BF16)
                psum = p if psum is None else psum + p
            l_scr[r0:r1, :] = alpha * l_scr[r0:r1, :] + jnp.sum(psum, axis=-1, keepdims=True)
            m_scr[r0:r1, :] = m_new
            a_scr[r0:r1, :] = alpha
        pv = jnp.dot(p_scr[:, :width], kv, preferred_element_type=F32)
        acc_scr[...] = _lane_tile(a_scr[...], B_KV_LATENT // 128) * acc_scr[...] + pv

    n_far = jnp.maximum(i - 1, 0)
    per_far = FAR_TILE // KTILE

    def far_body(jf, carry):
        attend(pl.multiple_of(jf * FAR_TILE, FAR_TILE), FAR_TILE, None)
        return carry

    lax.fori_loop(0, n_far // per_far, far_body, 0)

    if per_far > 1:
        @pl.when(n_far % per_far == 1)
        def _():
            attend(pl.multiple_of((n_far - 1) * KTILE, KTILE), KTILE, None)

    @pl.when(i >= 1)
    def _():
        attend(pl.multiple_of((i - 1) * KTILE, KTILE), KTILE, 1)

    attend(pl.multiple_of(i * KTILE, KTILE), KTILE, 0)
    inv_l = 1.0 / l_scr[...]
    o_ref[0] = (acc_scr[...] * _lane_tile(inv_l, B_KV_LATENT // 128)).astype(BF16)


def _attention(qa, qi, wi, ckv, ki, btab, bsz, seq):
    nqb = seq // QBLK
    gq = lambda b, i: (b * nqb + i, 0, 0)
    gb = lambda b, i: (b, 0, 0)
    return pl.pallas_call(
        _attn_kernel,
        out_shape=jax.ShapeDtypeStruct(qa.shape, BF16),
        grid=(bsz, nqb),
        in_specs=[
            pl.BlockSpec((1, B_HEADS * QBLK, B_KV_LATENT), gq),
            pl.BlockSpec((1, B_IDX_HEADS * QBLK, B_IDX_DIM), gq),
            pl.BlockSpec((1, B_IDX_HEADS * QBLK, 128), gq),
            pl.BlockSpec((1, seq, B_KV_LATENT), gb, pipeline_mode=pl.Buffered(1)),
            pl.BlockSpec((1, seq, B_IDX_DIM), gb, pipeline_mode=pl.Buffered(1)),
            pl.BlockSpec((2, B_HEADS, 128, 128), lambda b, i: (0, 0, 0, 0), pipeline_mode=pl.Buffered(1)),
        ],
        out_specs=pl.BlockSpec((1, B_HEADS * QBLK, B_KV_LATENT), gq),
        scratch_shapes=[
            pltpu.VMEM((QBLK, seq), I32),
            pltpu.VMEM((B_HEADS * QBLK, 128), F32),
            pltpu.VMEM((B_HEADS * QBLK, 128), F32),
            pltpu.VMEM((B_HEADS * QBLK, 128), F32),
            pltpu.VMEM((B_HEADS * QBLK, B_KV_LATENT), F32),
            pltpu.VMEM((B_HEADS * QBLK, FAR_TILE), F32),
            pltpu.VMEM((B_HEADS * QBLK, FAR_TILE), BF16),
            pltpu.VMEM((QBLK, FAR_TILE), F32),
            pltpu.VMEM((seq, QBLK), I32),
            pltpu.VMEM((QBLK, QBLK), I32),
        ],
        compiler_params=_cparams(("arbitrary", "arbitrary")),
        name="dsa_attention",
    )(qa, qi, wi, ckv.reshape(bsz, seq, -1), ki.reshape(bsz, seq, -1), btab)


def _bout_kernel(h_ref, ol_ref, mod_ref, wuv_ref, wout_ref, n2g_ref, rw_ref, h_out, hn_out, lg_out, o_scr):
    mod = mod_ref[0]
    for blk in range(ol_ref.shape[0]):
        for hh in range(B_HEADS):
            oh = jnp.dot(ol_ref[blk, hh * QBLK:(hh + 1) * QBLK, :], wuv_ref[hh], preferred_element_type=F32)
            o_scr[blk * QBLK:(blk + 1) * QBLK, hh * B_V_DIM:(hh + 1) * B_V_DIM] = oh.astype(BF16)
    y = jnp.dot(o_scr[...], wout_ref[...], preferred_element_type=F32)
    _residual_epilogue(h_ref[...], y, mod, n2g_ref[...], rw_ref[...], h_out, hn_out, lg_out)


def _bout(h, o_lat, mod_l, w_uv, w_out, n2g, router_w, seq):
    n, d = h.shape
    tm = ROW_TILE
    per_b = seq // tm
    qpb = tm // QBLK
    ne = router_w.shape[1]
    wuv_t = jnp.transpose(w_uv, (1, 0, 2)).astype(BF16)
    row = lambda i: (i, 0)
    return pl.pallas_call(
        _bout_kernel,
        out_shape=(jax.ShapeDtypeStruct((n, d), F32), jax.ShapeDtypeStruct((n, d // 2), I32),
                   jax.ShapeDtypeStruct((n, ne), F32)),
        grid=(n // tm,),
        in_specs=[
            pl.BlockSpec((tm, d), row),
            pl.BlockSpec((qpb, B_HEADS * QBLK, B_KV_LATENT), lambda i: (i, 0, 0)),
            pl.BlockSpec((1, 6, d), lambda i: (i // per_b, 0, 0)),
            _const_spec((B_HEADS, B_KV_LATENT, B_V_DIM)),
            _const_spec((B_HEADS * B_V_DIM, d)),
            _const_spec((1, d)),
            _const_spec((d, ne)),
        ],
        out_specs=(pl.BlockSpec((tm, d), row), pl.BlockSpec((tm, d // 2), row), pl.BlockSpec((tm, ne), row)),
        scratch_shapes=[pltpu.VMEM((tm, B_HEADS * B_V_DIM), BF16)],
        compiler_params=_cparams(("arbitrary",)),
        name="dsa_out",
    )(h, o_lat, mod_l, wuv_t, w_out.astype(BF16), n2g.reshape(1, d), router_w)


def _final_kernel(h1_ref, y0_ref, y1_ref, gt_ref, mod_ref, g_ref, o_ref):
    h = _moe_combine(h1_ref[...], y0_ref[...], y1_ref[...], gt_ref[...], mod_ref[0])
    o_ref[...] = h * lax.rsqrt(jnp.mean(h * h, axis=-1, keepdims=True) + EPS) * g_ref[...]


def _final(h1, y0, y1, gates, mod_l, final_g, seq):
    n, d = h1.shape
    tm = 512
    per_b = seq // tm
    row = lambda i: (i, 0)
    return pl.pallas_call(
        _final_kernel,
        out_shape=jax.ShapeDtypeStruct((n, d), F32),
        grid=(n // tm,),
        in_specs=[pl.BlockSpec((tm, d), row), pl.BlockSpec((tm, d), row), pl.BlockSpec((tm, d), row),
                  pl.BlockSpec((tm, TOP_K * 128), row), pl.BlockSpec((1, 6, d), lambda i: (i // per_b, 0, 0)),
                  _const_spec((1, d))],
        out_specs=pl.BlockSpec((tm, d), row),
        compiler_params=_cparams(("arbitrary",)),
        name="final_norm",
    )(h1, y0, y1, gates, mod_l, final_g.reshape(1, d))


def kernel(x, c, ada_w, ada_b, norm1_g, norm2_g, a_w_in, a_ln_g, a_ln_b, a_w_sp, a_b_sp, a_w_out, b_w_in,
           b_kv_norm_g, b_w_uk, b_w_uv, b_kidx_g, b_w_out, rel_bias, router_w, router_b, moe_w_gate,
           moe_w_up, moe_w_down, final_g):
    bsz, seq, d = x.shape
    n = bsz * seq
    mod = _adaln(c, ada_w, ada_b).reshape(ada_w.shape[0], bsz, 6, d)
    rw_pad = jnp.zeros((d, 128), F32).at[:, :N_EXPERTS].set(router_w)
    h = x.reshape(n, d)

    h1, hn2, logits = _gmlp_layer(h, mod[0], norm1_g[0], a_w_in[0], a_ln_g[0], a_ln_b[0], a_w_sp[0],
                                  a_b_sp[0], a_w_out[0], norm2_g[0], rw_pad, seq)
    y0, y1, gates = _moe(hn2, logits, router_b, moe_w_gate, moe_w_up, moe_w_down, 0)

    h, qa, ckv, qi, ki, wi = _bproj(h1, y0, y1, gates, mod[0], mod[1], norm1_g[1], b_w_in[0],
                                    b_kv_norm_g[0], b_w_uk[0], b_kidx_g[0], seq)
    o_lat = _attention(qa, qi, wi, ckv, ki, _bias_tables(rel_bias), bsz, seq)
    h1, hn2, logits = _bout(h, o_lat, mod[1], b_w_uv[0], b_w_out[0], norm2_g[1], rw_pad, seq)
    y0, y1, gates = _moe(hn2, logits, router_b, moe_w_gate, moe_w_up, moe_w_down, 1)

    out = _final(h1, y0, y1, gates, mod[1], final_g, seq)
    return out.reshape(bsz, seq, d)
```

```python
import functools
import math

import jax
import jax.numpy as jnp
from jax import lax
from jax.experimental import pallas as pl
from jax.experimental.pallas import tpu as pltpu

F32 = jnp.float32
BF16 = jnp.bfloat16
I32 = jnp.int32
HIGHEST = lax.Precision.HIGHEST

EPS = 1e-6
A_CHUNK = 128
A_GROUPS = 8
B_HEADS = 16
B_HEAD_DIM = 64
B_V_DIM = 64
B_KV_LATENT = 256
B_IDX_HEADS = 8
B_IDX_DIM = 64
B_TOPK_MAX = 256
QBLK = 256
KTILE = 256
FAR_TILE = 256
SM_ROWS = 128
REL_BUCKETS = 32
REL_MAX_DIST = 128
N_EXPERTS = 16
N_GROUPS = 4
EXPERTS_PER_GROUP = 4
TOP_K = 2
MOE_ROWS = 1280
MOE_SUB = 256
MOE_FT = 512

ROW_TILE = 256
GMLP_ROWS = 512
VMEM_LIMIT = 60 * 1024 * 1024

INT_MIN = -2 ** 31
NEG_BIG = -1e30


def _cparams(sem):
    return pltpu.CompilerParams(dimension_semantics=sem, vmem_limit_bytes=VMEM_LIMIT)


def _lane_tile(x, k):
    return x if k == 1 else jnp.concatenate([x] * k, axis=1)


def _mod_rmsnorm(h, g, scale, shift):
    ms = jnp.mean(h * h, axis=-1, keepdims=True)
    return (h * lax.rsqrt(ms + EPS) * g) * (1.0 + scale) + shift


def _gelu_tanh(x):
    c = math.sqrt(2.0 / math.pi)
    return 0.5 * x * (1.0 + jnp.tanh(c * (x + 0.044715 * (x * x * x))))


def _adaln_kernel(c_ref, w_ref, b_ref, o_ref):
    c = c_ref[...]
    sc = c * jax.nn.sigmoid(c)
    o_ref[0] = jnp.dot(sc, w_ref[0], precision=HIGHEST, preferred_element_type=F32) + b_ref[0]


def _adaln(c, ada_w, ada_b):
    depth, d, e = ada_w.shape
    bsz = c.shape[0]
    bp = 8
    c_pad = jnp.zeros((bp, d), F32).at[:bsz].set(c)
    tn = 1024
    out = pl.pallas_call(
        _adaln_kernel,
        out_shape=jax.ShapeDtypeStruct((depth, bp, e), F32),
        grid=(depth, e // tn),
        in_specs=[
            pl.BlockSpec((bp, d), lambda l, j: (0, 0)),
            pl.BlockSpec((1, d, tn), lambda l, j: (l, 0, j)),
            pl.BlockSpec((1, 1, tn), lambda l, j: (l, 0, j)),
        ],
        out_specs=pl.BlockSpec((1, bp, tn), lambda l, j: (l, 0, j)),
        compiler_params=_cparams(("arbitrary", "arbitrary")),
        name="adaln",
    )(c_pad, ada_w, ada_b.reshape(depth, 1, e))
    return out[:, :bsz]


def _residual_epilogue(h, y, mod, n2g, rw, h_out, hn_out, lg_out):
    h1 = h + mod[2:3] * y
    h_out[...] = h1
    hn2 = _mod_rmsnorm(h1, n2g, mod[4:5], mod[3:4])
    hn_hi = hn2.astype(BF16)
    bits = pltpu.bitcast(hn_hi.astype(F32), I32)
    half = bits.shape[1] // 2
    hn_out[...] = bits[:, :half] | lax.shift_right_logical(bits[:, half:], 16)
    hn_lo = (hn2 - hn_hi.astype(F32)).astype(BF16)
    rw_hi = rw.astype(BF16)
    rw_lo = (rw - rw_hi.astype(F32)).astype(BF16)
    lg_out[...] = (jnp.dot(hn_hi, rw_hi, preferred_element_type=F32)
                   + (jnp.dot(hn_lo, rw_hi, preferred_element_type=F32)
                      + jnp.dot(hn_hi, rw_lo, preferred_element_type=F32)))


def _gmlp_kernel(h_ref, mod_ref, n1g_ref, win_ref, lng_ref, lnb_ref, wsp_ref, bsp_ref, wout_ref,
                 n2g_ref, rw_ref, h_out, hn_out, lg_out, u_scr, v_scr, s_scr):
    tm = h_ref.shape[0]
    inner = u_scr.shape[1]
    gw = inner // A_GROUPS
    tn = 512
    h = h_ref[...]
    mod = mod_ref[0]
    hn = _mod_rmsnorm(h, n1g_ref[...], mod[1:2], mod[0:1]).astype(BF16)
    for j in range(2 * inner // tn):
        z = _gelu_tanh(jnp.dot(hn, win_ref[:, j * tn:(j + 1) * tn], preferred_element_type=F32))
        if j < inner // tn:
            u_scr[:, j * tn:(j + 1) * tn] = z
        else:
            jj = j - inner // tn
            v_scr[:, jj * tn:(jj + 1) * tn] = z
    vsum = jnp.zeros((tm, 1), F32)
    for j in range(inner // tn):
        vsum = vsum + jnp.sum(v_scr[:, j * tn:(j + 1) * tn], axis=-1, keepdims=True)
    mu = vsum * (1.0 / inner)
    vsq = jnp.zeros((tm, 1), F32)
    for j in range(inner // tn):
        d = v_scr[:, j * tn:(j + 1) * tn] - mu
        vsq = vsq + jnp.sum(d * d, axis=-1, keepdims=True)
    rstd = lax.rsqrt(vsq * (1.0 / inner) + EPS)
    row = lax.broadcasted_iota(I32, (A_CHUNK, A_CHUNK), 0)
    col = lax.broadcasted_iota(I32, (A_CHUNK, A_CHUNK), 1)
    tril = row >= col
    for g in range(A_GROUPS):
        ws = jnp.where(tril, wsp_ref[g], 0.0).astype(BF16)
        bcol = bsp_ref[:, g:g + 1]
        lg = lng_ref[:, g * gw:(g + 1) * gw]
        lb = lnb_ref[:, g * gw:(g + 1) * gw]
        for c in range(tm // A_CHUNK):
            r0, r1 = c * A_CHUNK, (c + 1) * A_CHUNK
            vt = v_scr[r0:r1, g * gw:(g + 1) * gw]
            vn = ((vt - mu[r0:r1]) * rstd[r0:r1]) * lg + lb
            fv = jnp.dot(ws, vn.astype(BF16), preferred_element_type=F32) + bcol
            s_scr[r0:r1, g * gw:(g + 1) * gw] = (u_scr[r0:r1, g * gw:(g + 1) * gw] * fv).astype(BF16)
    y = jnp.dot(s_scr[...], wout_ref[...], preferred_element_type=F32)
    _residual_epilogue(h, y, mod, n2g_ref[...], rw_ref[...], h_out, hn_out, lg_out)


def _const_spec(shape):
    nd = len(shape)
    return pl.BlockSpec(shape, lambda i, _nd=nd: (0,) * _nd, pipeline_mode=pl.Buffered(1))


def _gmlp_layer(h, mod_l, n1g, w_in, ln_g, ln_b, w_sp, b_sp, w_out, n2g, router_w, seq):
    n, d = h.shape
    inner = w_out.shape[0]
    tm = GMLP_ROWS
    per_b = seq // tm
    ne = router_w.shape[1]
    return pl.pallas_call(
        _gmlp_kernel,
        out_shape=(jax.ShapeDtypeStruct((n, d), F32), jax.ShapeDtypeStruct((n, d // 2), I32),
                   jax.ShapeDtypeStruct((n, ne), F32)),
        grid=(n // tm,),
        in_specs=[
            pl.BlockSpec((tm, d), lambda i: (i, 0)),
            pl.BlockSpec((1, 6, d), lambda i: (i // per_b, 0, 0)),
            _const_spec((1, d)),
            _const_spec((d, 2 * inner)),
            _const_spec((1, inner)),
            _const_spec((1, inner)),
            _const_spec((A_GROUPS, A_CHUNK, A_CHUNK)),
            _const_spec((A_CHUNK, A_GROUPS)),
            _const_spec((inner, d)),
            _const_spec((1, d)),
            _const_spec((d, ne)),
        ],
        out_specs=(pl.BlockSpec((tm, d), lambda i: (i, 0)), pl.BlockSpec((tm, d // 2), lambda i: (i, 0)),
                   pl.BlockSpec((tm, ne), lambda i: (i, 0))),
        scratch_shapes=[pltpu.VMEM((tm, inner), F32), pltpu.VMEM((tm, inner), F32),
                        pltpu.VMEM((tm, inner), BF16)],
        compiler_params=_cparams(("arbitrary",)),
        name="gmlp_layer",
    )(h, mod_l, n1g.reshape(1, d), w_in.astype(BF16), ln_g.reshape(1, inner), ln_b.reshape(1, inner),
      w_sp, b_sp.T, w_out.astype(BF16), n2g.reshape(1, d), router_w)


def _first_max4(rows):
    m = jnp.maximum(jnp.maximum(rows[0], rows[1]), jnp.maximum(rows[2], rows[3]))
    idx = jnp.where(rows[0] == m, 0, jnp.where(rows[1] == m, 1, jnp.where(rows[2] == m, 2, 3)))
    return m, idx.astype(I32)


def _route_kernel(lt_ref, rb_ref, e_ref, g_ref):
    aff = jax.nn.sigmoid(lt_ref[...])
    sel = aff + rb_ref[...]
    neg = jnp.float32(-jnp.inf)
    g_score, g_i1, g_i2 = [], [], []
    for g in range(N_GROUPS):
        rows = [sel[4 * g + k:4 * g + k + 1, :] for k in range(EXPERTS_PER_GROUP)]
        m1, i1 = _first_max4(rows)
        rest = [jnp.where(i1 == k, neg, rows[k]) for k in range(EXPERTS_PER_GROUP)]
        m2, i2 = _first_max4(rest)
        g_score.append(m1 + m2)
        g_i1.append(i1)
        g_i2.append(i2)
    _, grp = _first_max4(g_score)
    l1 = jnp.zeros_like(grp)
    l2 = jnp.zeros_like(grp)
    for g in range(N_GROUPS):
        l1 = jnp.where(grp == g, g_i1[g], l1)
        l2 = jnp.where(grp == g, g_i2[g], l2)
    e1 = grp * EXPERTS_PER_GROUP + l1
    e2 = grp * EXPERTS_PER_GROUP + l2
    a1 = jnp.zeros_like(g_score[0])
    a2 = jnp.zeros_like(g_score[0])
    for e in range(N_EXPERTS):
        ae = aff[e:e + 1, :]
        a1 = jnp.where(e1 == e, ae, a1)
        a2 = jnp.where(e2 == e, ae, a2)
    tot = a1 + a2
    e_ref[0:1, :] = e1
    e_ref[1:2, :] = e2
    g_ref[0:1, :] = a1 / tot
    g_ref[1:2, :] = a2 / tot


def _route(logits, router_b):
    n, ne = logits.shape[0], N_EXPERTS
    tn = min(2048, n)
    return pl.pallas_call(
        _route_kernel,
        out_shape=(jax.ShapeDtypeStruct((TOP_K, n), I32), jax.ShapeDtypeStruct((TOP_K, n), F32)),
        grid=(n // tn,),
        in_specs=[pl.BlockSpec((ne, tn), lambda i: (0, i)), pl.BlockSpec((ne, 1), lambda i: (0, 0))],
        out_specs=(pl.BlockSpec((TOP_K, tn), lambda i: (0, i)), pl.BlockSpec((TOP_K, tn), lambda i: (0, i))),
        compiler_params=_cparams(("arbitrary",)),
        name="route",
    )(logits[:, :ne].T, router_b.reshape(ne, 1))


def _slot_kernel(e_ref, slot_ref, cnt_ref):
    n = e_ref.shape[1]
    nblk = n // 128
    eid = lax.broadcasted_iota(I32, (N_EXPERTS, 128), 0)
    r = lax.broadcasted_iota(I32, (128, 128), 0)
    c = lax.broadcasted_iota(I32, (128, 128), 1)
    upper = jnp.where(r <= c, 1.0, 0.0).astype(BF16)

    def hits(b):
        o = pl.multiple_of(b * 128, 128)
        t1 = e_ref[0:1, pl.ds(o, 128)] == eid
        t2 = e_ref[1:2, pl.ds(o, 128)] == eid
        return o, t1, t2, jnp.where(t1, 1.0, 0.0) + jnp.where(t2, 1.0, 0.0)

    def count_body(b, acc):
        return acc + hits(b)[3]

    cnt = jnp.sum(lax.fori_loop(0, nblk, count_body, jnp.zeros((N_EXPERTS, 128), F32)),
                  axis=-1, keepdims=True)
    cnt_ref[...] = cnt.astype(I32)
    padded = jnp.zeros_like(cnt)
    for k in range(pl.cdiv(n * TOP_K, MOE_ROWS)):
        padded = padded + jnp.where(cnt > k * MOE_ROWS, float(MOE_ROWS), 0.0)
    er = lax.broadcasted_iota(I32, (N_EXPERTS, N_EXPERTS), 0)
    ec = lax.broadcasted_iota(I32, (N_EXPERTS, N_EXPERTS), 1)
    padded_row = jnp.sum(jnp.where(er == ec, padded, 0.0), axis=0, keepdims=True)
    start = jnp.sum(jnp.where(ec < er, padded_row, 0.0), axis=-1, keepdims=True)

    def slot_body(b, carry):
        o, t1, t2, t = hits(b)
        incl = jnp.dot(t.astype(BF16), upper, preferred_element_type=F32)
        pos = carry + incl - t
        slot_ref[0:1, pl.ds(o, 128)] = jnp.sum(jnp.where(t1, pos, 0.0), axis=0, keepdims=True).astype(I32)
        slot_ref[1:2, pl.ds(o, 128)] = jnp.sum(jnp.where(t2, pos, 0.0), axis=0, keepdims=True).astype(I32)
        return carry + incl[:, 127:128]

    lax.fori_loop(0, nblk, slot_body, start)


def _slot_tables(experts):
    n = experts.shape[1]
    a = n * TOP_K
    slot, counts = pl.pallas_call(
        _slot_kernel,
        out_shape=(jax.ShapeDtypeStruct((TOP_K, n), I32), jax.ShapeDtypeStruct((N_EXPERTS, 1), I32)),
        compiler_params=pltpu.CompilerParams(vmem_limit_bytes=VMEM_LIMIT),
        name="slots",
    )(experts)
    counts = counts.reshape(N_EXPERTS)
    padded = ((counts + MOE_ROWS - 1) // MOE_ROWS) * MOE_ROWS
    pad_end = jnp.cumsum(padded)
    pad_start = pad_end - padded
    n_items = pl.cdiv(a, MOE_ROWS) + N_EXPERTS
    p = n_items * MOE_ROWS
    tok = jnp.broadcast_to(jnp.arange(n, dtype=I32)[None, :], (TOP_K, n))
    tok_of_slot = jnp.zeros((p,), I32).at[slot.reshape(a)].set(tok.reshape(a))
    slot_of_assign = slot.T
    item_start = jnp.arange(n_items, dtype=I32) * MOE_ROWS
    n_used = pad_end[-1] // MOE_ROWS
    item_e_raw = jnp.minimum(jnp.searchsorted(pad_end, item_start, side='right'), N_EXPERTS - 1).astype(I32)
    used = item_start < pad_end[-1]
    last_e = item_e_raw[jnp.maximum(n_used - 1, 0)]
    item_e = jnp.where(used, item_e_raw, last_e).astype(I32)
    valid = jnp.clip(counts[item_e_raw] - (item_start - pad_start[item_e_raw]), 0, MOE_ROWS)
    item_rows = jnp.where(used, valid, 0).astype(I32)
    item_blk = jnp.where(used, jnp.arange(n_items, dtype=I32), jnp.maximum(n_used - 1, 0)).astype(I32)
    return tok_of_slot, slot_of_assign, item_e, item_rows, item_blk


def _moe_kernel(ie_ref, ir_ref, ib_ref, x_ref, wg_ref, wu_ref, wd_ref, o_ref, wg_s, wu_s, wd_s, x_s):
    it = pl.program_id(0)
    f = pl.program_id(1)
    nrows = ir_ref[it]
    half = x_ref.shape[1]

    @pl.when(f == 0)
    def _():
        o_ref[...] = jnp.zeros_like(o_ref)
        pk = x_ref[...]
        x_s[:, :half] = pltpu.bitcast(pk & jnp.int32(-65536), F32).astype(BF16)
        x_s[:, half:] = pltpu.bitcast(lax.shift_left(pk, 16), F32).astype(BF16)

    @pl.when(nrows > 0)
    def _():
        wg_s[...] = wg_ref[0, 0].astype(BF16)
        wu_s[...] = wu_ref[0, 0].astype(BF16)
        wd_s[...] = wd_ref[0, 0].astype(BF16)
        nsb = (nrows + MOE_SUB - 1) // MOE_SUB

        def sub_block(sb):
            r0 = pl.multiple_of(sb * MOE_SUB, MOE_SUB)
            x = x_s[pl.ds(r0, MOE_SUB), :]
            g = jnp.dot(x, wg_s[...], preferred_element_type=F32)
            u = jnp.dot(x, wu_s[...], preferred_element_type=F32)
            hmid = ((g * jax.nn.sigmoid(g)) * u).astype(BF16)
            o_ref[pl.ds(r0, MOE_SUB), :] += jnp.dot(hmid, wd_s[...], preferred_element_type=F32)

        def pair_body(pi, carry):
            sub_block(2 * pi)
            sub_block(2 * pi + 1)
            return carry

        lax.fori_loop(0, nsb // 2, pair_body, 0)

        @pl.when(nsb % 2 == 1)
        def _():
            sub_block(nsb - 1)


def _moe_experts(x_sorted, item_e, item_rows, item_blk, w_gate, w_up, w_down, layer):
    p, half = x_sorted.shape
    d = 2 * half
    n_items = p // MOE_ROWS
    dff = w_gate.shape[3]
    nf = dff // MOE_FT
    grid_spec = pltpu.PrefetchScalarGridSpec(
        num_scalar_prefetch=3,
        grid=(n_items, nf),
        in_specs=[
            pl.BlockSpec((MOE_ROWS, half), lambda i, f, ie, ir, ib: (ib[i], 0)),
            pl.BlockSpec((1, 1, d, MOE_FT),
                         lambda i, f, ie, ir, ib: (layer, ie[i], 0, jnp.where(ir[i] > 0, f, nf - 1))),
            pl.BlockSpec((1, 1, d, MOE_FT),
                         lambda i, f, ie, ir, ib: (layer, ie[i], 0, jnp.where(ir[i] > 0, f, nf - 1))),
            pl.BlockSpec((1, 1, MOE_FT, d),
                         lambda i, f, ie, ir, ib: (layer, ie[i], jnp.where(ir[i] > 0, f, nf - 1), 0)),
        ],
        out_specs=pl.BlockSpec((MOE_ROWS, d), lambda i, f, ie, ir, ib: (i, 0)),
        scratch_shapes=[pltpu.VMEM((d, MOE_FT), BF16), pltpu.VMEM((d, MOE_FT), BF16),
                        pltpu.VMEM((MOE_FT, d), BF16), pltpu.VMEM((MOE_ROWS, d), BF16)],
    )
    return pl.pallas_call(
        _moe_kernel,
        out_shape=jax.ShapeDtypeStruct((p, d), F32),
        grid_spec=grid_spec,
        compiler_params=_cparams(("arbitrary", "arbitrary")),
        name="moe_experts",
    )(item_e, item_rows, item_blk, x_sorted, w_gate, w_up, w_down)


def _moe(hn2, logits, router_b, w_gate, w_up, w_down, layer):
    experts, gates = _route(logits, router_b)
    tok_of_slot, slot_of_assign, item_e, item_rows, item_blk = _slot_tables(experts)
    x_sorted = hn2.at[tok_of_slot].get(mode='promise_in_bounds')
    y_slot = _moe_experts(x_sorted, item_e, item_rows, item_blk, w_gate, w_up, w_down, layer)
    y0 = y_slot.at[slot_of_assign[:, 0]].get(mode='promise_in_bounds')
    y1 = y_slot.at[slot_of_assign[:, 1]].get(mode='promise_in_bounds')
    n = hn2.shape[0]
    gates_rep = jnp.concatenate([jnp.broadcast_to(gates[k][:, None], (n, 128)) for k in range(TOP_K)], axis=1)
    return y0, y1, gates_rep


def _moe_combine(h1, y0, y1, gates, mod):
    reps = h1.shape[1] // 128
    g0 = _lane_tile(gates[:, :128], reps)
    g1 = _lane_tile(gates[:, 128:], reps)
    return h1 + mod[5:6] * (g0 * y0 + g1 * y1)


def _bproj_kernel(h1_ref, y0_ref, y1_ref, gt_ref, modp_ref, mod_ref, n1g_ref, w_ref, kvg_ref, kig_ref,
                  wuk_ref, h_out, qa_out, ckv_out, qi_out, ki_out, wi_out):
    tm = h1_ref.shape[0]
    h = _moe_combine(h1_ref[...], y0_ref[...], y1_ref[...], gt_ref[...], modp_ref[0])
    h_out[...] = h
    mod = mod_ref[0]
    hn = _mod_rmsnorm(h, n1g_ref[...], mod[1:2], mod[0:1]).astype(BF16)
    proj = jnp.dot(hn, w_ref[...], preferred_element_type=F32)
    o1 = B_HEADS * B_HEAD_DIM
    o2 = o1 + B_KV_LATENT
    o3 = o2 + B_IDX_HEADS * B_IDX_DIM
    ckv = proj[:, o1:o2]
    ckv = ckv * lax.rsqrt(jnp.mean(ckv * ckv, axis=-1, keepdims=True) + EPS) * kvg_ref[...]
    ckv_out[...] = ckv.astype(BF16)
    tail = proj[:, o3:o3 + 128]
    ki = tail[:, :B_IDX_DIM]
    ki = ki * lax.rsqrt(jnp.mean(ki * ki, axis=-1, keepdims=True) + EPS) * kig_ref[...]
    ki_out[...] = ki.astype(BF16)
    wi = tail[:, B_IDX_DIM:B_IDX_DIM + B_IDX_HEADS] * (B_IDX_HEADS ** -0.5 * B_IDX_DIM ** -0.5)
    scale = B_HEAD_DIM ** -0.5
    for blk in range(tm // QBLK):
        r0, r1 = blk * QBLK, (blk + 1) * QBLK
        for hh in range(B_HEADS):
            qh = proj[r0:r1, hh * B_HEAD_DIM:(hh + 1) * B_HEAD_DIM].astype(BF16)
            qa = jnp.dot(qh, wuk_ref[hh], preferred_element_type=F32) * scale
            qa_out[blk, hh * QBLK:(hh + 1) * QBLK, :] = qa.astype(BF16)
        for hh in range(B_IDX_HEADS):
            qi_out[blk, hh * QBLK:(hh + 1) * QBLK, :] = proj[r0:r1, o2 + hh * B_IDX_DIM:
                                                            o2 + (hh + 1) * B_IDX_DIM].astype(BF16)
            wi_out[blk, hh * QBLK:(hh + 1) * QBLK, :] = jnp.broadcast_to(wi[r0:r1, hh:hh + 1], (QBLK, 128))


def _bproj(h1, y0, y1, gates, mod_prev, mod_l, n1g, b_w_in, kv_g, w_uk, kidx_g, seq):
    n, d = h1.shape
    tm = ROW_TILE
    per_b = seq // tm
    nq = n // QBLK
    qpb = tm // QBLK
    o1 = B_HEADS * B_HEAD_DIM
    o2 = o1 + B_KV_LATENT
    o3 = o2 + B_IDX_HEADS * B_IDX_DIM
    wcat = jnp.zeros((d, o3 + 128), F32).at[:, :b_w_in.shape[1]].set(b_w_in).astype(BF16)
    wuk_t = jnp.transpose(w_uk, (1, 2, 0)).astype(BF16)
    row = lambda i: (i, 0)
    modm = lambda i: (i // per_b, 0, 0)
    blk3 = lambda i: (i, 0, 0)
    return pl.pallas_call(
        _bproj_kernel,
        out_shape=(
            jax.ShapeDtypeStruct((n, d), F32),
            jax.ShapeDtypeStruct((nq, B_HEADS * QBLK, B_KV_LATENT), BF16),
            jax.ShapeDtypeStruct((n, B_KV_LATENT), BF16),
            jax.ShapeDtypeStruct((nq, B_IDX_HEADS * QBLK, B_IDX_DIM), BF16),
            jax.ShapeDtypeStruct((n, B_IDX_DIM), BF16),
            jax.ShapeDtypeStruct((nq, B_IDX_HEADS * QBLK, 128), F32),
        ),
        grid=(n // tm,),
        in_specs=[
            pl.BlockSpec((tm, d), row), pl.BlockSpec((tm, d), row), pl.BlockSpec((tm, d), row),
            pl.BlockSpec((tm, TOP_K * 128), row),
            pl.BlockSpec((1, 6, d), modm), pl.BlockSpec((1, 6, d), modm),
            _const_spec((1, d)),
            _const_spec((d, o3 + 128)),
            _const_spec((1, B_KV_LATENT)),
            _const_spec((1, B_IDX_DIM)),
            _const_spec((B_HEADS, B_HEAD_DIM, B_KV_LATENT)),
        ],
        out_specs=(
            pl.BlockSpec((tm, d), row),
            pl.BlockSpec((qpb, B_HEADS * QBLK, B_KV_LATENT), blk3),
            pl.BlockSpec((tm, B_KV_LATENT), row),
            pl.BlockSpec((qpb, B_IDX_HEADS * QBLK, B_IDX_DIM), blk3),
            pl.BlockSpec((tm, B_IDX_DIM), row),
            pl.BlockSpec((qpb, B_IDX_HEADS * QBLK, 128), blk3),
        ),
        compiler_params=_cparams(("arbitrary",)),
        name="dsa_proj",
    )(h1, y0, y1, gates, mod_prev, mod_l, n1g.reshape(1, d), wcat, kv_g.reshape(1, -1),
      kidx_g.reshape(1, -1), wuk_t)


def _t5_bucket(dist):
    n = jnp.maximum(dist, 0)
    exact = REL_BUCKETS // 2
    nf = jnp.maximum(n, 1).astype(F32)
    large = exact + (jnp.log(nf / exact) / math.log(REL_MAX_DIST / exact)
                     * (REL_BUCKETS - exact)).astype(I32)
    large = jnp.minimum(large, REL_BUCKETS - 1)
    return jnp.where(n < exact, n, large)


def _bias_tables(rel_bias):
    assert REL_MAX_DIST <= 128
    t = jnp.arange(128, dtype=I32)[:, None]
    s = jnp.arange(128, dtype=I32)[None, :]
    far = rel_bias[REL_BUCKETS - 1]
    diag = rel_bias[_t5_bucket(t - s)] - far
    prev = rel_bias[_t5_bucket(t - s + 128)] - far
    return jnp.stack([jnp.transpose(diag, (2, 0, 1)), jnp.transpose(prev, (2, 0, 1))])


def _attn_kernel(qa_ref, qi_ref, wi_ref, ckv_ref, ki_ref, bt_ref, o_ref, kbuf, m_scr, l_scr, a_scr, acc_scr,
                 s_scr, p_scr, madd_scr, kbuft, tau_scr):
    i = pl.program_id(1)
    nt = i + 1
    t_row = i * QBLK + lax.broadcasted_iota(I32, (QBLK, KTILE), 0)
    lane = lax.broadcasted_iota(I32, (QBLK, KTILE), 1)

    qi = qi_ref[0]

    def score_body(j, carry):
        k0 = pl.multiple_of(j * KTILE, KTILE)
        kt = ki_ref[0, pl.ds(k0, KTILE), :]
        p = lax.dot_general(qi, kt, (((1,), (1,)), ((), ())), preferred_element_type=F32)
        p = jnp.maximum(p, 0.0) * _lane_tile(wi_ref[0], KTILE // 128)
        sc = p[0:QBLK]
        for hh in range(1, B_IDX_HEADS):
            sc = sc + p[hh * QBLK:(hh + 1) * QBLK]
        bits = pltpu.bitcast(sc + 0.0, I32)
        key = jnp.where(bits < 0, bits ^ jnp.int32(0x7FFFFFFF), bits)
        key = jnp.where(k0 + lane <= t_row, key, jnp.int32(INT_MIN))
        kbuf[:, pl.ds(k0, KTILE)] = key
        kbuft[pl.ds(k0, KTILE), :] = key.T
        return carry

    lax.fori_loop(0, nt, score_body, 0)

    def fold_lanes(x):
        out = x[:, 0:128]
        for k in range(1, KTILE // 128):
            out = out + x[:, k * 128:(k + 1) * 128]
        return out

    def count_ge(cand):
        def body(j, acc):
            k0 = pl.multiple_of(j * KTILE, KTILE)
            return acc + fold_lanes(jnp.where(kbuf[:, pl.ds(k0, KTILE)] >= cand, 1, 0))
        acc = lax.fori_loop(0, nt, body, jnp.zeros((QBLK, 128), I32))
        return jnp.sum(acc, axis=-1, keepdims=True)

    def count_ge_t(cand_row):
        def body(j, acc):
            k0 = pl.multiple_of(j * KTILE, KTILE)
            hit = jnp.where(kbuft[pl.ds(k0, KTILE), :] >= cand_row, 1, 0)
            return acc + jnp.sum(hit.reshape(KTILE // 8, 8, QBLK), axis=0)
        acc = lax.fori_loop(0, nt, body, jnp.zeros((8, QBLK), I32))
        return jnp.sum(acc, axis=0, keepdims=True)

    def bit_step(b, lo, n_ge):
        cand = lo + lax.shift_left(jnp.int32(1), 31 - b)
        cnt = count_ge_t(cand)
        take = cnt >= B_TOPK_MAX
        return jnp.where(take, cand, lo), jnp.where(take, cnt, n_ge)

    def bit_cond(c):
        return (c[0] < 32) & (c[3] == 0)

    def bit_body(c):
        b, lo, n_ge, _ = c
        lo, n_ge = bit_step(b, lo, n_ge)
        lo, n_ge = bit_step(b + 1, lo, n_ge)
        done = (jnp.max(n_ge) == B_TOPK_MAX).astype(I32)
        return b + 2, lo, n_ge, done

    _, tau_row, n_ge_row, _ = lax.while_loop(
        bit_cond, bit_body,
        (jnp.int32(0), jnp.full((1, QBLK), INT_MIN, I32), jnp.full((1, QBLK), nt * KTILE, I32),
         (i < 1).astype(I32)))
    tau_row = jnp.maximum(tau_row, jnp.int32(INT_MIN + 1))
    excess = (i >= 1) & (jnp.max(n_ge_row) > B_TOPK_MAX)

    def to_rows(row):
        return jnp.broadcast_to(row, (QBLK, QBLK)).T

    tau_scr[...] = to_rows(tau_row)

    @pl.when(excess)
    def _():
        tau = tau_scr[:, 0:1]
        n_ge = to_rows(n_ge_row)[:, 0:1]
        n_gt = count_ge(tau + 1)
        need = B_TOPK_MAX - n_gt

        def count_eq_before(pos):
            def body(j, acc):
                k0 = pl.multiple_of(j * KTILE, KTILE)
                hit = (kbuf[:, pl.ds(k0, KTILE)] == tau) & (k0 + lane < pos)
                return acc + fold_lanes(jnp.where(hit, 1, 0))
            acc = lax.fori_loop(0, nt, body, jnp.zeros((QBLK, 128), I32))
            return jnp.sum(acc, axis=-1, keepdims=True)

        def pos_body(b, pos):
            cand = pos + lax.shift_left(jnp.int32(1), 12 - b)
            return jnp.where(count_eq_before(cand) < need, cand, pos)

        pos = lax.fori_loop(0, 13, pos_body, jnp.zeros((QBLK, 1), I32))

        def drop_body(j, carry):
            k0 = pl.multiple_of(j * KTILE, KTILE)
            kk = kbuf[:, pl.ds(k0, KTILE)]
            drop = (kk == tau) & (k0 + lane > pos) & (n_ge > B_TOPK_MAX)
            kbuf[:, pl.ds(k0, KTILE)] = jnp.where(drop, jnp.int32(INT_MIN), kk)
            return carry

        lax.fori_loop(0, nt, drop_body, 0)

    m_scr[...] = jnp.full_like(m_scr, NEG_BIG)
    l_scr[...] = jnp.zeros_like(l_scr)
    acc_scr[...] = jnp.zeros_like(acc_scr)

    def attend(k0, width, tile_off):
        kv = ckv_ref[0, pl.ds(k0, width), :]
        s_scr[:, :width] = lax.dot_general(qa_ref[0], kv, (((1,), (1,)), ((), ())),
                                           preferred_element_type=F32)
        tau_t = _lane_tile(tau_scr[...], width // QBLK)
        madd_scr[:, :width] = jnp.where(kbuf[:, pl.ds(k0, width)] >= tau_t, 0.0, NEG_BIG)
        for r0 in range(0, B_HEADS * QBLK, SM_ROWS):
            r1 = r0 + SM_ROWS
            hh, q0 = r0 // QBLK, r0 % QBLK
            parts = []
            for kc in range(width // 128):
                sp = s_scr[r0:r1, kc * 128:(kc + 1) * 128] + madd_scr[q0:q0 + SM_ROWS, kc * 128:(kc + 1) * 128]
                if tile_off is not None:
                    rel = (QBLK // 128) * tile_off + q0 // 128 - kc
                    if rel in (0, 1):
                        sp = sp + bt_ref[rel, hh, q0 % 128:q0 % 128 + SM_ROWS, :]
                parts.append(sp)
            smax = parts[0]
            for sp in parts[1:]:
                smax = jnp.maximum(smax, sp)
            m_old = m_scr[r0:r1, :]
            m_new = jnp.maximum(m_old, jnp.max(smax, axis=-1, keepdims=True))
            alpha = jnp.exp(m_old - m_new)
            psum = None
            for kc, sp in enumerate(parts):
                p = jnp.exp(sp - m_new)
                p_scr[r0:r1, kc * 128:(kc + 1) * 128] = p.astype(BF16)
                psum = p if psum is None else psum + p
            l_scr[r0:r1, :] = alpha * l_scr[r0:r1, :] + jnp.sum(psum, axis=-1, keepdims=True)
            m_scr[r0:r1, :] = m_new
            a_scr[r0:r1, :] = alpha
        pv = jnp.dot(p_scr[:, :width], kv, preferred_element_type=F32)
        acc_scr[...] = _lane_tile(a_scr[...], B_KV_LATENT // 128) * acc_scr[...] + pv

    n_far = jnp.maximum(i - 1, 0)
    per_far = FAR_TILE // KTILE

    def far_body(jf, carry):
        attend(pl.multiple_of(jf * FAR_TILE, FAR_TILE), FAR_TILE, None)
        return carry

    lax.fori_loop(0, n_far // per_far, far_body, 0)

    if per_far > 1:
        @pl.when(n_far % per_far == 1)
        def _():
            attend(pl.multiple_of((n_far - 1) * KTILE, KTILE), KTILE, None)

    @pl.when(i >= 1)
    def _():
        attend(pl.multiple_of((i - 1) * KTILE, KTILE), KTILE, 1)

    attend(pl.multiple_of(i * KTILE, KTILE), KTILE, 0)
    inv_l = 1.0 / l_scr[...]
    o_ref[0] = (acc_scr[...] * _lane_tile(inv_l, B_KV_LATENT // 128)).astype(BF16)


def _attention(qa, qi, wi, ckv, ki, btab, bsz, seq):
    nqb = seq // QBLK
    gq = lambda b, i: (b * nqb + i, 0, 0)
    gb = lambda b, i: (b, 0, 0)
    return pl.pallas_call(
        _attn_kernel,
        out_shape=jax.ShapeDtypeStruct(qa.shape, BF16),
        grid=(bsz, nqb),
        in_specs=[
            pl.BlockSpec((1, B_HEADS * QBLK, B_KV_LATENT), gq),
            pl.BlockSpec((1, B_IDX_HEADS * QBLK, B_IDX_DIM), gq),
            pl.BlockSpec((1, B_IDX_HEADS * QBLK, 128), gq),
            pl.BlockSpec((1, seq, B_KV_LATENT), gb, pipeline_mode=pl.Buffered(1)),
            pl.BlockSpec((1, seq, B_IDX_DIM), gb, pipeline_mode=pl.Buffered(1)),
            pl.BlockSpec((2, B_HEADS, 128, 128), lambda b, i: (0, 0, 0, 0), pipeline_mode=pl.Buffered(1)),
        ],
        out_specs=pl.BlockSpec((1, B_HEADS * QBLK, B_KV_LATENT), gq),
        scratch_shapes=[
            pltpu.VMEM((QBLK, seq), I32),
            pltpu.VMEM((B_HEADS * QBLK, 128), F32),
            pltpu.VMEM((B_HEADS * QBLK, 128), F32),
            pltpu.VMEM((B_HEADS * QBLK, 128), F32),
            pltpu.VMEM((B_HEADS * QBLK, B_KV_LATENT), F32),
            pltpu.VMEM((B_HEADS * QBLK, FAR_TILE), F32),
            pltpu.VMEM((B_HEADS * QBLK, FAR_TILE), BF16),
            pltpu.VMEM((QBLK, FAR_TILE), F32),
            pltpu.VMEM((seq, QBLK), I32),
            pltpu.VMEM((QBLK, QBLK), I32),
        ],
        compiler_params=_cparams(("arbitrary", "arbitrary")),
        name="dsa_attention",
    )(qa, qi, wi, ckv.reshape(bsz, seq, -1), ki.reshape(bsz, seq, -1), btab)


def _bout_kernel(h_ref, ol_ref, mod_ref, wuv_ref, wout_ref, n2g_ref, rw_ref, h_out, hn_out, lg_out, o_scr):
    mod = mod_ref[0]
    for blk in range(ol_ref.shape[0]):
        for hh in range(B_HEADS):
            oh = jnp.dot(ol_ref[blk, hh * QBLK:(hh + 1) * QBLK, :], wuv_ref[hh], preferred_element_type=F32)
            o_scr[blk * QBLK:(blk + 1) * QBLK, hh * B_V_DIM:(hh + 1) * B_V_DIM] = oh.astype(BF16)
    y = jnp.dot(o_scr[...], wout_ref[...], preferred_element_type=F32)
    _residual_epilogue(h_ref[...], y, mod, n2g_ref[...], rw_ref[...], h_out, hn_out, lg_out)


def _bout(h, o_lat, mod_l, w_uv, w_out, n2g, router_w, seq):
    n, d = h.shape
    tm = ROW_TILE
    per_b = seq // tm
    qpb = tm // QBLK
    ne = router_w.shape[1]
    wuv_t = jnp.transpose(w_uv, (1, 0, 2)).astype(BF16)
    row = lambda i: (i, 0)
    return pl.pallas_call(
        _bout_kernel,
        out_shape=(jax.ShapeDtypeStruct((n, d), F32), jax.ShapeDtypeStruct((n, d // 2), I32),
                   jax.ShapeDtypeStruct((n, ne), F32)),
        grid=(n // tm,),
        in_specs=[
            pl.BlockSpec((tm, d), row),
            pl.BlockSpec((qpb, B_HEADS * QBLK, B_KV_LATENT), lambda i: (i, 0, 0)),
            pl.BlockSpec((1, 6, d), lambda i: (i // per_b, 0, 0)),
            _const_spec((B_HEADS, B_KV_LATENT, B_V_DIM)),
            _const_spec((B_HEADS * B_V_DIM, d)),
            _const_spec((1, d)),
            _const_spec((d, ne)),
        ],
        out_specs=(pl.BlockSpec((tm, d), row), pl.BlockSpec((tm, d // 2), row), pl.BlockSpec((tm, ne), row)),
        scratch_shapes=[pltpu.VMEM((tm, B_HEADS * B_V_DIM), BF16)],
        compiler_params=_cparams(("arbitrary",)),
        name="dsa_out",
    )(h, o_lat, mod_l, wuv_t, w_out.astype(BF16), n2g.reshape(1, d), router_w)


def _final_kernel(h1_ref, y0_ref, y1_ref, gt_ref, mod_ref, g_ref, o_ref):
    h = _moe_combine(h1_ref[...], y0_ref[...], y1_ref[...], gt_ref[...], mod_ref[0])
    o_ref[...] = h * lax.rsqrt(jnp.mean(h * h, axis=-1, keepdims=True) + EPS) * g_ref[...]


def _final(h1, y0, y1, gates, mod_l, final_g, seq):
    n, d = h1.shape
    tm = 512
    per_b = seq // tm
    row = lambda i: (i, 0)
    return pl.pallas_call(
        _final_kernel,
        out_shape=jax.ShapeDtypeStruct((n, d), F32),
        grid=(n // tm,),
        in_specs=[pl.BlockSpec((tm, d), row), pl.BlockSpec((tm, d), row), pl.BlockSpec((tm, d), row),
                  pl.BlockSpec((tm, TOP_K * 128), row), pl.BlockSpec((1, 6, d), lambda i: (i // per_b, 0, 0)),
                  _const_spec((1, d))],
        out_specs=pl.BlockSpec((tm, d), row),
        compiler_params=_cparams(("arbitrary",)),
        name="final_norm",
    )(h1, y0, y1, gates, mod_l, final_g.reshape(1, d))


def kernel(x, c, ada_w, ada_b, norm1_g, norm2_g, a_w_in, a_ln_g, a_ln_b, a_w_sp, a_b_sp, a_w_out, b_w_in,
           b_kv_norm_g, b_w_uk, b_w_uv, b_kidx_g, b_w_out, rel_bias, router_w, router_b, moe_w_gate,
           moe_w_up, moe_w_down, final_g):
    bsz, seq, d = x.shape
    n = bsz * seq
    assert seq % max(QBLK, GMLP_ROWS, ROW_TILE, 512) == 0 and d % 256 == 0
    assert min(B_TOPK_MAX, seq // 4) == B_TOPK_MAX and QBLK >= B_TOPK_MAX and KTILE == QBLK
    assert ada_w.shape[0] == 2 and moe_w_gate.shape[1] == N_EXPERTS and MOE_ROWS % MOE_SUB == 0
    mod = _adaln(c, ada_w, ada_b).reshape(ada_w.shape[0], bsz, 6, d)
    rw_pad = jnp.zeros((d, 128), F32).at[:, :N_EXPERTS].set(router_w)
    h = x.reshape(n, d)

    h1, hn2, logits = _gmlp_layer(h, mod[0], norm1_g[0], a_w_in[0], a_ln_g[0], a_ln_b[0], a_w_sp[0],
                                  a_b_sp[0], a_w_out[0], norm2_g[0], rw_pad, seq)
    y0, y1, gates = _moe(hn2, logits, router_b, moe_w_gate, moe_w_up, moe_w_down, 0)

    h, qa, ckv, qi, ki, wi = _bproj(h1, y0, y1, gates, mod[0], mod[1], norm1_g[1], b_w_in[0],
                                    b_kv_norm_g[0], b_w_uk[0], b_kidx_g[0], seq)
    o_lat = _attention(qa, qi, wi, ckv, ki, _bias_tables(rel_bias), bsz, seq)
    h1, hn2, logits = _bout(h, o_lat, mod[1], b_w_uv[0], b_w_out[0], norm2_g[1], rw_pad, seq)
    y0, y1, gates = _moe(hn2, logits, router_b, moe_w_gate, moe_w_up, moe_w_down, 1)

    out = _final(h1, y0, y1, gates, mod[1], final_g, seq)
    return out.reshape(bsz, seq, d)
```

```python
import functools
import math

import jax
import jax.numpy as jnp
from jax import lax
from jax.experimental import pallas as pl
from jax.experimental.pallas import tpu as pltpu

F32 = jnp.float32
BF16 = jnp.bfloat16
I32 = jnp.int32
HIGHEST = lax.Precision.HIGHEST

EPS = 1e-6
A_CHUNK = 128
A_GROUPS = 8
B_HEADS = 16
B_HEAD_DIM = 64
B_V_DIM = 64
B_KV_LATENT = 256
B_IDX_HEADS = 8
B_IDX_DIM = 64
B_TOPK_MAX = 256
QBLK = 256
KTILE = 256
FAR_TILE = 256
SM_ROWS = 128
REL_BUCKETS = 32
REL_MAX_DIST = 128
N_EXPERTS = 16
N_GROUPS = 4
EXPERTS_PER_GROUP = 4
TOP_K = 2
MOE_ROWS = 1280
MOE_SUB = 256
MOE_FT = 512

ROW_TILE = 256
GMLP_ROWS = 512
VMEM_LIMIT = 60 * 1024 * 1024

INT_MIN = -2 ** 31
NEG_BIG = -1e30


def _cparams(sem):
    return pltpu.CompilerParams(dimension_semantics=sem, vmem_limit_bytes=VMEM_LIMIT)


def _lane_tile(x, k):
    return x if k == 1 else jnp.concatenate([x] * k, axis=1)


def _mod_rmsnorm(h, g, scale, shift):
    ms = jnp.mean(h * h, axis=-1, keepdims=True)
    return (h * lax.rsqrt(ms + EPS) * g) * (1.0 + scale) + shift


def _gelu_tanh(x):
    c = math.sqrt(2.0 / math.pi)
    return 0.5 * x * (1.0 + jnp.tanh(c * (x + 0.044715 * (x * x * x))))


def _adaln_kernel(c_ref, w_ref, b_ref, o_ref):
    c = c_ref[...]
    sc = c * jax.nn.sigmoid(c)
    o_ref[0] = jnp.dot(sc, w_ref[0], precision=HIGHEST, preferred_element_type=F32) + b_ref[0]


def _adaln(c, ada_w, ada_b):
    depth, d, e = ada_w.shape
    bsz = c.shape[0]
    bp = 8
    c_pad = jnp.zeros((bp, d), F32).at[:bsz].set(c)
    tn = 1024
    out = pl.pallas_call(
        _adaln_kernel,
        out_shape=jax.ShapeDtypeStruct((depth, bp, e), F32),
        grid=(depth, e // tn),
        in_specs=[
            pl.BlockSpec((bp, d), lambda l, j: (0, 0)),
            pl.BlockSpec((1, d, tn), lambda l, j: (l, 0, j)),
            pl.BlockSpec((1, 1, tn), lambda l, j: (l, 0, j)),
        ],
        out_specs=pl.BlockSpec((1, bp, tn), lambda l, j: (l, 0, j)),
        compiler_params=_cparams(("arbitrary", "arbitrary")),
        name="adaln",
    )(c_pad, ada_w, ada_b.reshape(depth, 1, e))
    return out[:, :bsz]


def _residual_epilogue(h, y, mod, n2g, rw, h_out, hn_out, lg_out):
    h1 = h + mod[2:3] * y
    h_out[...] = h1
    hn2 = _mod_rmsnorm(h1, n2g, mod[4:5], mod[3:4])
    hn_hi = hn2.astype(BF16)
    bits = pltpu.bitcast(hn_hi.astype(F32), I32)
    half = bits.shape[1] // 2
    hn_out[...] = bits[:, :half] | lax.shift_right_logical(bits[:, half:], 16)
    hn_lo = (hn2 - hn_hi.astype(F32)).astype(BF16)
    rw_hi = rw.astype(BF16)
    rw_lo = (rw - rw_hi.astype(F32)).astype(BF16)
    lg_out[...] = (jnp.dot(hn_hi, rw_hi, preferred_element_type=F32)
                   + (jnp.dot(hn_lo, rw_hi, preferred_element_type=F32)
                      + jnp.dot(hn_hi, rw_lo, preferred_element_type=F32)))


def _gmlp_kernel(h_ref, mod_ref, n1g_ref, win_ref, lng_ref, lnb_ref, wsp_ref, bsp_ref, wout_ref,
                 n2g_ref, rw_ref, h_out, hn_out, lg_out, u_scr, v_scr, s_scr):
    tm = h_ref.shape[0]
    inner = u_scr.shape[1]
    gw = inner // A_GROUPS
    tn = 512
    h = h_ref[...]
    mod = mod_ref[0]
    hn = _mod_rmsnorm(h, n1g_ref[...], mod[1:2], mod[0:1]).astype(BF16)
    for j in range(2 * inner // tn):
        z = _gelu_tanh(jnp.dot(hn, win_ref[:, j * tn:(j + 1) * tn], preferred_element_type=F32))
        if j < inner // tn:
            u_scr[:, j * tn:(j + 1) * tn] = z
        else:
            jj = j - inner // tn
            v_scr[:, jj * tn:(jj + 1) * tn] = z
    vsum = jnp.zeros((tm, 1), F32)
    for j in range(inner // tn):
        vsum = vsum + jnp.sum(v_scr[:, j * tn:(j + 1) * tn], axis=-1, keepdims=True)
    mu = vsum * (1.0 / inner)
    vsq = jnp.zeros((tm, 1), F32)
    for j in range(inner // tn):
        d = v_scr[:, j * tn:(j + 1) * tn] - mu
        vsq = vsq + jnp.sum(d * d, axis=-1, keepdims=True)
    rstd = lax.rsqrt(vsq * (1.0 / inner) + EPS)
    row = lax.broadcasted_iota(I32, (A_CHUNK, A_CHUNK), 0)
    col = lax.broadcasted_iota(I32, (A_CHUNK, A_CHUNK), 1)
    tril = row >= col
    for g in range(A_GROUPS):
        ws = jnp.where(tril, wsp_ref[g], 0.0).astype(BF16)
        bcol = bsp_ref[:, g:g + 1]
        lg = lng_ref[:, g * gw:(g + 1) * gw]
        lb = lnb_ref[:, g * gw:(g + 1) * gw]
        for c in range(tm // A_CHUNK):
            r0, r1 = c * A_CHUNK, (c + 1) * A_CHUNK
            vt = v_scr[r0:r1, g * gw:(g + 1) * gw]
            vn = ((vt - mu[r0:r1]) * rstd[r0:r1]) * lg + lb
            fv = jnp.dot(ws, vn.astype(BF16), preferred_element_type=F32) + bcol
            s_scr[r0:r1, g * gw:(g + 1) * gw] = (u_scr[r0:r1, g * gw:(g + 1) * gw] * fv).astype(BF16)
    y = jnp.dot(s_scr[...], wout_ref[...], preferred_element_type=F32)
    _residual_epilogue(h, y, mod, n2g_ref[...], rw_ref[...], h_out, hn_out, lg_out)


def _const_spec(shape):
    nd = len(shape)
    return pl.BlockSpec(shape, lambda i, _nd=nd: (0,) * _nd, pipeline_mode=pl.Buffered(1))


def _gmlp_layer(h, mod_l, n1g, w_in, ln_g, ln_b, w_sp, b_sp, w_out, n2g, router_w, seq):
    n, d = h.shape
    inner = w_out.shape[0]
    tm = GMLP_ROWS
    per_b = seq // tm
    ne = router_w.shape[1]
    return pl.pallas_call(
        _gmlp_kernel,
        out_shape=(jax.ShapeDtypeStruct((n, d), F32), jax.ShapeDtypeStruct((n, d // 2), I32),
                   jax.ShapeDtypeStruct((n, ne), F32)),
        grid=(n // tm,),
        in_specs=[
            pl.BlockSpec((tm, d), lambda i: (i, 0)),
            pl.BlockSpec((1, 6, d), lambda i: (i // per_b, 0, 0)),
            _const_spec((1, d)),
            _const_spec((d, 2 * inner)),
            _const_spec((1, inner)),
            _const_spec((1, inner)),
            _const_spec((A_GROUPS, A_CHUNK, A_CHUNK)),
            _const_spec((A_CHUNK, A_GROUPS)),
            _const_spec((inner, d)),
            _const_spec((1, d)),
            _const_spec((d, ne)),
        ],
        out_specs=(pl.BlockSpec((tm, d), lambda i: (i, 0)), pl.BlockSpec((tm, d // 2), lambda i: (i, 0)),
                   pl.BlockSpec((tm, ne), lambda i: (i, 0))),
        scratch_shapes=[pltpu.VMEM((tm, inner), F32), pltpu.VMEM((tm, inner), F32),
                        pltpu.VMEM((tm, inner), BF16)],
        compiler_params=_cparams(("arbitrary",)),
        name="gmlp_layer",
    )(h, mod_l, n1g.reshape(1, d), w_in.astype(BF16), ln_g.reshape(1, inner), ln_b.reshape(1, inner),
      w_sp, b_sp.T, w_out.astype(BF16), n2g.reshape(1, d), router_w)


def _first_max4(rows):
    m = jnp.maximum(jnp.maximum(rows[0], rows[1]), jnp.maximum(rows[2], rows[3]))
    idx = jnp.where(rows[0] == m, 0, jnp.where(rows[1] == m, 1, jnp.where(rows[2] == m, 2, 3)))
    return m, idx.astype(I32)


def _route_kernel(lt_ref, rb_ref, e_ref, g_ref):
    aff = jax.nn.sigmoid(lt_ref[...])
    sel = aff + rb_ref[...]
    neg = jnp.float32(-jnp.inf)
    g_score, g_i1, g_i2 = [], [], []
    for g in range(N_GROUPS):
        rows = [sel[4 * g + k:4 * g + k + 1, :] for k in range(EXPERTS_PER_GROUP)]
        m1, i1 = _first_max4(rows)
        rest = [jnp.where(i1 == k, neg, rows[k]) for k in range(EXPERTS_PER_GROUP)]
        m2, i2 = _first_max4(rest)
        g_score.append(m1 + m2)
        g_i1.append(i1)
        g_i2.append(i2)
    _, grp = _first_max4(g_score)
    l1 = jnp.zeros_like(grp)
    l2 = jnp.zeros_like(grp)
    for g in range(N_GROUPS):
        l1 = jnp.where(grp == g, g_i1[g], l1)
        l2 = jnp.where(grp == g, g_i2[g], l2)
    e1 = grp * EXPERTS_PER_GROUP + l1
    e2 = grp * EXPERTS_PER_GROUP + l2
    a1 = jnp.zeros_like(g_score[0])
    a2 = jnp.zeros_like(g_score[0])
    for e in range(N_EXPERTS):
        ae = aff[e:e + 1, :]
        a1 = jnp.where(e1 == e, ae, a1)
        a2 = jnp.where(e2 == e, ae, a2)
    tot = a1 + a2
    e_ref[0:1, :] = e1
    e_ref[1:2, :] = e2
    g_ref[0:1, :] = a1 / tot
    g_ref[1:2, :] = a2 / tot


def _route(logits, router_b):
    n, ne = logits.shape[0], N_EXPERTS
    tn = min(2048, n)
    return pl.pallas_call(
        _route_kernel,
        out_shape=(jax.ShapeDtypeStruct((TOP_K, n), I32), jax.ShapeDtypeStruct((TOP_K, n), F32)),
        grid=(n // tn,),
        in_specs=[pl.BlockSpec((ne, tn), lambda i: (0, i)), pl.BlockSpec((ne, 1), lambda i: (0, 0))],
        out_specs=(pl.BlockSpec((TOP_K, tn), lambda i: (0, i)), pl.BlockSpec((TOP_K, tn), lambda i: (0, i))),
        compiler_params=_cparams(("arbitrary",)),
        name="route",
    )(logits[:, :ne].T, router_b.reshape(ne, 1))


def _slot_kernel(e_ref, slot_ref, cnt_ref):
    n = e_ref.shape[1]
    nblk = n // 128
    eid = lax.broadcasted_iota(I32, (N_EXPERTS, 128), 0)
    r = lax.broadcasted_iota(I32, (128, 128), 0)
    c = lax.broadcasted_iota(I32, (128, 128), 1)
    upper = jnp.where(r <= c, 1.0, 0.0).astype(BF16)

    def hits(b):
        o = pl.multiple_of(b * 128, 128)
        t1 = e_ref[0:1, pl.ds(o, 128)] == eid
        t2 = e_ref[1:2, pl.ds(o, 128)] == eid
        return o, t1, t2, jnp.where(t1, 1.0, 0.0) + jnp.where(t2, 1.0, 0.0)

    def count_body(b, acc):
        return acc + hits(b)[3]

    cnt = jnp.sum(lax.fori_loop(0, nblk, count_body, jnp.zeros((N_EXPERTS, 128), F32)),
                  axis=-1, keepdims=True)
    cnt_ref[...] = cnt.astype(I32)
    padded = jnp.zeros_like(cnt)
    for k in range(pl.cdiv(n * TOP_K, MOE_ROWS)):
        padded = padded + jnp.where(cnt > k * MOE_ROWS, float(MOE_ROWS), 0.0)
    er = lax.broadcasted_iota(I32, (N_EXPERTS, N_EXPERTS), 0)
    ec = lax.broadcasted_iota(I32, (N_EXPERTS, N_EXPERTS), 1)
    padded_row = jnp.sum(jnp.where(er == ec, padded, 0.0), axis=0, keepdims=True)
    start = jnp.sum(jnp.where(ec < er, padded_row, 0.0), axis=-1, keepdims=True)

    def slot_body(b, carry):
        o, t1, t2, t = hits(b)
        incl = jnp.dot(t.astype(BF16), upper, preferred_element_type=F32)
        pos = carry + incl - t
        slot_ref[0:1, pl.ds(o, 128)] = jnp.sum(jnp.where(t1, pos, 0.0), axis=0, keepdims=True).astype(I32)
        slot_ref[1:2, pl.ds(o, 128)] = jnp.sum(jnp.where(t2, pos, 0.0), axis=0, keepdims=True).astype(I32)
        return carry + incl[:, 127:128]

    lax.fori_loop(0, nblk, slot_body, start)


def _slot_tables(experts):
    n = experts.shape[1]
    a = n * TOP_K
    slot, counts = pl.pallas_call(
        _slot_kernel,
        out_shape=(jax.ShapeDtypeStruct((TOP_K, n), I32), jax.ShapeDtypeStruct((N_EXPERTS, 1), I32)),
        compiler_params=pltpu.CompilerParams(vmem_limit_bytes=VMEM_LIMIT),
        name="slots",
    )(experts)
    counts = counts.reshape(N_EXPERTS)
    padded = ((counts + MOE_ROWS - 1) // MOE_ROWS) * MOE_ROWS
    pad_end = jnp.cumsum(padded)
    pad_start = pad_end - padded
    n_items = pl.cdiv(a, MOE_ROWS) + N_EXPERTS
    p = n_items * MOE_ROWS
    tok = jnp.broadcast_to(jnp.arange(n, dtype=I32)[None, :], (TOP_K, n))
    tok_of_slot = jnp.zeros((p,), I32).at[slot.reshape(a)].set(tok.reshape(a))
    slot_of_assign = slot.T
    item_start = jnp.arange(n_items, dtype=I32) * MOE_ROWS
    n_used = pad_end[-1] // MOE_ROWS
    item_e_raw = jnp.minimum(jnp.searchsorted(pad_end, item_start, side='right'), N_EXPERTS - 1).astype(I32)
    used = item_start < pad_end[-1]
    last_e = item_e_raw[jnp.maximum(n_used - 1, 0)]
    item_e = jnp.where(used, item_e_raw, last_e).astype(I32)
    valid = jnp.clip(counts[item_e_raw] - (item_start - pad_start[item_e_raw]), 0, MOE_ROWS)
    item_rows = jnp.where(used, valid, 0).astype(I32)
    item_blk = jnp.where(used, jnp.arange(n_items, dtype=I32), jnp.maximum(n_used - 1, 0)).astype(I32)
    return tok_of_slot, slot_of_assign, item_e, item_rows, item_blk


def _moe_kernel(ie_ref, ir_ref, ib_ref, x_ref, wg_ref, wu_ref, wd_ref, o_ref, wg_s, wu_s, wd_s, x_s):
    it = pl.program_id(0)
    f = pl.program_id(1)
    nrows = ir_ref[it]
    half = x_ref.shape[1]

    @pl.when(f == 0)
    def _():
        o_ref[...] = jnp.zeros_like(o_ref)
        pk = x_ref[...]
        x_s[:, :half] = pltpu.bitcast(pk & jnp.int32(-65536), F32).astype(BF16)
        x_s[:, half:] = pltpu.bitcast(lax.shift_left(pk, 16), F32).astype(BF16)

    @pl.when(nrows > 0)
    def _():
        wg_s[...] = wg_ref[0, 0].astype(BF16)
        wu_s[...] = wu_ref[0, 0].astype(BF16)
        wd_s[...] = wd_ref[0, 0].astype(BF16)
        nsb = (nrows + MOE_SUB - 1) // MOE_SUB

        def sub_block(sb):
            r0 = pl.multiple_of(sb * MOE_SUB, MOE_SUB)
            x = x_s[pl.ds(r0, MOE_SUB), :]
            g = jnp.dot(x, wg_s[...], preferred_element_type=F32)
            u = jnp.dot(x, wu_s[...], preferred_element_type=F32)
            hmid = ((g * jax.nn.sigmoid(g)) * u).astype(BF16)
            o_ref[pl.ds(r0, MOE_SUB), :] += jnp.dot(hmid, wd_s[...], preferred_element_type=F32)

        def pair_body(pi, carry):
            sub_block(2 * pi)
            sub_block(2 * pi + 1)
            return carry

        lax.fori_loop(0, nsb // 2, pair_body, 0)

        @pl.when(nsb % 2 == 1)
        def _():
            sub_block(nsb - 1)


def _moe_experts(x_sorted, item_e, item_rows, item_blk, w_gate, w_up, w_down, layer):
    p, half = x_sorted.shape
    d = 2 * half
    n_items = p // MOE_ROWS
    dff = w_gate.shape[3]
    nf = dff // MOE_FT
    grid_spec = pltpu.PrefetchScalarGridSpec(
        num_scalar_prefetch=3,
        grid=(n_items, nf),
        in_specs=[
            pl.BlockSpec((MOE_ROWS, half), lambda i, f, ie, ir, ib: (ib[i], 0)),
            pl.BlockSpec((1, 1, d, MOE_FT),
                         lambda i, f, ie, ir, ib: (layer, ie[i], 0, jnp.where(ir[i] > 0, f, nf - 1))),
            pl.BlockSpec((1, 1, d, MOE_FT),
                         lambda i, f, ie, ir, ib: (layer, ie[i], 0, jnp.where(ir[i] > 0, f, nf - 1))),
            pl.BlockSpec((1, 1, MOE_FT, d),
                         lambda i, f, ie, ir, ib: (layer, ie[i], jnp.where(ir[i] > 0, f, nf - 1), 0)),
        ],
        out_specs=pl.BlockSpec((MOE_ROWS, d), lambda i, f, ie, ir, ib: (i, 0)),
        scratch_shapes=[pltpu.VMEM((d, MOE_FT), BF16), pltpu.VMEM((d, MOE_FT), BF16),
                        pltpu.VMEM((MOE_FT, d), BF16), pltpu.VMEM((MOE_ROWS, d), BF16)],
    )
    return pl.pallas_call(
        _moe_kernel,
        out_shape=jax.ShapeDtypeStruct((p, d), F32),
        grid_spec=grid_spec,
        compiler_params=_cparams(("arbitrary", "arbitrary")),
        name="moe_experts",
    )(item_e, item_rows, item_blk, x_sorted, w_gate, w_up, w_down)


def _moe(hn2, logits, router_b, w_gate, w_up, w_down, layer):
    experts, gates = _route(logits, router_b)
    tok_of_slot, slot_of_assign, item_e, item_rows, item_blk = _slot_tables(experts)
    x_sorted = lax.bitcast_convert_type(
        lax.bitcast_convert_type(hn2, F32).at[tok_of_slot].get(mode='promise_in_bounds'), I32)
    y_slot = _moe_experts(x_sorted, item_e, item_rows, item_blk, w_gate, w_up, w_down, layer)
    y0 = y_slot.at[slot_of_assign[:, 0]].get(mode='promise_in_bounds')
    y1 = y_slot.at[slot_of_assign[:, 1]].get(mode='promise_in_bounds')
    n = hn2.shape[0]
    gates_rep = jnp.concatenate([jnp.broadcast_to(gates[k][:, None], (n, 128)) for k in range(TOP_K)], axis=1)
    return y0, y1, gates_rep


def _moe_combine(h1, y0, y1, gates, mod):
    reps = h1.shape[1] // 128
    g0 = _lane_tile(gates[:, :128], reps)
    g1 = _lane_tile(gates[:, 128:], reps)
    return h1 + mod[5:6] * (g0 * y0 + g1 * y1)


def _bproj_kernel(h1_ref, y0_ref, y1_ref, gt_ref, modp_ref, mod_ref, n1g_ref, w_ref, kvg_ref, kig_ref,
                  wuk_ref, h_out, qa_out, ckv_out, qi_out, ki_out, wi_out):
    tm = h1_ref.shape[0]
    h = _moe_combine(h1_ref[...], y0_ref[...], y1_ref[...], gt_ref[...], modp_ref[0])
    h_out[...] = h
    mod = mod_ref[0]
    hn = _mod_rmsnorm(h, n1g_ref[...], mod[1:2], mod[0:1]).astype(BF16)
    proj = jnp.dot(hn, w_ref[...], preferred_element_type=F32)
    o1 = B_HEADS * B_HEAD_DIM
    o2 = o1 + B_KV_LATENT
    o3 = o2 + B_IDX_HEADS * B_IDX_DIM
    ckv = proj[:, o1:o2]
    ckv = ckv * lax.rsqrt(jnp.mean(ckv * ckv, axis=-1, keepdims=True) + EPS) * kvg_ref[...]
    ckv_out[...] = ckv.astype(BF16)
    tail = proj[:, o3:o3 + 128]
    ki = tail[:, :B_IDX_DIM]
    ki = ki * lax.rsqrt(jnp.mean(ki * ki, axis=-1, keepdims=True) + EPS) * kig_ref[...]
    ki_out[...] = ki.astype(BF16)
    wi = tail[:, B_IDX_DIM:B_IDX_DIM + B_IDX_HEADS] * (B_IDX_HEADS ** -0.5 * B_IDX_DIM ** -0.5)
    scale = B_HEAD_DIM ** -0.5
    for blk in range(tm // QBLK):
        r0, r1 = blk * QBLK, (blk + 1) * QBLK
        for hh in range(B_HEADS):
            qh = proj[r0:r1, hh * B_HEAD_DIM:(hh + 1) * B_HEAD_DIM].astype(BF16)
            qa = jnp.dot(qh, wuk_ref[hh], preferred_element_type=F32) * scale
            qa_out[blk, hh * QBLK:(hh + 1) * QBLK, :] = qa.astype(BF16)
        for hh in range(B_IDX_HEADS):
            qi_out[blk, hh * QBLK:(hh + 1) * QBLK, :] = proj[r0:r1, o2 + hh * B_IDX_DIM:
                                                            o2 + (hh + 1) * B_IDX_DIM].astype(BF16)
            wi_out[blk, hh * QBLK:(hh + 1) * QBLK, :] = jnp.broadcast_to(wi[r0:r1, hh:hh + 1], (QBLK, 128))


def _bproj(h1, y0, y1, gates, mod_prev, mod_l, n1g, b_w_in, kv_g, w_uk, kidx_g, seq):
    n, d = h1.shape
    tm = ROW_TILE
    per_b = seq // tm
    nq = n // QBLK
    qpb = tm // QBLK
    o1 = B_HEADS * B_HEAD_DIM
    o2 = o1 + B_KV_LATENT
    o3 = o2 + B_IDX_HEADS * B_IDX_DIM
    wcat = jnp.zeros((d, o3 + 128), F32).at[:, :b_w_in.shape[1]].set(b_w_in).astype(BF16)
    wuk_t = jnp.transpose(w_uk, (1, 2, 0)).astype(BF16)
    row = lambda i: (i, 0)
    modm = lambda i: (i // per_b, 0, 0)
    blk3 = lambda i: (i, 0, 0)
    return pl.pallas_call(
        _bproj_kernel,
        out_shape=(
            jax.ShapeDtypeStruct((n, d), F32),
            jax.ShapeDtypeStruct((nq, B_HEADS * QBLK, B_KV_LATENT), BF16),
            jax.ShapeDtypeStruct((n, B_KV_LATENT), BF16),
            jax.ShapeDtypeStruct((nq, B_IDX_HEADS * QBLK, B_IDX_DIM), BF16),
            jax.ShapeDtypeStruct((n, B_IDX_DIM), BF16),
            jax.ShapeDtypeStruct((nq, B_IDX_HEADS * QBLK, 128), F32),
        ),
        grid=(n // tm,),
        in_specs=[
            pl.BlockSpec((tm, d), row), pl.BlockSpec((tm, d), row), pl.BlockSpec((tm, d), row),
            pl.BlockSpec((tm, TOP_K * 128), row),
            pl.BlockSpec((1, 6, d), modm), pl.BlockSpec((1, 6, d), modm),
            _const_spec((1, d)),
            _const_spec((d, o3 + 128)),
            _const_spec((1, B_KV_LATENT)),
            _const_spec((1, B_IDX_DIM)),
            _const_spec((B_HEADS, B_HEAD_DIM, B_KV_LATENT)),
        ],
        out_specs=(
            pl.BlockSpec((tm, d), row),
            pl.BlockSpec((qpb, B_HEADS * QBLK, B_KV_LATENT), blk3),
            pl.BlockSpec((tm, B_KV_LATENT), row),
            pl.BlockSpec((qpb, B_IDX_HEADS * QBLK, B_IDX_DIM), blk3),
            pl.BlockSpec((tm, B_IDX_DIM), row),
            pl.BlockSpec((qpb, B_IDX_HEADS * QBLK, 128), blk3),
        ),
        compiler_params=_cparams(("arbitrary",)),
        name="dsa_proj",
    )(h1, y0, y1, gates, mod_prev, mod_l, n1g.reshape(1, d), wcat, kv_g.reshape(1, -1),
      kidx_g.reshape(1, -1), wuk_t)


def _t5_bucket(dist):
    n = jnp.maximum(dist, 0)
    exact = REL_BUCKETS // 2
    nf = jnp.maximum(n, 1).astype(F32)
    large = exact + (jnp.log(nf / exact) / math.log(REL_MAX_DIST / exact)
                     * (REL_BUCKETS - exact)).astype(I32)
    large = jnp.minimum(large, REL_BUCKETS - 1)
    return jnp.where(n < exact, n, large)


def _bias_tables(rel_bias):
    assert REL_MAX_DIST <= 128
    t = jnp.arange(128, dtype=I32)[:, None]
    s = jnp.arange(128, dtype=I32)[None, :]
    far = rel_bias[REL_BUCKETS - 1]
    diag = rel_bias[_t5_bucket(t - s)] - far
    prev = rel_bias[_t5_bucket(t - s + 128)] - far
    return jnp.stack([jnp.transpose(diag, (2, 0, 1)), jnp.transpose(prev, (2, 0, 1))])


def _attn_kernel(qa_ref, qi_ref, wi_ref, ckv_ref, ki_ref, bt_ref, o_ref, kbuf, m_scr, l_scr, a_scr, acc_scr,
                 s_scr, p_scr, madd_scr, kbuft, tau_scr):
    i = pl.program_id(1)
    nt = i + 1
    t_row = i * QBLK + lax.broadcasted_iota(I32, (QBLK, KTILE), 0)
    lane = lax.broadcasted_iota(I32, (QBLK, KTILE), 1)

    qi = qi_ref[0]

    def score_body(j, carry):
        k0 = pl.multiple_of(j * KTILE, KTILE)
        kt = ki_ref[0, pl.ds(k0, KTILE), :]
        p = lax.dot_general(qi, kt, (((1,), (1,)), ((), ())), preferred_element_type=F32)
        p = jnp.maximum(p, 0.0) * _lane_tile(wi_ref[0], KTILE // 128)
        sc = p[0:QBLK]
        for hh in range(1, B_IDX_HEADS):
            sc = sc + p[hh * QBLK:(hh + 1) * QBLK]
        bits = pltpu.bitcast(sc + 0.0, I32)
        key = jnp.where(bits < 0, bits ^ jnp.int32(0x7FFFFFFF), bits)
        key = jnp.where(k0 + lane <= t_row, key, jnp.int32(INT_MIN))
        kbuf[:, pl.ds(k0, KTILE)] = key
        kbuft[pl.ds(k0, KTILE), :] = key.T
        return carry

    lax.fori_loop(0, nt, score_body, 0)

    def fold_lanes(x):
        out = x[:, 0:128]
        for k in range(1, KTILE // 128):
            out = out + x[:, k * 128:(k + 1) * 128]
        return out

    def count_ge(cand):
        def body(j, acc):
            k0 = pl.multiple_of(j * KTILE, KTILE)
            return acc + fold_lanes(jnp.where(kbuf[:, pl.ds(k0, KTILE)] >= cand, 1, 0))
        acc = lax.fori_loop(0, nt, body, jnp.zeros((QBLK, 128), I32))
        return jnp.sum(acc, axis=-1, keepdims=True)

    def count_ge_t(cand_row):
        def body(j, acc):
            k0 = pl.multiple_of(j * KTILE, KTILE)
            hit = jnp.where(kbuft[pl.ds(k0, KTILE), :] >= cand_row, 1, 0)
            return acc + jnp.sum(hit.reshape(KTILE // 8, 8, QBLK), axis=0)
        acc = lax.fori_loop(0, nt, body, jnp.zeros((8, QBLK), I32))
        return jnp.sum(acc, axis=0, keepdims=True)

    def bit_step(b, lo, n_ge):
        cand = lo + lax.shift_left(jnp.int32(1), 31 - b)
        cnt = count_ge_t(cand)
        take = cnt >= B_TOPK_MAX
        return jnp.where(take, cand, lo), jnp.where(take, cnt, n_ge)

    def bit_cond(c):
        return (c[0] < 32) & (c[3] == 0)

    def bit_body(c):
        b, lo, n_ge, _ = c
        lo, n_ge = bit_step(b, lo, n_ge)
        lo, n_ge = bit_step(b + 1, lo, n_ge)
        done = (jnp.max(n_ge) == B_TOPK_MAX).astype(I32)
        return b + 2, lo, n_ge, done

    _, tau_row, n_ge_row, _ = lax.while_loop(
        bit_cond, bit_body,
        (jnp.int32(0), jnp.full((1, QBLK), INT_MIN, I32), jnp.full((1, QBLK), nt * KTILE, I32),
         (i < 1).astype(I32)))
    tau_row = jnp.maximum(tau_row, jnp.int32(INT_MIN + 1))
    excess = (i >= 1) & (jnp.max(n_ge_row) > B_TOPK_MAX)

    def to_rows(row):
        return jnp.broadcast_to(row, (QBLK, QBLK)).T

    tau_scr[...] = to_rows(tau_row)

    @pl.when(excess)
    def _():
        tau = tau_scr[:, 0:1]
        n_ge = to_rows(n_ge_row)[:, 0:1]
        n_gt = count_ge(tau + 1)
        need = B_TOPK_MAX - n_gt

        def count_eq_before(pos):
            def body(j, acc):
                k0 = pl.multiple_of(j * KTILE, KTILE)
                hit = (kbuf[:, pl.ds(k0, KTILE)] == tau) & (k0 + lane < pos)
                return acc + fold_lanes(jnp.where(hit, 1, 0))
            acc = lax.fori_loop(0, nt, body, jnp.zeros((QBLK, 128), I32))
            return jnp.sum(acc, axis=-1, keepdims=True)

        def pos_body(b, pos):
            cand = pos + lax.shift_left(jnp.int32(1), 12 - b)
            return jnp.where(count_eq_before(cand) < need, cand, pos)

        pos = lax.fori_loop(0, 13, pos_body, jnp.zeros((QBLK, 1), I32))

        def drop_body(j, carry):
            k0 = pl.multiple_of(j * KTILE, KTILE)
            kk = kbuf[:, pl.ds(k0, KTILE)]
            drop = (kk == tau) & (k0 + lane > pos) & (n_ge > B_TOPK_MAX)
            kbuf[:, pl.ds(k0, KTILE)] = jnp.where(drop, jnp.int32(INT_MIN), kk)
            return carry

        lax.fori_loop(0, nt, drop_body, 0)

    m_scr[...] = jnp.full_like(m_scr, NEG_BIG)
    l_scr[...] = jnp.zeros_like(l_scr)
    acc_scr[...] = jnp.zeros_like(acc_scr)

    def attend(k0, width, tile_off):
        kv = ckv_ref[0, pl.ds(k0, width), :]
        s_scr[:, :width] = lax.dot_general(qa_ref[0], kv, (((1,), (1,)), ((), ())),
                                           preferred_element_type=F32)
        tau_t = _lane_tile(tau_scr[...], width // QBLK)
        madd_scr[:, :width] = jnp.where(kbuf[:, pl.ds(k0, width)] >= tau_t, 0.0, NEG_BIG)
        for r0 in range(0, B_HEADS * QBLK, SM_ROWS):
            r1 = r0 + SM_ROWS
            hh, q0 = r0 // QBLK, r0 % QBLK
            parts = []
            for kc in range(width // 128):
                sp = s_scr[r0:r1, kc * 128:(kc + 1) * 128] + madd_scr[q0:q0 + SM_ROWS, kc * 128:(kc + 1) * 128]
                if tile_off is not None:
                    rel = (QBLK // 128) * tile_off + q0 // 128 - kc
                    if rel in (0, 1):
                        sp = sp + bt_ref[rel, hh, q0 % 128:q0 % 128 + SM_ROWS, :]
                parts.append(sp)
            smax = parts[0]
            for sp in parts[1:]:
                smax = jnp.maximum(smax, sp)
            m_old = m_scr[r0:r1, :]
            m_new = jnp.maximum(m_old, jnp.max(smax, axis=-1, keepdims=True))
            alpha = jnp.exp(m_old - m_new)
            psum = None
            for kc, sp in enumerate(parts):
                p = jnp.exp(sp - m_new)
                p_scr[r0:r1, kc * 128:(kc + 1) * 128] = p.astype(BF16)
                psum = p if psum is None else psum + p
            l_scr[r0:r1, :] = alpha * l_scr[r0:r1, :] + jnp.sum(psum, axis=-1, keepdims=True)
            m_scr[r0:r1, :] = m_new
            a_scr[r0:r1, :] = alpha
        pv = jnp.dot(p_scr[:, :width], kv, preferred_element_type=F32)
        acc_scr[...] = _lane_tile(a_scr[...], B_KV_LATENT // 128) * acc_scr[...] + pv

    n_far = jnp.maximum(i - 1, 0)
    per_far = FAR_TILE // KTILE

    def far_body(jf, carry):
        attend(pl.multiple_of(jf * FAR_TILE, FAR_TILE), FAR_TILE, None)
        return carry

    lax.fori_loop(0, n_far // per_far, far_body, 0)

    if per_far > 1:
        @pl.when(n_far % per_far == 1)
        def _():
            attend(pl.multiple_of((n_far - 1) * KTILE, KTILE), KTILE, None)

    @pl.when(i >= 1)
    def _():
        attend(pl.multiple_of((i - 1) * KTILE, KTILE), KTILE, 1)

    attend(pl.multiple_of(i * KTILE, KTILE), KTILE, 0)
    inv_l = 1.0 / l_scr[...]
    o_ref[0] = (acc_scr[...] * _lane_tile(inv_l, B_KV_LATENT // 128)).astype(BF16)


def _attention(qa, qi, wi, ckv, ki, btab, bsz, seq):
    nqb = seq // QBLK
    gq = lambda b, i: (b * nqb + i, 0, 0)
    gb = lambda b, i: (b, 0, 0)
    return pl.pallas_call(
        _attn_kernel,
        out_shape=jax.ShapeDtypeStruct(qa.shape, BF16),
        grid=(bsz, nqb),
        in_specs=[
            pl.BlockSpec((1, B_HEADS * QBLK, B_KV_LATENT), gq),
            pl.BlockSpec((1, B_IDX_HEADS * QBLK, B_IDX_DIM), gq),
            pl.BlockSpec((1, B_IDX_HEADS * QBLK, 128), gq),
            pl.BlockSpec((1, seq, B_KV_LATENT), gb, pipeline_mode=pl.Buffered(1)),
            pl.BlockSpec((1, seq, B_IDX_DIM), gb, pipeline_mode=pl.Buffered(1)),
            pl.BlockSpec((2, B_HEADS, 128, 128), lambda b, i: (0, 0, 0, 0), pipeline_mode=pl.Buffered(1)),
        ],
        out_specs=pl.BlockSpec((1, B_HEADS * QBLK, B_KV_LATENT), gq),
        scratch_shapes=[
            pltpu.VMEM((QBLK, seq), I32),
            pltpu.VMEM((B_HEADS * QBLK, 128), F32),
            pltpu.VMEM((B_HEADS * QBLK, 128), F32),
            pltpu.VMEM((B_HEADS * QBLK, 128), F32),
            pltpu.VMEM((B_HEADS * QBLK, B_KV_LATENT), F32),
            pltpu.VMEM((B_HEADS * QBLK, FAR_TILE), F32),
            pltpu.VMEM((B_HEADS * QBLK, FAR_TILE), BF16),
            pltpu.VMEM((QBLK, FAR_TILE), F32),
            pltpu.VMEM((seq, QBLK), I32),
            pltpu.VMEM((QBLK, QBLK), I32),
        ],
        compiler_params=_cparams(("arbitrary", "arbitrary")),
        name="dsa_attention",
    )(qa, qi, wi, ckv.reshape(bsz, seq, -1), ki.reshape(bsz, seq, -1), btab)


def _bout_kernel(h_ref, ol_ref, mod_ref, wuv_ref, wout_ref, n2g_ref, rw_ref, h_out, hn_out, lg_out, o_scr):
    mod = mod_ref[0]
    for blk in range(ol_ref.shape[0]):
        for hh in range(B_HEADS):
            oh = jnp.dot(ol_ref[blk, hh * QBLK:(hh + 1) * QBLK, :], wuv_ref[hh], preferred_element_type=F32)
            o_scr[blk * QBLK:(blk + 1) * QBLK, hh * B_V_DIM:(hh + 1) * B_V_DIM] = oh.astype(BF16)
    y = jnp.dot(o_scr[...], wout_ref[...], preferred_element_type=F32)
    _residual_epilogue(h_ref[...], y, mod, n2g_ref[...], rw_ref[...], h_out, hn_out, lg_out)


def _bout(h, o_lat, mod_l, w_uv, w_out, n2g, router_w, seq):
    n, d = h.shape
    tm = ROW_TILE
    per_b = seq // tm
    qpb = tm // QBLK
    ne = router_w.shape[1]
    wuv_t = jnp.transpose(w_uv, (1, 0, 2)).astype(BF16)
    row = lambda i: (i, 0)
    return pl.pallas_call(
        _bout_kernel,
        out_shape=(jax.ShapeDtypeStruct((n, d), F32), jax.ShapeDtypeStruct((n, d // 2), I32),
                   jax.ShapeDtypeStruct((n, ne), F32)),
        grid=(n // tm,),
        in_specs=[
            pl.BlockSpec((tm, d), row),
            pl.BlockSpec((qpb, B_HEADS * QBLK, B_KV_LATENT), lambda i: (i, 0, 0)),
            pl.BlockSpec((1, 6, d), lambda i: (i // per_b, 0, 0)),
            _const_spec((B_HEADS, B_KV_LATENT, B_V_DIM)),
            _const_spec((B_HEADS * B_V_DIM, d)),
            _const_spec((1, d)),
            _const_spec((d, ne)),
        ],
        out_specs=(pl.BlockSpec((tm, d), row), pl.BlockSpec((tm, d // 2), row), pl.BlockSpec((tm, ne), row)),
        scratch_shapes=[pltpu.VMEM((tm, B_HEADS * B_V_DIM), BF16)],
        compiler_params=_cparams(("arbitrary",)),
        name="dsa_out",
    )(h, o_lat, mod_l, wuv_t, w_out.astype(BF16), n2g.reshape(1, d), router_w)


def _final_kernel(h1_ref, y0_ref, y1_ref, gt_ref, mod_ref, g_ref, o_ref):
    h = _moe_combine(h1_ref[...], y0_ref[...], y1_ref[...], gt_ref[...], mod_ref[0])
    o_ref[...] = h * lax.rsqrt(jnp.mean(h * h, axis=-1, keepdims=True) + EPS) * g_ref[...]


def _final(h1, y0, y1, gates, mod_l, final_g, seq):
    n, d = h1.shape
    tm = 512
    per_b = seq // tm
    row = lambda i: (i, 0)
    return pl.pallas_call(
        _final_kernel,
        out_shape=jax.ShapeDtypeStruct((n, d), F32),
        grid=(n // tm,),
        in_specs=[pl.BlockSpec((tm, d), row), pl.BlockSpec((tm, d), row), pl.BlockSpec((tm, d), row),
                  pl.BlockSpec((tm, TOP_K * 128), row), pl.BlockSpec((1, 6, d), lambda i: (i // per_b, 0, 0)),
                  _const_spec((1, d))],
        out_specs=pl.BlockSpec((tm, d), row),
        compiler_params=_cparams(("arbitrary",)),
        name="final_norm",
    )(h1, y0, y1, gates, mod_l, final_g.reshape(1, d))


def kernel(x, c, ada_w, ada_b, norm1_g, norm2_g, a_w_in, a_ln_g, a_ln_b, a_w_sp, a_b_sp, a_w_out, b_w_in,
           b_kv_norm_g, b_w_uk, b_w_uv, b_kidx_g, b_w_out, rel_bias, router_w, router_b, moe_w_gate,
           moe_w_up, moe_w_down, final_g):
    bsz, seq, d = x.shape
    n = bsz * seq
    assert seq % max(QBLK, GMLP_ROWS, ROW_TILE, 512) == 0 and d % 256 == 0
    assert min(B_TOPK_MAX, seq // 4) == B_TOPK_MAX and QBLK >= B_TOPK_MAX and KTILE == QBLK
    assert ada_w.shape[0] == 2 and moe_w_gate.shape[1] == N_EXPERTS and MOE_ROWS % MOE_SUB == 0
    mod = _adaln(c, ada_w, ada_b).reshape(ada_w.shape[0], bsz, 6, d)
    rw_pad = jnp.zeros((d, 128), F32).at[:, :N_EXPERTS].set(router_w)
    h = x.reshape(n, d)

    h1, hn2, logits = _gmlp_layer(h, mod[0], norm1_g[0], a_w_in[0], a_ln_g[0], a_ln_b[0], a_w_sp[0],
                                  a_b_sp[0], a_w_out[0], norm2_g[0], rw_pad, seq)
    y0, y1, gates = _moe(hn2, logits, router_b, moe_w_gate, moe_w_up, moe_w_down, 0)

    h, qa, ckv, qi, ki, wi = _bproj(h1, y0, y1, gates, mod[0], mod[1], norm1_g[1], b_w_in[0],
                                    b_kv_norm_g[0], b_w_uk[0], b_kidx_g[0], seq)
    o_lat = _attention(qa, qi, wi, ckv, ki, _bias_tables(rel_bias), bsz, seq)
    h1, hn2, logits = _bout(h, o_lat, mod[1], b_w_uv[0], b_w_out[0], norm2_g[1], rw_pad, seq)
    y0, y1, gates = _moe(hn2, logits, router_b, moe_w_gate, moe_w_up, moe_w_down, 1)

    out = _final(h1, y0, y1, gates, mod[1], final_g, seq)
    return out.reshape(bsz, seq, d)
```

```python
import functools
import math

import jax
import jax.numpy as jnp
from jax import lax
from jax.experimental import pallas as pl
from jax.experimental.pallas import tpu as pltpu

F32 = jnp.float32
BF16 = jnp.bfloat16
I32 = jnp.int32
HIGHEST = lax.Precision.HIGHEST

EPS = 1e-6
A_CHUNK = 128
A_GROUPS = 8
B_HEADS = 16
B_HEAD_DIM = 64
B_V_DIM = 64
B_KV_LATENT = 256
B_IDX_HEADS = 8
B_IDX_DIM = 64
B_TOPK_MAX = 256
QBLK = 256
KTILE = 256
FAR_TILE = 256
SM_ROWS = 128
REL_BUCKETS = 32
REL_MAX_DIST = 128
N_EXPERTS = 16
N_GROUPS = 4
EXPERTS_PER_GROUP = 4
TOP_K = 2
MOE_ROWS = 1280
MOE_SUB = 256
MOE_FT = 512

ROW_TILE = 256
GMLP_ROWS = 512
VMEM_LIMIT = 60 * 1024 * 1024

INT_MIN = -2 ** 31
NEG_BIG = -1e30


def _cparams(sem):
    return pltpu.CompilerParams(dimension_semantics=sem, vmem_limit_bytes=VMEM_LIMIT)


def _lane_tile(x, k):
    return x if k == 1 else jnp.concatenate([x] * k, axis=1)


def _mod_rmsnorm(h, g, scale, shift):
    ms = jnp.mean(h * h, axis=-1, keepdims=True)
    return (h * lax.rsqrt(ms + EPS) * g) * (1.0 + scale) + shift


def _gelu_tanh(x):
    c = math.sqrt(2.0 / math.pi)
    return 0.5 * x * (1.0 + jnp.tanh(c * (x + 0.044715 * (x * x * x))))


def _adaln_kernel(c_ref, w_ref, b_ref, o_ref):
    c = c_ref[...]
    sc = c * jax.nn.sigmoid(c)
    o_ref[0] = jnp.dot(sc, w_ref[0], precision=HIGHEST, preferred_element_type=F32) + b_ref[0]


def _adaln(c, ada_w, ada_b):
    depth, d, e = ada_w.shape
    bsz = c.shape[0]
    bp = 8
    c_pad = jnp.zeros((bp, d), F32).at[:bsz].set(c)
    tn = 1024
    out = pl.pallas_call(
        _adaln_kernel,
        out_shape=jax.ShapeDtypeStruct((depth, bp, e), F32),
        grid=(depth, e // tn),
        in_specs=[
            pl.BlockSpec((bp, d), lambda l, j: (0, 0)),
            pl.BlockSpec((1, d, tn), lambda l, j: (l, 0, j)),
            pl.BlockSpec((1, 1, tn), lambda l, j: (l, 0, j)),
        ],
        out_specs=pl.BlockSpec((1, bp, tn), lambda l, j: (l, 0, j)),
        compiler_params=_cparams(("arbitrary", "arbitrary")),
        name="adaln",
    )(c_pad, ada_w, ada_b.reshape(depth, 1, e))
    return out[:, :bsz]


def _residual_epilogue(h, y, mod, n2g, rw, h_out, hn_out, lg_out):
    h1 = h + mod[2:3] * y
    h_out[...] = h1
    hn2 = _mod_rmsnorm(h1, n2g, mod[4:5], mod[3:4])
    hn_hi = hn2.astype(BF16)
    bits = pltpu.bitcast(hn_hi.astype(F32), I32)
    half = bits.shape[1] // 2
    hn_out[...] = bits[:, :half] | lax.shift_right_logical(bits[:, half:], 16)
    hn_lo = (hn2 - hn_hi.astype(F32)).astype(BF16)
    rw_hi = rw.astype(BF16)
    rw_lo = (rw - rw_hi.astype(F32)).astype(BF16)
    lg_out[...] = (jnp.dot(hn_hi, rw_hi, preferred_element_type=F32)
                   + (jnp.dot(hn_lo, rw_hi, preferred_element_type=F32)
                      + jnp.dot(hn_hi, rw_lo, preferred_element_type=F32)))


def _gmlp_kernel(h_ref, mod_ref, n1g_ref, win_ref, lng_ref, lnb_ref, wsp_ref, bsp_ref, wout_ref,
                 n2g_ref, rw_ref, h_out, hn_out, lg_out, u_scr, v_scr, s_scr):
    tm = h_ref.shape[0]
    inner = u_scr.shape[1]
    gw = inner // A_GROUPS
    tn = 512
    h = h_ref[...]
    mod = mod_ref[0]
    hn = _mod_rmsnorm(h, n1g_ref[...], mod[1:2], mod[0:1]).astype(BF16)
    for j in range(2 * inner // tn):
        z = _gelu_tanh(jnp.dot(hn, win_ref[:, j * tn:(j + 1) * tn], preferred_element_type=F32))
        if j < inner // tn:
            u_scr[:, j * tn:(j + 1) * tn] = z
        else:
            jj = j - inner // tn
            v_scr[:, jj * tn:(jj + 1) * tn] = z
    vsum = jnp.zeros((tm, 1), F32)
    for j in range(inner // tn):
        vsum = vsum + jnp.sum(v_scr[:, j * tn:(j + 1) * tn], axis=-1, keepdims=True)
    mu = vsum * (1.0 / inner)
    vsq = jnp.zeros((tm, 1), F32)
    for j in range(inner // tn):
        d = v_scr[:, j * tn:(j + 1) * tn] - mu
        vsq = vsq + jnp.sum(d * d, axis=-1, keepdims=True)
    rstd = lax.rsqrt(vsq * (1.0 / inner) + EPS)
    row = lax.broadcasted_iota(I32, (A_CHUNK, A_CHUNK), 0)
    col = lax.broadcasted_iota(I32, (A_CHUNK, A_CHUNK), 1)
    tril = row >= col
    for g in range(A_GROUPS):
        ws = jnp.where(tril, wsp_ref[g], 0.0).astype(BF16)
        bcol = bsp_ref[:, g:g + 1]
        lg = lng_ref[:, g * gw:(g + 1) * gw]
        lb = lnb_ref[:, g * gw:(g + 1) * gw]
        for c in range(tm // A_CHUNK):
            r0, r1 = c * A_CHUNK, (c + 1) * A_CHUNK
            vt = v_scr[r0:r1, g * gw:(g + 1) * gw]
            vn = ((vt - mu[r0:r1]) * rstd[r0:r1]) * lg + lb
            fv = jnp.dot(ws, vn.astype(BF16), preferred_element_type=F32) + bcol
            s_scr[r0:r1, g * gw:(g + 1) * gw] = (u_scr[r0:r1, g * gw:(g + 1) * gw] * fv).astype(BF16)
    y = jnp.dot(s_scr[...], wout_ref[...], preferred_element_type=F32)
    _residual_epilogue(h, y, mod, n2g_ref[...], rw_ref[...], h_out, hn_out, lg_out)


def _const_spec(shape):
    nd = len(shape)
    return pl.BlockSpec(shape, lambda i, _nd=nd: (0,) * _nd, pipeline_mode=pl.Buffered(1))


def _gmlp_layer(h, mod_l, n1g, w_in, ln_g, ln_b, w_sp, b_sp, w_out, n2g, router_w, seq):
    n, d = h.shape
    inner = w_out.shape[0]
    tm = GMLP_ROWS
    per_b = seq // tm
    ne = router_w.shape[1]
    return pl.pallas_call(
        _gmlp_kernel,
        out_shape=(jax.ShapeDtypeStruct((n, d), F32), jax.ShapeDtypeStruct((n, d // 2), I32),
                   jax.ShapeDtypeStruct((n, ne), F32)),
        grid=(n // tm,),
        in_specs=[
            pl.BlockSpec((tm, d), lambda i: (i, 0)),
            pl.BlockSpec((1, 6, d), lambda i: (i // per_b, 0, 0)),
            _const_spec((1, d)),
            _const_spec((d, 2 * inner)),
            _const_spec((1, inner)),
            _const_spec((1, inner)),
            _const_spec((A_GROUPS, A_CHUNK, A_CHUNK)),
            _const_spec((A_CHUNK, A_GROUPS)),
            _const_spec((inner, d)),
            _const_spec((1, d)),
            _const_spec((d, ne)),
        ],
        out_specs=(pl.BlockSpec((tm, d), lambda i: (i, 0)), pl.BlockSpec((tm, d // 2), lambda i: (i, 0)),
                   pl.BlockSpec((tm, ne), lambda i: (i, 0))),
        scratch_shapes=[pltpu.VMEM((tm, inner), F32), pltpu.VMEM((tm, inner), F32),
                        pltpu.VMEM((tm, inner), BF16)],
        compiler_params=_cparams(("arbitrary",)),
        name="gmlp_layer",
    )(h, mod_l, n1g.reshape(1, d), w_in.astype(BF16), ln_g.reshape(1, inner), ln_b.reshape(1, inner),
      w_sp, b_sp.T, w_out.astype(BF16), n2g.reshape(1, d), router_w)


def _first_max4(rows):
    m = jnp.maximum(jnp.maximum(rows[0], rows[1]), jnp.maximum(rows[2], rows[3]))
    idx = jnp.where(rows[0] == m, 0, jnp.where(rows[1] == m, 1, jnp.where(rows[2] == m, 2, 3)))
    return m, idx.astype(I32)


def _route_kernel(lt_ref, rb_ref, e_ref, g_ref):
    aff = jax.nn.sigmoid(lt_ref[...])
    sel = aff + rb_ref[...]
    neg = jnp.float32(-jnp.inf)
    g_score, g_i1, g_i2 = [], [], []
    for g in range(N_GROUPS):
        rows = [sel[4 * g + k:4 * g + k + 1, :] for k in range(EXPERTS_PER_GROUP)]
        m1, i1 = _first_max4(rows)
        rest = [jnp.where(i1 == k, neg, rows[k]) for k in range(EXPERTS_PER_GROUP)]
        m2, i2 = _first_max4(rest)
        g_score.append(m1 + m2)
        g_i1.append(i1)
        g_i2.append(i2)
    _, grp = _first_max4(g_score)
    l1 = jnp.zeros_like(grp)
    l2 = jnp.zeros_like(grp)
    for g in range(N_GROUPS):
        l1 = jnp.where(grp == g, g_i1[g], l1)
        l2 = jnp.where(grp == g, g_i2[g], l2)
    e1 = grp * EXPERTS_PER_GROUP + l1
    e2 = grp * EXPERTS_PER_GROUP + l2
    a1 = jnp.zeros_like(g_score[0])
    a2 = jnp.zeros_like(g_score[0])
    for e in range(N_EXPERTS):
        ae = aff[e:e + 1, :]
        a1 = jnp.where(e1 == e, ae, a1)
        a2 = jnp.where(e2 == e, ae, a2)
    tot = a1 + a2
    e_ref[0:1, :] = e1
    e_ref[1:2, :] = e2
    g_ref[0:1, :] = a1 / tot
    g_ref[1:2, :] = a2 / tot


def _route(logits, router_b):
    n, ne = logits.shape[0], N_EXPERTS
    tn = min(2048, n)
    return pl.pallas_call(
        _route_kernel,
        out_shape=(jax.ShapeDtypeStruct((TOP_K, n), I32), jax.ShapeDtypeStruct((TOP_K, n), F32)),
        grid=(n // tn,),
        in_specs=[pl.BlockSpec((ne, tn), lambda i: (0, i)), pl.BlockSpec((ne, 1), lambda i: (0, 0))],
        out_specs=(pl.BlockSpec((TOP_K, tn), lambda i: (0, i)), pl.BlockSpec((TOP_K, tn), lambda i: (0, i))),
        compiler_params=_cparams(("arbitrary",)),
        name="route",
    )(logits[:, :ne].T, router_b.reshape(ne, 1))


def _slot_kernel(e_ref, slot_ref, cnt_ref):
    n = e_ref.shape[1]
    nblk = n // 128
    eid = lax.broadcasted_iota(I32, (N_EXPERTS, 128), 0)
    r = lax.broadcasted_iota(I32, (128, 128), 0)
    c = lax.broadcasted_iota(I32, (128, 128), 1)
    upper = jnp.where(r <= c, 1.0, 0.0).astype(BF16)

    def hits(b):
        o = pl.multiple_of(b * 128, 128)
        t1 = e_ref[0:1, pl.ds(o, 128)] == eid
        t2 = e_ref[1:2, pl.ds(o, 128)] == eid
        return o, t1, t2, jnp.where(t1, 1.0, 0.0) + jnp.where(t2, 1.0, 0.0)

    def count_body(b, acc):
        return acc + hits(b)[3]

    cnt = jnp.sum(lax.fori_loop(0, nblk, count_body, jnp.zeros((N_EXPERTS, 128), F32)),
                  axis=-1, keepdims=True)
    cnt_ref[...] = cnt.astype(I32)
    padded = jnp.zeros_like(cnt)
    for k in range(pl.cdiv(n * TOP_K, MOE_ROWS)):
        padded = padded + jnp.where(cnt > k * MOE_ROWS, float(MOE_ROWS), 0.0)
    er = lax.broadcasted_iota(I32, (N_EXPERTS, N_EXPERTS), 0)
    ec = lax.broadcasted_iota(I32, (N_EXPERTS, N_EXPERTS), 1)
    padded_row = jnp.sum(jnp.where(er == ec, padded, 0.0), axis=0, keepdims=True)
    start = jnp.sum(jnp.where(ec < er, padded_row, 0.0), axis=-1, keepdims=True)

    def slot_body(b, carry):
        o, t1, t2, t = hits(b)
        incl = jnp.dot(t.astype(BF16), upper, preferred_element_type=F32)
        pos = carry + incl - t
        slot_ref[0:1, pl.ds(o, 128)] = jnp.sum(jnp.where(t1, pos, 0.0), axis=0, keepdims=True).astype(I32)
        slot_ref[1:2, pl.ds(o, 128)] = jnp.sum(jnp.where(t2, pos, 0.0), axis=0, keepdims=True).astype(I32)
        return carry + incl[:, 127:128]

    lax.fori_loop(0, nblk, slot_body, start)


def _slot_tables(experts):
    n = experts.shape[1]
    a = n * TOP_K
    slot, counts = pl.pallas_call(
        _slot_kernel,
        out_shape=(jax.ShapeDtypeStruct((TOP_K, n), I32), jax.ShapeDtypeStruct((N_EXPERTS, 1), I32)),
        compiler_params=pltpu.CompilerParams(vmem_limit_bytes=VMEM_LIMIT),
        name="slots",
    )(experts)
    counts = counts.reshape(N_EXPERTS)
    padded = ((counts + MOE_ROWS - 1) // MOE_ROWS) * MOE_ROWS
    pad_end = jnp.cumsum(padded)
    pad_start = pad_end - padded
    n_items = pl.cdiv(a, MOE_ROWS) + N_EXPERTS
    p = n_items * MOE_ROWS
    tok = jnp.broadcast_to(jnp.arange(n, dtype=I32)[None, :], (TOP_K, n))
    tok_of_slot = jnp.zeros((p,), I32).at[slot.reshape(a)].set(tok.reshape(a))
    slot_of_assign = slot.T
    item_start = jnp.arange(n_items, dtype=I32) * MOE_ROWS
    n_used = pad_end[-1] // MOE_ROWS
    item_e_raw = jnp.minimum(jnp.searchsorted(pad_end, item_start, side='right'), N_EXPERTS - 1).astype(I32)
    used = item_start < pad_end[-1]
    last_e = item_e_raw[jnp.maximum(n_used - 1, 0)]
    item_e = jnp.where(used, item_e_raw, last_e).astype(I32)
    valid = jnp.clip(counts[item_e_raw] - (item_start - pad_start[item_e_raw]), 0, MOE_ROWS)
    item_rows = jnp.where(used, valid, 0).astype(I32)
    item_blk = jnp.where(used, jnp.arange(n_items, dtype=I32), jnp.maximum(n_used - 1, 0)).astype(I32)
    return tok_of_slot, slot_of_assign, item_e, item_rows, item_blk


def _moe_kernel(ie_ref, ir_ref, ib_ref, x_ref, wg_ref, wu_ref, wd_ref, o_ref, wg_s, wu_s, wd_s, x_s):
    it = pl.program_id(0)
    f = pl.program_id(1)
    nrows = ir_ref[it]
    half = x_ref.shape[1]

    @pl.when(f == 0)
    def _():
        o_ref[...] = jnp.zeros_like(o_ref)
        pk = x_ref[...]
        x_s[:, :half] = pltpu.bitcast(pk & jnp.int32(-65536), F32).astype(BF16)
        x_s[:, half:] = pltpu.bitcast(lax.shift_left(pk, 16), F32).astype(BF16)

    @pl.when(nrows > 0)
    def _():
        wg_s[...] = wg_ref[0, 0].astype(BF16)
        wu_s[...] = wu_ref[0, 0].astype(BF16)
        wd_s[...] = wd_ref[0, 0].astype(BF16)
        nsb = (nrows + MOE_SUB - 1) // MOE_SUB

        def sub_block(sb):
            r0 = pl.multiple_of(sb * MOE_SUB, MOE_SUB)
            x = x_s[pl.ds(r0, MOE_SUB), :]
            g = jnp.dot(x, wg_s[...], preferred_element_type=F32)
            u = jnp.dot(x, wu_s[...], preferred_element_type=F32)
            hmid = ((g * jax.nn.sigmoid(g)) * u).astype(BF16)
            o_ref[pl.ds(r0, MOE_SUB), :] += jnp.dot(hmid, wd_s[...], preferred_element_type=F32)

        def quad_body(qi, carry):
            for u in range(4):
                sub_block(4 * qi + u)
            return carry

        lax.fori_loop(0, nsb // 4, quad_body, 0)
        done = (nsb // 4) * 4

        @pl.when(nsb % 4 >= 2)
        def _():
            sub_block(done)
            sub_block(done + 1)

        @pl.when(nsb % 2 == 1)
        def _():
            sub_block(nsb - 1)


def _moe_experts(x_sorted, item_e, item_rows, item_blk, w_gate, w_up, w_down, layer):
    p, half = x_sorted.shape
    d = 2 * half
    n_items = p // MOE_ROWS
    dff = w_gate.shape[3]
    nf = dff // MOE_FT
    grid_spec = pltpu.PrefetchScalarGridSpec(
        num_scalar_prefetch=3,
        grid=(n_items, nf),
        in_specs=[
            pl.BlockSpec((MOE_ROWS, half), lambda i, f, ie, ir, ib: (ib[i], 0)),
            pl.BlockSpec((1, 1, d, MOE_FT),
                         lambda i, f, ie, ir, ib: (layer, ie[i], 0, jnp.where(ir[i] > 0, f, nf - 1))),
            pl.BlockSpec((1, 1, d, MOE_FT),
                         lambda i, f, ie, ir, ib: (layer, ie[i], 0, jnp.where(ir[i] > 0, f, nf - 1))),
            pl.BlockSpec((1, 1, MOE_FT, d),
                         lambda i, f, ie, ir, ib: (layer, ie[i], jnp.where(ir[i] > 0, f, nf - 1), 0)),
        ],
        out_specs=pl.BlockSpec((MOE_ROWS, d), lambda i, f, ie, ir, ib: (i, 0)),
        scratch_shapes=[pltpu.VMEM((d, MOE_FT), BF16), pltpu.VMEM((d, MOE_FT), BF16),
                        pltpu.VMEM((MOE_FT, d), BF16), pltpu.VMEM((MOE_ROWS, d), BF16)],
    )
    return pl.pallas_call(
        _moe_kernel,
        out_shape=jax.ShapeDtypeStruct((p, d), F32),
        grid_spec=grid_spec,
        compiler_params=_cparams(("arbitrary", "arbitrary")),
        name="moe_experts",
    )(item_e, item_rows, item_blk, x_sorted, w_gate, w_up, w_down)


def _moe(hn2, logits, router_b, w_gate, w_up, w_down, layer):
    experts, gates = _route(logits, router_b)
    tok_of_slot, slot_of_assign, item_e, item_rows, item_blk = _slot_tables(experts)
    x_sorted = hn2.at[tok_of_slot].get(mode='promise_in_bounds')
    y_slot = _moe_experts(x_sorted, item_e, item_rows, item_blk, w_gate, w_up, w_down, layer)
    y0 = y_slot.at[slot_of_assign[:, 0]].get(mode='promise_in_bounds')
    y1 = y_slot.at[slot_of_assign[:, 1]].get(mode='promise_in_bounds')
    n = hn2.shape[0]
    gates_rep = jnp.concatenate([jnp.broadcast_to(gates[k][:, None], (n, 128)) for k in range(TOP_K)], axis=1)
    return y0, y1, gates_rep


def _moe_combine(h1, y0, y1, gates, mod):
    reps = h1.shape[1] // 128
    g0 = _lane_tile(gates[:, :128], reps)
    g1 = _lane_tile(gates[:, 128:], reps)
    return h1 + mod[5:6] * (g0 * y0 + g1 * y1)


def _bproj_kernel(h1_ref, y0_ref, y1_ref, gt_ref, modp_ref, mod_ref, n1g_ref, w_ref, kvg_ref, kig_ref,
                  wuk_ref, h_out, qa_out, ckv_out, qi_out, ki_out, wi_out):
    tm = h1_ref.shape[0]
    h = _moe_combine(h1_ref[...], y0_ref[...], y1_ref[...], gt_ref[...], modp_ref[0])
    h_out[...] = h
    mod = mod_ref[0]
    hn = _mod_rmsnorm(h, n1g_ref[...], mod[1:2], mod[0:1]).astype(BF16)
    proj = jnp.dot(hn, w_ref[...], preferred_element_type=F32)
    o1 = B_HEADS * B_HEAD_DIM
    o2 = o1 + B_KV_LATENT
    o3 = o2 + B_IDX_HEADS * B_IDX_DIM
    ckv = proj[:, o1:o2]
    ckv = ckv * lax.rsqrt(jnp.mean(ckv * ckv, axis=-1, keepdims=True) + EPS) * kvg_ref[...]
    ckv_out[...] = ckv.astype(BF16)
    tail = proj[:, o3:o3 + 128]
    ki = tail[:, :B_IDX_DIM]
    ki = ki * lax.rsqrt(jnp.mean(ki * ki, axis=-1, keepdims=True) + EPS) * kig_ref[...]
    ki_out[...] = ki.astype(BF16)
    wi = tail[:, B_IDX_DIM:B_IDX_DIM + B_IDX_HEADS] * (B_IDX_HEADS ** -0.5 * B_IDX_DIM ** -0.5)
    scale = B_HEAD_DIM ** -0.5
    for blk in range(tm // QBLK):
        r0, r1 = blk * QBLK, (blk + 1) * QBLK
        for hh in range(B_HEADS):
            qh = proj[r0:r1, hh * B_HEAD_DIM:(hh + 1) * B_HEAD_DIM].astype(BF16)
            qa = jnp.dot(qh, wuk_ref[hh], preferred_element_type=F32) * scale
            qa_out[blk, hh * QBLK:(hh + 1) * QBLK, :] = qa.astype(BF16)
        for hh in range(B_IDX_HEADS):
            qi_out[blk, hh * QBLK:(hh + 1) * QBLK, :] = proj[r0:r1, o2 + hh * B_IDX_DIM:
                                                            o2 + (hh + 1) * B_IDX_DIM].astype(BF16)
            wi_out[blk, hh * QBLK:(hh + 1) * QBLK, :] = jnp.broadcast_to(wi[r0:r1, hh:hh + 1], (QBLK, 128))


def _bproj(h1, y0, y1, gates, mod_prev, mod_l, n1g, b_w_in, kv_g, w_uk, kidx_g, seq):
    n, d = h1.shape
    tm = ROW_TILE
    per_b = seq // tm
    nq = n // QBLK
    qpb = tm // QBLK
    o1 = B_HEADS * B_HEAD_DIM
    o2 = o1 + B_KV_LATENT
    o3 = o2 + B_IDX_HEADS * B_IDX_DIM
    wcat = jnp.zeros((d, o3 + 128), F32).at[:, :b_w_in.shape[1]].set(b_w_in).astype(BF16)
    wuk_t = jnp.transpose(w_uk, (1, 2, 0)).astype(BF16)
    row = lambda i: (i, 0)
    modm = lambda i: (i // per_b, 0, 0)
    blk3 = lambda i: (i, 0, 0)
    return pl.pallas_call(
        _bproj_kernel,
        out_shape=(
            jax.ShapeDtypeStruct((n, d), F32),
            jax.ShapeDtypeStruct((nq, B_HEADS * QBLK, B_KV_LATENT), BF16),
            jax.ShapeDtypeStruct((n, B_KV_LATENT), BF16),
            jax.ShapeDtypeStruct((nq, B_IDX_HEADS * QBLK, B_IDX_DIM), BF16),
            jax.ShapeDtypeStruct((n, B_IDX_DIM), BF16),
            jax.ShapeDtypeStruct((nq, B_IDX_HEADS * QBLK, 128), F32),
        ),
        grid=(n // tm,),
        in_specs=[
            pl.BlockSpec((tm, d), row), pl.BlockSpec((tm, d), row), pl.BlockSpec((tm, d), row),
            pl.BlockSpec((tm, TOP_K * 128), row),
            pl.BlockSpec((1, 6, d), modm), pl.BlockSpec((1, 6, d), modm),
            _const_spec((1, d)),
            _const_spec((d, o3 + 128)),
            _const_spec((1, B_KV_LATENT)),
            _const_spec((1, B_IDX_DIM)),
            _const_spec((B_HEADS, B_HEAD_DIM, B_KV_LATENT)),
        ],
        out_specs=(
            pl.BlockSpec((tm, d), row),
            pl.BlockSpec((qpb, B_HEADS * QBLK, B_KV_LATENT), blk3),
            pl.BlockSpec((tm, B_KV_LATENT), row),
            pl.BlockSpec((qpb, B_IDX_HEADS * QBLK, B_IDX_DIM), blk3),
            pl.BlockSpec((tm, B_IDX_DIM), row),
            pl.BlockSpec((qpb, B_IDX_HEADS * QBLK, 128), blk3),
        ),
        compiler_params=_cparams(("arbitrary",)),
        name="dsa_proj",
    )(h1, y0, y1, gates, mod_prev, mod_l, n1g.reshape(1, d), wcat, kv_g.reshape(1, -1),
      kidx_g.reshape(1, -1), wuk_t)


def _t5_bucket(dist):
    n = jnp.maximum(dist, 0)
    exact = REL_BUCKETS // 2
    nf = jnp.maximum(n, 1).astype(F32)
    large = exact + (jnp.log(nf / exact) / math.log(REL_MAX_DIST / exact)
                     * (REL_BUCKETS - exact)).astype(I32)
    large = jnp.minimum(large, REL_BUCKETS - 1)
    return jnp.where(n < exact, n, large)


def _bias_tables(rel_bias):
    assert REL_MAX_DIST <= 128
    t = jnp.arange(128, dtype=I32)[:, None]
    s = jnp.arange(128, dtype=I32)[None, :]
    far = rel_bias[REL_BUCKETS - 1]
    diag = rel_bias[_t5_bucket(t - s)] - far
    prev = rel_bias[_t5_bucket(t - s + 128)] - far
    return jnp.stack([jnp.transpose(diag, (2, 0, 1)), jnp.transpose(prev, (2, 0, 1))])


def _attn_kernel(qa_ref, qi_ref, wi_ref, ckv_ref, ki_ref, bt_ref, o_ref, kbuf, m_scr, l_scr, a_scr, acc_scr,
                 s_scr, p_scr, madd_scr, kbuft, tau_scr):
    i = pl.program_id(1)
    nt = i + 1
    t_row = i * QBLK + lax.broadcasted_iota(I32, (QBLK, KTILE), 0)
    lane = lax.broadcasted_iota(I32, (QBLK, KTILE), 1)

    qi = qi_ref[0]

    def score_body(j, carry):
        k0 = pl.multiple_of(j * KTILE, KTILE)
        kt = ki_ref[0, pl.ds(k0, KTILE), :]
        p = lax.dot_general(qi, kt, (((1,), (1,)), ((), ())), preferred_element_type=F32)
        p = jnp.maximum(p, 0.0) * _lane_tile(wi_ref[0], KTILE // 128)
        sc = p[0:QBLK]
        for hh in range(1, B_IDX_HEADS):
            sc = sc + p[hh * QBLK:(hh + 1) * QBLK]
        bits = pltpu.bitcast(sc + 0.0, I32)
        key = jnp.where(bits < 0, bits ^ jnp.int32(0x7FFFFFFF), bits)
        key = jnp.where(k0 + lane <= t_row, key, jnp.int32(INT_MIN))
        kbuf[:, pl.ds(k0, KTILE)] = key
        kbuft[pl.ds(k0, KTILE), :] = key.T
        return carry

    lax.fori_loop(0, nt, score_body, 0)

    def fold_lanes(x):
        out = x[:, 0:128]
        for k in range(1, KTILE // 128):
            out = out + x[:, k * 128:(k + 1) * 128]
        return out

    def count_ge(cand):
        def body(j, acc):
            k0 = pl.multiple_of(j * KTILE, KTILE)
            return acc + fold_lanes(jnp.where(kbuf[:, pl.ds(k0, KTILE)] >= cand, 1, 0))
        acc = lax.fori_loop(0, nt, body, jnp.zeros((QBLK, 128), I32))
        return jnp.sum(acc, axis=-1, keepdims=True)

    def count_ge_t(cand_row):
        def body(j, acc):
            k0 = pl.multiple_of(j * KTILE, KTILE)
            hit = jnp.where(kbuft[pl.ds(k0, KTILE), :] >= cand_row, 1, 0)
            return acc + jnp.sum(hit.reshape(KTILE // 8, 8, QBLK), axis=0)
        acc = lax.fori_loop(0, nt, body, jnp.zeros((8, QBLK), I32))
        return jnp.sum(acc, axis=0, keepdims=True)

    def bit_step(b, lo, n_ge):
        cand = lo + lax.shift_left(jnp.int32(1), 31 - b)
        cnt = count_ge_t(cand)
        take = cnt >= B_TOPK_MAX
        return jnp.where(take, cand, lo), jnp.where(take, cnt, n_ge)

    def bit_cond(c):
        return (c[0] < 32) & (c[3] == 0)

    def bit_body(c):
        b, lo, n_ge, _ = c
        lo, n_ge = bit_step(b, lo, n_ge)
        lo, n_ge = bit_step(b + 1, lo, n_ge)
        done = (jnp.max(n_ge) == B_TOPK_MAX).astype(I32)
        return b + 2, lo, n_ge, done

    _, tau_row, n_ge_row, _ = lax.while_loop(
        bit_cond, bit_body,
        (jnp.int32(0), jnp.full((1, QBLK), INT_MIN, I32), jnp.full((1, QBLK), nt * KTILE, I32),
         (i < 1).astype(I32)))
    tau_row = jnp.maximum(tau_row, jnp.int32(INT_MIN + 1))
    excess = (i >= 1) & (jnp.max(n_ge_row) > B_TOPK_MAX)

    def to_rows(row):
        return jnp.broadcast_to(row, (QBLK, QBLK)).T

    tau_scr[...] = to_rows(tau_row)

    @pl.when(excess)
    def _():
        tau = tau_scr[:, 0:1]
        n_ge = to_rows(n_ge_row)[:, 0:1]
        n_gt = count_ge(tau + 1)
        need = B_TOPK_MAX - n_gt

        def count_eq_before(pos):
            def body(j, acc):
                k0 = pl.multiple_of(j * KTILE, KTILE)
                hit = (kbuf[:, pl.ds(k0, KTILE)] == tau) & (k0 + lane < pos)
                return acc + fold_lanes(jnp.where(hit, 1, 0))
            acc = lax.fori_loop(0, nt, body, jnp.zeros((QBLK, 128), I32))
            return jnp.sum(acc, axis=-1, keepdims=True)

        def pos_body(b, pos):
            cand = pos + lax.shift_left(jnp.int32(1), 12 - b)
            return jnp.where(count_eq_before(cand) < need, cand, pos)

        pos = lax.fori_loop(0, 13, pos_body, jnp.zeros((QBLK, 1), I32))

        def drop_body(j, carry):
            k0 = pl.multiple_of(j * KTILE, KTILE)
            kk = kbuf[:, pl.ds(k0, KTILE)]
            drop = (kk == tau) & (k0 + lane > pos) & (n_ge > B_TOPK_MAX)
            kbuf[:, pl.ds(k0, KTILE)] = jnp.where(drop, jnp.int32(INT_MIN), kk)
            return carry

        lax.fori_loop(0, nt, drop_body, 0)

    m_scr[...] = jnp.full_like(m_scr, NEG_BIG)
    l_scr[...] = jnp.zeros_like(l_scr)
    acc_scr[...] = jnp.zeros_like(acc_scr)

    def attend(k0, width, tile_off):
        kv = ckv_ref[0, pl.ds(k0, width), :]
        s_scr[:, :width] = lax.dot_general(qa_ref[0], kv, (((1,), (1,)), ((), ())),
                                           preferred_element_type=F32)
        tau_t = _lane_tile(tau_scr[...], width // QBLK)
        madd_scr[:, :width] = jnp.where(kbuf[:, pl.ds(k0, width)] >= tau_t, 0.0, NEG_BIG)
        for r0 in range(0, B_HEADS * QBLK, SM_ROWS):
            r1 = r0 + SM_ROWS
            hh, q0 = r0 // QBLK, r0 % QBLK
            parts = []
            for kc in range(width // 128):
                sp = s_scr[r0:r1, kc * 128:(kc + 1) * 128] + madd_scr[q0:q0 + SM_ROWS, kc * 128:(kc + 1) * 128]
                if tile_off is not None:
                    rel = (QBLK // 128) * tile_off + q0 // 128 - kc
                    if rel in (0, 1):
                        sp = sp + bt_ref[rel, hh, q0 % 128:q0 % 128 + SM_ROWS, :]
                parts.append(sp)
            smax = parts[0]
            for sp in parts[1:]:
                smax = jnp.maximum(smax, sp)
            m_old = m_scr[r0:r1, :]
            m_new = jnp.maximum(m_old, jnp.max(smax, axis=-1, keepdims=True))
            alpha = jnp.exp(m_old - m_new)
            psum = None
            for kc, sp in enumerate(parts):
                p = jnp.exp(sp - m_new)
                p_scr[r0:r1, kc * 128:(kc + 1) * 128] = p.astype(BF16)
                psum = p if psum is None else psum + p
            l_scr[r0:r1, :] = alpha * l_scr[r0:r1, :] + jnp.sum(psum, axis=-1, keepdims=True)
            m_scr[r0:r1, :] = m_new
            a_scr[r0:r1, :] = alpha
        pv = jnp.dot(p_scr[:, :width], kv, preferred_element_type=F32)
        acc_scr[...] = _lane_tile(a_scr[...], B_KV_LATENT // 128) * acc_scr[...] + pv

    n_far = jnp.maximum(i - 1, 0)
    per_far = FAR_TILE // KTILE

    def far_body(jf, carry):
        attend(pl.multiple_of(jf * FAR_TILE, FAR_TILE), FAR_TILE, None)
        return carry

    lax.fori_loop(0, n_far // per_far, far_body, 0)

    if per_far > 1:
        @pl.when(n_far % per_far == 1)
        def _():
            attend(pl.multiple_of((n_far - 1) * KTILE, KTILE), KTILE, None)

    @pl.when(i >= 1)
    def _():
        attend(pl.multiple_of((i - 1) * KTILE, KTILE), KTILE, 1)

    attend(pl.multiple_of(i * KTILE, KTILE), KTILE, 0)
    inv_l = 1.0 / l_scr[...]
    o_ref[0] = (acc_scr[...] * _lane_tile(inv_l, B_KV_LATENT // 128)).astype(BF16)


def _attention(qa, qi, wi, ckv, ki, btab, bsz, seq):
    nqb = seq // QBLK
    gq = lambda b, i: (b * nqb + i, 0, 0)
    gb = lambda b, i: (b, 0, 0)
    return pl.pallas_call(
        _attn_kernel,
        out_shape=jax.ShapeDtypeStruct(qa.shape, BF16),
        grid=(bsz, nqb),
        in_specs=[
            pl.BlockSpec((1, B_HEADS * QBLK, B_KV_LATENT), gq),
            pl.BlockSpec((1, B_IDX_HEADS * QBLK, B_IDX_DIM), gq),
            pl.BlockSpec((1, B_IDX_HEADS * QBLK, 128), gq),
            pl.BlockSpec((1, seq, B_KV_LATENT), gb, pipeline_mode=pl.Buffered(1)),
            pl.BlockSpec((1, seq, B_IDX_DIM), gb, pipeline_mode=pl.Buffered(1)),
            pl.BlockSpec((2, B_HEADS, 128, 128), lambda b, i: (0, 0, 0, 0), pipeline_mode=pl.Buffered(1)),
        ],
        out_specs=pl.BlockSpec((1, B_HEADS * QBLK, B_KV_LATENT), gq),
        scratch_shapes=[
            pltpu.VMEM((QBLK, seq), I32),
            pltpu.VMEM((B_HEADS * QBLK, 128), F32),
            pltpu.VMEM((B_HEADS * QBLK, 128), F32),
            pltpu.VMEM((B_HEADS * QBLK, 128), F32),
            pltpu.VMEM((B_HEADS * QBLK, B_KV_LATENT), F32),
            pltpu.VMEM((B_HEADS * QBLK, FAR_TILE), F32),
            pltpu.VMEM((B_HEADS * QBLK, FAR_TILE), BF16),
            pltpu.VMEM((QBLK, FAR_TILE), F32),
            pltpu.VMEM((seq, QBLK), I32),
            pltpu.VMEM((QBLK, QBLK), I32),
        ],
        compiler_params=_cparams(("arbitrary", "arbitrary")),
        name="dsa_attention",
    )(qa, qi, wi, ckv.reshape(bsz, seq, -1), ki.reshape(bsz, seq, -1), btab)


def _bout_kernel(h_ref, ol_ref, mod_ref, wuv_ref, wout_ref, n2g_ref, rw_ref, h_out, hn_out, lg_out, o_scr):
    mod = mod_ref[0]
    for blk in range(ol_ref.shape[0]):
        for hh in range(B_HEADS):
            oh = jnp.dot(ol_ref[blk, hh * QBLK:(hh + 1) * QBLK, :], wuv_ref[hh], preferred_element_type=F32)
            o_scr[blk * QBLK:(blk + 1) * QBLK, hh * B_V_DIM:(hh + 1) * B_V_DIM] = oh.astype(BF16)
    y = jnp.dot(o_scr[...], wout_ref[...], preferred_element_type=F32)
    _residual_epilogue(h_ref[...], y, mod, n2g_ref[...], rw_ref[...], h_out, hn_out, lg_out)


def _bout(h, o_lat, mod_l, w_uv, w_out, n2g, router_w, seq):
    n, d = h.shape
    tm = ROW_TILE
    per_b = seq // tm
    qpb = tm // QBLK
    ne = router_w.shape[1]
    wuv_t = jnp.transpose(w_uv, (1, 0, 2)).astype(BF16)
    row = lambda i: (i, 0)
    return pl.pallas_call(
        _bout_kernel,
        out_shape=(jax.ShapeDtypeStruct((n, d), F32), jax.ShapeDtypeStruct((n, d // 2), I32),
                   jax.ShapeDtypeStruct((n, ne), F32)),
        grid=(n // tm,),
        in_specs=[
            pl.BlockSpec((tm, d), row),
            pl.BlockSpec((qpb, B_HEADS * QBLK, B_KV_LATENT), lambda i: (i, 0, 0)),
            pl.BlockSpec((1, 6, d), lambda i: (i // per_b, 0, 0)),
            _const_spec((B_HEADS, B_KV_LATENT, B_V_DIM)),
            _const_spec((B_HEADS * B_V_DIM, d)),
            _const_spec((1, d)),
            _const_spec((d, ne)),
        ],
        out_specs=(pl.BlockSpec((tm, d), row), pl.BlockSpec((tm, d // 2), row), pl.BlockSpec((tm, ne), row)),
        scratch_shapes=[pltpu.VMEM((tm, B_HEADS * B_V_DIM), BF16)],
        compiler_params=_cparams(("arbitrary",)),
        name="dsa_out",
    )(h, o_lat, mod_l, wuv_t, w_out.astype(BF16), n2g.reshape(1, d), router_w)


def _final_kernel(h1_ref, y0_ref, y1_ref, gt_ref, mod_ref, g_ref, o_ref):
    h = _moe_combine(h1_ref[...], y0_ref[...], y1_ref[...], gt_ref[...], mod_ref[0])
    o_ref[...] = h * lax.rsqrt(jnp.mean(h * h, axis=-1, keepdims=True) + EPS) * g_ref[...]


def _final(h1, y0, y1, gates, mod_l, final_g, seq):
    n, d = h1.shape
    tm = 512
    per_b = seq // tm
    row = lambda i: (i, 0)
    return pl.pallas_call(
        _final_kernel,
        out_shape=jax.ShapeDtypeStruct((n, d), F32),
        grid=(n // tm,),
        in_specs=[pl.BlockSpec((tm, d), row), pl.BlockSpec((tm, d), row), pl.BlockSpec((tm, d), row),
                  pl.BlockSpec((tm, TOP_K * 128), row), pl.BlockSpec((1, 6, d), lambda i: (i // per_b, 0, 0)),
                  _const_spec((1, d))],
        out_specs=pl.BlockSpec((tm, d), row),
        compiler_params=_cparams(("arbitrary",)),
        name="final_norm",
    )(h1, y0, y1, gates, mod_l, final_g.reshape(1, d))


def kernel(x, c, ada_w, ada_b, norm1_g, norm2_g, a_w_in, a_ln_g, a_ln_b, a_w_sp, a_b_sp, a_w_out, b_w_in,
           b_kv_norm_g, b_w_uk, b_w_uv, b_kidx_g, b_w_out, rel_bias, router_w, router_b, moe_w_gate,
           moe_w_up, moe_w_down, final_g):
    bsz, seq, d = x.shape
    n = bsz * seq
    assert seq % max(QBLK, GMLP_ROWS, ROW_TILE, 512) == 0 and d % 256 == 0
    assert min(B_TOPK_MAX, seq // 4) == B_TOPK_MAX and QBLK >= B_TOPK_MAX and KTILE == QBLK
    assert ada_w.shape[0] == 2 and moe_w_gate.shape[1] == N_EXPERTS and MOE_ROWS % MOE_SUB == 0
    mod = _adaln(c, ada_w, ada_b).reshape(ada_w.shape[0], bsz, 6, d)
    rw_pad = jnp.zeros((d, 128), F32).at[:, :N_EXPERTS].set(router_w)
    h = x.reshape(n, d)

    h1, hn2, logits = _gmlp_layer(h, mod[0], norm1_g[0], a_w_in[0], a_ln_g[0], a_ln_b[0], a_w_sp[0],
                                  a_b_sp[0], a_w_out[0], norm2_g[0], rw_pad, seq)
    y0, y1, gates = _moe(hn2, logits, router_b, moe_w_gate, moe_w_up, moe_w_down, 0)

    h, qa, ckv, qi, ki, wi = _bproj(h1, y0, y1, gates, mod[0], mod[1], norm1_g[1], b_w_in[0],
                                    b_kv_norm_g[0], b_w_uk[0], b_kidx_g[0], seq)
    o_lat = _attention(qa, qi, wi, ckv, ki, _bias_tables(rel_bias), bsz, seq)
    h1, hn2, logits = _bout(h, o_lat, mod[1], b_w_uv[0], b_w_out[0], norm2_g[1], rw_pad, seq)
    y0, y1, gates = _moe(hn2, logits, router_b, moe_w_gate, moe_w_up, moe_w_down, 1)

    out = _final(h1, y0, y1, gates, mod[1], final_g, seq)
    return out.reshape(bsz, seq, d)
```

```python
import functools
import math

import jax
import jax.numpy as jnp
from jax import lax
from jax.experimental import pallas as pl
from jax.experimental.pallas import tpu as pltpu

F32 = jnp.float32
BF16 = jnp.bfloat16
I32 = jnp.int32
HIGHEST = lax.Precision.HIGHEST

EPS = 1e-6
A_CHUNK = 128
A_GROUPS = 8
B_HEADS = 16
B_HEAD_DIM = 64
B_V_DIM = 64
B_KV_LATENT = 256
B_IDX_HEADS = 8
B_IDX_DIM = 64
B_TOPK_MAX = 256
QBLK = 256
KTILE = 256
FAR_TILE = 256
SM_ROWS = 128
REL_BUCKETS = 32
REL_MAX_DIST = 128
N_EXPERTS = 16
N_GROUPS = 4
EXPERTS_PER_GROUP = 4
TOP_K = 2
MOE_ROWS = 1280
MOE_SUB = 256
MOE_FT = 512

ROW_TILE = 256
GMLP_ROWS = 512
VMEM_LIMIT = 60 * 1024 * 1024

INT_MIN = -2 ** 31
NEG_BIG = -1e30


def _cparams(sem):
    return pltpu.CompilerParams(dimension_semantics=sem, vmem_limit_bytes=VMEM_LIMIT)


def _lane_tile(x, k):
    return x if k == 1 else jnp.concatenate([x] * k, axis=1)


def _mod_rmsnorm(h, g, scale, shift):
    ms = jnp.mean(h * h, axis=-1, keepdims=True)
    return (h * lax.rsqrt(ms + EPS) * g) * (1.0 + scale) + shift


def _gelu_tanh(x):
    c = math.sqrt(2.0 / math.pi)
    return 0.5 * x * (1.0 + jnp.tanh(c * (x + 0.044715 * (x * x * x))))


def _adaln_kernel(c_ref, w_ref, b_ref, o_ref):
    c = c_ref[...]
    sc = c * jax.nn.sigmoid(c)
    o_ref[0] = jnp.dot(sc, w_ref[0], precision=HIGHEST, preferred_element_type=F32) + b_ref[0]


def _adaln(c, ada_w, ada_b):
    depth, d, e = ada_w.shape
    bsz = c.shape[0]
    bp = 8
    c_pad = jnp.zeros((bp, d), F32).at[:bsz].set(c)
    tn = 1024
    out = pl.pallas_call(
        _adaln_kernel,
        out_shape=jax.ShapeDtypeStruct((depth, bp, e), F32),
        grid=(depth, e // tn),
        in_specs=[
            pl.BlockSpec((bp, d), lambda l, j: (0, 0)),
            pl.BlockSpec((1, d, tn), lambda l, j: (l, 0, j)),
            pl.BlockSpec((1, 1, tn), lambda l, j: (l, 0, j)),
        ],
        out_specs=pl.BlockSpec((1, bp, tn), lambda l, j: (l, 0, j)),
        compiler_params=_cparams(("arbitrary", "arbitrary")),
        name="adaln",
    )(c_pad, ada_w, ada_b.reshape(depth, 1, e))
    return out[:, :bsz]


def _residual_epilogue(h, y, mod, n2g, rw, h_out, hn_out, lg_out):
    h1 = h + mod[2:3] * y
    h_out[...] = h1
    hn2 = _mod_rmsnorm(h1, n2g, mod[4:5], mod[3:4])
    hn_hi = hn2.astype(BF16)
    bits = pltpu.bitcast(hn_hi.astype(F32), I32)
    half = bits.shape[1] // 2
    hn_out[...] = bits[:, :half] | lax.shift_right_logical(bits[:, half:], 16)
    hn_lo = (hn2 - hn_hi.astype(F32)).astype(BF16)
    rw_hi = rw.astype(BF16)
    rw_lo = (rw - rw_hi.astype(F32)).astype(BF16)
    lg_out[...] = (jnp.dot(hn_hi, rw_hi, preferred_element_type=F32)
                   + (jnp.dot(hn_lo, rw_hi, preferred_element_type=F32)
                      + jnp.dot(hn_hi, rw_lo, preferred_element_type=F32)))


def _gmlp_kernel(h_ref, mod_ref, n1g_ref, win_ref, lng_ref, lnb_ref, wsp_ref, bsp_ref, wout_ref,
                 n2g_ref, rw_ref, h_out, hn_out, lg_out, u_scr, v_scr, s_scr):
    tm = h_ref.shape[0]
    inner = u_scr.shape[1]
    gw = inner // A_GROUPS
    tn = 512
    h = h_ref[...]
    mod = mod_ref[0]
    hn = _mod_rmsnorm(h, n1g_ref[...], mod[1:2], mod[0:1]).astype(BF16)
    for j in range(2 * inner // tn):
        z = _gelu_tanh(jnp.dot(hn, win_ref[:, j * tn:(j + 1) * tn], preferred_element_type=F32))
        if j < inner // tn:
            u_scr[:, j * tn:(j + 1) * tn] = z
        else:
            jj = j - inner // tn
            v_scr[:, jj * tn:(jj + 1) * tn] = z
    vsum = jnp.zeros((tm, 1), F32)
    for j in range(inner // tn):
        vsum = vsum + jnp.sum(v_scr[:, j * tn:(j + 1) * tn], axis=-1, keepdims=True)
    mu = vsum * (1.0 / inner)
    vsq = jnp.zeros((tm, 1), F32)
    for j in range(inner // tn):
        d = v_scr[:, j * tn:(j + 1) * tn] - mu
        vsq = vsq + jnp.sum(d * d, axis=-1, keepdims=True)
    rstd = lax.rsqrt(vsq * (1.0 / inner) + EPS)
    row = lax.broadcasted_iota(I32, (A_CHUNK, A_CHUNK), 0)
    col = lax.broadcasted_iota(I32, (A_CHUNK, A_CHUNK), 1)
    tril = row >= col
    for g in range(A_GROUPS):
        ws = jnp.where(tril, wsp_ref[g], 0.0).astype(BF16)
        bcol = bsp_ref[:, g:g + 1]
        lg = lng_ref[:, g * gw:(g + 1) * gw]
        lb = lnb_ref[:, g * gw:(g + 1) * gw]
        for c in range(tm // A_CHUNK):
            r0, r1 = c * A_CHUNK, (c + 1) * A_CHUNK
            vt = v_scr[r0:r1, g * gw:(g + 1) * gw]
            vn = ((vt - mu[r0:r1]) * rstd[r0:r1]) * lg + lb
            fv = jnp.dot(ws, vn.astype(BF16), preferred_element_type=F32) + bcol
            s_scr[r0:r1, g * gw:(g + 1) * gw] = (u_scr[r0:r1, g * gw:(g + 1) * gw] * fv).astype(BF16)
    y = jnp.dot(s_scr[...], wout_ref[...], preferred_element_type=F32)
    _residual_epilogue(h, y, mod, n2g_ref[...], rw_ref[...], h_out, hn_out, lg_out)


def _const_spec(shape):
    nd = len(shape)
    return pl.BlockSpec(shape, lambda i, _nd=nd: (0,) * _nd, pipeline_mode=pl.Buffered(1))


def _gmlp_layer(h, mod_l, n1g, w_in, ln_g, ln_b, w_sp, b_sp, w_out, n2g, router_w, seq):
    n, d = h.shape
    inner = w_out.shape[0]
    tm = GMLP_ROWS
    per_b = seq // tm
    ne = router_w.shape[1]
    return pl.pallas_call(
        _gmlp_kernel,
        out_shape=(jax.ShapeDtypeStruct((n, d), F32), jax.ShapeDtypeStruct((n, d // 2), I32),
                   jax.ShapeDtypeStruct((n, ne), F32)),
        grid=(n // tm,),
        in_specs=[
            pl.BlockSpec((tm, d), lambda i: (i, 0)),
            pl.BlockSpec((1, 6, d), lambda i: (i // per_b, 0, 0)),
            _const_spec((1, d)),
            _const_spec((d, 2 * inner)),
            _const_spec((1, inner)),
            _const_spec((1, inner)),
            _const_spec((A_GROUPS, A_CHUNK, A_CHUNK)),
            _const_spec((A_CHUNK, A_GROUPS)),
            _const_spec((inner, d)),
            _const_spec((1, d)),
            _const_spec((d, ne)),
        ],
        out_specs=(pl.BlockSpec((tm, d), lambda i: (i, 0)), pl.BlockSpec((tm, d // 2), lambda i: (i, 0)),
                   pl.BlockSpec((tm, ne), lambda i: (i, 0))),
        scratch_shapes=[pltpu.VMEM((tm, inner), F32), pltpu.VMEM((tm, inner), F32),
                        pltpu.VMEM((tm, inner), BF16)],
        compiler_params=_cparams(("arbitrary",)),
        name="gmlp_layer",
    )(h, mod_l, n1g.reshape(1, d), w_in.astype(BF16), ln_g.reshape(1, inner), ln_b.reshape(1, inner),
      w_sp, b_sp.T, w_out.astype(BF16), n2g.reshape(1, d), router_w)


def _first_max4(rows):
    m = jnp.maximum(jnp.maximum(rows[0], rows[1]), jnp.maximum(rows[2], rows[3]))
    idx = jnp.where(rows[0] == m, 0, jnp.where(rows[1] == m, 1, jnp.where(rows[2] == m, 2, 3)))
    return m, idx.astype(I32)


def _route_kernel(lt_ref, rb_ref, e_ref, g_ref):
    aff = jax.nn.sigmoid(lt_ref[...])
    sel = aff + rb_ref[...]
    neg = jnp.float32(-jnp.inf)
    g_score, g_i1, g_i2 = [], [], []
    for g in range(N_GROUPS):
        rows = [sel[4 * g + k:4 * g + k + 1, :] for k in range(EXPERTS_PER_GROUP)]
        m1, i1 = _first_max4(rows)
        rest = [jnp.where(i1 == k, neg, rows[k]) for k in range(EXPERTS_PER_GROUP)]
        m2, i2 = _first_max4(rest)
        g_score.append(m1 + m2)
        g_i1.append(i1)
        g_i2.append(i2)
    _, grp = _first_max4(g_score)
    l1 = jnp.zeros_like(grp)
    l2 = jnp.zeros_like(grp)
    for g in range(N_GROUPS):
        l1 = jnp.where(grp == g, g_i1[g], l1)
        l2 = jnp.where(grp == g, g_i2[g], l2)
    e1 = grp * EXPERTS_PER_GROUP + l1
    e2 = grp * EXPERTS_PER_GROUP + l2
    a1 = jnp.zeros_like(g_score[0])
    a2 = jnp.zeros_like(g_score[0])
    for e in range(N_EXPERTS):
        ae = aff[e:e + 1, :]
        a1 = jnp.where(e1 == e, ae, a1)
        a2 = jnp.where(e2 == e, ae, a2)
    tot = a1 + a2
    e_ref[0:1, :] = e1
    e_ref[1:2, :] = e2
    g_ref[0:1, :] = a1 / tot
    g_ref[1:2, :] = a2 / tot


def _route(logits, router_b):
    n, ne = logits.shape[0], N_EXPERTS
    tn = min(2048, n)
    return pl.pallas_call(
        _route_kernel,
        out_shape=(jax.ShapeDtypeStruct((TOP_K, n), I32), jax.ShapeDtypeStruct((TOP_K, n), F32)),
        grid=(n // tn,),
        in_specs=[pl.BlockSpec((ne, tn), lambda i: (0, i)), pl.BlockSpec((ne, 1), lambda i: (0, 0))],
        out_specs=(pl.BlockSpec((TOP_K, tn), lambda i: (0, i)), pl.BlockSpec((TOP_K, tn), lambda i: (0, i))),
        compiler_params=_cparams(("arbitrary",)),
        name="route",
    )(logits[:, :ne].T, router_b.reshape(ne, 1))


def _slot_kernel(e_ref, slot_ref, cnt_ref):
    n = e_ref.shape[1]
    nblk = n // 128
    eid = lax.broadcasted_iota(I32, (N_EXPERTS, 128), 0)
    r = lax.broadcasted_iota(I32, (128, 128), 0)
    c = lax.broadcasted_iota(I32, (128, 128), 1)
    upper = jnp.where(r <= c, 1.0, 0.0).astype(BF16)

    def hits(b):
        o = pl.multiple_of(b * 128, 128)
        t1 = e_ref[0:1, pl.ds(o, 128)] == eid
        t2 = e_ref[1:2, pl.ds(o, 128)] == eid
        return o, t1, t2, jnp.where(t1, 1.0, 0.0) + jnp.where(t2, 1.0, 0.0)

    def count_body(b, acc):
        return acc + hits(b)[3]

    cnt = jnp.sum(lax.fori_loop(0, nblk, count_body, jnp.zeros((N_EXPERTS, 128), F32)),
                  axis=-1, keepdims=True)
    cnt_ref[...] = cnt.astype(I32)
    padded = jnp.zeros_like(cnt)
    for k in range(pl.cdiv(n * TOP_K, MOE_ROWS)):
        padded = padded + jnp.where(cnt > k * MOE_ROWS, float(MOE_ROWS), 0.0)
    er = lax.broadcasted_iota(I32, (N_EXPERTS, N_EXPERTS), 0)
    ec = lax.broadcasted_iota(I32, (N_EXPERTS, N_EXPERTS), 1)
    padded_row = jnp.sum(jnp.where(er == ec, padded, 0.0), axis=0, keepdims=True)
    start = jnp.sum(jnp.where(ec < er, padded_row, 0.0), axis=-1, keepdims=True)

    def slot_body(b, carry):
        o, t1, t2, t = hits(b)
        incl = jnp.dot(t.astype(BF16), upper, preferred_element_type=F32)
        pos = carry + incl - t
        slot_ref[0:1, pl.ds(o, 128)] = jnp.sum(jnp.where(t1, pos, 0.0), axis=0, keepdims=True).astype(I32)
        slot_ref[1:2, pl.ds(o, 128)] = jnp.sum(jnp.where(t2, pos, 0.0), axis=0, keepdims=True).astype(I32)
        return carry + incl[:, 127:128]

    lax.fori_loop(0, nblk, slot_body, start)


def _slot_tables(experts):
    n = experts.shape[1]
    a = n * TOP_K
    slot, counts = pl.pallas_call(
        _slot_kernel,
        out_shape=(jax.ShapeDtypeStruct((TOP_K, n), I32), jax.ShapeDtypeStruct((N_EXPERTS, 1), I32)),
        compiler_params=pltpu.CompilerParams(vmem_limit_bytes=VMEM_LIMIT),
        name="slots",
    )(experts)
    counts = counts.reshape(N_EXPERTS)
    padded = ((counts + MOE_ROWS - 1) // MOE_ROWS) * MOE_ROWS
    pad_end = jnp.cumsum(padded)
    pad_start = pad_end - padded
    n_items = pl.cdiv(a, MOE_ROWS) + N_EXPERTS
    p = n_items * MOE_ROWS
    tok = jnp.broadcast_to(jnp.arange(n, dtype=I32)[None, :], (TOP_K, n))
    tok_of_slot = jnp.zeros((p,), I32).at[slot.reshape(a)].set(tok.reshape(a), unique_indices=True,
                                                               mode='promise_in_bounds')
    slot_of_assign = slot.T
    item_start = jnp.arange(n_items, dtype=I32) * MOE_ROWS
    n_used = pad_end[-1] // MOE_ROWS
    item_e_raw = jnp.minimum(jnp.searchsorted(pad_end, item_start, side='right'), N_EXPERTS - 1).astype(I32)
    used = item_start < pad_end[-1]
    last_e = item_e_raw[jnp.maximum(n_used - 1, 0)]
    item_e = jnp.where(used, item_e_raw, last_e).astype(I32)
    valid = jnp.clip(counts[item_e_raw] - (item_start - pad_start[item_e_raw]), 0, MOE_ROWS)
    item_rows = jnp.where(used, valid, 0).astype(I32)
    item_blk = jnp.where(used, jnp.arange(n_items, dtype=I32), jnp.maximum(n_used - 1, 0)).astype(I32)
    return tok_of_slot, slot_of_assign, item_e, item_rows, item_blk


def _moe_kernel(ie_ref, ir_ref, ib_ref, x_ref, wg_ref, wu_ref, wd_ref, o_ref, wg_s, wu_s, wd_s, x_s):
    it = pl.program_id(0)
    f = pl.program_id(1)
    nrows = ir_ref[it]
    half = x_ref.shape[1]

    @pl.when(f == 0)
    def _():
        o_ref[...] = jnp.zeros_like(o_ref)
        pk = x_ref[...]
        x_s[:, :half] = pltpu.bitcast(pk & jnp.int32(-65536), F32).astype(BF16)
        x_s[:, half:] = pltpu.bitcast(lax.shift_left(pk, 16), F32).astype(BF16)

    @pl.when(nrows > 0)
    def _():
        wg_s[...] = wg_ref[0, 0].astype(BF16)
        wu_s[...] = wu_ref[0, 0].astype(BF16)
        wd_s[...] = wd_ref[0, 0].astype(BF16)
        nsb = (nrows + MOE_SUB - 1) // MOE_SUB

        def sub_block(sb):
            r0 = pl.multiple_of(sb * MOE_SUB, MOE_SUB)
            x = x_s[pl.ds(r0, MOE_SUB), :]
            g = jnp.dot(x, wg_s[...], preferred_element_type=F32)
            u = jnp.dot(x, wu_s[...], preferred_element_type=F32)
            hmid = ((g * jax.nn.sigmoid(g)) * u).astype(BF16)
            o_ref[pl.ds(r0, MOE_SUB), :] += jnp.dot(hmid, wd_s[...], preferred_element_type=F32)

        def quad_body(qi, carry):
            for u in range(4):
                sub_block(4 * qi + u)
            return carry

        lax.fori_loop(0, nsb // 4, quad_body, 0)
        done = (nsb // 4) * 4

        @pl.when(nsb % 4 >= 2)
        def _():
            sub_block(done)
            sub_block(done + 1)

        @pl.when(nsb % 2 == 1)
        def _():
            sub_block(nsb - 1)


def _moe_experts(x_sorted, item_e, item_rows, item_blk, w_gate, w_up, w_down, layer):
    p, half = x_sorted.shape
    d = 2 * half
    n_items = p // MOE_ROWS
    dff = w_gate.shape[3]
    nf = dff // MOE_FT
    grid_spec = pltpu.PrefetchScalarGridSpec(
        num_scalar_prefetch=3,
        grid=(n_items, nf),
        in_specs=[
            pl.BlockSpec((MOE_ROWS, half), lambda i, f, ie, ir, ib: (ib[i], 0)),
            pl.BlockSpec((1, 1, d, MOE_FT),
                         lambda i, f, ie, ir, ib: (layer, ie[i], 0, jnp.where(ir[i] > 0, f, nf - 1))),
            pl.BlockSpec((1, 1, d, MOE_FT),
                         lambda i, f, ie, ir, ib: (layer, ie[i], 0, jnp.where(ir[i] > 0, f, nf - 1))),
            pl.BlockSpec((1, 1, MOE_FT, d),
                         lambda i, f, ie, ir, ib: (layer, ie[i], jnp.where(ir[i] > 0, f, nf - 1), 0)),
        ],
        out_specs=pl.BlockSpec((MOE_ROWS, d), lambda i, f, ie, ir, ib: (i, 0)),
        scratch_shapes=[pltpu.VMEM((d, MOE_FT), BF16), pltpu.VMEM((d, MOE_FT), BF16),
                        pltpu.VMEM((MOE_FT, d), BF16), pltpu.VMEM((MOE_ROWS, d), BF16)],
    )
    return pl.pallas_call(
        _moe_kernel,
        out_shape=jax.ShapeDtypeStruct((p, d), F32),
        grid_spec=grid_spec,
        compiler_params=_cparams(("arbitrary", "arbitrary")),
        name="moe_experts",
    )(item_e, item_rows, item_blk, x_sorted, w_gate, w_up, w_down)


def _moe(hn2, logits, router_b, w_gate, w_up, w_down, layer):
    experts, gates = _route(logits, router_b)
    tok_of_slot, slot_of_assign, item_e, item_rows, item_blk = _slot_tables(experts)
    x_sorted = hn2.at[tok_of_slot].get(mode='promise_in_bounds')
    y_slot = _moe_experts(x_sorted, item_e, item_rows, item_blk, w_gate, w_up, w_down, layer)
    y0 = y_slot.at[slot_of_assign[:, 0]].get(mode='promise_in_bounds')
    y1 = y_slot.at[slot_of_assign[:, 1]].get(mode='promise_in_bounds')
    n = hn2.shape[0]
    gates_rep = jnp.concatenate([jnp.broadcast_to(gates[k][:, None], (n, 128)) for k in range(TOP_K)], axis=1)
    return y0, y1, gates_rep


def _moe_combine(h1, y0, y1, gates, mod):
    reps = h1.shape[1] // 128
    g0 = _lane_tile(gates[:, :128], reps)
    g1 = _lane_tile(gates[:, 128:], reps)
    return h1 + mod[5:6] * (g0 * y0 + g1 * y1)


def _bproj_kernel(h1_ref, y0_ref, y1_ref, gt_ref, modp_ref, mod_ref, n1g_ref, w_ref, kvg_ref, kig_ref,
                  wuk_ref, h_out, qa_out, ckv_out, qi_out, ki_out, wi_out):
    tm = h1_ref.shape[0]
    h = _moe_combine(h1_ref[...], y0_ref[...], y1_ref[...], gt_ref[...], modp_ref[0])
    h_out[...] = h
    mod = mod_ref[0]
    hn = _mod_rmsnorm(h, n1g_ref[...], mod[1:2], mod[0:1]).astype(BF16)
    proj = jnp.dot(hn, w_ref[...], preferred_element_type=F32)
    o1 = B_HEADS * B_HEAD_DIM
    o2 = o1 + B_KV_LATENT
    o3 = o2 + B_IDX_HEADS * B_IDX_DIM
    ckv = proj[:, o1:o2]
    ckv = ckv * lax.rsqrt(jnp.mean(ckv * ckv, axis=-1, keepdims=True) + EPS) * kvg_ref[...]
    ckv_out[...] = ckv.astype(BF16)
    tail = proj[:, o3:o3 + 128]
    ki = tail[:, :B_IDX_DIM]
    ki = ki * lax.rsqrt(jnp.mean(ki * ki, axis=-1, keepdims=True) + EPS) * kig_ref[...]
    ki_out[...] = ki.astype(BF16)
    wi = tail[:, B_IDX_DIM:B_IDX_DIM + B_IDX_HEADS] * (B_IDX_HEADS ** -0.5 * B_IDX_DIM ** -0.5)
    scale = B_HEAD_DIM ** -0.5
    for blk in range(tm // QBLK):
        r0, r1 = blk * QBLK, (blk + 1) * QBLK
        for hh in range(B_HEADS):
            qh = proj[r0:r1, hh * B_HEAD_DIM:(hh + 1) * B_HEAD_DIM].astype(BF16)
            qa = jnp.dot(qh, wuk_ref[hh], preferred_element_type=F32) * scale
            qa_out[blk, hh * QBLK:(hh + 1) * QBLK, :] = qa.astype(BF16)
        for hh in range(B_IDX_HEADS):
            qi_out[blk, hh * QBLK:(hh + 1) * QBLK, :] = proj[r0:r1, o2 + hh * B_IDX_DIM:
                                                            o2 + (hh + 1) * B_IDX_DIM].astype(BF16)
            wi_out[blk, hh * QBLK:(hh + 1) * QBLK, :] = jnp.broadcast_to(wi[r0:r1, hh:hh + 1], (QBLK, 128))


def _bproj(h1, y0, y1, gates, mod_prev, mod_l, n1g, b_w_in, kv_g, w_uk, kidx_g, seq):
    n, d = h1.shape
    tm = ROW_TILE
    per_b = seq // tm
    nq = n // QBLK
    qpb = tm // QBLK
    o1 = B_HEADS * B_HEAD_DIM
    o2 = o1 + B_KV_LATENT
    o3 = o2 + B_IDX_HEADS * B_IDX_DIM
    wcat = jnp.zeros((d, o3 + 128), F32).at[:, :b_w_in.shape[1]].set(b_w_in).astype(BF16)
    wuk_t = jnp.transpose(w_uk, (1, 2, 0)).astype(BF16)
    row = lambda i: (i, 0)
    modm = lambda i: (i // per_b, 0, 0)
    blk3 = lambda i: (i, 0, 0)
    return pl.pallas_call(
        _bproj_kernel,
        out_shape=(
            jax.ShapeDtypeStruct((n, d), F32),
            jax.ShapeDtypeStruct((nq, B_HEADS * QBLK, B_KV_LATENT), BF16),
            jax.ShapeDtypeStruct((n, B_KV_LATENT), BF16),
            jax.ShapeDtypeStruct((nq, B_IDX_HEADS * QBLK, B_IDX_DIM), BF16),
            jax.ShapeDtypeStruct((n, B_IDX_DIM), BF16),
            jax.ShapeDtypeStruct((nq, B_IDX_HEADS * QBLK, 128), F32),
        ),
        grid=(n // tm,),
        in_specs=[
            pl.BlockSpec((tm, d), row), pl.BlockSpec((tm, d), row), pl.BlockSpec((tm, d), row),
            pl.BlockSpec((tm, TOP_K * 128), row),
            pl.BlockSpec((1, 6, d), modm), pl.BlockSpec((1, 6, d), modm),
            _const_spec((1, d)),
            _const_spec((d, o3 + 128)),
            _const_spec((1, B_KV_LATENT)),
            _const_spec((1, B_IDX_DIM)),
            _const_spec((B_HEADS, B_HEAD_DIM, B_KV_LATENT)),
        ],
        out_specs=(
            pl.BlockSpec((tm, d), row),
            pl.BlockSpec((qpb, B_HEADS * QBLK, B_KV_LATENT), blk3),
            pl.BlockSpec((tm, B_KV_LATENT), row),
            pl.BlockSpec((qpb, B_IDX_HEADS * QBLK, B_IDX_DIM), blk3),
            pl.BlockSpec((tm, B_IDX_DIM), row),
            pl.BlockSpec((qpb, B_IDX_HEADS * QBLK, 128), blk3),
        ),
        compiler_params=_cparams(("arbitrary",)),
        name="dsa_proj",
    )(h1, y0, y1, gates, mod_prev, mod_l, n1g.reshape(1, d), wcat, kv_g.reshape(1, -1),
      kidx_g.reshape(1, -1), wuk_t)


def _t5_bucket(dist):
    n = jnp.maximum(dist, 0)
    exact = REL_BUCKETS // 2
    nf = jnp.maximum(n, 1).astype(F32)
    large = exact + (jnp.log(nf / exact) / math.log(REL_MAX_DIST / exact)
                     * (REL_BUCKETS - exact)).astype(I32)
    large = jnp.minimum(large, REL_BUCKETS - 1)
    return jnp.where(n < exact, n, large)


def _bias_tables(rel_bias):
    assert REL_MAX_DIST <= 128
    t = jnp.arange(128, dtype=I32)[:, None]
    s = jnp.arange(128, dtype=I32)[None, :]
    far = rel_bias[REL_BUCKETS - 1]
    diag = rel_bias[_t5_bucket(t - s)] - far
    prev = rel_bias[_t5_bucket(t - s + 128)] - far
    return jnp.stack([jnp.transpose(diag, (2, 0, 1)), jnp.transpose(prev, (2, 0, 1))])


def _attn_kernel(qa_ref, qi_ref, wi_ref, ckv_ref, ki_ref, bt_ref, o_ref, kbuf, m_scr, l_scr, a_scr, acc_scr,
                 s_scr, p_scr, madd_scr, kbuft, tau_scr):
    i = pl.program_id(1)
    nt = i + 1
    t_row = i * QBLK + lax.broadcasted_iota(I32, (QBLK, KTILE), 0)
    lane = lax.broadcasted_iota(I32, (QBLK, KTILE), 1)

    qi = qi_ref[0]

    def score_body(j, carry):
        k0 = pl.multiple_of(j * KTILE, KTILE)
        kt = ki_ref[0, pl.ds(k0, KTILE), :]
        p = lax.dot_general(qi, kt, (((1,), (1,)), ((), ())), preferred_element_type=F32)
        p = jnp.maximum(p, 0.0) * _lane_tile(wi_ref[0], KTILE // 128)
        sc = p[0:QBLK]
        for hh in range(1, B_IDX_HEADS):
            sc = sc + p[hh * QBLK:(hh + 1) * QBLK]
        bits = pltpu.bitcast(sc + 0.0, I32)
        key = jnp.where(bits < 0, bits ^ jnp.int32(0x7FFFFFFF), bits)
        key = jnp.where(k0 + lane <= t_row, key, jnp.int32(INT_MIN))
        kbuf[:, pl.ds(k0, KTILE)] = key
        kbuft[pl.ds(k0, KTILE), :] = key.T
        return carry

    lax.fori_loop(0, nt, score_body, 0)

    def fold_lanes(x):
        out = x[:, 0:128]
        for k in range(1, KTILE // 128):
            out = out + x[:, k * 128:(k + 1) * 128]
        return out

    def count_ge(cand):
        def body(j, acc):
            k0 = pl.multiple_of(j * KTILE, KTILE)
            return acc + fold_lanes(jnp.where(kbuf[:, pl.ds(k0, KTILE)] >= cand, 1, 0))
        acc = lax.fori_loop(0, nt, body, jnp.zeros((QBLK, 128), I32))
        return jnp.sum(acc, axis=-1, keepdims=True)

    def count_ge_t(cand_row):
        def body(j, acc):
            k0 = pl.multiple_of(j * KTILE, KTILE)
            hit = jnp.where(kbuft[pl.ds(k0, KTILE), :] >= cand_row, 1, 0)
            return acc + jnp.sum(hit.reshape(KTILE // 8, 8, QBLK), axis=0)
        acc = lax.fori_loop(0, nt, body, jnp.zeros((8, QBLK), I32))
        return jnp.sum(acc, axis=0, keepdims=True)

    def bit_step(b, lo, n_ge):
        cand = lo + lax.shift_left(jnp.int32(1), 31 - b)
        cnt = count_ge_t(cand)
        take = cnt >= B_TOPK_MAX
        return jnp.where(take, cand, lo), jnp.where(take, cnt, n_ge)

    def bit_cond(c):
        return (c[0] < 32) & (c[3] == 0)

    def bit_body(c):
        b, lo, n_ge, _ = c
        lo, n_ge = bit_step(b, lo, n_ge)
        lo, n_ge = bit_step(b + 1, lo, n_ge)
        done = (jnp.max(n_ge) == B_TOPK_MAX).astype(I32)
        return b + 2, lo, n_ge, done

    _, tau_row, n_ge_row, _ = lax.while_loop(
        bit_cond, bit_body,
        (jnp.int32(0), jnp.full((1, QBLK), INT_MIN, I32), jnp.full((1, QBLK), nt * KTILE, I32),
         (i < 1).astype(I32)))
    tau_row = jnp.maximum(tau_row, jnp.int32(INT_MIN + 1))
    excess = (i >= 1) & (jnp.max(n_ge_row) > B_TOPK_MAX)

    def to_rows(row):
        return jnp.broadcast_to(row, (QBLK, QBLK)).T

    tau_scr[...] = to_rows(tau_row)

    @pl.when(excess)
    def _():
        tau = tau_scr[:, 0:1]
        n_ge = to_rows(n_ge_row)[:, 0:1]
        n_gt = count_ge(tau + 1)
        need = B_TOPK_MAX - n_gt

        def count_eq_before(pos):
            def body(j, acc):
                k0 = pl.multiple_of(j * KTILE, KTILE)
                hit = (kbuf[:, pl.ds(k0, KTILE)] == tau) & (k0 + lane < pos)
                return acc + fold_lanes(jnp.where(hit, 1, 0))
            acc = lax.fori_loop(0, nt, body, jnp.zeros((QBLK, 128), I32))
            return jnp.sum(acc, axis=-1, keepdims=True)

        def pos_body(b, pos):
            cand = pos + lax.shift_left(jnp.int32(1), 12 - b)
            return jnp.where(count_eq_before(cand) < need, cand, pos)

        pos = lax.fori_loop(0, 13, pos_body, jnp.zeros((QBLK, 1), I32))

        def drop_body(j, carry):
            k0 = pl.multiple_of(j * KTILE, KTILE)
            kk = kbuf[:, pl.ds(k0, KTILE)]
            drop = (kk == tau) & (k0 + lane > pos) & (n_ge > B_TOPK_MAX)
            kbuf[:, pl.ds(k0, KTILE)] = jnp.where(drop, jnp.int32(INT_MIN), kk)
            return carry

        lax.fori_loop(0, nt, drop_body, 0)

    m_scr[...] = jnp.full_like(m_scr, NEG_BIG)
    l_scr[...] = jnp.zeros_like(l_scr)
    acc_scr[...] = jnp.zeros_like(acc_scr)

    def attend(k0, width, tile_off):
        kv = ckv_ref[0, pl.ds(k0, width), :]
        s_scr[:, :width] = lax.dot_general(qa_ref[0], kv, (((1,), (1,)), ((), ())),
                                           preferred_element_type=F32)
        tau_t = _lane_tile(tau_scr[...], width // QBLK)
        madd_scr[:, :width] = jnp.where(kbuf[:, pl.ds(k0, width)] >= tau_t, 0.0, NEG_BIG)
        for r0 in range(0, B_HEADS * QBLK, SM_ROWS):
            r1 = r0 + SM_ROWS
            hh, q0 = r0 // QBLK, r0 % QBLK
            parts = []
            for kc in range(width // 128):
                sp = s_scr[r0:r1, kc * 128:(kc + 1) * 128] + madd_scr[q0:q0 + SM_ROWS, kc * 128:(kc + 1) * 128]
                if tile_off is not None:
                    rel = (QBLK // 128) * tile_off + q0 // 128 - kc
                    if rel in (0, 1):
                        sp = sp + bt_ref[rel, hh, q0 % 128:q0 % 128 + SM_ROWS, :]
                parts.append(sp)
            smax = parts[0]
            for sp in parts[1:]:
                smax = jnp.maximum(smax, sp)
            m_old = m_scr[r0:r1, :]
            m_new = jnp.maximum(m_old, jnp.max(smax, axis=-1, keepdims=True))
            alpha = jnp.exp(m_old - m_new)
            psum = None
            for kc, sp in enumerate(parts):
                p = jnp.exp(sp - m_new)
                p_scr[r0:r1, kc * 128:(kc + 1) * 128] = p.astype(BF16)
                psum = p if psum is None else psum + p
            l_scr[r0:r1, :] = alpha * l_scr[r0:r1, :] + jnp.sum(psum, axis=-1, keepdims=True)
            m_scr[r0:r1, :] = m_new
            a_scr[r0:r1, :] = alpha
        pv = jnp.dot(p_scr[:, :width], kv, preferred_element_type=F32)
        acc_scr[...] = _lane_tile(a_scr[...], B_KV_LATENT // 128) * acc_scr[...] + pv

    n_far = jnp.maximum(i - 1, 0)
    per_far = FAR_TILE // KTILE

    def far_body(jf, carry):
        attend(pl.multiple_of(jf * FAR_TILE, FAR_TILE), FAR_TILE, None)
        return carry

    lax.fori_loop(0, n_far // per_far, far_body, 0)

    if per_far > 1:
        @pl.when(n_far % per_far == 1)
        def _():
            attend(pl.multiple_of((n_far - 1) * KTILE, KTILE), KTILE, None)

    @pl.when(i >= 1)
    def _():
        attend(pl.multiple_of((i - 1) * KTILE, KTILE), KTILE, 1)

    attend(pl.multiple_of(i * KTILE, KTILE), KTILE, 0)
    inv_l = 1.0 / l_scr[...]
    o_ref[0] = (acc_scr[...] * _lane_tile(inv_l, B_KV_LATENT // 128)).astype(BF16)


def _attention(qa, qi, wi, ckv, ki, btab, bsz, seq):
    nqb = seq // QBLK
    gq = lambda b, i: (b * nqb + i, 0, 0)
    gb = lambda b, i: (b, 0, 0)
    return pl.pallas_call(
        _attn_kernel,
        out_shape=jax.ShapeDtypeStruct(qa.shape, BF16),
        grid=(bsz, nqb),
        in_specs=[
            pl.BlockSpec((1, B_HEADS * QBLK, B_KV_LATENT), gq),
            pl.BlockSpec((1, B_IDX_HEADS * QBLK, B_IDX_DIM), gq),
            pl.BlockSpec((1, B_IDX_HEADS * QBLK, 128), gq),
            pl.BlockSpec((1, seq, B_KV_LATENT), gb, pipeline_mode=pl.Buffered(1)),
            pl.BlockSpec((1, seq, B_IDX_DIM), gb, pipeline_mode=pl.Buffered(1)),
            pl.BlockSpec((2, B_HEADS, 128, 128), lambda b, i: (0, 0, 0, 0), pipeline_mode=pl.Buffered(1)),
        ],
        out_specs=pl.BlockSpec((1, B_HEADS * QBLK, B_KV_LATENT), gq),
        scratch_shapes=[
            pltpu.VMEM((QBLK, seq), I32),
            pltpu.VMEM((B_HEADS * QBLK, 128), F32),
            pltpu.VMEM((B_HEADS * QBLK, 128), F32),
            pltpu.VMEM((B_HEADS * QBLK, 128), F32),
            pltpu.VMEM((B_HEADS * QBLK, B_KV_LATENT), F32),
            pltpu.VMEM((B_HEADS * QBLK, FAR_TILE), F32),
            pltpu.VMEM((B_HEADS * QBLK, FAR_TILE), BF16),
            pltpu.VMEM((QBLK, FAR_TILE), F32),
            pltpu.VMEM((seq, QBLK), I32),
            pltpu.VMEM((QBLK, QBLK), I32),
        ],
        compiler_params=_cparams(("arbitrary", "arbitrary")),
        name="dsa_attention",
    )(qa, qi, wi, ckv.reshape(bsz, seq, -1), ki.reshape(bsz, seq, -1), btab)


def _bout_kernel(h_ref, ol_ref, mod_ref, wuv_ref, wout_ref, n2g_ref, rw_ref, h_out, hn_out, lg_out, o_scr):
    mod = mod_ref[0]
    for blk in range(ol_ref.shape[0]):
        for hh in range(B_HEADS):
            oh = jnp.dot(ol_ref[blk, hh * QBLK:(hh + 1) * QBLK, :], wuv_ref[hh], preferred_element_type=F32)
            o_scr[blk * QBLK:(blk + 1) * QBLK, hh * B_V_DIM:(hh + 1) * B_V_DIM] = oh.astype(BF16)
    y = jnp.dot(o_scr[...], wout_ref[...], preferred_element_type=F32)
    _residual_epilogue(h_ref[...], y, mod, n2g_ref[...], rw_ref[...], h_out, hn_out, lg_out)


def _bout(h, o_lat, mod_l, w_uv, w_out, n2g, router_w, seq):
    n, d = h.shape
    tm = ROW_TILE
    per_b = seq // tm
    qpb = tm // QBLK
    ne = router_w.shape[1]
    wuv_t = jnp.transpose(w_uv, (1, 0, 2)).astype(BF16)
    row = lambda i: (i, 0)
    return pl.pallas_call(
        _bout_kernel,
        out_shape=(jax.ShapeDtypeStruct((n, d), F32), jax.ShapeDtypeStruct((n, d // 2), I32),
                   jax.ShapeDtypeStruct((n, ne), F32)),
        grid=(n // tm,),
        in_specs=[
            pl.BlockSpec((tm, d), row),
            pl.BlockSpec((qpb, B_HEADS * QBLK, B_KV_LATENT), lambda i: (i, 0, 0)),
            pl.BlockSpec((1, 6, d), lambda i: (i // per_b, 0, 0)),
            _const_spec((B_HEADS, B_KV_LATENT, B_V_DIM)),
            _const_spec((B_HEADS * B_V_DIM, d)),
            _const_spec((1, d)),
            _const_spec((d, ne)),
        ],
        out_specs=(pl.BlockSpec((tm, d), row), pl.BlockSpec((tm, d // 2), row), pl.BlockSpec((tm, ne), row)),
        scratch_shapes=[pltpu.VMEM((tm, B_HEADS * B_V_DIM), BF16)],
        compiler_params=_cparams(("arbitrary",)),
        name="dsa_out",
    )(h, o_lat, mod_l, wuv_t, w_out.astype(BF16), n2g.reshape(1, d), router_w)


def _final_kernel(h1_ref, y0_ref, y1_ref, gt_ref, mod_ref, g_ref, o_ref):
    h = _moe_combine(h1_ref[...], y0_ref[...], y1_ref[...], gt_ref[...], mod_ref[0])
    o_ref[...] = h * lax.rsqrt(jnp.mean(h * h, axis=-1, keepdims=True) + EPS) * g_ref[...]


def _final(h1, y0, y1, gates, mod_l, final_g, seq):
    n, d = h1.shape
    tm = 512
    per_b = seq // tm
    row = lambda i: (i, 0)
    return pl.pallas_call(
        _final_kernel,
        out_shape=jax.ShapeDtypeStruct((n, d), F32),
        grid=(n // tm,),
        in_specs=[pl.BlockSpec((tm, d), row), pl.BlockSpec((tm, d), row), pl.BlockSpec((tm, d), row),
                  pl.BlockSpec((tm, TOP_K * 128), row), pl.BlockSpec((1, 6, d), lambda i: (i // per_b, 0, 0)),
                  _const_spec((1, d))],
        out_specs=pl.BlockSpec((tm, d), row),
        compiler_params=_cparams(("arbitrary",)),
        name="final_norm",
    )(h1, y0, y1, gates, mod_l, final_g.reshape(1, d))


def kernel(x, c, ada_w, ada_b, norm1_g, norm2_g, a_w_in, a_ln_g, a_ln_b, a_w_sp, a_b_sp, a_w_out, b_w_in,
           b_kv_norm_g, b_w_uk, b_w_uv, b_kidx_g, b_w_out, rel_bias, router_w, router_b, moe_w_gate,
           moe_w_up, moe_w_down, final_g):
    bsz, seq, d = x.shape
    n = bsz * seq
    assert seq % max(QBLK, GMLP_ROWS, ROW_TILE, 512) == 0 and d % 256 == 0
    assert min(B_TOPK_MAX, seq // 4) == B_TOPK_MAX and QBLK >= B_TOPK_MAX and KTILE == QBLK
    assert ada_w.shape[0] == 2 and moe_w_gate.shape[1] == N_EXPERTS and MOE_ROWS % MOE_SUB == 0
    mod = _adaln(c, ada_w, ada_b).reshape(ada_w.shape[0], bsz, 6, d)
    rw_pad = jnp.zeros((d, 128), F32).at[:, :N_EXPERTS].set(router_w)
    h = x.reshape(n, d)

    h1, hn2, logits = _gmlp_layer(h, mod[0], norm1_g[0], a_w_in[0], a_ln_g[0], a_ln_b[0], a_w_sp[0],
                                  a_b_sp[0], a_w_out[0], norm2_g[0], rw_pad, seq)
    y0, y1, gates = _moe(hn2, logits, router_b, moe_w_gate, moe_w_up, moe_w_down, 0)

    h, qa, ckv, qi, ki, wi = _bproj(h1, y0, y1, gates, mod[0], mod[1], norm1_g[1], b_w_in[0],
                                    b_kv_norm_g[0], b_w_uk[0], b_kidx_g[0], seq)
    o_lat = _attention(qa, qi, wi, ckv, ki, _bias_tables(rel_bias), bsz, seq)
    h1, hn2, logits = _bout(h, o_lat, mod[1], b_w_uv[0], b_w_out[0], norm2_g[1], rw_pad, seq)
    y0, y1, gates = _moe(hn2, logits, router_b, moe_w_gate, moe_w_up, moe_w_down, 1)

    out = _final(h1, y0, y1, gates, mod[1], final_g, seq)
    return out.reshape(bsz, seq, d)
```

```python
import functools
import math

import jax
import jax.numpy as jnp
from jax import lax
from jax.experimental import pallas as pl
from jax.experimental.pallas import tpu as pltpu

F32 = jnp.float32
BF16 = jnp.bfloat16
I32 = jnp.int32
HIGHEST = lax.Precision.HIGHEST

EPS = 1e-6
A_CHUNK = 128
A_GROUPS = 8
B_HEADS = 16
B_HEAD_DIM = 64
B_V_DIM = 64
B_KV_LATENT = 256
B_IDX_HEADS = 8
B_IDX_DIM = 64
B_TOPK_MAX = 256
QBLK = 256
KTILE = 256
FAR_TILE = 256
SM_ROWS = 128
REL_BUCKETS = 32
REL_MAX_DIST = 128
N_EXPERTS = 16
N_GROUPS = 4
EXPERTS_PER_GROUP = 4
TOP_K = 2
MOE_ROWS = 2304
MOE_SUB = 256
MOE_FT = 512

ROW_TILE = 256
GMLP_ROWS = 512
VMEM_LIMIT = 60 * 1024 * 1024

INT_MIN = -2 ** 31
NEG_BIG = -1e30


def _cparams(sem):
    return pltpu.CompilerParams(dimension_semantics=sem, vmem_limit_bytes=VMEM_LIMIT)


def _lane_tile(x, k):
    return x if k == 1 else jnp.concatenate([x] * k, axis=1)


def _mod_rmsnorm(h, g, scale, shift):
    ms = jnp.mean(h * h, axis=-1, keepdims=True)
    return (h * lax.rsqrt(ms + EPS) * g) * (1.0 + scale) + shift


def _gelu_tanh(x):
    c = math.sqrt(2.0 / math.pi)
    return 0.5 * x * (1.0 + jnp.tanh(c * (x + 0.044715 * (x * x * x))))


def _adaln_kernel(c_ref, w_ref, b_ref, o_ref):
    c = c_ref[...]
    sc = c * jax.nn.sigmoid(c)
    o_ref[0] = jnp.dot(sc, w_ref[0], precision=HIGHEST, preferred_element_type=F32) + b_ref[0]


def _adaln(c, ada_w, ada_b):
    depth, d, e = ada_w.shape
    bsz = c.shape[0]
    bp = 8
    c_pad = jnp.zeros((bp, d), F32).at[:bsz].set(c)
    tn = 1024
    out = pl.pallas_call(
        _adaln_kernel,
        out_shape=jax.ShapeDtypeStruct((depth, bp, e), F32),
        grid=(depth, e // tn),
        in_specs=[
            pl.BlockSpec((bp, d), lambda l, j: (0, 0)),
            pl.BlockSpec((1, d, tn), lambda l, j: (l, 0, j)),
            pl.BlockSpec((1, 1, tn), lambda l, j: (l, 0, j)),
        ],
        out_specs=pl.BlockSpec((1, bp, tn), lambda l, j: (l, 0, j)),
        compiler_params=_cparams(("arbitrary", "arbitrary")),
        name="adaln",
    )(c_pad, ada_w, ada_b.reshape(depth, 1, e))
    return out[:, :bsz]


def _residual_epilogue(h, y, mod, n2g, rw, h_out, hn_out, lg_out):
    h1 = h + mod[2:3] * y
    h_out[...] = h1
    hn2 = _mod_rmsnorm(h1, n2g, mod[4:5], mod[3:4])
    hn_hi = hn2.astype(BF16)
    bits = pltpu.bitcast(hn_hi.astype(F32), I32)
    half = bits.shape[1] // 2
    hn_out[...] = bits[:, :half] | lax.shift_right_logical(bits[:, half:], 16)
    hn_lo = (hn2 - hn_hi.astype(F32)).astype(BF16)
    rw_hi = rw.astype(BF16)
    rw_lo = (rw - rw_hi.astype(F32)).astype(BF16)
    lg_out[...] = (jnp.dot(hn_hi, rw_hi, preferred_element_type=F32)
                   + (jnp.dot(hn_lo, rw_hi, preferred_element_type=F32)
                      + jnp.dot(hn_hi, rw_lo, preferred_element_type=F32)))


def _gmlp_kernel(h_ref, mod_ref, n1g_ref, win_ref, lng_ref, lnb_ref, wsp_ref, bsp_ref, wout_ref,
                 n2g_ref, rw_ref, h_out, hn_out, lg_out, u_scr, v_scr, s_scr):
    tm = h_ref.shape[0]
    inner = u_scr.shape[1]
    gw = inner // A_GROUPS
    tn = 512
    h = h_ref[...]
    mod = mod_ref[0]
    hn = _mod_rmsnorm(h, n1g_ref[...], mod[1:2], mod[0:1]).astype(BF16)
    for j in range(2 * inner // tn):
        z = _gelu_tanh(jnp.dot(hn, win_ref[:, j * tn:(j + 1) * tn], preferred_element_type=F32))
        if j < inner // tn:
            u_scr[:, j * tn:(j + 1) * tn] = z
        else:
            jj = j - inner // tn
            v_scr[:, jj * tn:(jj + 1) * tn] = z
    vsum = jnp.zeros((tm, 1), F32)
    for j in range(inner // tn):
        vsum = vsum + jnp.sum(v_scr[:, j * tn:(j + 1) * tn], axis=-1, keepdims=True)
    mu = vsum * (1.0 / inner)
    vsq = jnp.zeros((tm, 1), F32)
    for j in range(inner // tn):
        d = v_scr[:, j * tn:(j + 1) * tn] - mu
        vsq = vsq + jnp.sum(d * d, axis=-1, keepdims=True)
    rstd = lax.rsqrt(vsq * (1.0 / inner) + EPS)
    row = lax.broadcasted_iota(I32, (A_CHUNK, A_CHUNK), 0)
    col = lax.broadcasted_iota(I32, (A_CHUNK, A_CHUNK), 1)
    tril = row >= col
    for g in range(A_GROUPS):
        ws = jnp.where(tril, wsp_ref[g], 0.0).astype(BF16)
        bcol = bsp_ref[:, g:g + 1]
        lg = lng_ref[:, g * gw:(g + 1) * gw]
        lb = lnb_ref[:, g * gw:(g + 1) * gw]
        for c in range(tm // A_CHUNK):
            r0, r1 = c * A_CHUNK, (c + 1) * A_CHUNK
            vt = v_scr[r0:r1, g * gw:(g + 1) * gw]
            vn = ((vt - mu[r0:r1]) * rstd[r0:r1]) * lg + lb
            fv = jnp.dot(ws, vn.astype(BF16), preferred_element_type=F32) + bcol
            s_scr[r0:r1, g * gw:(g + 1) * gw] = (u_scr[r0:r1, g * gw:(g + 1) * gw] * fv).astype(BF16)
    y = jnp.dot(s_scr[...], wout_ref[...], preferred_element_type=F32)
    _residual_epilogue(h, y, mod, n2g_ref[...], rw_ref[...], h_out, hn_out, lg_out)


def _const_spec(shape):
    nd = len(shape)
    return pl.BlockSpec(shape, lambda i, _nd=nd: (0,) * _nd, pipeline_mode=pl.Buffered(1))


def _gmlp_layer(h, mod_l, n1g, w_in, ln_g, ln_b, w_sp, b_sp, w_out, n2g, router_w, seq):
    n, d = h.shape
    inner = w_out.shape[0]
    tm = GMLP_ROWS
    per_b = seq // tm
    ne = router_w.shape[1]
    return pl.pallas_call(
        _gmlp_kernel,
        out_shape=(jax.ShapeDtypeStruct((n, d), F32), jax.ShapeDtypeStruct((n, d // 2), I32),
                   jax.ShapeDtypeStruct((n, ne), F32)),
        grid=(n // tm,),
        in_specs=[
            pl.BlockSpec((tm, d), lambda i: (i, 0)),
            pl.BlockSpec((1, 6, d), lambda i: (i // per_b, 0, 0)),
            _const_spec((1, d)),
            _const_spec((d, 2 * inner)),
            _const_spec((1, inner)),
            _const_spec((1, inner)),
            _const_spec((A_GROUPS, A_CHUNK, A_CHUNK)),
            _const_spec((A_CHUNK, A_GROUPS)),
            _const_spec((inner, d)),
            _const_spec((1, d)),
            _const_spec((d, ne)),
        ],
        out_specs=(pl.BlockSpec((tm, d), lambda i: (i, 0)), pl.BlockSpec((tm, d // 2), lambda i: (i, 0)),
                   pl.BlockSpec((tm, ne), lambda i: (i, 0))),
        scratch_shapes=[pltpu.VMEM((tm, inner), F32), pltpu.VMEM((tm, inner), F32),
                        pltpu.VMEM((tm, inner), BF16)],
        compiler_params=_cparams(("arbitrary",)),
        name="gmlp_layer",
    )(h, mod_l, n1g.reshape(1, d), w_in.astype(BF16), ln_g.reshape(1, inner), ln_b.reshape(1, inner),
      w_sp, b_sp.T, w_out.astype(BF16), n2g.reshape(1, d), router_w)


def _first_max4(rows):
    m = jnp.maximum(jnp.maximum(rows[0], rows[1]), jnp.maximum(rows[2], rows[3]))
    idx = jnp.where(rows[0] == m, 0, jnp.where(rows[1] == m, 1, jnp.where(rows[2] == m, 2, 3)))
    return m, idx.astype(I32)


def _route_kernel(lt_ref, rb_ref, e_ref, g_ref):
    aff = jax.nn.sigmoid(lt_ref[...])
    sel = aff + rb_ref[...]
    neg = jnp.float32(-jnp.inf)
    g_score, g_i1, g_i2 = [], [], []
    for g in range(N_GROUPS):
        rows = [sel[4 * g + k:4 * g + k + 1, :] for k in range(EXPERTS_PER_GROUP)]
        m1, i1 = _first_max4(rows)
        rest = [jnp.where(i1 == k, neg, rows[k]) for k in range(EXPERTS_PER_GROUP)]
        m2, i2 = _first_max4(rest)
        g_score.append(m1 + m2)
        g_i1.append(i1)
        g_i2.append(i2)
    _, grp = _first_max4(g_score)
    l1 = jnp.zeros_like(grp)
    l2 = jnp.zeros_like(grp)
    for g in range(N_GROUPS):
        l1 = jnp.where(grp == g, g_i1[g], l1)
        l2 = jnp.where(grp == g, g_i2[g], l2)
    e1 = grp * EXPERTS_PER_GROUP + l1
    e2 = grp * EXPERTS_PER_GROUP + l2
    a1 = jnp.zeros_like(g_score[0])
    a2 = jnp.zeros_like(g_score[0])
    for e in range(N_EXPERTS):
        ae = aff[e:e + 1, :]
        a1 = jnp.where(e1 == e, ae, a1)
        a2 = jnp.where(e2 == e, ae, a2)
    tot = a1 + a2
    e_ref[0:1, :] = e1
    e_ref[1:2, :] = e2
    g_ref[0:1, :] = a1 / tot
    g_ref[1:2, :] = a2 / tot


def _route(logits, router_b):
    n, ne = logits.shape[0], N_EXPERTS
    tn = min(2048, n)
    return pl.pallas_call(
        _route_kernel,
        out_shape=(jax.ShapeDtypeStruct((TOP_K, n), I32), jax.ShapeDtypeStruct((TOP_K, n), F32)),
        grid=(n // tn,),
        in_specs=[pl.BlockSpec((ne, tn), lambda i: (0, i)), pl.BlockSpec((ne, 1), lambda i: (0, 0))],
        out_specs=(pl.BlockSpec((TOP_K, tn), lambda i: (0, i)), pl.BlockSpec((TOP_K, tn), lambda i: (0, i))),
        compiler_params=_cparams(("arbitrary",)),
        name="route",
    )(logits[:, :ne].T, router_b.reshape(ne, 1))


def _slot_kernel(e_ref, slot_ref, cnt_ref):
    n = e_ref.shape[1]
    nblk = n // 128
    eid = lax.broadcasted_iota(I32, (N_EXPERTS, 128), 0)
    r = lax.broadcasted_iota(I32, (128, 128), 0)
    c = lax.broadcasted_iota(I32, (128, 128), 1)
    upper = jnp.where(r <= c, 1.0, 0.0).astype(BF16)

    def hits(b):
        o = pl.multiple_of(b * 128, 128)
        t1 = e_ref[0:1, pl.ds(o, 128)] == eid
        t2 = e_ref[1:2, pl.ds(o, 128)] == eid
        return o, t1, t2, jnp.where(t1, 1.0, 0.0) + jnp.where(t2, 1.0, 0.0)

    def count_body(b, acc):
        return acc + hits(b)[3]

    cnt = jnp.sum(lax.fori_loop(0, nblk, count_body, jnp.zeros((N_EXPERTS, 128), F32)),
                  axis=-1, keepdims=True)
    cnt_ref[...] = cnt.astype(I32)
    padded = jnp.zeros_like(cnt)
    for k in range(pl.cdiv(n * TOP_K, MOE_ROWS)):
        padded = padded + jnp.where(cnt > k * MOE_ROWS, float(MOE_ROWS), 0.0)
    er = lax.broadcasted_iota(I32, (N_EXPERTS, N_EXPERTS), 0)
    ec = lax.broadcasted_iota(I32, (N_EXPERTS, N_EXPERTS), 1)
    padded_row = jnp.sum(jnp.where(er == ec, padded, 0.0), axis=0, keepdims=True)
    start = jnp.sum(jnp.where(ec < er, padded_row, 0.0), axis=-1, keepdims=True)

    def slot_body(b, carry):
        o, t1, t2, t = hits(b)
        incl = jnp.dot(t.astype(BF16), upper, preferred_element_type=F32)
        pos = carry + incl - t
        slot_ref[0:1, pl.ds(o, 128)] = jnp.sum(jnp.where(t1, pos, 0.0), axis=0, keepdims=True).astype(I32)
        slot_ref[1:2, pl.ds(o, 128)] = jnp.sum(jnp.where(t2, pos, 0.0), axis=0, keepdims=True).astype(I32)
        return carry + incl[:, 127:128]

    lax.fori_loop(0, nblk, slot_body, start)


def _slot_tables(experts):
    n = experts.shape[1]
    a = n * TOP_K
    slot, counts = pl.pallas_call(
        _slot_kernel,
        out_shape=(jax.ShapeDtypeStruct((TOP_K, n), I32), jax.ShapeDtypeStruct((N_EXPERTS, 1), I32)),
        compiler_params=pltpu.CompilerParams(vmem_limit_bytes=VMEM_LIMIT),
        name="slots",
    )(experts)
    counts = counts.reshape(N_EXPERTS)
    padded = ((counts + MOE_ROWS - 1) // MOE_ROWS) * MOE_ROWS
    pad_end = jnp.cumsum(padded)
    pad_start = pad_end - padded
    n_items = pl.cdiv(a, MOE_ROWS) + N_EXPERTS
    p = n_items * MOE_ROWS
    tok = jnp.broadcast_to(jnp.arange(n, dtype=I32)[None, :], (TOP_K, n))
    tok_of_slot = jnp.zeros((p,), I32).at[slot.reshape(a)].set(tok.reshape(a))
    slot_of_assign = slot.T
    item_start = jnp.arange(n_items, dtype=I32) * MOE_ROWS
    n_used = pad_end[-1] // MOE_ROWS
    item_e_raw = jnp.minimum(jnp.searchsorted(pad_end, item_start, side='right'), N_EXPERTS - 1).astype(I32)
    used = item_start < pad_end[-1]
    last_e = item_e_raw[jnp.maximum(n_used - 1, 0)]
    item_e = jnp.where(used, item_e_raw, last_e).astype(I32)
    valid = jnp.clip(counts[item_e_raw] - (item_start - pad_start[item_e_raw]), 0, MOE_ROWS)
    item_rows = jnp.where(used, valid, 0).astype(I32)
    item_blk = jnp.where(used, jnp.arange(n_items, dtype=I32), jnp.maximum(n_used - 1, 0)).astype(I32)
    return tok_of_slot, slot_of_assign, item_e, item_rows, item_blk


def _moe_kernel(ie_ref, ir_ref, ib_ref, x_ref, wg_ref, wu_ref, wd_ref, o_ref, wg_s, wu_s, wd_s, x_s):
    it = pl.program_id(0)
    f = pl.program_id(1)
    nrows = ir_ref[it]
    half = x_ref.shape[1]

    @pl.when(f == 0)
    def _():
        o_ref[...] = jnp.zeros_like(o_ref)
        pk = x_ref[...]
        x_s[:, :half] = pltpu.bitcast(pk & jnp.int32(-65536), F32).astype(BF16)
        x_s[:, half:] = pltpu.bitcast(lax.shift_left(pk, 16), F32).astype(BF16)

    @pl.when(nrows > 0)
    def _():
        wg_s[...] = wg_ref[0, 0].astype(BF16)
        wu_s[...] = wu_ref[0, 0].astype(BF16)
        wd_s[...] = wd_ref[0, 0].astype(BF16)
        nsb = (nrows + MOE_SUB - 1) // MOE_SUB

        def sub_block(sb):
            r0 = pl.multiple_of(sb * MOE_SUB, MOE_SUB)
            x = x_s[pl.ds(r0, MOE_SUB), :]
            g = jnp.dot(x, wg_s[...], preferred_element_type=F32)
            u = jnp.dot(x, wu_s[...], preferred_element_type=F32)
            hmid = ((g * jax.nn.sigmoid(g)) * u).astype(BF16)
            o_ref[pl.ds(r0, MOE_SUB), :] += jnp.dot(hmid, wd_s[...], preferred_element_type=F32)

        def quad_body(qi, carry):
            for u in range(4):
                sub_block(4 * qi + u)
            return carry

        lax.fori_loop(0, nsb // 4, quad_body, 0)
        done = (nsb // 4) * 4

        @pl.when(nsb % 4 >= 2)
        def _():
            sub_block(done)
            sub_block(done + 1)

        @pl.when(nsb % 2 == 1)
        def _():
            sub_block(nsb - 1)


def _moe_experts(x_sorted, item_e, item_rows, item_blk, w_gate, w_up, w_down, layer):
    p, half = x_sorted.shape
    d = 2 * half
    n_items = p // MOE_ROWS
    dff = w_gate.shape[3]
    nf = dff // MOE_FT
    grid_spec = pltpu.PrefetchScalarGridSpec(
        num_scalar_prefetch=3,
        grid=(n_items, nf),
        in_specs=[
            pl.BlockSpec((MOE_ROWS, half), lambda i, f, ie, ir, ib: (ib[i], 0)),
            pl.BlockSpec((1, 1, d, MOE_FT),
                         lambda i, f, ie, ir, ib: (layer, ie[i], 0, jnp.where(ir[i] > 0, f, nf - 1))),
            pl.BlockSpec((1, 1, d, MOE_FT),
                         lambda i, f, ie, ir, ib: (layer, ie[i], 0, jnp.where(ir[i] > 0, f, nf - 1))),
            pl.BlockSpec((1, 1, MOE_FT, d),
                         lambda i, f, ie, ir, ib: (layer, ie[i], jnp.where(ir[i] > 0, f, nf - 1), 0)),
        ],
        out_specs=pl.BlockSpec((MOE_ROWS, d), lambda i, f, ie, ir, ib: (i, 0)),
        scratch_shapes=[pltpu.VMEM((d, MOE_FT), BF16), pltpu.VMEM((d, MOE_FT), BF16),
                        pltpu.VMEM((MOE_FT, d), BF16), pltpu.VMEM((MOE_ROWS, d), BF16)],
    )
    return pl.pallas_call(
        _moe_kernel,
        out_shape=jax.ShapeDtypeStruct((p, d), F32),
        grid_spec=grid_spec,
        compiler_params=_cparams(("arbitrary", "arbitrary")),
        name="moe_experts",
    )(item_e, item_rows, item_blk, x_sorted, w_gate, w_up, w_down)


def _moe(hn2, logits, router_b, w_gate, w_up, w_down, layer):
    experts, gates = _route(logits, router_b)
    tok_of_slot, slot_of_assign, item_e, item_rows, item_blk = _slot_tables(experts)
    x_sorted = hn2.at[tok_of_slot].get(mode='promise_in_bounds')
    y_slot = _moe_experts(x_sorted, item_e, item_rows, item_blk, w_gate, w_up, w_down, layer)
    y0 = y_slot.at[slot_of_assign[:, 0]].get(mode='promise_in_bounds')
    y1 = y_slot.at[slot_of_assign[:, 1]].get(mode='promise_in_bounds')
    n = hn2.shape[0]
    gates_rep = jnp.concatenate([jnp.broadcast_to(gates[k][:, None], (n, 128)) for k in range(TOP_K)], axis=1)
    return y0, y1, gates_rep


def _moe_combine(h1, y0, y1, gates, mod):
    reps = h1.shape[1] // 128
    g0 = _lane_tile(gates[:, :128], reps)
    g1 = _lane_tile(gates[:, 128:], reps)
    return h1 + mod[5:6] * (g0 * y0 + g1 * y1)


def _bproj_kernel(h1_ref, y0_ref, y1_ref, gt_ref, modp_ref, mod_ref, n1g_ref, w_ref, kvg_ref, kig_ref,
                  wuk_ref, h_out, qa_out, ckv_out, qi_out, ki_out, wi_out):
    tm = h1_ref.shape[0]
    h = _moe_combine(h1_ref[...], y0_ref[...], y1_ref[...], gt_ref[...], modp_ref[0])
    h_out[...] = h
    mod = mod_ref[0]
    hn = _mod_rmsnorm(h, n1g_ref[...], mod[1:2], mod[0:1]).astype(BF16)
    proj = jnp.dot(hn, w_ref[...], preferred_element_type=F32)
    o1 = B_HEADS * B_HEAD_DIM
    o2 = o1 + B_KV_LATENT
    o3 = o2 + B_IDX_HEADS * B_IDX_DIM
    ckv = proj[:, o1:o2]
    ckv = ckv * lax.rsqrt(jnp.mean(ckv * ckv, axis=-1, keepdims=True) + EPS) * kvg_ref[...]
    ckv_out[...] = ckv.astype(BF16)
    tail = proj[:, o3:o3 + 128]
    ki = tail[:, :B_IDX_DIM]
    ki = ki * lax.rsqrt(jnp.mean(ki * ki, axis=-1, keepdims=True) + EPS) * kig_ref[...]
    ki_out[...] = ki.astype(BF16)
    wi = tail[:, B_IDX_DIM:B_IDX_DIM + B_IDX_HEADS] * (B_IDX_HEADS ** -0.5 * B_IDX_DIM ** -0.5)
    scale = B_HEAD_DIM ** -0.5
    for blk in range(tm // QBLK):
        r0, r1 = blk * QBLK, (blk + 1) * QBLK
        for hh in range(B_HEADS):
            qh = proj[r0:r1, hh * B_HEAD_DIM:(hh + 1) * B_HEAD_DIM].astype(BF16)
            qa = jnp.dot(qh, wuk_ref[hh], preferred_element_type=F32) * scale
            qa_out[blk, hh * QBLK:(hh + 1) * QBLK, :] = qa.astype(BF16)
        for hh in range(B_IDX_HEADS):
            qi_out[blk, hh * QBLK:(hh + 1) * QBLK, :] = proj[r0:r1, o2 + hh * B_IDX_DIM:
                                                            o2 + (hh + 1) * B_IDX_DIM].astype(BF16)
            wi_out[blk, hh * QBLK:(hh + 1) * QBLK, :] = jnp.broadcast_to(wi[r0:r1, hh:hh + 1], (QBLK, 128))


def _bproj(h1, y0, y1, gates, mod_prev, mod_l, n1g, b_w_in, kv_g, w_uk, kidx_g, seq):
    n, d = h1.shape
    tm = ROW_TILE
    per_b = seq // tm
    nq = n // QBLK
    qpb = tm // QBLK
    o1 = B_HEADS * B_HEAD_DIM
    o2 = o1 + B_KV_LATENT
    o3 = o2 + B_IDX_HEADS * B_IDX_DIM
    wcat = jnp.zeros((d, o3 + 128), F32).at[:, :b_w_in.shape[1]].set(b_w_in).astype(BF16)
    wuk_t = jnp.transpose(w_uk, (1, 2, 0)).astype(BF16)
    row = lambda i: (i, 0)
    modm = lambda i: (i // per_b, 0, 0)
    blk3 = lambda i: (i, 0, 0)
    return pl.pallas_call(
        _bproj_kernel,
        out_shape=(
            jax.ShapeDtypeStruct((n, d), F32),
            jax.ShapeDtypeStruct((nq, B_HEADS * QBLK, B_KV_LATENT), BF16),
            jax.ShapeDtypeStruct((n, B_KV_LATENT), BF16),
            jax.ShapeDtypeStruct((nq, B_IDX_HEADS * QBLK, B_IDX_DIM), BF16),
            jax.ShapeDtypeStruct((n, B_IDX_DIM), BF16),
            jax.ShapeDtypeStruct((nq, B_IDX_HEADS * QBLK, 128), F32),
        ),
        grid=(n // tm,),
        in_specs=[
            pl.BlockSpec((tm, d), row), pl.BlockSpec((tm, d), row), pl.BlockSpec((tm, d), row),
            pl.BlockSpec((tm, TOP_K * 128), row),
            pl.BlockSpec((1, 6, d), modm), pl.BlockSpec((1, 6, d), modm),
            _const_spec((1, d)),
            _const_spec((d, o3 + 128)),
            _const_spec((1, B_KV_LATENT)),
            _const_spec((1, B_IDX_DIM)),
            _const_spec((B_HEADS, B_HEAD_DIM, B_KV_LATENT)),
        ],
        out_specs=(
            pl.BlockSpec((tm, d), row),
            pl.BlockSpec((qpb, B_HEADS * QBLK, B_KV_LATENT), blk3),
            pl.BlockSpec((tm, B_KV_LATENT), row),
            pl.BlockSpec((qpb, B_IDX_HEADS * QBLK, B_IDX_DIM), blk3),
            pl.BlockSpec((tm, B_IDX_DIM), row),
            pl.BlockSpec((qpb, B_IDX_HEADS * QBLK, 128), blk3),
        ),
        compiler_params=_cparams(("arbitrary",)),
        name="dsa_proj",
    )(h1, y0, y1, gates, mod_prev, mod_l, n1g.reshape(1, d), wcat, kv_g.reshape(1, -1),
      kidx_g.reshape(1, -1), wuk_t)


def _t5_bucket(dist):
    n = jnp.maximum(dist, 0)
    exact = REL_BUCKETS // 2
    nf = jnp.maximum(n, 1).astype(F32)
    large = exact + (jnp.log(nf / exact) / math.log(REL_MAX_DIST / exact)
                     * (REL_BUCKETS - exact)).astype(I32)
    large = jnp.minimum(large, REL_BUCKETS - 1)
    return jnp.where(n < exact, n, large)


def _bias_tables(rel_bias):
    assert REL_MAX_DIST <= 128
    t = jnp.arange(128, dtype=I32)[:, None]
    s = jnp.arange(128, dtype=I32)[None, :]
    far = rel_bias[REL_BUCKETS - 1]
    diag = rel_bias[_t5_bucket(t - s)] - far
    prev = rel_bias[_t5_bucket(t - s + 128)] - far
    return jnp.stack([jnp.transpose(diag, (2, 0, 1)), jnp.transpose(prev, (2, 0, 1))])


def _attn_kernel(qa_ref, qi_ref, wi_ref, ckv_ref, ki_ref, bt_ref, o_ref, kbuf, m_scr, l_scr, a_scr, acc_scr,
                 s_scr, p_scr, madd_scr, kbuft, tau_scr):
    i = pl.program_id(1)
    nt = i + 1
    t_row = i * QBLK + lax.broadcasted_iota(I32, (QBLK, KTILE), 0)
    lane = lax.broadcasted_iota(I32, (QBLK, KTILE), 1)

    qi = qi_ref[0]

    def score_body(j, carry):
        k0 = pl.multiple_of(j * KTILE, KTILE)
        kt = ki_ref[0, pl.ds(k0, KTILE), :]
        p = lax.dot_general(qi, kt, (((1,), (1,)), ((), ())), preferred_element_type=F32)
        p = jnp.maximum(p, 0.0) * _lane_tile(wi_ref[0], KTILE // 128)
        sc = p[0:QBLK]
        for hh in range(1, B_IDX_HEADS):
            sc = sc + p[hh * QBLK:(hh + 1) * QBLK]
        bits = pltpu.bitcast(sc + 0.0, I32)
        key = jnp.where(bits < 0, bits ^ jnp.int32(0x7FFFFFFF), bits)
        key = jnp.where(k0 + lane <= t_row, key, jnp.int32(INT_MIN))
        kbuf[:, pl.ds(k0, KTILE)] = key
        kbuft[pl.ds(k0, KTILE), :] = key.T
        return carry

    lax.fori_loop(0, nt, score_body, 0)

    def fold_lanes(x):
        out = x[:, 0:128]
        for k in range(1, KTILE // 128):
            out = out + x[:, k * 128:(k + 1) * 128]
        return out

    def count_ge(cand):
        def body(j, acc):
            k0 = pl.multiple_of(j * KTILE, KTILE)
            return acc + fold_lanes(jnp.where(kbuf[:, pl.ds(k0, KTILE)] >= cand, 1, 0))
        acc = lax.fori_loop(0, nt, body, jnp.zeros((QBLK, 128), I32))
        return jnp.sum(acc, axis=-1, keepdims=True)

    def count_ge_t(cand_row):
        def body(j, acc):
            k0 = pl.multiple_of(j * KTILE, KTILE)
            hit = jnp.where(kbuft[pl.ds(k0, KTILE), :] >= cand_row, 1, 0)
            return acc + jnp.sum(hit.reshape(KTILE // 8, 8, QBLK), axis=0)
        acc = lax.fori_loop(0, nt, body, jnp.zeros((8, QBLK), I32))
        return jnp.sum(acc, axis=0, keepdims=True)

    def bit_step(b, lo, n_ge):
        cand = lo + lax.shift_left(jnp.int32(1), 31 - b)
        cnt = count_ge_t(cand)
        take = cnt >= B_TOPK_MAX
        return jnp.where(take, cand, lo), jnp.where(take, cnt, n_ge)

    def bit_cond(c):
        return (c[0] < 32) & (c[3] == 0)

    def bit_body(c):
        b, lo, n_ge, _ = c
        lo, n_ge = bit_step(b, lo, n_ge)
        lo, n_ge = bit_step(b + 1, lo, n_ge)
        done = (jnp.max(n_ge) == B_TOPK_MAX).astype(I32)
        return b + 2, lo, n_ge, done

    _, tau_row, n_ge_row, _ = lax.while_loop(
        bit_cond, bit_body,
        (jnp.int32(0), jnp.full((1, QBLK), INT_MIN, I32), jnp.full((1, QBLK), nt * KTILE, I32),
         (i < 1).astype(I32)))
    tau_row = jnp.maximum(tau_row, jnp.int32(INT_MIN + 1))
    excess = (i >= 1) & (jnp.max(n_ge_row) > B_TOPK_MAX)

    def to_rows(row):
        return jnp.broadcast_to(row, (QBLK, QBLK)).T

    tau_scr[...] = to_rows(tau_row)

    @pl.when(excess)
    def _():
        tau = tau_scr[:, 0:1]
        n_ge = to_rows(n_ge_row)[:, 0:1]
        n_gt = count_ge(tau + 1)
        need = B_TOPK_MAX - n_gt

        def count_eq_before(pos):
            def body(j, acc):
                k0 = pl.multiple_of(j * KTILE, KTILE)
                hit = (kbuf[:, pl.ds(k0, KTILE)] == tau) & (k0 + lane < pos)
                return acc + fold_lanes(jnp.where(hit, 1, 0))
            acc = lax.fori_loop(0, nt, body, jnp.zeros((QBLK, 128), I32))
            return jnp.sum(acc, axis=-1, keepdims=True)

        def pos_body(b, pos):
            cand = pos + lax.shift_left(jnp.int32(1), 12 - b)
            return jnp.where(count_eq_before(cand) < need, cand, pos)

        pos = lax.fori_loop(0, 13, pos_body, jnp.zeros((QBLK, 1), I32))

        def drop_body(j, carry):
            k0 = pl.multiple_of(j * KTILE, KTILE)
            kk = kbuf[:, pl.ds(k0, KTILE)]
            drop = (kk == tau) & (k0 + lane > pos) & (n_ge > B_TOPK_MAX)
            kbuf[:, pl.ds(k0, KTILE)] = jnp.where(drop, jnp.int32(INT_MIN), kk)
            return carry

        lax.fori_loop(0, nt, drop_body, 0)

    m_scr[...] = jnp.full_like(m_scr, NEG_BIG)
    l_scr[...] = jnp.zeros_like(l_scr)
    acc_scr[...] = jnp.zeros_like(acc_scr)

    def attend(k0, width, tile_off):
        kv = ckv_ref[0, pl.ds(k0, width), :]
        s_scr[:, :width] = lax.dot_general(qa_ref[0], kv, (((1,), (1,)), ((), ())),
                                           preferred_element_type=F32)
        tau_t = _lane_tile(tau_scr[...], width // QBLK)
        madd_scr[:, :width] = jnp.where(kbuf[:, pl.ds(k0, width)] >= tau_t, 0.0, NEG_BIG)
        for r0 in range(0, B_HEADS * QBLK, SM_ROWS):
            r1 = r0 + SM_ROWS
            hh, q0 = r0 // QBLK, r0 % QBLK
            parts = []
            for kc in range(width // 128):
                sp = s_scr[r0:r1, kc * 128:(kc + 1) * 128] + madd_scr[q0:q0 + SM_ROWS, kc * 128:(kc + 1) * 128]
                if tile_off is not None:
                    rel = (QBLK // 128) * tile_off + q0 // 128 - kc
                    if rel in (0, 1):
                        sp = sp + bt_ref[rel, hh, q0 % 128:q0 % 128 + SM_ROWS, :]
                parts.append(sp)
            smax = parts[0]
            for sp in parts[1:]:
                smax = jnp.maximum(smax, sp)
            m_old = m_scr[r0:r1, :]
            m_new = jnp.maximum(m_old, jnp.max(smax, axis=-1, keepdims=True))
            alpha = jnp.exp(m_old - m_new)
            psum = None
            for kc, sp in enumerate(parts):
                p = jnp.exp(sp - m_new)
                p_scr[r0:r1, kc * 128:(kc + 1) * 128] = p.astype(BF16)
                psum = p if psum is None else psum + p
            l_scr[r0:r1, :] = alpha * l_scr[r0:r1, :] + jnp.sum(psum, axis=-1, keepdims=True)
            m_scr[r0:r1, :] = m_new
            a_scr[r0:r1, :] = alpha
        pv = jnp.dot(p_scr[:, :width], kv, preferred_element_type=F32)
        acc_scr[...] = _lane_tile(a_scr[...], B_KV_LATENT // 128) * acc_scr[...] + pv

    n_far = jnp.maximum(i - 1, 0)
    per_far = FAR_TILE // KTILE

    def far_body(jf, carry):
        attend(pl.multiple_of(jf * FAR_TILE, FAR_TILE), FAR_TILE, None)
        return carry

    lax.fori_loop(0, n_far // per_far, far_body, 0)

    if per_far > 1:
        @pl.when(n_far % per_far == 1)
        def _():
            attend(pl.multiple_of((n_far - 1) * KTILE, KTILE), KTILE, None)

    @pl.when(i >= 1)
    def _():
        attend(pl.multiple_of((i - 1) * KTILE, KTILE), KTILE, 1)

    attend(pl.multiple_of(i * KTILE, KTILE), KTILE, 0)
    inv_l = 1.0 / l_scr[...]
    o_ref[0] = (acc_scr[...] * _lane_tile(inv_l, B_KV_LATENT // 128)).astype(BF16)


def _attention(qa, qi, wi, ckv, ki, btab, bsz, seq):
    nqb = seq // QBLK
    gq = lambda b, i: (b * nqb + i, 0, 0)
    gb = lambda b, i: (b, 0, 0)
    return pl.pallas_call(
        _attn_kernel,
        out_shape=jax.ShapeDtypeStruct(qa.shape, BF16),
        grid=(bsz, nqb),
        in_specs=[
            pl.BlockSpec((1, B_HEADS * QBLK, B_KV_LATENT), gq),
            pl.BlockSpec((1, B_IDX_HEADS * QBLK, B_IDX_DIM), gq),
            pl.BlockSpec((1, B_IDX_HEADS * QBLK, 128), gq),
            pl.BlockSpec((1, seq, B_KV_LATENT), gb, pipeline_mode=pl.Buffered(1)),
            pl.BlockSpec((1, seq, B_IDX_DIM), gb, pipeline_mode=pl.Buffered(1)),
            pl.BlockSpec((2, B_HEADS, 128, 128), lambda b, i: (0, 0, 0, 0), pipeline_mode=pl.Buffered(1)),
        ],
        out_specs=pl.BlockSpec((1, B_HEADS * QBLK, B_KV_LATENT), gq),
        scratch_shapes=[
            pltpu.VMEM((QBLK, seq), I32),
            pltpu.VMEM((B_HEADS * QBLK, 128), F32),
            pltpu.VMEM((B_HEADS * QBLK, 128), F32),
            pltpu.VMEM((B_HEADS * QBLK, 128), F32),
            pltpu.VMEM((B_HEADS * QBLK, B_KV_LATENT), F32),
            pltpu.VMEM((B_HEADS * QBLK, FAR_TILE), F32),
            pltpu.VMEM((B_HEADS * QBLK, FAR_TILE), BF16),
            pltpu.VMEM((QBLK, FAR_TILE), F32),
            pltpu.VMEM((seq, QBLK), I32),
            pltpu.VMEM((QBLK, QBLK), I32),
        ],
        compiler_params=_cparams(("arbitrary", "arbitrary")),
        name="dsa_attention",
    )(qa, qi, wi, ckv.reshape(bsz, seq, -1), ki.reshape(bsz, seq, -1), btab)


def _bout_kernel(h_ref, ol_ref, mod_ref, wuv_ref, wout_ref, n2g_ref, rw_ref, h_out, hn_out, lg_out, o_scr):
    mod = mod_ref[0]
    for blk in range(ol_ref.shape[0]):
        for hh in range(B_HEADS):
            oh = jnp.dot(ol_ref[blk, hh * QBLK:(hh + 1) * QBLK, :], wuv_ref[hh], preferred_element_type=F32)
            o_scr[blk * QBLK:(blk + 1) * QBLK, hh * B_V_DIM:(hh + 1) * B_V_DIM] = oh.astype(BF16)
    y = jnp.dot(o_scr[...], wout_ref[...], preferred_element_type=F32)
    _residual_epilogue(h_ref[...], y, mod, n2g_ref[...], rw_ref[...], h_out, hn_out, lg_out)


def _bout(h, o_lat, mod_l, w_uv, w_out, n2g, router_w, seq):
    n, d = h.shape
    tm = ROW_TILE
    per_b = seq // tm
    qpb = tm // QBLK
    ne = router_w.shape[1]
    wuv_t = jnp.transpose(w_uv, (1, 0, 2)).astype(BF16)
    row = lambda i: (i, 0)
    return pl.pallas_call(
        _bout_kernel,
        out_shape=(jax.ShapeDtypeStruct((n, d), F32), jax.ShapeDtypeStruct((n, d // 2), I32),
                   jax.ShapeDtypeStruct((n, ne), F32)),
        grid=(n // tm,),
        in_specs=[
            pl.BlockSpec((tm, d), row),
            pl.BlockSpec((qpb, B_HEADS * QBLK, B_KV_LATENT), lambda i: (i, 0, 0)),
            pl.BlockSpec((1, 6, d), lambda i: (i // per_b, 0, 0)),
            _const_spec((B_HEADS, B_KV_LATENT, B_V_DIM)),
            _const_spec((B_HEADS * B_V_DIM, d)),
            _const_spec((1, d)),
            _const_spec((d, ne)),
        ],
        out_specs=(pl.BlockSpec((tm, d), row), pl.BlockSpec((tm, d // 2), row), pl.BlockSpec((tm, ne), row)),
        scratch_shapes=[pltpu.VMEM((tm, B_HEADS * B_V_DIM), BF16)],
        compiler_params=_cparams(("arbitrary",)),
        name="dsa_out",
    )(h, o_lat, mod_l, wuv_t, w_out.astype(BF16), n2g.reshape(1, d), router_w)


def _final_kernel(h1_ref, y0_ref, y1_ref, gt_ref, mod_ref, g_ref, o_ref):
    h = _moe_combine(h1_ref[...], y0_ref[...], y1_ref[...], gt_ref[...], mod_ref[0])
    o_ref[...] = h * lax.rsqrt(jnp.mean(h * h, axis=-1, keepdims=True) + EPS) * g_ref[...]


def _final(h1, y0, y1, gates, mod_l, final_g, seq):
    n, d = h1.shape
    tm = 512
    per_b = seq // tm
    row = lambda i: (i, 0)
    return pl.pallas_call(
        _final_kernel,
        out_shape=jax.ShapeDtypeStruct((n, d), F32),
        grid=(n // tm,),
        in_specs=[pl.BlockSpec((tm, d), row), pl.BlockSpec((tm, d), row), pl.BlockSpec((tm, d), row),
                  pl.BlockSpec((tm, TOP_K * 128), row), pl.BlockSpec((1, 6, d), lambda i: (i // per_b, 0, 0)),
                  _const_spec((1, d))],
        out_specs=pl.BlockSpec((tm, d), row),
        compiler_params=_cparams(("arbitrary",)),
        name="final_norm",
    )(h1, y0, y1, gates, mod_l, final_g.reshape(1, d))


def kernel(x, c, ada_w, ada_b, norm1_g, norm2_g, a_w_in, a_ln_g, a_ln_b, a_w_sp, a_b_sp, a_w_out, b_w_in,
           b_kv_norm_g, b_w_uk, b_w_uv, b_kidx_g, b_w_out, rel_bias, router_w, router_b, moe_w_gate,
           moe_w_up, moe_w_down, final_g):
    bsz, seq, d = x.shape
    n = bsz * seq
    assert seq % max(QBLK, GMLP_ROWS, ROW_TILE, 512) == 0 and d % 256 == 0
    assert min(B_TOPK_MAX, seq // 4) == B_TOPK_MAX and QBLK >= B_TOPK_MAX and KTILE == QBLK
    assert ada_w.shape[0] == 2 and moe_w_gate.shape[1] == N_EXPERTS and MOE_ROWS % MOE_SUB == 0
    mod = _adaln(c, ada_w, ada_b).reshape(ada_w.shape[0], bsz, 6, d)
    rw_pad = jnp.zeros((d, 128), F32).at[:, :N_EXPERTS].set(router_w)
    h = x.reshape(n, d)

    h1, hn2, logits = _gmlp_layer(h, mod[0], norm1_g[0], a_w_in[0], a_ln_g[0], a_ln_b[0], a_w_sp[0],
                                  a_b_sp[0], a_w_out[0], norm2_g[0], rw_pad, seq)
    y0, y1, gates = _moe(hn2, logits, router_b, moe_w_gate, moe_w_up, moe_w_down, 0)

    h, qa, ckv, qi, ki, wi = _bproj(h1, y0, y1, gates, mod[0], mod[1], norm1_g[1], b_w_in[0],
                                    b_kv_norm_g[0], b_w_uk[0], b_kidx_g[0], seq)
    o_lat = _attention(qa, qi, wi, ckv, ki, _bias_tables(rel_bias), bsz, seq)
    h1, hn2, logits = _bout(h, o_lat, mod[1], b_w_uv[0], b_w_out[0], norm2_g[1], rw_pad, seq)
    y0, y1, gates = _moe(hn2, logits, router_b, moe_w_gate, moe_w_up, moe_w_down, 1)

    out = _final(h1, y0, y1, gates, mod[1], final_g, seq)
    return out.reshape(bsz, seq, d)
```
